```python
import math
import jax, jax.numpy as jnp
from jax import lax
import numpy as np

D_MODEL = 2048
BATCH = 16
SEQ = 256
DEPTH = 2
DEC_BATCH = 4
DEC_SEQ = 1024
PAST_LEN = 256

GRID_W = 64
HEAD_DIM = 128
EPS = 1e-6
GLA_H = 4
GLA_DK = 64
GLA_DV = 128
GLA_W = GLA_H * GLA_DV
GLA_LR = 16
GLA_GATE_NORM = 16.0
GLA_CHUNK = 64
ATT_HQ = 8
ATT_HKV = 2
ATT_W = ATT_HQ * HEAD_DIM
Q_BLOCK = 128
ROPE_THETA = 10000.0
DN_H = 4
DN_DK = 128
DN_DV = 128
DN_W = DN_H * DN_DV
DN_CONV = 3
DN_CHUNK = 64
MIX_W = GLA_W + ATT_W + DN_W
FF = 4 * D_MODEL
N_MOD = 6
SPLITS = (GLA_H * GLA_DK, GLA_H * GLA_DK, GLA_W, GLA_W, 2 * GLA_LR,
          ATT_HQ * HEAD_DIM, ATT_HKV * HEAD_DIM, ATT_HKV * HEAD_DIM,
          3 * DN_W, DN_W, 2 * DN_H, 2 * DN_H)
IN_COLS = sum(SPLITS)

kernel_name = 'hybrid_gla_gqa_deltanet_diffusion_step'


def _rms_norm(x, g):
    xf = x.astype(jnp.float32)
    y = xf * lax.rsqrt(jnp.mean(xf * xf, axis=-1, keepdims=True) + EPS)
    return (y * g.astype(jnp.float32)).astype(x.dtype)


def _l2norm(x):
    return x * lax.rsqrt(jnp.sum(x * x, axis=-1, keepdims=True) + EPS)


def _flip(t):
    return jnp.flip(t, axis=1)


def _split_cols(p):
    idx = np.cumsum(SPLITS)[:-1].tolist()
    return jnp.split(p, idx, axis=-1)


def _to_chunks(t, C):
    B, L, H = t.shape[:3]
    t = t.reshape((B, L // C, C, H) + t.shape[3:])
    return jnp.moveaxis(t, (1, 3), (0, 2))


def _from_chunks(o):
    o = jnp.moveaxis(o, (0, 2), (1, 3))
    B, n, C, H, d = o.shape
    return o.reshape(B, n * C, H, d)


def _modulation(cond, w_mod, b_mod):
    m = jax.nn.silu(cond) @ w_mod + b_mod
    return [t[:, None, :] for t in jnp.split(m, N_MOD, axis=-1)]


def _gla_scan(q, k, v, la, S0):
    C = GLA_CHUNK
    mask = jnp.tril(jnp.ones((C, C), bool))[:, :, None]

    def step(S, xs):
        qc, kc, vc, gc = xs
        b = jnp.cumsum(gc, axis=-2)
        diff = b[..., :, None, :] - b[..., None, :, :]
        dec = jnp.where(mask, jnp.exp(jnp.where(mask, diff, 0.0)), 0.0)
        A = jnp.einsum('bhid,bhijd,bhjd->bhij', qc, dec, kc)
        o = jnp.einsum('bhij,bhjv->bhiv', A, vc) + jnp.einsum('bhid,bhdv->bhiv', qc * jnp.exp(b), S)
        bl = b[..., -1, :]
        S = S * jnp.exp(bl)[..., None] + jnp.einsum('bhjd,bhjv->bhdv', kc * jnp.exp(bl[..., None, :] - b), vc)
        return S, o

    S, o = lax.scan(step, S0, (_to_chunks(q, C), _to_chunks(k, C), _to_chunks(v, C), _to_chunks(la, C)))
    return _from_chunks(o), S


def _delta_scan(q, k, v, g, beta, S0):
    C = DN_CHUNK
    incl = jnp.tril(jnp.ones((C, C), bool))
    strict = jnp.tril(jnp.ones((C, C), bool), -1)
    dv = v.shape[-1]

    def step(S, xs):
        qc, kc, vc, gc, bc = xs
        G = jnp.cumsum(gc, axis=-1)
        diff = G[..., :, None] - G[..., None, :]
        gam = jnp.where(incl, jnp.exp(jnp.where(incl, diff, 0.0)), 0.0)
        kb = kc * bc[..., None]
        A = jnp.where(strict, jnp.einsum('bhid,bhjd->bhij', kb, kc) * gam, 0.0)
        rhs = jnp.concatenate([vc * bc[..., None], kb * jnp.exp(G)[..., None]], axis=-1)
        sol = lax.linalg.triangular_solve(A, rhs, left_side=True, lower=True, unit_diagonal=True)
        u, w = sol[..., :dv], sol[..., dv:]
        v_new = u - jnp.einsum('bhcd,bhdv->bhcv', w, S)
        att = jnp.einsum('bhid,bhjd->bhij', qc, kc) * gam
        o = jnp.einsum('bhid,bhdv->bhiv', qc * jnp.exp(G)[..., None], S) + jnp.einsum('bhij,bhjv->bhiv', att, v_new)
        Gl = G[..., -1:]
        S = S * jnp.exp(Gl)[..., None] + jnp.einsum('bhjd,bhjv->bhdv', kc * jnp.exp(Gl - G)[..., None], v_new)
        return S, o

    S, o = lax.scan(step, S0, (_to_chunks(q, C), _to_chunks(k, C), _to_chunks(v, C),
                                _to_chunks(g, C), _to_chunks(beta, C)))
    return _from_chunks(o), S


def _gla_mixer(gq, gk, gv, gr, glr, w2, bias, norm_g, S0):
    B, L, _ = gq.shape
    f32 = jnp.float32
    q = gq.reshape(B, L, GLA_H, GLA_DK).astype(f32) * GLA_DK ** -0.5
    k = gk.reshape(B, L, GLA_H, GLA_DK).astype(f32)
    v = gv.reshape(B, L, GLA_H, GLA_DV).astype(f32)
    lr = glr.reshape(B, L, 2, GLA_LR).astype(f32)
    la = jax.nn.log_sigmoid(jnp.einsum('blzr,zrk->blzk', lr, w2.astype(f32)) + bias.astype(f32)) / GLA_GATE_NORM
    la = la.reshape(B, L, 2, GLA_H, GLA_DK)
    S0 = S0.astype(f32)
    of, Sf = _gla_scan(q, k, v, la[:, :, 0], S0[:, 0])
    ob, Sb = _gla_scan(_flip(q), _flip(k), _flip(v), _flip(la[:, :, 1]), S0[:, 1])
    o = of + _flip(ob)
    o = _rms_norm(o, norm_g) * jax.nn.silu(gr.reshape(B, L, GLA_H, GLA_DV).astype(f32))
    return o.reshape(B, L, GLA_W).astype(gq.dtype), jnp.stack([Sf, Sb], axis=1)


def _axial_rope(x):
    B, L, H, hd = x.shape
    f32 = jnp.float32
    n_rows = L // GRID_W
    row = jnp.repeat(jnp.arange(n_rows), GRID_W)
    col = jnp.tile(jnp.arange(GRID_W), n_rows)
    half = hd // 2
    inv = ROPE_THETA ** (-jnp.arange(0, half, 2, dtype=f32) / half)

    def rot(xa, pos):
        ang = pos.astype(f32)[:, None] * inv
        cos = jnp.cos(ang)[None, :, None, :]
        sin = jnp.sin(ang)[None, :, None, :]
        x1, x2 = jnp.split(xa.astype(f32), 2, axis=-1)
        return jnp.concatenate([x1 * cos - x2 * sin, x2 * cos + x1 * sin], axis=-1)

    return jnp.concatenate([rot(x[..., :half], row), rot(x[..., half:], col)], axis=-1).astype(x.dtype)


def _attention(q, k, v):
    B, L, HQ, hd = q.shape
    hkv = k.shape[2]
    G = HQ // hkv
    nb = L // Q_BLOCK
    qb = jnp.moveaxis(q.reshape(B, nb, Q_BLOCK, hkv, G, hd), 1, 0)
    scale = hd ** -0.5

    def block(qi):
        s = jnp.einsum('bqhgd,bkhd->bhgqk', qi, k).astype(jnp.float32) * scale
        p = jax.nn.softmax(s, axis=-1)
        return jnp.einsum('bhgqk,bkhd->bqhgd', p.astype(v.dtype), v)

    o = lax.map(block, qb)
    return jnp.moveaxis(o, 0, 1).reshape(B, L, HQ, hd)


def _gqa_mixer(aq, ak, av, q_g, k_g, kv_ctx):
    B, L, _ = aq.shape
    q = _rms_norm(aq.reshape(B, L, ATT_HQ, HEAD_DIM), q_g)
    k = _rms_norm(ak.reshape(B, L, ATT_HKV, HEAD_DIM), k_g)
    v = av.reshape(B, L, ATT_HKV, HEAD_DIM)
    if kv_ctx is None:
        o = _attention(q, k, v)
    else:
        k_ctx, v_ctx = kv_ctx
        k_all = jnp.concatenate([k_ctx.astype(k.dtype), _axial_rope(k)], axis=1)
        v_all = jnp.concatenate([v_ctx.astype(v.dtype), v], axis=1)
        o = _attention(_axial_rope(q), k_all, v_all)
    return o.reshape(B, L, ATT_W), k, v


def _short_conv(x, w):
    Ch = x.shape[-1]
    y = lax.conv_general_dilated(x, w[:, None, :], window_strides=(1,),
                                 padding=[(DN_CONV // 2, DN_CONV // 2)],
                                 dimension_numbers=('NWC', 'WIO', 'NWC'),
                                 feature_group_count=Ch)
    return jax.nn.silu(y)


def _deltanet_mixer(dqkv, dz, da, db, conv_w, a_log, dt_bias, norm_g, S0):
    B, L, _ = dqkv.shape
    f32 = jnp.float32
    qkv = _short_conv(dqkv.astype(f32), conv_w.astype(f32))
    q, k, v = jnp.split(qkv, 3, axis=-1)
    q = _l2norm(q.reshape(B, L, DN_H, DN_DK)) * DN_DK ** -0.5
    k = _l2norm(k.reshape(B, L, DN_H, DN_DK))
    v = v.reshape(B, L, DN_H, DN_DV)
    a = da.reshape(B, L, 2, DN_H).astype(f32)
    b = db.reshape(B, L, 2, DN_H).astype(f32)
    g = -jnp.exp(a_log.astype(f32)) * jax.nn.softplus(a + dt_bias.astype(f32))
    beta = jax.nn.sigmoid(b)
    S0 = S0.astype(f32)
    of, Sf = _delta_scan(q, k, v, g[:, :, 0], beta[:, :, 0], S0[:, 0])
    ob, Sb = _delta_scan(_flip(q), _flip(k), _flip(v), _flip(g[:, :, 1]), _flip(beta[:, :, 1]), S0[:, 1])
    o = of + _flip(ob)
    o = _rms_norm(o, norm_g) * jax.nn.silu(dz.reshape(B, L, DN_H, DN_DV).astype(f32))
    return o.reshape(B, L, DN_W).astype(dqkv.dtype), jnp.stack([Sf, Sb], axis=1)


def _layer(x, mods, lw, ctx):
    shift1, scale1, gate1, shift2, scale2, gate2 = mods
    B, L, _ = x.shape
    h = _rms_norm(x, lw['norm1_g']) * (1.0 + scale1) + shift1
    gq, gk, gv, gr, glr, aq, ak, av, dqkv, dz, da, db = _split_cols(h @ lw['w_in'])
    if ctx is None:
        kv_ctx = None
        s_gla0 = jnp.zeros((B, 2, GLA_H, GLA_DK, GLA_DV), jnp.float32)
        s_dn0 = jnp.zeros((B, 2, DN_H, DN_DK, DN_DV), jnp.float32)
    else:
        k_ctx, v_ctx, s_gla0, s_dn0 = ctx
        kv_ctx = (k_ctx, v_ctx)
    o_gla, s_gla = _gla_mixer(gq, gk, gv, gr, glr, lw['gla_w2'], lw['gla_b'], lw['gla_norm_g'], s_gla0)
    o_att, k, v = _gqa_mixer(aq, ak, av, lw['q_norm_g'], lw['k_norm_g'], kv_ctx)
    o_dn, s_dn = _deltanet_mixer(dqkv, dz, da, db, lw['dn_conv'], lw['dn_a_log'], lw['dn_dt_bias'],
                                 lw['dn_norm_g'], s_dn0)
    mix = jnp.concatenate([o_gla, o_att, o_dn], axis=-1) @ lw['w_out']
    x = x + gate1 * mix
    h = _rms_norm(x, lw['norm2_g']) * (1.0 + scale2) + shift2
    ff = jnp.square(jax.nn.relu(h @ lw['w_ff1'])) @ lw['w_ff2']
    x = x + gate2 * ff
    return x, (k, v, s_gla.astype(x.dtype), s_dn.astype(x.dtype))


def setup_inputs(seed: int = 0) -> dict:
    key = jax.random.key(seed)
    ks = jax.random.split(key, 32)
    f32 = jnp.float32

    def nrm(k, shape, s):
        return jax.random.normal(k, shape, f32) * s

    dt = jnp.exp(jax.random.uniform(ks[20], (DEPTH, 2, DN_H), f32, math.log(1e-3), math.log(1e-1)))
    return {
        'x_prompt': nrm(ks[0], (BATCH, SEQ, D_MODEL), 1.0),
        'x_sample': nrm(ks[1], (DEC_BATCH, DEC_SEQ, D_MODEL), 1.0),
        'cache_k': nrm(ks[2], (DEC_BATCH, DEPTH, PAST_LEN, ATT_HKV, HEAD_DIM), 1.0),
        'cache_v': nrm(ks[3], (DEC_BATCH, DEPTH, PAST_LEN, ATT_HKV, HEAD_DIM), 1.0),
        'state_gla': nrm(ks[4], (DEC_BATCH, DEPTH, 2, GLA_H, GLA_DK, GLA_DV), 1.0),
        'state_dn': nrm(ks[5], (DEC_BATCH, DEPTH, 2, DN_H, DN_DK, DN_DV), 0.3),
        'c': nrm(ks[6], (DEC_BATCH, D_MODEL), 1.0),
        'c_ctx': nrm(ks[7], (D_MODEL,), 1.0),
        'norm1_g': 1.0 + nrm(ks[8], (DEPTH, D_MODEL), 0.02),
        'norm2_g': 1.0 + nrm(ks[9], (DEPTH, D_MODEL), 0.02),
        'w_mod': nrm(ks[10], (DEPTH, D_MODEL, N_MOD * D_MODEL), 0.5 * D_MODEL ** -0.5),
        'b_mod': nrm(ks[11], (DEPTH, N_MOD * D_MODEL), 0.02),
        'w_in': nrm(ks[12], (DEPTH, D_MODEL, IN_COLS), D_MODEL ** -0.5),
        'gla_w2': nrm(ks[13], (DEPTH, 2, GLA_LR, GLA_H * GLA_DK), GLA_LR ** -0.5),
        'gla_b': nrm(ks[14], (DEPTH, 2, GLA_H * GLA_DK), 0.1),
        'gla_norm_g': 1.0 + nrm(ks[15], (DEPTH, GLA_DV), 0.02),
        'q_norm_g': 1.0 + nrm(ks[16], (DEPTH, HEAD_DIM), 0.02),
        'k_norm_g': 1.0 + nrm(ks[17], (DEPTH, HEAD_DIM), 0.02),
        'dn_conv': nrm(ks[18], (DEPTH, DN_CONV, 3 * DN_W), DN_CONV ** -0.5),
        'dn_a_log': jnp.log(jax.random.uniform(ks[19], (DEPTH, 2, DN_H), f32, 1.0, 16.0)),
        'dn_dt_bias': dt + jnp.log(-jnp.expm1(-dt)),
        'dn_norm_g': 1.0 + nrm(ks[21], (DEPTH, DN_DV), 0.02),
        'w_out': nrm(ks[22], (DEPTH, MIX_W, D_MODEL), MIX_W ** -0.5),
        'w_ff1': nrm(ks[23], (DEPTH, D_MODEL, FF), D_MODEL ** -0.5),
        'w_ff2': nrm(ks[24], (DEPTH, FF, D_MODEL), FF ** -0.5),
    }


def reference(x_prompt, x_sample, cache_k, cache_v, state_gla, state_dn, c, c_ctx,
              norm1_g, norm2_g, w_mod, b_mod, w_in, gla_w2, gla_b, gla_norm_g,
              q_norm_g, k_norm_g, dn_conv, dn_a_log, dn_dt_bias, dn_norm_g,
              w_out, w_ff1, w_ff2):
    stacked = dict(norm1_g=norm1_g, norm2_g=norm2_g, w_in=w_in, gla_w2=gla_w2, gla_b=gla_b,
                   gla_norm_g=gla_norm_g, q_norm_g=q_norm_g, k_norm_g=k_norm_g, dn_conv=dn_conv,
                   dn_a_log=dn_a_log, dn_dt_bias=dn_dt_bias, dn_norm_g=dn_norm_g,
                   w_out=w_out, w_ff1=w_ff1, w_ff2=w_ff2)

    x = x_prompt
    ks, vs, sgs, sds = [], [], [], []
    for l in range(DEPTH):
        lw = {n: a[l] for n, a in stacked.items()}
        mods = _modulation(c_ctx[None, :], w_mod[l], b_mod[l])
        x, (k, v, s_gla, s_dn) = _layer(x, mods, lw, None)
        ks.append(k)
        vs.append(v)
        sgs.append(s_gla)
        sds.append(s_dn)
    y_prompt = x
    new_cache_k = jnp.stack(ks, axis=1)
    new_cache_v = jnp.stack(vs, axis=1)
    new_state_gla = jnp.stack(sgs, axis=1)
    new_state_dn = jnp.stack(sds, axis=1)

    x = x_sample
    for l in range(DEPTH):
        lw = {n: a[l] for n, a in stacked.items()}
        mods = _modulation(c, w_mod[l], b_mod[l])
        ctx = (cache_k[:, l], cache_v[:, l], state_gla[:, l], state_dn[:, l])
        x, _ = _layer(x, mods, lw, ctx)
    y_sample = x
    return (y_prompt, y_sample, new_cache_k, new_cache_v, new_state_gla, new_state_dn)
```

```python
import functools
import math

import numpy as np
import jax
import jax.numpy as jnp
from jax import lax
from jax.experimental import pallas as pl
from jax.experimental.pallas import tpu as pltpu

F32 = jnp.float32
BF16 = jnp.bfloat16

D_MODEL = 2048
SEQ = 256
DEC_SEQ = 1024
PAST_LEN = 256
GRID_W = 64
HEAD_DIM = 128
EPS = 1e-6
GLA_H = 4
GLA_DK = 64
GLA_DV = 128
GLA_LR = 16
GLA_GATE_NORM = 16.0
ATT_HQ = 8
ATT_HKV = 2
ROPE_THETA = 10000.0
DN_H = 4
DN_DK = 128
DN_DV = 128
DN_CONV = 3
CHUNK = 64
FF = 4 * D_MODEL
N_MOD = 6
LANES = 128

P_AQ = 0
P_AKV = 1024
P_GLA = 1536
P_DN = 3072
P_DZ = 4608
P_MAIN = 5120
S_DA = 32
S_DB = 40

VMEM_LIMIT = 56 * 1024 * 1024


def _cparams(sem):
    return pltpu.CompilerParams(dimension_semantics=sem, vmem_limit_bytes=VMEM_LIMIT)


def _bdot(a, b):
    return jnp.dot(a.astype(BF16), b.astype(BF16), preferred_element_type=F32)


def _bdot_nt(a, b):
    return lax.dot_general(a.astype(BF16), b.astype(BF16), (((1,), (1,)), ((), ())),
                           preferred_element_type=F32)


def _bdot_tn(a, b):
    return lax.dot_general(a.astype(BF16), b.astype(BF16), (((0,), (0,)), ((), ())),
                           preferred_element_type=F32)


def _split3(x):
    hi = x.astype(BF16)
    r = x - hi.astype(F32)
    mid = r.astype(BF16)
    lo = (r - mid.astype(F32)).astype(BF16)
    return hi, mid, lo


def _dot_exact_lhs(lhs_bf16, x):
    hi, mid, lo = _split3(x)
    d = functools.partial(jnp.dot, preferred_element_type=F32)
    return d(lhs_bf16, hi) + d(lhs_bf16, mid) + d(lhs_bf16, lo)


def _dot3(a, b):
    ah, am, _ = _split3(a)
    bh, bm, _ = _split3(b)
    d = functools.partial(jnp.dot, preferred_element_type=F32)
    return d(ah, bh) + (d(ah, bm) + d(am, bh))


def _silu(x):
    return x * jax.nn.sigmoid(x)


def _softplus(x):
    return jnp.maximum(x, 0.0) + jnp.log1p(jnp.exp(-jnp.abs(x)))


def _rms(x, g):
    return x * lax.rsqrt(jnp.mean(x * x, axis=-1, keepdims=True) + EPS) * g


def _mods_kernel(c_ref, w_ref, b_ref, o_ref):
    o_ref[...] = _bdot(_silu(c_ref[...]), w_ref[...]) + b_ref[...]


def _modulation(cond8, w_mod, b_mod):
    depth = w_mod.shape[0]
    n_out = N_MOD * D_MODEL
    tn = 1024
    return pl.pallas_call(
        _mods_kernel,
        grid=(depth, n_out // tn),
        in_specs=[
            pl.BlockSpec((8, D_MODEL), lambda l, j: (0, 0)),
            pl.BlockSpec((None, D_MODEL, tn), lambda l, j: (l, 0, j)),
            pl.BlockSpec((None, 1, tn), lambda l, j: (l, 0, j)),
        ],
        out_specs=pl.BlockSpec((None, 8, tn), lambda l, j: (l, 0, j)),
        out_shape=jax.ShapeDtypeStruct((depth, 8, n_out), F32),
        compiler_params=_cparams(("arbitrary", "arbitrary")),
        name="modulation",
    )(cond8, w_mod, b_mod.reshape(depth, 1, n_out))


def _group_of_tile(i, tm, n_ctx_rows):
    n_ctx_tiles = n_ctx_rows // tm
    per_lat = DEC_SEQ // tm
    return jnp.where(i < n_ctx_tiles, 0, 1 + (i - n_ctx_tiles) // per_lat)


def _inproj_kernel(x_ref, mod_ref, g_ref, w_ref, ws_ref, p_ref, ps_ref, h_sc):
    @pl.when(pl.program_id(1) == 0)
    def _():
        m = mod_ref[...]
        h = _rms(x_ref[...], g_ref[...]) * (1.0 + m[1:2]) + m[0:1]
        hb = h.astype(BF16)
        h_sc[...] = hb
        ps_ref[...] = jnp.dot(hb, ws_ref[...], preferred_element_type=F32)

    p_ref[...] = jnp.dot(h_sc[...], w_ref[...], preferred_element_type=F32)


def _inproj(x, mods_l, g1, w_main, w_small, n_ctx_rows, tm=512, tn=1024):
    rows = x.shape[0]
    grp = functools.partial(_group_of_tile, tm=tm, n_ctx_rows=n_ctx_rows)
    return pl.pallas_call(
        _inproj_kernel,
        grid=(rows // tm, P_MAIN // tn),
        in_specs=[
            pl.BlockSpec((tm, D_MODEL), lambda i, j: (i, 0)),
            pl.BlockSpec((None, N_MOD, D_MODEL), lambda i, j: (grp(i), 0, 0)),
            pl.BlockSpec((1, D_MODEL), lambda i, j: (0, 0)),
            pl.BlockSpec((D_MODEL, tn), lambda i, j: (0, j)),
            pl.BlockSpec((D_MODEL, LANES), lambda i, j: (0, 0)),
        ],
        out_specs=[
            pl.BlockSpec((tm, tn), lambda i, j: (i, j)),
            pl.BlockSpec((tm, LANES), lambda i, j: (i, 0)),
        ],
        out_shape=[
            jax.ShapeDtypeStruct((rows, P_MAIN), F32),
            jax.ShapeDtypeStruct((rows, LANES), F32),
        ],
        scratch_shapes=[pltpu.VMEM((tm, D_MODEL), BF16)],
        compiler_params=_cparams(("arbitrary", "arbitrary")),
        name="inproj",
    )(x, mods_l, g1, w_main, w_small)


def _outproj_kernel(x_ref, og_ref, oa_ref, od_ref, mod_ref, g_ref, w_ref, x1_ref, h2_ref):
    m = mod_ref[...]
    d = functools.partial(jnp.dot, preferred_element_type=F32)
    mix = (d(og_ref[...], w_ref[0:512, :]) + d(oa_ref[...], w_ref[512:1536, :])
           + d(od_ref[...], w_ref[1536:2048, :]))
    x1 = x_ref[...] + m[2:3] * mix
    x1_ref[...] = x1
    h2_ref[...] = (_rms(x1, g_ref[...]) * (1.0 + m[4:5]) + m[3:4]).astype(BF16)


def _outproj(x, o_gla, o_att, o_dn, mods_l, g2, w_out, n_ctx_rows, tm=512):
    rows = x.shape[0]
    grp = functools.partial(_group_of_tile, tm=tm, n_ctx_rows=n_ctx_rows)
    return pl.pallas_call(
        _outproj_kernel,
        grid=(rows // tm,),
        in_specs=[
            pl.BlockSpec((tm, D_MODEL), lambda i: (i, 0)),
            pl.BlockSpec((tm, 512), lambda i: (i, 0)),
            pl.BlockSpec((tm, 1024), lambda i: (i, 0)),
            pl.BlockSpec((tm, 512), lambda i: (i, 0)),
            pl.BlockSpec((None, N_MOD, D_MODEL), lambda i: (grp(i), 0, 0)),
            pl.BlockSpec((1, D_MODEL), lambda i: (0, 0)),
            pl.BlockSpec((D_MODEL, D_MODEL), lambda i: (0, 0)),
        ],
        out_specs=[
            pl.BlockSpec((tm, D_MODEL), lambda i: (i, 0)),
            pl.BlockSpec((tm, D_MODEL), lambda i: (i, 0)),
        ],
        out_shape=[
            jax.ShapeDtypeStruct((rows, D_MODEL), F32),
            jax.ShapeDtypeStruct((rows, D_MODEL), BF16),
        ],
        compiler_params=_cparams(("arbitrary",)),
        name="outproj",
    )(x, o_gla, o_att, o_dn, mods_l, g2, w_out)


def _ffn_kernel(h_ref, x1_ref, mod_ref, w1_ref, w2_ref, y_ref):
    j = pl.program_id(1)
    t = jnp.dot(h_ref[...], w1_ref[...], preferred_element_type=F32)
    t = jnp.square(jnp.maximum(t, 0.0))
    part = jnp.dot(t.astype(BF16), w2_ref[...], preferred_element_type=F32)

    @pl.when(j == 0)
    def _():
        y_ref[...] = part

    @pl.when(j > 0)
    def _():
        y_ref[...] += part

    @pl.when(j == pl.num_programs(1) - 1)
    def _():
        y_ref[...] = x1_ref[...] + mod_ref[5:6, :] * y_ref[...]


def _ffn(h2, x1, mods_l, w1, w2, n_ctx_rows, tm=512, tf=512):
    rows = x1.shape[0]
    grp = functools.partial(_group_of_tile, tm=tm, n_ctx_rows=n_ctx_rows)
    return pl.pallas_call(
        _ffn_kernel,
        grid=(rows // tm, FF // tf),
        in_specs=[
            pl.BlockSpec((tm, D_MODEL), lambda i, j: (i, 0)),
            pl.BlockSpec((tm, D_MODEL), lambda i, j: (i, 0)),
            pl.BlockSpec((None, N_MOD, D_MODEL), lambda i, j: (grp(i), 0, 0)),
            pl.BlockSpec((D_MODEL, tf), lambda i, j: (0, j)),
            pl.BlockSpec((tf, D_MODEL), lambda i, j: (j, 0)),
        ],
        out_specs=pl.BlockSpec((tm, D_MODEL), lambda i, j: (i, 0)),
        out_shape=jax.ShapeDtypeStruct((rows, D_MODEL), F32),
        compiler_params=_cparams(("arbitrary", "arbitrary")),
        name="ffn",
    )(h2, x1, mods_l, w1, w2)


def _rope_tables():
    half = HEAD_DIM // 2
    pos = np.arange(DEC_SEQ)
    row = (pos // GRID_W).astype(np.float32)
    col = (pos % GRID_W).astype(np.float32)
    inv = (ROPE_THETA ** (-np.arange(0, half, 2, dtype=np.float32) / half)).astype(np.float32)
    ar = row[:, None] * inv[None, :]
    ac = col[:, None] * inv[None, :]
    cos = np.concatenate([np.cos(ar), np.cos(ar), np.cos(ac), np.cos(ac)], axis=1)
    sin = np.concatenate([-np.sin(ar), np.sin(ar), -np.sin(ac), np.sin(ac)], axis=1)
    return jnp.asarray(cos, F32), jnp.asarray(sin, F32)


def _rope(x, cos, sin):
    lane = lax.broadcasted_iota(jnp.int32, x.shape, 1)
    first = (lane % 64) < 32
    partner = jnp.where(first, pltpu.roll(x, 96, 1), pltpu.roll(x, 32, 1))
    return x * cos + partner * sin


def _softmax_pv(q, kb, vb):
    s = lax.dot_general(q.astype(BF16), kb, (((1,), (1,)), ((), ())),
                        preferred_element_type=F32) * (HEAD_DIM ** -0.5)
    m = jnp.max(s, axis=-1, keepdims=True)
    p = jnp.exp(s - m)
    l = jnp.sum(p, axis=-1, keepdims=True)
    return jnp.dot(p.astype(BF16), vb, preferred_element_type=F32) / l


def _att_ctx_kernel(aq_ref, akv_ref, qg_ref, kg_ref, o_ref, nk_ref, nv_ref):
    qg = qg_ref[...]
    kg = kg_ref[...]
    for hk in range(ATT_HKV):
        ks = slice(hk * HEAD_DIM, (hk + 1) * HEAD_DIM)
        kn = _rms(akv_ref[:, ks], kg)
        v = akv_ref[:, 256 + hk * HEAD_DIM:256 + (hk + 1) * HEAD_DIM]
        nk_ref[:, ks] = kn
        nv_ref[:, ks] = v
        kb = kn.astype(BF16)
        vb = v.astype(BF16)
        for g in range(ATT_HQ // ATT_HKV):
            qs = slice((hk * 4 + g) * HEAD_DIM, (hk * 4 + g + 1) * HEAD_DIM)
            qn = _rms(aq_ref[:, qs], qg)
            o_ref[:, qs] = _softmax_pv(qn, kb, vb).astype(BF16)


def _att_lat_kernel(aq_ref, akv_ref, qg_ref, kg_ref, cos_ref, sin_ref, ck_ref, cv_ref, o_ref,
                    k_sc, v_sc):
    qg = qg_ref[...]
    kg = kg_ref[...]
    qb = 256
    for hk in range(ATT_HKV):
        ks = slice(hk * HEAD_DIM, (hk + 1) * HEAD_DIM)
        k_sc[hk, 0:PAST_LEN, :] = ck_ref[:, ks].astype(BF16)
        v_sc[hk, 0:PAST_LEN, :] = cv_ref[:, ks].astype(BF16)
        kn = _rope(_rms(akv_ref[:, ks], kg), cos_ref[...], sin_ref[...])
        k_sc[hk, PAST_LEN:, :] = kn.astype(BF16)
        v_sc[hk, PAST_LEN:, :] = akv_ref[:, 256 + hk * HEAD_DIM:256 + (hk + 1) * HEAD_DIM].astype(BF16)

    def body(r, carry):
        r0 = pl.multiple_of(r * qb, qb)
        cos = cos_ref[pl.ds(r0, qb), :]
        sin = sin_ref[pl.ds(r0, qb), :]
        for hk in range(ATT_HKV):
            kb = k_sc[hk]
            vb = v_sc[hk]
            for g in range(ATT_HQ // ATT_HKV):
                qs = slice((hk * 4 + g) * HEAD_DIM, (hk * 4 + g + 1) * HEAD_DIM)
                qn = _rope(_rms(aq_ref[pl.ds(r0, qb), qs], qg), cos, sin)
                o_ref[pl.ds(r0, qb), qs] = _softmax_pv(qn, kb, vb).astype(BF16)
        return carry

    lax.fori_loop(0, DEC_SEQ // qb, body, 0)


def _att_ctx(p_main, qg, kg, n_seq):
    L = SEQ
    return pl.pallas_call(
        _att_ctx_kernel,
        grid=(n_seq,),
        in_specs=[
            pl.BlockSpec((L, 1024), lambda b: (b, P_AQ // 1024)),
            pl.BlockSpec((L, 512), lambda b: (b, P_AKV // 512)),
            pl.BlockSpec((1, HEAD_DIM), lambda b: (0, 0)),
            pl.BlockSpec((1, HEAD_DIM), lambda b: (0, 0)),
        ],
        out_specs=[
            pl.BlockSpec((L, 1024), lambda b: (b, 0)),
            pl.BlockSpec((None, L, 256), lambda b: (b, 0, 0)),
            pl.BlockSpec((None, L, 256), lambda b: (b, 0, 0)),
        ],
        out_shape=[
            jax.ShapeDtypeStruct((n_seq * L, 1024), BF16),
            jax.ShapeDtypeStruct((n_seq, L, 256), F32),
            jax.ShapeDtypeStruct((n_seq, L, 256), F32),
        ],
        compiler_params=_cparams(("arbitrary",)),
        name="att_ctx",
    )(p_main, p_main, qg, kg)


def _att_lat(p_main, qg, kg, cos, sin, ck, cv, n_seq, row_blk0):
    L = DEC_SEQ
    return pl.pallas_call(
        _att_lat_kernel,
        grid=(n_seq,),
        in_specs=[
            pl.BlockSpec((L, 1024), lambda b: (row_blk0 + b, P_AQ // 1024)),
            pl.BlockSpec((L, 512), lambda b: (row_blk0 + b, P_AKV // 512)),
            pl.BlockSpec((1, HEAD_DIM), lambda b: (0, 0)),
            pl.BlockSpec((1, HEAD_DIM), lambda b: (0, 0)),
            pl.BlockSpec((L, HEAD_DIM), lambda b: (0, 0)),
            pl.BlockSpec((L, HEAD_DIM), lambda b: (0, 0)),
            pl.BlockSpec((None, PAST_LEN, 256), lambda b: (b, 0, 0)),
            pl.BlockSpec((None, PAST_LEN, 256), lambda b: (b, 0, 0)),
        ],
        out_specs=pl.BlockSpec((L, 1024), lambda b: (b, 0)),
        out_shape=jax.ShapeDtypeStruct((n_seq * L, 1024), BF16),
        scratch_shapes=[
            pltpu.VMEM((ATT_HKV, PAST_LEN + L, HEAD_DIM), BF16),
            pltpu.VMEM((ATT_HKV, PAST_LEN + L, HEAD_DIM), BF16),
        ],
        compiler_params=_cparams(("arbitrary",)),
        name="att_lat",
    )(p_main, p_main, qg, kg, cos, sin, ck, cv)


_GLA_LEVELS = (32, 16, 8, 4, 2, 1)


def _gla_consts():
    C = CHUNK
    i = np.arange(C)[:, None]
    t = np.arange(C)[None, :]
    mats = [t <= i, t > i]
    masks = [i == t]
    for s in _GLA_LEVELS:
        m = (i // s) * s
        if s > 1:
            mats.append((t > m) & (t <= i))
        mats.append((t > i) & (t <= m + s))
        masks.append((i // (2 * s) == t // (2 * s)) & ((i // s) % 2 == 1) & ((t // s) % 2 == 0))
    fwd = np.concatenate(mats, 0)
    bwd = np.concatenate([mm[::-1, ::-1] for mm in mats], 0)
    mstack = np.stack([fwd, bwd]).astype(np.float32)
    mf = np.stack([np.concatenate([mm, mm], 0) for mm in masks])
    mb = np.stack([np.concatenate([mm[::-1, ::-1], mm[::-1, ::-1]], 0) for mm in masks])
    lmask = np.stack([mf, mb]).astype(np.float32)
    return jnp.asarray(mstack, BF16), jnp.asarray(lmask, F32)


def _gla_kernel(L, has_state, *refs):
    if has_state:
        (pg_ref, ps_ref, w2_ref, gb_ref, ng_ref, ms_ref, lm_ref, s0_ref,
         o_ref, so_ref, la_sc, of_sc, ob_sc, st_sc) = refs
    else:
        (pg_ref, ps_ref, w2_ref, gb_ref, ng_ref, ms_ref, lm_ref,
         o_ref, so_ref, la_sc, of_sc, ob_sc, st_sc) = refs
    n = L // CHUNK
    C = CHUNK

    ps = ps_ref[...]
    for z in range(2):
        x = _dot3(ps, w2_ref[z]) + gb_ref[z]
        la_sc[z] = (jnp.minimum(x, 0.0) - jnp.log1p(jnp.exp(-jnp.abs(x)))) * (1.0 / GLA_GATE_NORM)

    if has_state:
        st_sc[...] = s0_ref[...]
    else:
        st_sc[...] = jnp.zeros(st_sc.shape, F32)

    lane = lax.broadcasted_iota(jnp.int32, (C, LANES), 1)
    first = lane < 64

    def stack_heads(t):
        return jnp.concatenate([jnp.where(first, t, 0.0), jnp.where(first, 0.0, t)], axis=0)

    def body(c, carry):
        for z in range(2):
            cc = c if z == 0 else n - 1 - c
            r0 = pl.multiple_of(cc * C, C)
            rows = pl.ds(r0, C)
            last = C - 1 if z == 0 else 0
            osc = of_sc if z == 0 else ob_sc
            for p in range(2):
                cols = slice(p * LANES, (p + 1) * LANES)
                q = pg_ref[rows, cols] * (GLA_DK ** -0.5)
                k = pg_ref[rows, 256 + p * LANES:256 + (p + 1) * LANES]
                e = jnp.exp(_dot_exact_lhs(ms_ref[z], la_sc[z, rows, cols]))
                eb = e[0:C]
                erem = e[C:2 * C]
                st = st_sc[z, p]
                a2 = _bdot_nt(stack_heads(q), k) * lm_ref[z, 0]
                blk = 2
                for li, s in enumerate(_GLA_LEVELS):
                    if s > 1:
                        qs = q * e[blk * C:(blk + 1) * C]
                        blk += 1
                    else:
                        qs = q
                    ks = k * e[blk * C:(blk + 1) * C]
                    blk += 1
                    a2 = a2 + _bdot_nt(stack_heads(qs), ks) * lm_ref[z, li + 1]
                inter = _bdot_nt(stack_heads(q * eb), st)
                kl = k * erem
                zs = []
                for hh in range(2):
                    h = 2 * p + hh
                    v = pg_ref[rows, 512 + h * LANES:512 + (h + 1) * LANES]
                    o = _bdot(a2[hh * C:(hh + 1) * C], v) + inter[hh * C:(hh + 1) * C]
                    osc[rows, h * LANES:(h + 1) * LANES] = o
                    zs.append(_bdot_tn(v, kl))
                st_sc[z, p] = st * eb[last:last + 1, :] + jnp.where(first[0:1], zs[0], zs[1])
        return carry

    lax.fori_loop(0, n, body, 0)

    ng = ng_ref[...]

    def fin(c, carry):
        r0 = pl.multiple_of(c * C, C)
        rows = pl.ds(r0, C)
        for h in range(GLA_H):
            cols = slice(h * LANES, (h + 1) * LANES)
            o = of_sc[rows, cols] + ob_sc[rows, cols]
            gate = _silu(pg_ref[rows, 1024 + h * LANES:1024 + (h + 1) * LANES])
            o_ref[rows, cols] = (_rms(o, ng) * gate).astype(BF16)
        return carry

    lax.fori_loop(0, n, fin, 0)
    so_ref[...] = st_sc[...]


def _gla(p_main, p_small, w2p, gbias, ng, consts, s0, L, n_seq, row_blk0):
    mstack, lmask = consts
    has_state = s0 is not None
    in_specs = [
        pl.BlockSpec((L, 1536), lambda b: (row_blk0 + b, P_GLA // 1536)),
        pl.BlockSpec((L, LANES), lambda b: (row_blk0 + b, 0)),
        pl.BlockSpec((2, LANES, 256), lambda b: (0, 0, 0)),
        pl.BlockSpec((2, 1, 256), lambda b: (0, 0, 0)),
        pl.BlockSpec((1, GLA_DV), lambda b: (0, 0)),
        pl.BlockSpec(mstack.shape, lambda b: (0, 0, 0)),
        pl.BlockSpec(lmask.shape, lambda b: (0, 0, 0, 0)),
    ]
    args = [p_main, p_small, w2p, gbias, ng, mstack, lmask]
    if has_state:
        in_specs.append(pl.BlockSpec((None, 2, 2, GLA_DV, LANES), lambda b: (b, 0, 0, 0, 0)))
        args.append(s0)
    return pl.pallas_call(
        functools.partial(_gla_kernel, L, has_state),
        grid=(n_seq,),
        in_specs=in_specs,
        out_specs=[
            pl.BlockSpec((L, 512), lambda b: (b, 0)),
            pl.BlockSpec((None, 2, 2, GLA_DV, LANES), lambda b: (b, 0, 0, 0, 0)),
        ],
        out_shape=[
            jax.ShapeDtypeStruct((n_seq * L, 512), BF16),
            jax.ShapeDtypeStruct((n_seq, 2, 2, GLA_DV, LANES), F32),
        ],
        scratch_shapes=[
            pltpu.VMEM((2, L, 256), F32),
            pltpu.VMEM((L, 512), F32),
            pltpu.VMEM((L, 512), F32),
            pltpu.VMEM((2, 2, GLA_DV, LANES), F32),
        ],
        compiler_params=_cparams(("arbitrary",)),
        name="gla_lat" if has_state else "gla_ctx",
    )(*args)


def _gla_state_to_kernel(s):
    B = s.shape[0]
    s = s.reshape(B, 2, 2, 2, GLA_DK, GLA_DV)
    return jnp.transpose(s, (0, 1, 2, 5, 3, 4)).reshape(B, 2, 2, GLA_DV, 2 * GLA_DK)


def _gla_state_from_kernel(s):
    B = s.shape[0]
    s = s.reshape(B, 2, 2, GLA_DV, 2, GLA_DK)
    return jnp.transpose(s, (0, 1, 2, 4, 5, 3)).reshape(B, 2, GLA_H, GLA_DK, GLA_DV)


def _dn_kernel(L, has_state, *refs):
    if has_state:
        (pd_ref, dz_ref, ps_ref, cw_ref, al_ref, dtb_ref, ng_ref, s0_ref,
         o_ref, so_ref, qkv_sc, gb_sc, bb_sc, of_sc, ob_sc, s_sc) = refs
    else:
        (pd_ref, dz_ref, ps_ref, cw_ref, al_ref, dtb_ref, ng_ref,
         o_ref, so_ref, qkv_sc, gb_sc, bb_sc, of_sc, ob_sc, s_sc) = refs
    n = L // CHUNK
    C = CHUNK

    rowi = lax.broadcasted_iota(jnp.int32, (L, LANES), 0)
    not_first = rowi > 0
    not_last = rowi < L - 1

    def conv_tile(j, carry):
        c0 = pl.multiple_of(j * LANES, LANES)
        x = pd_ref[:, pl.ds(c0, LANES)]
        w = cw_ref[:, pl.ds(c0, LANES)]
        prev = jnp.where(not_first, pltpu.roll(x, 1, 0), 0.0)
        nxt = jnp.where(not_last, pltpu.roll(x, L - 1, 0), 0.0)
        y = _silu(prev * w[0:1] + x * w[1:2] + nxt * w[2:3])
        inv = lax.rsqrt(jnp.sum(y * y, axis=-1, keepdims=True) + EPS)
        scale = jnp.where(j < DN_H, inv * (DN_DK ** -0.5), jnp.where(j < 2 * DN_H, inv, 1.0))
        qkv_sc[:, pl.ds(c0, LANES)] = y * scale
        return carry

    lax.fori_loop(0, 3 * DN_H, conv_tile, 0)

    ps = ps_ref[...]
    lane_l = lax.broadcasted_iota(jnp.int32, (L, LANES), 1)
    g_all = -jnp.exp(al_ref[...]) * _softplus(ps + dtb_ref[...])
    b_all = jax.nn.sigmoid(ps)
    for idx in range(2 * DN_H):
        gcol = jnp.sum(jnp.where(lane_l == S_DA + idx, g_all, 0.0), axis=-1, keepdims=True)
        bcol = jnp.sum(jnp.where(lane_l == S_DB + idx, b_all, 0.0), axis=-1, keepdims=True)
        gb_sc[idx] = jnp.broadcast_to(gcol, (L, LANES))
        bb_sc[idx] = jnp.broadcast_to(bcol, (L, LANES))

    if has_state:
        s_sc[...] = s0_ref[...]
    else:
        s_sc[...] = jnp.zeros(s_sc.shape, F32)

    ri = lax.broadcasted_iota(jnp.int32, (C, C), 0)
    ci = lax.broadcasted_iota(jnp.int32, (C, C), 1)
    eye = (ri == ci).astype(F32)
    ones_b = jnp.ones((C, C), BF16)

    def body(c, carry):
        for z in range(2):
            cc = c if z == 0 else n - 1 - c
            r0 = pl.multiple_of(cc * C, C)
            rows = pl.ds(r0, C)
            last = C - 1 if z == 0 else 0
            osc = of_sc if z == 0 else ob_sc
            incl = (ci <= ri) if z == 0 else (ci >= ri)
            strict = (ci < ri) if z == 0 else (ci > ri)
            m_b = incl.astype(BF16)
            mt_f = (ci >= ri) if z == 0 else (ci <= ri)
            for h in range(DN_H):
                cols = slice(h * LANES, (h + 1) * LANES)
                q = qkv_sc[rows, cols]
                k = qkv_sc[rows, 512 + h * LANES:512 + (h + 1) * LANES]
                v = qkv_sc[rows, 1024 + h * LANES:1024 + (h + 1) * LANES]
                gb = gb_sc[z * DN_H + h, rows, :]
                bb = bb_sc[z * DN_H + h, rows, :]
                gi = _dot_exact_lhs(m_b, gb)
                gj = _dot_exact_lhs(ones_b, jnp.where(mt_f, gb[:, 0:C], 0.0))
                gam = jnp.where(incl, jnp.exp(jnp.where(incl, gi[:, 0:C] - gj, 0.0)), 0.0)
                kb = k * bb
                a = jnp.where(strict, _bdot_nt(kb, k) * gam, 0.0)
                t = eye - a
                pw = a
                for _ in range(5):
                    pw = _dot3(pw, pw)
                    t = t + _dot3(t, pw)
                eg = jnp.exp(gi)
                u = _dot3(t, v * bb)
                w = _dot3(t, kb * eg)
                s = s_sc[z, h]
                v_new = u - _bdot(w, s)
                att = _bdot_nt(q, k) * gam
                o = _bdot(q * eg, s) + _bdot(att, v_new)
                osc[rows, cols] = o
                gt = gi[last:last + 1, :]
                s_sc[z, h] = s * jnp.exp(gt) + _bdot_tn(k * jnp.exp(gt - gi), v_new)
        return carry

    lax.fori_loop(0, n, body, 0)

    ng = ng_ref[...]

    def fin(c, carry):
        r0 = pl.multiple_of(c * C, C)
        rows = pl.ds(r0, C)
        for h in range(DN_H):
            cols = slice(h * LANES, (h + 1) * LANES)
            o = of_sc[rows, cols] + ob_sc[rows, cols]
            o_ref[rows, cols] = (_rms(o, ng) * _silu(dz_ref[rows, cols])).astype(BF16)
        return carry

    lax.fori_loop(0, n, fin, 0)
    so_ref[...] = s_sc[...]


def _dn(p_main, p_small, conv_w, alog_row, dtb_row, ng, s0, L, n_seq, row_blk0):
    has_state = s0 is not None
    in_specs = [
        pl.BlockSpec((L, 1536), lambda b: (row_blk0 + b, P_DN // 1536)),
        pl.BlockSpec((L, 512), lambda b: (row_blk0 + b, P_DZ // 512)),
        pl.BlockSpec((L, LANES), lambda b: (row_blk0 + b, 0)),
        pl.BlockSpec((DN_CONV, 3 * DN_H * DN_DK), lambda b: (0, 0)),
        pl.BlockSpec((1, LANES), lambda b: (0, 0)),
        pl.BlockSpec((1, LANES), lambda b: (0, 0)),
        pl.BlockSpec((1, DN_DV), lambda b: (0, 0)),
    ]
    args = [p_main, p_main, p_small, conv_w, alog_row, dtb_row, ng]
    if has_state:
        in_specs.append(pl.BlockSpec((None, 2, DN_H, DN_DK, DN_DV), lambda b: (b, 0, 0, 0, 0)))
        args.append(s0)
    return pl.pallas_call(
        functools.partial(_dn_kernel, L, has_state),
        grid=(n_seq,),
        in_specs=in_specs,
        out_specs=[
            pl.BlockSpec((L, 512), lambda b: (b, 0)),
            pl.BlockSpec((None, 2, DN_H, DN_DK, DN_DV), lambda b: (b, 0, 0, 0, 0)),
        ],
        out_shape=[
            jax.ShapeDtypeStruct((n_seq * L, 512), BF16),
            jax.ShapeDtypeStruct((n_seq, 2, DN_H, DN_DK, DN_DV), F32),
        ],
        scratch_shapes=[
            pltpu.VMEM((L, 1536), F32),
            pltpu.VMEM((2 * DN_H, L, LANES), F32),
            pltpu.VMEM((2 * DN_H, L, LANES), F32),
            pltpu.VMEM((L, 512), F32),
            pltpu.VMEM((L, 512), F32),
            pltpu.VMEM((2, DN_H, DN_DK, DN_DV), F32),
        ],
        compiler_params=_cparams(("arbitrary",)),
        name="dn_lat" if has_state else "dn_ctx",
    )(*args)


def _prep_w_in(w_in_l):
    off = np.cumsum((0, 256, 256, 512, 512, 32, 1024, 256, 256, 1536, 512, 8, 8))
    gq, gk, gv, gr, glr, aq, ak, av, dqkv, dz, da, db = [
        w_in_l[:, int(off[i]):int(off[i + 1])] for i in range(12)]
    main = jnp.concatenate([aq, ak, av, gq, gk, gv, gr, dqkv, dz], axis=1).astype(BF16)
    pad = jnp.zeros((D_MODEL, LANES - 48), w_in_l.dtype)
    small = jnp.concatenate([glr, da, db, pad], axis=1).astype(BF16)
    return main, small


def _lane_row(vals, offset):
    return jnp.zeros((1, LANES), F32).at[0, offset:offset + vals.shape[0]].set(vals.astype(F32))


def kernel(x_prompt, x_sample, cache_k, cache_v, state_gla, state_dn, c, c_ctx, norm1_g, norm2_g,
           w_mod, b_mod, w_in, gla_w2, gla_b, gla_norm_g, q_norm_g, k_norm_g, dn_conv, dn_a_log,
           dn_dt_bias, dn_norm_g, w_out, w_ff1, w_ff2):
    n_ctx, n_lat = x_prompt.shape[0], x_sample.shape[0]
    depth = w_in.shape[0]
    n_ctx_rows = n_ctx * SEQ
    assert n_ctx_rows % DEC_SEQ == 0 and n_lat <= 7
    lat_blk0 = n_ctx_rows // DEC_SEQ

    x = jnp.concatenate([x_prompt.reshape(n_ctx_rows, D_MODEL),
                         x_sample.reshape(n_lat * DEC_SEQ, D_MODEL)], axis=0)
    cond8 = jnp.concatenate([c_ctx[None, :], c, jnp.zeros((7 - n_lat, D_MODEL), F32)], axis=0)
    mods = _modulation(cond8, w_mod, b_mod).reshape(depth, 8, N_MOD, D_MODEL)

    cos, sin = _rope_tables()
    gla_consts = _gla_consts()
    ck = cache_k.reshape(n_lat, depth, PAST_LEN, ATT_HKV * HEAD_DIM)
    cv = cache_v.reshape(n_lat, depth, PAST_LEN, ATT_HKV * HEAD_DIM)

    nks, nvs, sgs, sds = [], [], [], []
    for l in range(depth):
        w_main, w_small = _prep_w_in(w_in[l])
        p_main, p_small = _inproj(x, mods[l], norm1_g[l][None, :], w_main, w_small, n_ctx_rows)

        qg, kg = q_norm_g[l][None, :], k_norm_g[l][None, :]
        oa_c, nk, nv = _att_ctx(p_main, qg, kg, n_ctx)
        oa_l = _att_lat(p_main, qg, kg, cos, sin, ck[:, l], cv[:, l], n_lat, lat_blk0)

        w2p = jnp.zeros((2, LANES, GLA_H * GLA_DK), F32)
        w2p = w2p.at[0, 0:GLA_LR].set(gla_w2[l, 0]).at[1, GLA_LR:2 * GLA_LR].set(gla_w2[l, 1])
        gbias = gla_b[l][:, None, :]
        gng = gla_norm_g[l][None, :]
        og_c, sg = _gla(p_main, p_small, w2p, gbias, gng, gla_consts, None, SEQ, n_ctx, 0)
        og_l, _ = _gla(p_main, p_small, w2p, gbias, gng, gla_consts,
                       _gla_state_to_kernel(state_gla[:, l]), DEC_SEQ, n_lat, lat_blk0)

        alog_row = _lane_row(dn_a_log[l].reshape(-1), S_DA)
        dtb_row = _lane_row(dn_dt_bias[l].reshape(-1), S_DA)
        dng = dn_norm_g[l][None, :]
        od_c, sd = _dn(p_main, p_small, dn_conv[l], alog_row, dtb_row, dng, None, SEQ, n_ctx, 0)
        od_l, _ = _dn(p_main, p_small, dn_conv[l], alog_row, dtb_row, dng, state_dn[:, l],
                      DEC_SEQ, n_lat, lat_blk0)

        o_gla = jnp.concatenate([og_c, og_l], axis=0)
        o_att = jnp.concatenate([oa_c, oa_l], axis=0)
        o_dn = jnp.concatenate([od_c, od_l], axis=0)
        x1, h2 = _outproj(x, o_gla, o_att, o_dn, mods[l], norm2_g[l][None, :],
                          w_out[l].astype(BF16), n_ctx_rows)
        x = _ffn(h2, x1, mods[l], w_ff1[l].astype(BF16), w_ff2[l].astype(BF16), n_ctx_rows)

        nks.append(nk.reshape(n_ctx, SEQ, ATT_HKV, HEAD_DIM))
        nvs.append(nv.reshape(n_ctx, SEQ, ATT_HKV, HEAD_DIM))
        sgs.append(_gla_state_from_kernel(sg))
        sds.append(sd)

    y_prompt = x[:n_ctx_rows].reshape(n_ctx, SEQ, D_MODEL)
    y_sample = x[n_ctx_rows:].reshape(n_lat, DEC_SEQ, D_MODEL)
    return (y_prompt, y_sample, jnp.stack(nks, axis=1), jnp.stack(nvs, axis=1),
            jnp.stack(sgs, axis=1), jnp.stack(sds, axis=1))
```

```python
import functools
import math

import numpy as np
import jax
import jax.numpy as jnp
from jax import lax
from jax.experimental import pallas as pl
from jax.experimental.pallas import tpu as pltpu

F32 = jnp.float32
BF16 = jnp.bfloat16

D_MODEL = 2048
SEQ = 256
DEC_SEQ = 1024
PAST_LEN = 256
GRID_W = 64
HEAD_DIM = 128
EPS = 1e-6
GLA_H = 4
GLA_DK = 64
GLA_DV = 128
GLA_LR = 16
GLA_GATE_NORM = 16.0
ATT_HQ = 8
ATT_HKV = 2
ROPE_THETA = 10000.0
DN_H = 4
DN_DK = 128
DN_DV = 128
DN_CONV = 3
CHUNK = 64
FF = 4 * D_MODEL
N_MOD = 6
LANES = 128

P_AQ = 0
P_AKV = 1024
P_GLA = 1536
P_DN = 3072
P_DZ = 4608
P_MAIN = 5120
S_DA = 32
S_DB = 40

VMEM_LIMIT = 56 * 1024 * 1024


def _cparams(sem):
    return pltpu.CompilerParams(dimension_semantics=sem, vmem_limit_bytes=VMEM_LIMIT)


def _bdot(a, b):
    return jnp.dot(a.astype(BF16), b.astype(BF16), preferred_element_type=F32)


def _bdot_nt(a, b):
    return lax.dot_general(a.astype(BF16), b.astype(BF16), (((1,), (1,)), ((), ())),
                           preferred_element_type=F32)


def _bdot_tn(a, b):
    return lax.dot_general(a.astype(BF16), b.astype(BF16), (((0,), (0,)), ((), ())),
                           preferred_element_type=F32)


def _split3(x):
    hi = x.astype(BF16)
    r = x - hi.astype(F32)
    mid = r.astype(BF16)
    lo = (r - mid.astype(F32)).astype(BF16)
    return hi, mid, lo


def _dot_exact_lhs(lhs_bf16, x):
    hi, mid, lo = _split3(x)
    d = functools.partial(jnp.dot, preferred_element_type=F32)
    return d(lhs_bf16, hi) + d(lhs_bf16, mid) + d(lhs_bf16, lo)


def _dot3(a, b):
    ah, am, _ = _split3(a)
    bh, bm, _ = _split3(b)
    d = functools.partial(jnp.dot, preferred_element_type=F32)
    return d(ah, bh) + (d(ah, bm) + d(am, bh))


def _silu(x):
    return x * jax.nn.sigmoid(x)


def _softplus(x):
    return jnp.maximum(x, 0.0) + jnp.log1p(jnp.exp(-jnp.abs(x)))


def _rms(x, g):
    return x * lax.rsqrt(jnp.mean(x * x, axis=-1, keepdims=True) + EPS) * g


def _mods_kernel(c_ref, w_ref, b_ref, o_ref):
    o_ref[...] = _bdot(_silu(c_ref[...]), w_ref[...]) + b_ref[...]


def _modulation(cond8, w_mod, b_mod):
    depth = w_mod.shape[0]
    n_out = N_MOD * D_MODEL
    tn = 1024
    return pl.pallas_call(
        _mods_kernel,
        grid=(depth, n_out // tn),
        in_specs=[
            pl.BlockSpec((8, D_MODEL), lambda l, j: (0, 0)),
            pl.BlockSpec((None, D_MODEL, tn), lambda l, j: (l, 0, j)),
            pl.BlockSpec((None, 1, tn), lambda l, j: (l, 0, j)),
        ],
        out_specs=pl.BlockSpec((None, 8, tn), lambda l, j: (l, 0, j)),
        out_shape=jax.ShapeDtypeStruct((depth, 8, n_out), F32),
        compiler_params=_cparams(("arbitrary", "arbitrary")),
        name="modulation",
    )(cond8, w_mod, b_mod.reshape(depth, 1, n_out))


def _group_of_tile(i, tm, n_ctx_rows):
    n_ctx_tiles = n_ctx_rows // tm
    per_lat = DEC_SEQ // tm
    return jnp.where(i < n_ctx_tiles, 0, 1 + (i - n_ctx_tiles) // per_lat)


def _inproj_kernel(x_ref, mod_ref, g_ref, w_ref, ws_ref, p_ref, ps_ref, h_sc):
    @pl.when(pl.program_id(1) == 0)
    def _():
        m = mod_ref[...]
        h = _rms(x_ref[...], g_ref[...]) * (1.0 + m[1:2]) + m[0:1]
        hb = h.astype(BF16)
        h_sc[...] = hb
        ps_ref[...] = jnp.dot(hb, ws_ref[...], preferred_element_type=F32)

    p_ref[...] = jnp.dot(h_sc[...], w_ref[...], preferred_element_type=F32)


def _inproj(x, mods_l, g1, w_main, w_small, n_ctx_rows, tm=512, tn=1024):
    rows = x.shape[0]
    grp = functools.partial(_group_of_tile, tm=tm, n_ctx_rows=n_ctx_rows)
    return pl.pallas_call(
        _inproj_kernel,
        grid=(rows // tm, P_MAIN // tn),
        in_specs=[
            pl.BlockSpec((tm, D_MODEL), lambda i, j: (i, 0)),
            pl.BlockSpec((None, N_MOD, D_MODEL), lambda i, j: (grp(i), 0, 0)),
            pl.BlockSpec((1, D_MODEL), lambda i, j: (0, 0)),
            pl.BlockSpec((D_MODEL, tn), lambda i, j: (0, j)),
            pl.BlockSpec((D_MODEL, LANES), lambda i, j: (0, 0)),
        ],
        out_specs=[
            pl.BlockSpec((tm, tn), lambda i, j: (i, j)),
            pl.BlockSpec((tm, LANES), lambda i, j: (i, 0)),
        ],
        out_shape=[
            jax.ShapeDtypeStruct((rows, P_MAIN), F32),
            jax.ShapeDtypeStruct((rows, LANES), F32),
        ],
        scratch_shapes=[pltpu.VMEM((tm, D_MODEL), BF16)],
        compiler_params=_cparams(("arbitrary", "arbitrary")),
        name="inproj",
    )(x, mods_l, g1, w_main, w_small)


def _outproj_kernel(x_ref, og_ref, oa_ref, od_ref, mod_ref, g_ref, w_ref, x1_ref, h2_ref):
    m = mod_ref[...]
    d = functools.partial(jnp.dot, preferred_element_type=F32)
    mix = (d(og_ref[...], w_ref[0:512, :]) + d(oa_ref[...], w_ref[512:1536, :])
           + d(od_ref[...], w_ref[1536:2048, :]))
    x1 = x_ref[...] + m[2:3] * mix
    x1_ref[...] = x1
    h2_ref[...] = (_rms(x1, g_ref[...]) * (1.0 + m[4:5]) + m[3:4]).astype(BF16)


def _outproj(x, o_gla, o_att, o_dn, mods_l, g2, w_out, n_ctx_rows, tm=512):
    rows = x.shape[0]
    grp = functools.partial(_group_of_tile, tm=tm, n_ctx_rows=n_ctx_rows)
    return pl.pallas_call(
        _outproj_kernel,
        grid=(rows // tm,),
        in_specs=[
            pl.BlockSpec((tm, D_MODEL), lambda i: (i, 0)),
            pl.BlockSpec((tm, 512), lambda i: (i, 0)),
            pl.BlockSpec((tm, 1024), lambda i: (i, 0)),
            pl.BlockSpec((tm, 512), lambda i: (i, 0)),
            pl.BlockSpec((None, N_MOD, D_MODEL), lambda i: (grp(i), 0, 0)),
            pl.BlockSpec((1, D_MODEL), lambda i: (0, 0)),
            pl.BlockSpec((D_MODEL, D_MODEL), lambda i: (0, 0)),
        ],
        out_specs=[
            pl.BlockSpec((tm, D_MODEL), lambda i: (i, 0)),
            pl.BlockSpec((tm, D_MODEL), lambda i: (i, 0)),
        ],
        out_shape=[
            jax.ShapeDtypeStruct((rows, D_MODEL), F32),
            jax.ShapeDtypeStruct((rows, D_MODEL), BF16),
        ],
        compiler_params=_cparams(("arbitrary",)),
        name="outproj",
    )(x, o_gla, o_att, o_dn, mods_l, g2, w_out)


def _ffn_kernel(h_ref, x1_ref, mod_ref, w1_ref, w2_ref, y_ref):
    j = pl.program_id(1)
    t = jnp.dot(h_ref[...], w1_ref[...], preferred_element_type=F32)
    t = jnp.square(jnp.maximum(t, 0.0))
    part = jnp.dot(t.astype(BF16), w2_ref[...], preferred_element_type=F32)

    @pl.when(j == 0)
    def _():
        y_ref[...] = part

    @pl.when(j > 0)
    def _():
        y_ref[...] += part

    @pl.when(j == pl.num_programs(1) - 1)
    def _():
        y_ref[...] = x1_ref[...] + mod_ref[5:6, :] * y_ref[...]


def _ffn(h2, x1, mods_l, w1, w2, n_ctx_rows, tm=512, tf=1024):
    rows = x1.shape[0]
    grp = functools.partial(_group_of_tile, tm=tm, n_ctx_rows=n_ctx_rows)
    return pl.pallas_call(
        _ffn_kernel,
        grid=(rows // tm, FF // tf),
        in_specs=[
            pl.BlockSpec((tm, D_MODEL), lambda i, j: (i, 0)),
            pl.BlockSpec((tm, D_MODEL), lambda i, j: (i, 0)),
            pl.BlockSpec((None, N_MOD, D_MODEL), lambda i, j: (grp(i), 0, 0)),
            pl.BlockSpec((D_MODEL, tf), lambda i, j: (0, j)),
            pl.BlockSpec((tf, D_MODEL), lambda i, j: (j, 0)),
        ],
        out_specs=pl.BlockSpec((tm, D_MODEL), lambda i, j: (i, 0)),
        out_shape=jax.ShapeDtypeStruct((rows, D_MODEL), F32),
        compiler_params=_cparams(("arbitrary", "arbitrary")),
        name="ffn",
    )(h2, x1, mods_l, w1, w2)


def _rope_tables():
    half = HEAD_DIM // 2
    pos = np.arange(DEC_SEQ)
    row = (pos // GRID_W).astype(np.float32)
    col = (pos % GRID_W).astype(np.float32)
    inv = (ROPE_THETA ** (-np.arange(0, half, 2, dtype=np.float32) / half)).astype(np.float32)
    ar = row[:, None] * inv[None, :]
    ac = col[:, None] * inv[None, :]
    cos = np.concatenate([np.cos(ar), np.cos(ar), np.cos(ac), np.cos(ac)], axis=1)
    sin = np.concatenate([-np.sin(ar), np.sin(ar), -np.sin(ac), np.sin(ac)], axis=1)
    return jnp.asarray(cos, F32), jnp.asarray(sin, F32)


def _rope(x, cos, sin):
    lane = lax.broadcasted_iota(jnp.int32, x.shape, 1)
    first = (lane % 64) < 32
    partner = jnp.where(first, pltpu.roll(x, 96, 1), pltpu.roll(x, 32, 1))
    return x * cos + partner * sin


def _softmax_pv(q, kb, vb):
    s = lax.dot_general(q.astype(BF16), kb, (((1,), (1,)), ((), ())),
                        preferred_element_type=F32) * (HEAD_DIM ** -0.5)
    m = jnp.max(s, axis=-1, keepdims=True)
    p = jnp.exp(s - m)
    l = jnp.sum(p, axis=-1, keepdims=True)
    return jnp.dot(p.astype(BF16), vb, preferred_element_type=F32) / l


def _att_ctx_kernel(aq_ref, akv_ref, qg_ref, kg_ref, o_ref, nk_ref, nv_ref):
    qg = qg_ref[...]
    kg = kg_ref[...]
    for hk in range(ATT_HKV):
        ks = slice(hk * HEAD_DIM, (hk + 1) * HEAD_DIM)
        kn = _rms(akv_ref[:, ks], kg)
        v = akv_ref[:, 256 + hk * HEAD_DIM:256 + (hk + 1) * HEAD_DIM]
        nk_ref[:, ks] = kn
        nv_ref[:, ks] = v
        kb = kn.astype(BF16)
        vb = v.astype(BF16)
        for g in range(ATT_HQ // ATT_HKV):
            qs = slice((hk * 4 + g) * HEAD_DIM, (hk * 4 + g + 1) * HEAD_DIM)
            qn = _rms(aq_ref[:, qs], qg)
            o_ref[:, qs] = _softmax_pv(qn, kb, vb).astype(BF16)


def _att_lat_kernel(aq_ref, akv_ref, qg_ref, kg_ref, cos_ref, sin_ref, ck_ref, cv_ref, o_ref,
                    k_sc, v_sc):
    qg = qg_ref[...]
    kg = kg_ref[...]
    qb = 256
    for hk in range(ATT_HKV):
        ks = slice(hk * HEAD_DIM, (hk + 1) * HEAD_DIM)
        k_sc[hk, 0:PAST_LEN, :] = ck_ref[:, ks].astype(BF16)
        v_sc[hk, 0:PAST_LEN, :] = cv_ref[:, ks].astype(BF16)
        kn = _rope(_rms(akv_ref[:, ks], kg), cos_ref[...], sin_ref[...])
        k_sc[hk, PAST_LEN:, :] = kn.astype(BF16)
        v_sc[hk, PAST_LEN:, :] = akv_ref[:, 256 + hk * HEAD_DIM:256 + (hk + 1) * HEAD_DIM].astype(BF16)

    def body(r, carry):
        r0 = pl.multiple_of(r * qb, qb)
        cos = cos_ref[pl.ds(r0, qb), :]
        sin = sin_ref[pl.ds(r0, qb), :]
        for hk in range(ATT_HKV):
            kb = k_sc[hk]
            vb = v_sc[hk]
            for g in range(ATT_HQ // ATT_HKV):
                qs = slice((hk * 4 + g) * HEAD_DIM, (hk * 4 + g + 1) * HEAD_DIM)
                qn = _rope(_rms(aq_ref[pl.ds(r0, qb), qs], qg), cos, sin)
                o_ref[pl.ds(r0, qb), qs] = _softmax_pv(qn, kb, vb).astype(BF16)
        return carry

    lax.fori_loop(0, DEC_SEQ // qb, body, 0)


def _att_ctx(p_main, qg, kg, n_seq):
    L = SEQ
    return pl.pallas_call(
        _att_ctx_kernel,
        grid=(n_seq,),
        in_specs=[
            pl.BlockSpec((L, 1024), lambda b: (b, P_AQ // 1024)),
            pl.BlockSpec((L, 512), lambda b: (b, P_AKV // 512)),
            pl.BlockSpec((1, HEAD_DIM), lambda b: (0, 0)),
            pl.BlockSpec((1, HEAD_DIM), lambda b: (0, 0)),
        ],
        out_specs=[
            pl.BlockSpec((L, 1024), lambda b: (b, 0)),
            pl.BlockSpec((None, L, 256), lambda b: (b, 0, 0)),
            pl.BlockSpec((None, L, 256), lambda b: (b, 0, 0)),
        ],
        out_shape=[
            jax.ShapeDtypeStruct((n_seq * L, 1024), BF16),
            jax.ShapeDtypeStruct((n_seq, L, 256), F32),
            jax.ShapeDtypeStruct((n_seq, L, 256), F32),
        ],
        compiler_params=_cparams(("arbitrary",)),
        name="att_ctx",
    )(p_main, p_main, qg, kg)


def _att_lat(p_main, qg, kg, cos, sin, ck, cv, n_seq, row_blk0):
    L = DEC_SEQ
    return pl.pallas_call(
        _att_lat_kernel,
        grid=(n_seq,),
        in_specs=[
            pl.BlockSpec((L, 1024), lambda b: (row_blk0 + b, P_AQ // 1024)),
            pl.BlockSpec((L, 512), lambda b: (row_blk0 + b, P_AKV // 512)),
            pl.BlockSpec((1, HEAD_DIM), lambda b: (0, 0)),
            pl.BlockSpec((1, HEAD_DIM), lambda b: (0, 0)),
            pl.BlockSpec((L, HEAD_DIM), lambda b: (0, 0)),
            pl.BlockSpec((L, HEAD_DIM), lambda b: (0, 0)),
            pl.BlockSpec((None, PAST_LEN, 256), lambda b: (b, 0, 0)),
            pl.BlockSpec((None, PAST_LEN, 256), lambda b: (b, 0, 0)),
        ],
        out_specs=pl.BlockSpec((L, 1024), lambda b: (b, 0)),
        out_shape=jax.ShapeDtypeStruct((n_seq * L, 1024), BF16),
        scratch_shapes=[
            pltpu.VMEM((ATT_HKV, PAST_LEN + L, HEAD_DIM), BF16),
            pltpu.VMEM((ATT_HKV, PAST_LEN + L, HEAD_DIM), BF16),
        ],
        compiler_params=_cparams(("arbitrary",)),
        name="att_lat",
    )(p_main, p_main, qg, kg, cos, sin, ck, cv)


_GLA_LEVELS = (32, 16, 8, 4, 2, 1)


def _gla_consts():
    C = CHUNK
    i = np.arange(C)[:, None]
    t = np.arange(C)[None, :]
    mats = [t <= i, t > i]
    masks = [i == t]
    for s in _GLA_LEVELS:
        m = (i // s) * s
        if s > 1:
            mats.append((t > m) & (t <= i))
        mats.append((t > i) & (t <= m + s))
        masks.append((i // (2 * s) == t // (2 * s)) & ((i // s) % 2 == 1) & ((t // s) % 2 == 0))
    fwd = np.concatenate(mats, 0)
    bwd = np.concatenate([mm[::-1, ::-1] for mm in mats], 0)
    mstack = np.stack([fwd, bwd]).astype(np.float32)
    mf = np.stack([np.concatenate([mm, mm], 0) for mm in masks])
    mb = np.stack([np.concatenate([mm[::-1, ::-1], mm[::-1, ::-1]], 0) for mm in masks])
    lmask = np.stack([mf, mb]).astype(np.float32)
    return jnp.asarray(mstack, BF16), jnp.asarray(lmask, F32)


def _gla_kernel(L, has_state, *refs):
    if has_state:
        (pg_ref, ps_ref, w2_ref, gb_ref, ng_ref, ms_ref, lm_ref, s0_ref,
         o_ref, so_ref, la_sc, of_sc, ob_sc, st_sc) = refs
    else:
        (pg_ref, ps_ref, w2_ref, gb_ref, ng_ref, ms_ref, lm_ref,
         o_ref, so_ref, la_sc, of_sc, ob_sc, st_sc) = refs
    n = L // CHUNK
    C = CHUNK

    ps = ps_ref[...]
    for z in range(2):
        x = _dot3(ps, w2_ref[z]) + gb_ref[z]
        la_sc[z] = (jnp.minimum(x, 0.0) - jnp.log1p(jnp.exp(-jnp.abs(x)))) * (1.0 / GLA_GATE_NORM)

    if has_state:
        st_sc[...] = s0_ref[...]
    else:
        st_sc[...] = jnp.zeros(st_sc.shape, F32)

    lane = lax.broadcasted_iota(jnp.int32, (C, LANES), 1)
    first = lane < 64

    def stack_heads(t):
        return jnp.concatenate([jnp.where(first, t, 0.0), jnp.where(first, 0.0, t)], axis=0)

    chains = [(z, p) for z in range(2) for p in range(2)]
    nch = len(chains)

    def body(c, carry):
        rows = [pl.ds(pl.multiple_of(c * C, C), C), pl.ds(pl.multiple_of((n - 1 - c) * C, C), C)]
        last = [C - 1, 0]
        q = [pg_ref[rows[z], p * LANES:(p + 1) * LANES] * (GLA_DK ** -0.5) for z, p in chains]
        k = [pg_ref[rows[z], 256 + p * LANES:256 + (p + 1) * LANES] for z, p in chains]
        e = [jnp.exp(_dot_exact_lhs(ms_ref[z], la_sc[z, rows[z], p * LANES:(p + 1) * LANES]))
             for z, p in chains]
        st = [st_sc[z, p] for z, p in chains]
        a2 = [_bdot_nt(stack_heads(q[i]), k[i]) * lm_ref[z, 0] for i, (z, p) in enumerate(chains)]
        blk = 2
        for li, s in enumerate(_GLA_LEVELS):
            if s > 1:
                qs = [q[i] * e[i][blk * C:(blk + 1) * C] for i in range(nch)]
                blk += 1
            else:
                qs = q
            ks = [k[i] * e[i][blk * C:(blk + 1) * C] for i in range(nch)]
            blk += 1
            a2 = [a2[i] + _bdot_nt(stack_heads(qs[i]), ks[i]) * lm_ref[z, li + 1]
                  for i, (z, p) in enumerate(chains)]
        inter = [_bdot_nt(stack_heads(q[i] * e[i][0:C]), st[i]) for i in range(nch)]
        kl = [k[i] * e[i][C:2 * C] for i in range(nch)]
        for i, (z, p) in enumerate(chains):
            osc = of_sc if z == 0 else ob_sc
            zs = []
            for hh in range(2):
                h = 2 * p + hh
                v = pg_ref[rows[z], 512 + h * LANES:512 + (h + 1) * LANES]
                o = _bdot(a2[i][hh * C:(hh + 1) * C], v) + inter[i][hh * C:(hh + 1) * C]
                osc[rows[z], h * LANES:(h + 1) * LANES] = o
                zs.append(_bdot_tn(v, kl[i]))
            st_sc[z, p] = (st[i] * e[i][last[z]:last[z] + 1, :]
                           + jnp.where(first[0:1], zs[0], zs[1]))
        return carry

    lax.fori_loop(0, n, body, 0)

    ng = ng_ref[...]

    def fin(c, carry):
        r0 = pl.multiple_of(c * C, C)
        rows = pl.ds(r0, C)
        for h in range(GLA_H):
            cols = slice(h * LANES, (h + 1) * LANES)
            o = of_sc[rows, cols] + ob_sc[rows, cols]
            gate = _silu(pg_ref[rows, 1024 + h * LANES:1024 + (h + 1) * LANES])
            o_ref[rows, cols] = (_rms(o, ng) * gate).astype(BF16)
        return carry

    lax.fori_loop(0, n, fin, 0)
    so_ref[...] = st_sc[...]


def _gla(p_main, p_small, w2p, gbias, ng, consts, s0, L, n_seq, row_blk0):
    mstack, lmask = consts
    has_state = s0 is not None
    in_specs = [
        pl.BlockSpec((L, 1536), lambda b: (row_blk0 + b, P_GLA // 1536)),
        pl.BlockSpec((L, LANES), lambda b: (row_blk0 + b, 0)),
        pl.BlockSpec((2, LANES, 256), lambda b: (0, 0, 0)),
        pl.BlockSpec((2, 1, 256), lambda b: (0, 0, 0)),
        pl.BlockSpec((1, GLA_DV), lambda b: (0, 0)),
        pl.BlockSpec(mstack.shape, lambda b: (0, 0, 0)),
        pl.BlockSpec(lmask.shape, lambda b: (0, 0, 0, 0)),
    ]
    args = [p_main, p_small, w2p, gbias, ng, mstack, lmask]
    if has_state:
        in_specs.append(pl.BlockSpec((None, 2, 2, GLA_DV, LANES), lambda b: (b, 0, 0, 0, 0)))
        args.append(s0)
    return pl.pallas_call(
        functools.partial(_gla_kernel, L, has_state),
        grid=(n_seq,),
        in_specs=in_specs,
        out_specs=[
            pl.BlockSpec((L, 512), lambda b: (b, 0)),
            pl.BlockSpec((None, 2, 2, GLA_DV, LANES), lambda b: (b, 0, 0, 0, 0)),
        ],
        out_shape=[
            jax.ShapeDtypeStruct((n_seq * L, 512), BF16),
            jax.ShapeDtypeStruct((n_seq, 2, 2, GLA_DV, LANES), F32),
        ],
        scratch_shapes=[
            pltpu.VMEM((2, L, 256), F32),
            pltpu.VMEM((L, 512), F32),
            pltpu.VMEM((L, 512), F32),
            pltpu.VMEM((2, 2, GLA_DV, LANES), F32),
        ],
        compiler_params=_cparams(("arbitrary",)),
        name="gla_lat" if has_state else "gla_ctx",
    )(*args)


def _gla_state_to_kernel(s):
    B = s.shape[0]
    s = s.reshape(B, 2, 2, 2, GLA_DK, GLA_DV)
    return jnp.transpose(s, (0, 1, 2, 5, 3, 4)).reshape(B, 2, 2, GLA_DV, 2 * GLA_DK)


def _gla_state_from_kernel(s):
    B = s.shape[0]
    s = s.reshape(B, 2, 2, GLA_DV, 2, GLA_DK)
    return jnp.transpose(s, (0, 1, 2, 4, 5, 3)).reshape(B, 2, GLA_H, GLA_DK, GLA_DV)


def _dn_kernel(L, has_state, *refs):
    if has_state:
        (pd_ref, dz_ref, ps_ref, cw_ref, al_ref, dtb_ref, ng_ref, s0_ref,
         o_ref, so_ref, qkv_sc, gb_sc, bb_sc, of_sc, ob_sc, s_sc) = refs
    else:
        (pd_ref, dz_ref, ps_ref, cw_ref, al_ref, dtb_ref, ng_ref,
         o_ref, so_ref, qkv_sc, gb_sc, bb_sc, of_sc, ob_sc, s_sc) = refs
    n = L // CHUNK
    C = CHUNK

    rowi = lax.broadcasted_iota(jnp.int32, (L, LANES), 0)
    not_first = rowi > 0
    not_last = rowi < L - 1

    def conv_tile(j, carry):
        c0 = pl.multiple_of(j * LANES, LANES)
        x = pd_ref[:, pl.ds(c0, LANES)]
        w = cw_ref[:, pl.ds(c0, LANES)]
        prev = jnp.where(not_first, pltpu.roll(x, 1, 0), 0.0)
        nxt = jnp.where(not_last, pltpu.roll(x, L - 1, 0), 0.0)
        y = _silu(prev * w[0:1] + x * w[1:2] + nxt * w[2:3])
        inv = lax.rsqrt(jnp.sum(y * y, axis=-1, keepdims=True) + EPS)
        scale = jnp.where(j < DN_H, inv * (DN_DK ** -0.5), jnp.where(j < 2 * DN_H, inv, 1.0))
        qkv_sc[:, pl.ds(c0, LANES)] = y * scale
        return carry

    lax.fori_loop(0, 3 * DN_H, conv_tile, 0)

    ps = ps_ref[...]
    lane_l = lax.broadcasted_iota(jnp.int32, (L, LANES), 1)
    g_all = -jnp.exp(al_ref[...]) * _softplus(ps + dtb_ref[...])
    b_all = jax.nn.sigmoid(ps)
    for idx in range(2 * DN_H):
        gcol = jnp.sum(jnp.where(lane_l == S_DA + idx, g_all, 0.0), axis=-1, keepdims=True)
        bcol = jnp.sum(jnp.where(lane_l == S_DB + idx, b_all, 0.0), axis=-1, keepdims=True)
        gb_sc[idx] = jnp.broadcast_to(gcol, (L, LANES))
        bb_sc[idx] = jnp.broadcast_to(bcol, (L, LANES))

    if has_state:
        s_sc[...] = s0_ref[...]
    else:
        s_sc[...] = jnp.zeros(s_sc.shape, F32)

    ri = lax.broadcasted_iota(jnp.int32, (C, C), 0)
    ci = lax.broadcasted_iota(jnp.int32, (C, C), 1)
    eye = (ri == ci).astype(F32)
    ones_b = jnp.ones((C, C), BF16)

    incl = [ci <= ri, ci >= ri]
    strict = [ci < ri, ci > ri]
    m_b = [m.astype(BF16) for m in incl]
    chains = [(z, h) for z in range(2) for h in range(DN_H)]
    nch = len(chains)

    def body(c, carry):
        rows = [pl.ds(pl.multiple_of(c * C, C), C), pl.ds(pl.multiple_of((n - 1 - c) * C, C), C)]
        last = [C - 1, 0]

        def ld(base, z, h):
            return qkv_sc[rows[z], base + h * LANES:base + (h + 1) * LANES]

        q = [ld(0, z, h) for z, h in chains]
        k = [ld(512, z, h) for z, h in chains]
        v = [ld(1024, z, h) for z, h in chains]
        gb = [gb_sc[z * DN_H + h, rows[z], :] for z, h in chains]
        bb = [bb_sc[z * DN_H + h, rows[z], :] for z, h in chains]
        gi = [_dot_exact_lhs(m_b[z], gb[i]) for i, (z, h) in enumerate(chains)]
        gj = [_dot_exact_lhs(ones_b, jnp.where(incl[1 - z], gb[i][:, 0:C], 0.0))
              for i, (z, h) in enumerate(chains)]
        gam = [jnp.where(incl[z], jnp.exp(jnp.where(incl[z], gi[i][:, 0:C] - gj[i], 0.0)), 0.0)
               for i, (z, h) in enumerate(chains)]
        kb = [k[i] * bb[i] for i in range(nch)]
        a = [jnp.where(strict[z], _bdot_nt(kb[i], k[i]) * gam[i], 0.0)
             for i, (z, h) in enumerate(chains)]
        t = [eye - a[i] for i in range(nch)]
        pw = a
        for _ in range(5):
            pw = [_dot3(pw[i], pw[i]) for i in range(nch)]
            t = [t[i] + _dot3(t[i], pw[i]) for i in range(nch)]
        eg = [jnp.exp(gi[i]) for i in range(nch)]
        u = [_dot3(t[i], v[i] * bb[i]) for i in range(nch)]
        w = [_dot3(t[i], kb[i] * eg[i]) for i in range(nch)]
        s = [s_sc[z, h] for z, h in chains]
        v_new = [u[i] - _bdot(w[i], s[i]) for i in range(nch)]
        att = [_bdot_nt(q[i], k[i]) * gam[i] for i in range(nch)]
        o = [_bdot(q[i] * eg[i], s[i]) + _bdot(att[i], v_new[i]) for i in range(nch)]
        for i, (z, h) in enumerate(chains):
            osc = of_sc if z == 0 else ob_sc
            osc[rows[z], h * LANES:(h + 1) * LANES] = o[i]
        gt = [gi[i][last[z]:last[z] + 1, :] for i, (z, h) in enumerate(chains)]
        upd = [_bdot_tn(k[i] * jnp.exp(gt[i] - gi[i]), v_new[i]) for i in range(nch)]
        for i, (z, h) in enumerate(chains):
            s_sc[z, h] = s[i] * jnp.exp(gt[i]) + upd[i]
        return carry

    lax.fori_loop(0, n, body, 0)

    ng = ng_ref[...]

    def fin(c, carry):
        r0 = pl.multiple_of(c * C, C)
        rows = pl.ds(r0, C)
        for h in range(DN_H):
            cols = slice(h * LANES, (h + 1) * LANES)
            o = of_sc[rows, cols] + ob_sc[rows, cols]
            o_ref[rows, cols] = (_rms(o, ng) * _silu(dz_ref[rows, cols])).astype(BF16)
        return carry

    lax.fori_loop(0, n, fin, 0)
    so_ref[...] = s_sc[...]


def _dn(p_main, p_small, conv_w, alog_row, dtb_row, ng, s0, L, n_seq, row_blk0):
    has_state = s0 is not None
    in_specs = [
        pl.BlockSpec((L, 1536), lambda b: (row_blk0 + b, P_DN // 1536)),
        pl.BlockSpec((L, 512), lambda b: (row_blk0 + b, P_DZ // 512)),
        pl.BlockSpec((L, LANES), lambda b: (row_blk0 + b, 0)),
        pl.BlockSpec((DN_CONV, 3 * DN_H * DN_DK), lambda b: (0, 0)),
        pl.BlockSpec((1, LANES), lambda b: (0, 0)),
        pl.BlockSpec((1, LANES), lambda b: (0, 0)),
        pl.BlockSpec((1, DN_DV), lambda b: (0, 0)),
    ]
    args = [p_main, p_main, p_small, conv_w, alog_row, dtb_row, ng]
    if has_state:
        in_specs.append(pl.BlockSpec((None, 2, DN_H, DN_DK, DN_DV), lambda b: (b, 0, 0, 0, 0)))
        args.append(s0)
    return pl.pallas_call(
        functools.partial(_dn_kernel, L, has_state),
        grid=(n_seq,),
        in_specs=in_specs,
        out_specs=[
            pl.BlockSpec((L, 512), lambda b: (b, 0)),
            pl.BlockSpec((None, 2, DN_H, DN_DK, DN_DV), lambda b: (b, 0, 0, 0, 0)),
        ],
        out_shape=[
            jax.ShapeDtypeStruct((n_seq * L, 512), BF16),
            jax.ShapeDtypeStruct((n_seq, 2, DN_H, DN_DK, DN_DV), F32),
        ],
        scratch_shapes=[
            pltpu.VMEM((L, 1536), F32),
            pltpu.VMEM((2 * DN_H, L, LANES), F32),
            pltpu.VMEM((2 * DN_H, L, LANES), F32),
            pltpu.VMEM((L, 512), F32),
            pltpu.VMEM((L, 512), F32),
            pltpu.VMEM((2, DN_H, DN_DK, DN_DV), F32),
        ],
        compiler_params=_cparams(("arbitrary",)),
        name="dn_lat" if has_state else "dn_ctx",
    )(*args)


def _prep_w_in(w_in_l):
    off = np.cumsum((0, 256, 256, 512, 512, 32, 1024, 256, 256, 1536, 512, 8, 8))
    gq, gk, gv, gr, glr, aq, ak, av, dqkv, dz, da, db = [
        w_in_l[:, int(off[i]):int(off[i + 1])] for i in range(12)]
    main = jnp.concatenate([aq, ak, av, gq, gk, gv, gr, dqkv, dz], axis=1).astype(BF16)
    pad = jnp.zeros((D_MODEL, LANES - 48), w_in_l.dtype)
    small = jnp.concatenate([glr, da, db, pad], axis=1).astype(BF16)
    return main, small


def _lane_row(vals, offset):
    return jnp.zeros((1, LANES), F32).at[0, offset:offset + vals.shape[0]].set(vals.astype(F32))


def kernel(x_prompt, x_sample, cache_k, cache_v, state_gla, state_dn, c, c_ctx, norm1_g, norm2_g,
           w_mod, b_mod, w_in, gla_w2, gla_b, gla_norm_g, q_norm_g, k_norm_g, dn_conv, dn_a_log,
           dn_dt_bias, dn_norm_g, w_out, w_ff1, w_ff2):
    n_ctx, n_lat = x_prompt.shape[0], x_sample.shape[0]
    depth = w_in.shape[0]
    n_ctx_rows = n_ctx * SEQ
    assert n_ctx_rows % DEC_SEQ == 0 and n_lat <= 7
    lat_blk0 = n_ctx_rows // DEC_SEQ

    x = jnp.concatenate([x_prompt.reshape(n_ctx_rows, D_MODEL),
                         x_sample.reshape(n_lat * DEC_SEQ, D_MODEL)], axis=0)
    cond8 = jnp.concatenate([c_ctx[None, :], c, jnp.zeros((7 - n_lat, D_MODEL), F32)], axis=0)
    mods = _modulation(cond8, w_mod, b_mod).reshape(depth, 8, N_MOD, D_MODEL)

    cos, sin = _rope_tables()
    gla_consts = _gla_consts()
    ck = cache_k.reshape(n_lat, depth, PAST_LEN, ATT_HKV * HEAD_DIM)
    cv = cache_v.reshape(n_lat, depth, PAST_LEN, ATT_HKV * HEAD_DIM)

    nks, nvs, sgs, sds = [], [], [], []
    for l in range(depth):
        w_main, w_small = _prep_w_in(w_in[l])
        p_main, p_small = _inproj(x, mods[l], norm1_g[l][None, :], w_main, w_small, n_ctx_rows)

        qg, kg = q_norm_g[l][None, :], k_norm_g[l][None, :]
        oa_c, nk, nv = _att_ctx(p_main, qg, kg, n_ctx)
        oa_l = _att_lat(p_main, qg, kg, cos, sin, ck[:, l], cv[:, l], n_lat, lat_blk0)

        w2p = jnp.zeros((2, LANES, GLA_H * GLA_DK), F32)
        w2p = w2p.at[0, 0:GLA_LR].set(gla_w2[l, 0]).at[1, GLA_LR:2 * GLA_LR].set(gla_w2[l, 1])
        gbias = gla_b[l][:, None, :]
        gng = gla_norm_g[l][None, :]
        og_c, sg = _gla(p_main, p_small, w2p, gbias, gng, gla_consts, None, SEQ, n_ctx, 0)
        og_l, _ = _gla(p_main, p_small, w2p, gbias, gng, gla_consts,
                       _gla_state_to_kernel(state_gla[:, l]), DEC_SEQ, n_lat, lat_blk0)

        alog_row = _lane_row(dn_a_log[l].reshape(-1), S_DA)
        dtb_row = _lane_row(dn_dt_bias[l].reshape(-1), S_DA)
        dng = dn_norm_g[l][None, :]
        od_c, sd = _dn(p_main, p_small, dn_conv[l], alog_row, dtb_row, dng, None, SEQ, n_ctx, 0)
        od_l, _ = _dn(p_main, p_small, dn_conv[l], alog_row, dtb_row, dng, state_dn[:, l],
                      DEC_SEQ, n_lat, lat_blk0)

        o_gla = jnp.concatenate([og_c, og_l], axis=0)
        o_att = jnp.concatenate([oa_c, oa_l], axis=0)
        o_dn = jnp.concatenate([od_c, od_l], axis=0)
        x1, h2 = _outproj(x, o_gla, o_att, o_dn, mods[l], norm2_g[l][None, :],
                          w_out[l].astype(BF16), n_ctx_rows)
        x = _ffn(h2, x1, mods[l], w_ff1[l].astype(BF16), w_ff2[l].astype(BF16), n_ctx_rows)

        nks.append(nk.reshape(n_ctx, SEQ, ATT_HKV, HEAD_DIM))
        nvs.append(nv.reshape(n_ctx, SEQ, ATT_HKV, HEAD_DIM))
        sgs.append(_gla_state_from_kernel(sg))
        sds.append(sd)

    y_prompt = x[:n_ctx_rows].reshape(n_ctx, SEQ, D_MODEL)
    y_sample = x[n_ctx_rows:].reshape(n_lat, DEC_SEQ, D_MODEL)
    return (y_prompt, y_sample, jnp.stack(nks, axis=1), jnp.stack(nvs, axis=1),
            jnp.stack(sgs, axis=1), jnp.stack(sds, axis=1))
```

```python
import functools
import math

import numpy as np
import jax
import jax.numpy as jnp
from jax import lax
from jax.experimental import pallas as pl
from jax.experimental.pallas import tpu as pltpu

F32 = jnp.float32
BF16 = jnp.bfloat16

D_MODEL = 2048
SEQ = 256
DEC_SEQ = 1024
PAST_LEN = 256
GRID_W = 64
HEAD_DIM = 128
EPS = 1e-6
GLA_H = 4
GLA_DK = 64
GLA_DV = 128
GLA_LR = 16
GLA_GATE_NORM = 16.0
ATT_HQ = 8
ATT_HKV = 2
ROPE_THETA = 10000.0
DN_H = 4
DN_DK = 128
DN_DV = 128
DN_CONV = 3
CHUNK = 64
FF = 4 * D_MODEL
N_MOD = 6
LANES = 128

P_AQ = 0
P_AKV = 1024
P_GLA = 1536
P_DN = 3072
P_DZ = 4608
P_MAIN = 5120
S_DA = 32
S_DB = 40

VMEM_LIMIT = 56 * 1024 * 1024


def _cparams(sem):
    return pltpu.CompilerParams(dimension_semantics=sem, vmem_limit_bytes=VMEM_LIMIT)


def _bdot(a, b):
    return jnp.dot(a.astype(BF16), b.astype(BF16), preferred_element_type=F32)


def _bdot_nt(a, b):
    return lax.dot_general(a.astype(BF16), b.astype(BF16), (((1,), (1,)), ((), ())),
                           preferred_element_type=F32)


def _bdot_tn(a, b):
    return lax.dot_general(a.astype(BF16), b.astype(BF16), (((0,), (0,)), ((), ())),
                           preferred_element_type=F32)


def _split3(x):
    hi = x.astype(BF16)
    r = x - hi.astype(F32)
    mid = r.astype(BF16)
    lo = (r - mid.astype(F32)).astype(BF16)
    return hi, mid, lo


def _dot_exact_lhs(lhs_bf16, x):
    hi, mid, lo = _split3(x)
    d = functools.partial(jnp.dot, preferred_element_type=F32)
    return d(lhs_bf16, hi) + d(lhs_bf16, mid) + d(lhs_bf16, lo)


def _dot_exact_lhs_wide(lhs_bf16, x):
    n = x.shape[1]
    hi = x.astype(BF16)
    mid = (x - hi.astype(F32)).astype(BF16)
    r = jnp.dot(lhs_bf16, jnp.concatenate([hi, mid], axis=1), preferred_element_type=F32)
    return r[:, 0:n] + r[:, n:2 * n]


def _dot3(a, b):
    ah, am, _ = _split3(a)
    bh, bm, _ = _split3(b)
    d = functools.partial(jnp.dot, preferred_element_type=F32)
    return d(ah, bh) + (d(ah, bm) + d(am, bh))


def _silu(x):
    return x * jax.nn.sigmoid(x)


def _softplus(x):
    return jnp.maximum(x, 0.0) + jnp.log1p(jnp.exp(-jnp.abs(x)))


def _rms(x, g):
    return x * lax.rsqrt(jnp.mean(x * x, axis=-1, keepdims=True) + EPS) * g


def _mods_kernel(c_ref, w_ref, b_ref, o_ref):
    o_ref[...] = _bdot(_silu(c_ref[...]), w_ref[...]) + b_ref[...]


def _modulation(cond8, w_mod, b_mod):
    depth = w_mod.shape[0]
    n_out = N_MOD * D_MODEL
    tn = 1024
    return pl.pallas_call(
        _mods_kernel,
        grid=(depth, n_out // tn),
        in_specs=[
            pl.BlockSpec((8, D_MODEL), lambda l, j: (0, 0)),
            pl.BlockSpec((None, D_MODEL, tn), lambda l, j: (l, 0, j)),
            pl.BlockSpec((None, 1, tn), lambda l, j: (l, 0, j)),
        ],
        out_specs=pl.BlockSpec((None, 8, tn), lambda l, j: (l, 0, j)),
        out_shape=jax.ShapeDtypeStruct((depth, 8, n_out), F32),
        compiler_params=_cparams(("arbitrary", "arbitrary")),
        name="modulation",
    )(cond8, w_mod, b_mod.reshape(depth, 1, n_out))


def _group_of_tile(i, tm, n_ctx_rows):
    n_ctx_tiles = n_ctx_rows // tm
    per_lat = DEC_SEQ // tm
    return jnp.where(i < n_ctx_tiles, 0, 1 + (i - n_ctx_tiles) // per_lat)


def _row_specs(arrays, width, tm, nct):
    if len(arrays) == 1:
        return [pl.BlockSpec((tm, width), lambda i, *_: (i, 0))]
    return [pl.BlockSpec((tm, width), lambda i, *_: (jnp.minimum(i, nct - 1), 0)),
            pl.BlockSpec((tm, width), lambda i, *_: (jnp.maximum(i - nct, 0), 0))]


def _row_load(refs, nct):
    if len(refs) == 1:
        return refs[0][...]
    return jnp.where(pl.program_id(0) < nct, refs[0][...], refs[1][...])


def _inproj_kernel(nct, n_x, *refs):
    x_refs = refs[:n_x]
    mod_ref, g_ref, w_ref, ws_ref, p_ref, ps_ref, h_sc = refs[n_x:]

    @pl.when(pl.program_id(1) == 0)
    def _():
        m = mod_ref[...]
        h = _rms(_row_load(x_refs, nct), g_ref[...]) * (1.0 + m[1:2]) + m[0:1]
        hb = h.astype(BF16)
        h_sc[...] = hb
        ps_ref[...] = jnp.dot(hb, ws_ref[...], preferred_element_type=F32)

    p_ref[...] = jnp.dot(h_sc[...], w_ref[...], preferred_element_type=F32)


def _inproj(xs, mods_l, g1, w_main, w_small, n_ctx_rows, tm=1024, tn=512):
    rows = sum(x.shape[0] for x in xs)
    nct = n_ctx_rows // tm
    grp = functools.partial(_group_of_tile, tm=tm, n_ctx_rows=n_ctx_rows)
    return pl.pallas_call(
        functools.partial(_inproj_kernel, nct, len(xs)),
        grid=(rows // tm, P_MAIN // tn),
        in_specs=_row_specs(xs, D_MODEL, tm, nct) + [
            pl.BlockSpec((None, N_MOD, D_MODEL), lambda i, j: (grp(i), 0, 0)),
            pl.BlockSpec((1, D_MODEL), lambda i, j: (0, 0)),
            pl.BlockSpec((D_MODEL, tn), lambda i, j: (0, j)),
            pl.BlockSpec((D_MODEL, LANES), lambda i, j: (0, 0)),
        ],
        out_specs=[
            pl.BlockSpec((tm, tn), lambda i, j: (i, j)),
            pl.BlockSpec((tm, LANES), lambda i, j: (i, 0)),
        ],
        out_shape=[
            jax.ShapeDtypeStruct((rows, P_MAIN), F32),
            jax.ShapeDtypeStruct((rows, LANES), F32),
        ],
        scratch_shapes=[pltpu.VMEM((tm, D_MODEL), BF16)],
        compiler_params=_cparams(("arbitrary", "arbitrary")),
        name="inproj",
    )(*xs, mods_l, g1, w_main, w_small)


def _outproj_kernel(nct, n_x, *refs):
    x_refs = refs[:n_x]
    (ogc_ref, ogl_ref, oac_ref, oal_ref, odc_ref, odl_ref, mod_ref, g_ref, w_ref,
     x1_ref, h2_ref) = refs[n_x:]
    m = mod_ref[...]
    d = functools.partial(jnp.dot, preferred_element_type=F32)
    mix = (d(_row_load((ogc_ref, ogl_ref), nct), w_ref[0:512, :])
           + d(_row_load((oac_ref, oal_ref), nct), w_ref[512:1536, :])
           + d(_row_load((odc_ref, odl_ref), nct), w_ref[1536:2048, :]))
    x1 = _row_load(x_refs, nct) + m[2:3] * mix
    x1_ref[...] = x1
    h2_ref[...] = (_rms(x1, g_ref[...]) * (1.0 + m[4:5]) + m[3:4]).astype(BF16)


def _outproj(xs, o_gla, o_att, o_dn, mods_l, g2, w_out, n_ctx_rows, tm=512):
    rows = sum(x.shape[0] for x in xs)
    nct = n_ctx_rows // tm
    grp = functools.partial(_group_of_tile, tm=tm, n_ctx_rows=n_ctx_rows)
    return pl.pallas_call(
        functools.partial(_outproj_kernel, nct, len(xs)),
        grid=(rows // tm,),
        in_specs=(_row_specs(xs, D_MODEL, tm, nct) + _row_specs(o_gla, 512, tm, nct)
                  + _row_specs(o_att, 1024, tm, nct) + _row_specs(o_dn, 512, tm, nct) + [
                      pl.BlockSpec((None, N_MOD, D_MODEL), lambda i: (grp(i), 0, 0)),
                      pl.BlockSpec((1, D_MODEL), lambda i: (0, 0)),
                      pl.BlockSpec((D_MODEL, D_MODEL), lambda i: (0, 0),
                                   pipeline_mode=pl.Buffered(1)),
                  ]),
        out_specs=[
            pl.BlockSpec((tm, D_MODEL), lambda i: (i, 0)),
            pl.BlockSpec((tm, D_MODEL), lambda i: (i, 0)),
        ],
        out_shape=[
            jax.ShapeDtypeStruct((rows, D_MODEL), F32),
            jax.ShapeDtypeStruct((rows, D_MODEL), BF16),
        ],
        compiler_params=_cparams(("arbitrary",)),
        name="outproj",
    )(*xs, *o_gla, *o_att, *o_dn, mods_l, g2, w_out)


def _ffn_kernel(nct, n_out, h_ref, x1_ref, mod_ref, w1_ref, w2_ref, *refs):
    y_refs, acc_sc = refs[:n_out], refs[n_out]
    i, j = pl.program_id(0), pl.program_id(1)
    last = pl.num_programs(1) - 1
    t = jnp.dot(h_ref[...], w1_ref[...], preferred_element_type=F32)
    t = jnp.square(jnp.maximum(t, 0.0))
    part = jnp.dot(t.astype(BF16), w2_ref[...], preferred_element_type=F32)

    @pl.when(j == 0)
    def _():
        acc_sc[...] = part

    @pl.when((j > 0) & (j < last))
    def _():
        acc_sc[...] += part

    def result():
        return x1_ref[...] + mod_ref[5:6, :] * (acc_sc[...] + part)

    if n_out == 1:
        @pl.when(j == last)
        def _():
            y_refs[0][...] = result()
    else:
        @pl.when((j == last) & (i < nct))
        def _():
            y_refs[0][...] = result()

        @pl.when((j == last) & (i >= nct))
        def _():
            y_refs[1][...] = result()


def _ffn(h2, x1, mods_l, w1, w2, n_ctx_rows, split_out, tm=512, tf=1024):
    rows = x1.shape[0]
    nct = n_ctx_rows // tm
    grp = functools.partial(_group_of_tile, tm=tm, n_ctx_rows=n_ctx_rows)
    assert FF // tf >= 2
    if split_out:
        out_specs = [pl.BlockSpec((tm, D_MODEL), lambda i, j: (jnp.minimum(i, nct - 1), 0)),
                     pl.BlockSpec((tm, D_MODEL), lambda i, j: (jnp.maximum(i - nct, 0), 0))]
        out_shape = [jax.ShapeDtypeStruct((n_ctx_rows, D_MODEL), F32),
                     jax.ShapeDtypeStruct((rows - n_ctx_rows, D_MODEL), F32)]
    else:
        out_specs = [pl.BlockSpec((tm, D_MODEL), lambda i, j: (i, 0))]
        out_shape = [jax.ShapeDtypeStruct((rows, D_MODEL), F32)]
    return pl.pallas_call(
        functools.partial(_ffn_kernel, nct, len(out_specs)),
        grid=(rows // tm, FF // tf),
        in_specs=[
            pl.BlockSpec((tm, D_MODEL), lambda i, j: (i, 0)),
            pl.BlockSpec((tm, D_MODEL), lambda i, j: (i, 0)),
            pl.BlockSpec((None, N_MOD, D_MODEL), lambda i, j: (grp(i), 0, 0)),
            pl.BlockSpec((D_MODEL, tf), lambda i, j: (0, j)),
            pl.BlockSpec((tf, D_MODEL), lambda i, j: (j, 0)),
        ],
        out_specs=out_specs,
        out_shape=out_shape,
        scratch_shapes=[pltpu.VMEM((tm, D_MODEL), F32)],
        compiler_params=_cparams(("arbitrary", "arbitrary")),
        name="ffn",
    )(h2, x1, mods_l, w1, w2)


def _rope_tables():
    half = HEAD_DIM // 2
    pos = np.arange(DEC_SEQ)
    row = (pos // GRID_W).astype(np.float32)
    col = (pos % GRID_W).astype(np.float32)
    inv = (ROPE_THETA ** (-np.arange(0, half, 2, dtype=np.float32) / half)).astype(np.float32)
    ar = row[:, None] * inv[None, :]
    ac = col[:, None] * inv[None, :]
    cos = np.concatenate([np.cos(ar), np.cos(ar), np.cos(ac), np.cos(ac)], axis=1)
    sin = np.concatenate([-np.sin(ar), np.sin(ar), -np.sin(ac), np.sin(ac)], axis=1)
    return jnp.asarray(cos, F32), jnp.asarray(sin, F32)


def _rope(x, cos, sin):
    lane = lax.broadcasted_iota(jnp.int32, x.shape, 1)
    first = (lane % 64) < 32
    partner = jnp.where(first, pltpu.roll(x, 96, 1), pltpu.roll(x, 32, 1))
    return x * cos + partner * sin


ATT_G = ATT_HQ // ATT_HKV


def _softmax_pv(q, kb, vb):
    s = lax.dot_general(q.astype(BF16), kb, (((1,), (1,)), ((), ())), preferred_element_type=F32)
    m = jnp.max(s, axis=-1, keepdims=True)
    p = jnp.exp(s - m)
    l = jnp.sum(p, axis=-1, keepdims=True)
    return jnp.dot(p.astype(BF16), vb, preferred_element_type=F32) / l


def _group_attention(q_heads, kb, vb, store):
    for g, q in enumerate(q_heads):
        store(g, _softmax_pv(q, kb, vb).astype(BF16))


def _att_ctx_kernel(aq_ref, akv_ref, qg_ref, kg_ref, o_ref, nk_ref, nv_ref):
    qg = qg_ref[...] * (HEAD_DIM ** -0.5)
    kg = kg_ref[...]
    for hk in range(ATT_HKV):
        ks = slice(hk * HEAD_DIM, (hk + 1) * HEAD_DIM)
        kn = _rms(akv_ref[:, ks], kg)
        v = akv_ref[:, 256 + hk * HEAD_DIM:256 + (hk + 1) * HEAD_DIM]
        nk_ref[:, ks] = kn
        nv_ref[:, ks] = v

        def head_cols(g, hk=hk):
            return slice((hk * ATT_G + g) * HEAD_DIM, (hk * ATT_G + g + 1) * HEAD_DIM)

        def store(g, val):
            o_ref[:, head_cols(g)] = val

        qs = [_rms(aq_ref[:, head_cols(g)], qg) for g in range(ATT_G)]
        _group_attention(qs, kn.astype(BF16), v.astype(BF16), store)


def _att_lat_kernel(aq_ref, akv_ref, qg_ref, kg_ref, cos_ref, sin_ref, ck_ref, cv_ref, o_ref,
                    k_sc, v_sc):
    qg = qg_ref[...] * (HEAD_DIM ** -0.5)
    kg = kg_ref[...]
    qb = 256
    for hk in range(ATT_HKV):
        ks = slice(hk * HEAD_DIM, (hk + 1) * HEAD_DIM)
        k_sc[hk, 0:PAST_LEN, :] = ck_ref[:, ks].astype(BF16)
        v_sc[hk, 0:PAST_LEN, :] = cv_ref[:, ks].astype(BF16)
        kn = _rope(_rms(akv_ref[:, ks], kg), cos_ref[...], sin_ref[...])
        k_sc[hk, PAST_LEN:, :] = kn.astype(BF16)
        v_sc[hk, PAST_LEN:, :] = akv_ref[:, 256 + hk * HEAD_DIM:256 + (hk + 1) * HEAD_DIM].astype(BF16)

    def body(r, carry):
        r0 = pl.multiple_of(r * qb, qb)
        cos = cos_ref[pl.ds(r0, qb), :]
        sin = sin_ref[pl.ds(r0, qb), :]
        for hk in range(ATT_HKV):
            def head_cols(g, hk=hk):
                return slice((hk * ATT_G + g) * HEAD_DIM, (hk * ATT_G + g + 1) * HEAD_DIM)

            def store(g, val):
                o_ref[pl.ds(r0, qb), head_cols(g)] = val

            qs = [_rope(_rms(aq_ref[pl.ds(r0, qb), head_cols(g)], qg), cos, sin)
                  for g in range(ATT_G)]
            _group_attention(qs, k_sc[hk], v_sc[hk], store)
        return carry

    lax.fori_loop(0, DEC_SEQ // qb, body, 0)


def _att_ctx(p_main, qg, kg, n_seq):
    L = SEQ
    return pl.pallas_call(
        _att_ctx_kernel,
        grid=(n_seq,),
        in_specs=[
            pl.BlockSpec((L, 1024), lambda b: (b, P_AQ // 1024)),
            pl.BlockSpec((L, 512), lambda b: (b, P_AKV // 512)),
            pl.BlockSpec((1, HEAD_DIM), lambda b: (0, 0)),
            pl.BlockSpec((1, HEAD_DIM), lambda b: (0, 0)),
        ],
        out_specs=[
            pl.BlockSpec((L, 1024), lambda b: (b, 0)),
            pl.BlockSpec((None, L, 256), lambda b: (b, 0, 0)),
            pl.BlockSpec((None, L, 256), lambda b: (b, 0, 0)),
        ],
        out_shape=[
            jax.ShapeDtypeStruct((n_seq * L, 1024), BF16),
            jax.ShapeDtypeStruct((n_seq, L, 256), F32),
            jax.ShapeDtypeStruct((n_seq, L, 256), F32),
        ],
        compiler_params=_cparams(("arbitrary",)),
        name="att_ctx",
    )(p_main, p_main, qg, kg)


def _att_lat(p_main, qg, kg, cos, sin, ck, cv, layer, n_seq, row_blk0):
    L = DEC_SEQ
    return pl.pallas_call(
        _att_lat_kernel,
        grid=(n_seq,),
        in_specs=[
            pl.BlockSpec((L, 1024), lambda b: (row_blk0 + b, P_AQ // 1024)),
            pl.BlockSpec((L, 512), lambda b: (row_blk0 + b, P_AKV // 512)),
            pl.BlockSpec((1, HEAD_DIM), lambda b: (0, 0)),
            pl.BlockSpec((1, HEAD_DIM), lambda b: (0, 0)),
            pl.BlockSpec((L, HEAD_DIM), lambda b: (0, 0)),
            pl.BlockSpec((L, HEAD_DIM), lambda b: (0, 0)),
            pl.BlockSpec((None, None, PAST_LEN, 256), lambda b: (b, layer, 0, 0)),
            pl.BlockSpec((None, None, PAST_LEN, 256), lambda b: (b, layer, 0, 0)),
        ],
        out_specs=pl.BlockSpec((L, 1024), lambda b: (b, 0)),
        out_shape=jax.ShapeDtypeStruct((n_seq * L, 1024), BF16),
        scratch_shapes=[
            pltpu.VMEM((ATT_HKV, PAST_LEN + L, HEAD_DIM), BF16),
            pltpu.VMEM((ATT_HKV, PAST_LEN + L, HEAD_DIM), BF16),
        ],
        compiler_params=_cparams(("arbitrary",)),
        name="att_lat",
    )(p_main, p_main, qg, kg, cos, sin, ck, cv)


_GLA_LEVELS = (32, 16, 8, 4, 2, 1)


def _gla_consts():
    C = CHUNK
    i = np.arange(C)[:, None]
    t = np.arange(C)[None, :]
    mats = [t <= i, t > i]
    masks = [i == t]
    for s in _GLA_LEVELS:
        m = (i // s) * s
        if s > 1:
            mats.append((t > m) & (t <= i))
        mats.append((t > i) & (t <= m + s))
        masks.append((i // (2 * s) == t // (2 * s)) & ((i // s) % 2 == 1) & ((t // s) % 2 == 0))
    fwd = np.concatenate(mats, 0)
    bwd = np.concatenate([mm[::-1, ::-1] for mm in mats], 0)
    mstack = np.stack([fwd, bwd]).astype(np.float32)
    mf = np.stack([np.concatenate([mm, mm], 0) for mm in masks])
    mb = np.stack([np.concatenate([mm[::-1, ::-1], mm[::-1, ::-1]], 0) for mm in masks])
    lmask = np.stack([mf, mb]).astype(np.float32)
    return jnp.asarray(mstack, BF16), jnp.asarray(lmask, F32)


def _gla_kernel(L, has_state, *refs):
    if has_state:
        (pg_ref, ps_ref, w2_ref, gb_ref, ng_ref, ms_ref, lm_ref, s0_ref,
         o_ref, so_ref, la_sc, of_sc, ob_sc, st_sc) = refs
    else:
        (pg_ref, ps_ref, w2_ref, gb_ref, ng_ref, ms_ref, lm_ref,
         o_ref, so_ref, la_sc, of_sc, ob_sc, st_sc) = refs
    n = L // CHUNK
    C = CHUNK

    ps = ps_ref[...]
    for z in range(2):
        x = _dot3(ps, w2_ref[z]) + gb_ref[z]
        la_sc[z] = (jnp.minimum(x, 0.0) - jnp.log1p(jnp.exp(-jnp.abs(x)))) * (1.0 / GLA_GATE_NORM)

    if has_state:
        for z in range(2):
            for p in range(2):
                pair = jnp.concatenate([s0_ref[z, 2 * p], s0_ref[z, 2 * p + 1]], axis=0)
                st_sc[z, p] = jnp.transpose(pair)
    else:
        st_sc[...] = jnp.zeros(st_sc.shape, F32)

    lane = lax.broadcasted_iota(jnp.int32, (C, LANES), 1)
    first = lane < 64

    def stack_heads(t):
        return jnp.concatenate([jnp.where(first, t, 0.0), jnp.where(first, 0.0, t)], axis=0)

    chains = [(z, p) for z in range(2) for p in range(2)]
    nch = len(chains)

    def body(c, carry):
        rows = [pl.ds(pl.multiple_of(c * C, C), C), pl.ds(pl.multiple_of((n - 1 - c) * C, C), C)]
        last = [C - 1, 0]
        q = [pg_ref[rows[z], p * LANES:(p + 1) * LANES] * (GLA_DK ** -0.5) for z, p in chains]
        k = [pg_ref[rows[z], 256 + p * LANES:256 + (p + 1) * LANES] for z, p in chains]
        e = [jnp.exp(_dot_exact_lhs_wide(ms_ref[z], la_sc[z, rows[z], p * LANES:(p + 1) * LANES]))
             for z, p in chains]
        st = [st_sc[z, p] for z, p in chains]
        a2 = [_bdot_nt(stack_heads(q[i]), k[i]) * lm_ref[z, 0] for i, (z, p) in enumerate(chains)]
        blk = 2
        for li, s in enumerate(_GLA_LEVELS):
            if s > 1:
                qs = [q[i] * e[i][blk * C:(blk + 1) * C] for i in range(nch)]
                blk += 1
            else:
                qs = q
            ks = [k[i] * e[i][blk * C:(blk + 1) * C] for i in range(nch)]
            blk += 1
            a2 = [a2[i] + _bdot_nt(stack_heads(qs[i]), ks[i]) * lm_ref[z, li + 1]
                  for i, (z, p) in enumerate(chains)]
        inter = [_bdot_nt(stack_heads(q[i] * e[i][0:C]), st[i]) for i in range(nch)]
        kl = [k[i] * e[i][C:2 * C] for i in range(nch)]
        for i, (z, p) in enumerate(chains):
            osc = of_sc if z == 0 else ob_sc
            zs = []
            for hh in range(2):
                h = 2 * p + hh
                v = pg_ref[rows[z], 512 + h * LANES:512 + (h + 1) * LANES]
                o = _bdot(a2[i][hh * C:(hh + 1) * C], v) + inter[i][hh * C:(hh + 1) * C]
                osc[rows[z], h * LANES:(h + 1) * LANES] = o
                zs.append(_bdot_tn(v, kl[i]))
            st_sc[z, p] = (st[i] * e[i][last[z]:last[z] + 1, :]
                           + jnp.where(first[0:1], zs[0], zs[1]))
        return carry

    lax.fori_loop(0, n, body, 0)

    ng = ng_ref[...]

    def fin(c, carry):
        r0 = pl.multiple_of(c * C, C)
        rows = pl.ds(r0, C)
        for h in range(GLA_H):
            cols = slice(h * LANES, (h + 1) * LANES)
            o = of_sc[rows, cols] + ob_sc[rows, cols]
            gate = _silu(pg_ref[rows, 1024 + h * LANES:1024 + (h + 1) * LANES])
            o_ref[rows, cols] = (_rms(o, ng) * gate).astype(BF16)
        return carry

    lax.fori_loop(0, n, fin, 0)
    for z in range(2):
        for p in range(2):
            pair = jnp.transpose(st_sc[z, p])
            so_ref[z, 2 * p] = pair[0:GLA_DK]
            so_ref[z, 2 * p + 1] = pair[GLA_DK:2 * GLA_DK]


def _gla(p_main, p_small, w2p, gbias, ng, consts, s0, layer, L, n_seq, row_blk0):
    mstack, lmask = consts
    has_state = s0 is not None
    state_blk = (None, 2, GLA_H, GLA_DK, GLA_DV)
    in_specs = [
        pl.BlockSpec((L, 1536), lambda b: (row_blk0 + b, P_GLA // 1536)),
        pl.BlockSpec((L, LANES), lambda b: (row_blk0 + b, 0)),
        pl.BlockSpec((2, LANES, 256), lambda b: (0, 0, 0)),
        pl.BlockSpec((2, 1, 256), lambda b: (0, 0, 0)),
        pl.BlockSpec((1, GLA_DV), lambda b: (0, 0)),
        pl.BlockSpec(mstack.shape, lambda b: (0, 0, 0)),
        pl.BlockSpec(lmask.shape, lambda b: (0, 0, 0, 0)),
    ]
    args = [p_main, p_small, w2p, gbias, ng, mstack, lmask]
    if has_state:
        in_specs.append(pl.BlockSpec((None,) + state_blk, lambda b: (b, layer, 0, 0, 0, 0)))
        args.append(s0)
    return pl.pallas_call(
        functools.partial(_gla_kernel, L, has_state),
        grid=(n_seq,),
        in_specs=in_specs,
        out_specs=[
            pl.BlockSpec((L, 512), lambda b: (b, 0)),
            pl.BlockSpec(state_blk, lambda b: (b, 0, 0, 0, 0)),
        ],
        out_shape=[
            jax.ShapeDtypeStruct((n_seq * L, 512), BF16),
            jax.ShapeDtypeStruct((n_seq, 2, GLA_H, GLA_DK, GLA_DV), F32),
        ],
        scratch_shapes=[
            pltpu.VMEM((2, L, 256), F32),
            pltpu.VMEM((L, 512), F32),
            pltpu.VMEM((L, 512), F32),
            pltpu.VMEM((2, 2, GLA_DV, LANES), F32),
        ],
        compiler_params=_cparams(("arbitrary",)),
        name="gla_lat" if has_state else "gla_ctx",
    )(*args)


def _dn_kernel(L, has_state, *refs):
    if has_state:
        (pd_ref, dz_ref, ps_ref, cw_ref, al_ref, dtb_ref, ng_ref, s0_ref,
         o_ref, so_ref, qkv_sc, g_sc, bb_sc, of_sc, ob_sc, s_sc) = refs
    else:
        (pd_ref, dz_ref, ps_ref, cw_ref, al_ref, dtb_ref, ng_ref,
         o_ref, so_ref, qkv_sc, g_sc, bb_sc, of_sc, ob_sc, s_sc) = refs
    n = L // CHUNK
    C = CHUNK

    rowi = lax.broadcasted_iota(jnp.int32, (L, LANES), 0)
    not_first = rowi > 0
    not_last = rowi < L - 1

    def conv_tile(j, carry):
        c0 = pl.multiple_of(j * LANES, LANES)
        x = pd_ref[:, pl.ds(c0, LANES)]
        w = cw_ref[:, pl.ds(c0, LANES)]
        prev = jnp.where(not_first, pltpu.roll(x, 1, 0), 0.0)
        nxt = jnp.where(not_last, pltpu.roll(x, L - 1, 0), 0.0)
        y = _silu(prev * w[0:1] + x * w[1:2] + nxt * w[2:3])
        inv = lax.rsqrt(jnp.sum(y * y, axis=-1, keepdims=True) + EPS)
        scale = jnp.where(j < DN_H, inv * (DN_DK ** -0.5), jnp.where(j < 2 * DN_H, inv, 1.0))
        qkv_sc[:, pl.ds(c0, LANES)] = y * scale
        return carry

    lax.fori_loop(0, 3 * DN_H, conv_tile, 0)

    ps = ps_ref[...]
    lane_l = lax.broadcasted_iota(jnp.int32, (L, LANES), 1)
    g_sc[...] = -jnp.exp(al_ref[...]) * _softplus(ps + dtb_ref[...])
    b_all = jax.nn.sigmoid(ps)
    for idx in range(2 * DN_H):
        bcol = jnp.sum(jnp.where(lane_l == S_DB + idx, b_all, 0.0), axis=-1, keepdims=True)
        bb_sc[idx] = jnp.broadcast_to(bcol, (L, LANES))

    if has_state:
        s_sc[...] = s0_ref[...]
    else:
        s_sc[...] = jnp.zeros(s_sc.shape, F32)

    ri = lax.broadcasted_iota(jnp.int32, (C, C), 0)
    ci = lax.broadcasted_iota(jnp.int32, (C, C), 1)
    lane_c =lax.broadcasted_iota(jnp.int32, (C, LANES), 1)

    incl = [ci <= ri, ci >= ri]
    strict = [ci < ri, ci > ri]
    m_b = [m.astype(BF16) for m in incl]
    chains = [(z, h) for z in range(2) for h in range(DN_H)]
    nch = len(chains)

    def body(c, carry):
        rows = [pl.ds(pl.multiple_of(c * C, C), C), pl.ds(pl.multiple_of((n - 1 - c) * C, C), C)]
        last = [C - 1, 0]

        def ld(base, z, h):
            return qkv_sc[rows[z], base + h * LANES:base + (h + 1) * LANES]

        q = [ld(0, z, h) for z, h in chains]
        k = [ld(512, z, h) for z, h in chains]
        v = [ld(1024, z, h) for z, h in chains]
        bb = [bb_sc[z * DN_H + h, rows[z], :] for z, h in chains]
        gc = [_dot_exact_lhs(m_b[z], g_sc[rows[z], :]) for z in range(2)]
        gct = [jnp.transpose(jnp.concatenate([gc[z], gc[z]], axis=0)) for z in range(2)]
        gi = [jnp.broadcast_to(jnp.sum(jnp.where(lane_c == S_DA + z * DN_H + h, gc[z], 0.0),
                                       axis=-1, keepdims=True), (C, LANES)) for z, h in chains]
        gj = [jnp.broadcast_to(gct[z][S_DA + z * DN_H + h:S_DA + z * DN_H + h + 1, 0:C], (C, C))
              for z, h in chains]
        gam = [jnp.where(incl[z], jnp.exp(jnp.where(incl[z], gi[i][:, 0:C] - gj[i], 0.0)), 0.0)
               for i, (z, h) in enumerate(chains)]
        kb = [k[i] * bb[i] for i in range(nch)]
        kq = [_bdot_nt(jnp.concatenate([kb[i], q[i]], axis=0), k[i]) for i in range(nch)]
        a = [jnp.where(strict[z], kq[i][0:C] * gam[i], 0.0) for i, (z, h) in enumerate(chains)]
        att = [kq[i][C:2 * C] * gam[i] for i in range(nch)]
        nt = [-a[i] for i in range(nch)]
        pw = a
        for _ in range(5):
            pw = [_dot3(pw[i], pw[i]) for i in range(nch)]
            nt = [nt[i] + pw[i] + _dot3(nt[i], pw[i]) for i in range(nch)]
        eg = [jnp.exp(gi[i]) for i in range(nch)]
        rhs = [jnp.concatenate([v[i] * bb[i], kb[i] * eg[i]], axis=1) for i in range(nch)]
        uw = [rhs[i] + _bdot(nt[i], rhs[i]) for i in range(nch)]
        s = [s_sc[z, h] for z, h in chains]
        ws = [_bdot(jnp.concatenate([uw[i][:, LANES:], q[i] * eg[i]], axis=0), s[i])
              for i in range(nch)]
        v_new = [uw[i][:, 0:LANES] - ws[i][0:C] for i in range(nch)]
        o = [ws[i][C:2 * C] + _bdot(att[i], v_new[i]) for i in range(nch)]
        for i, (z, h) in enumerate(chains):
            osc = of_sc if z == 0 else ob_sc
            osc[rows[z], h * LANES:(h + 1) * LANES] = o[i]
        gt = [gi[i][last[z]:last[z] + 1, :] for i, (z, h) in enumerate(chains)]
        upd = [_bdot_tn(k[i] * jnp.exp(gt[i] - gi[i]), v_new[i]) for i in range(nch)]
        for i, (z, h) in enumerate(chains):
            s_sc[z, h] = s[i] * jnp.exp(gt[i]) + upd[i]
        return carry

    lax.fori_loop(0, n, body, 0)

    ng = ng_ref[...]

    def fin(c, carry):
        r0 = pl.multiple_of(c * C, C)
        rows = pl.ds(r0, C)
        for h in range(DN_H):
            cols = slice(h * LANES, (h + 1) * LANES)
            o = of_sc[rows, cols] + ob_sc[rows, cols]
            o_ref[rows, cols] = (_rms(o, ng) * _silu(dz_ref[rows, cols])).astype(BF16)
        return carry

    lax.fori_loop(0, n, fin, 0)
    so_ref[...] = s_sc[...]


def _dn(p_main, p_small, conv_w, alog_row, dtb_row, ng, s0, layer, L, n_seq, row_blk0):
    has_state = s0 is not None
    in_specs = [
        pl.BlockSpec((L, 1536), lambda b: (row_blk0 + b, P_DN // 1536)),
        pl.BlockSpec((L, 512), lambda b: (row_blk0 + b, P_DZ // 512)),
        pl.BlockSpec((L, LANES), lambda b: (row_blk0 + b, 0)),
        pl.BlockSpec((DN_CONV, 3 * DN_H * DN_DK), lambda b: (0, 0)),
        pl.BlockSpec((1, LANES), lambda b: (0, 0)),
        pl.BlockSpec((1, LANES), lambda b: (0, 0)),
        pl.BlockSpec((1, DN_DV), lambda b: (0, 0)),
    ]
    args = [p_main, p_main, p_small, conv_w, alog_row, dtb_row, ng]
    if has_state:
        in_specs.append(pl.BlockSpec((None, None, 2, DN_H, DN_DK, DN_DV),
                                     lambda b: (b, layer, 0, 0, 0, 0)))
        args.append(s0)
    return pl.pallas_call(
        functools.partial(_dn_kernel, L, has_state),
        grid=(n_seq,),
        in_specs=in_specs,
        out_specs=[
            pl.BlockSpec((L, 512), lambda b: (b, 0)),
            pl.BlockSpec((None, 2, DN_H, DN_DK, DN_DV), lambda b: (b, 0, 0, 0, 0)),
        ],
        out_shape=[
            jax.ShapeDtypeStruct((n_seq * L, 512), BF16),
            jax.ShapeDtypeStruct((n_seq, 2, DN_H, DN_DK, DN_DV), F32),
        ],
        scratch_shapes=[
            pltpu.VMEM((L, 1536), F32),
            pltpu.VMEM((L, LANES), F32),
            pltpu.VMEM((2 * DN_H, L, LANES), F32),
            pltpu.VMEM((L, 512), F32),
            pltpu.VMEM((L, 512), F32),
            pltpu.VMEM((2, DN_H, DN_DK, DN_DV), F32),
        ],
        compiler_params=_cparams(("arbitrary",)),
        name="dn_lat" if has_state else "dn_ctx",
    )(*args)


def _prep_w_in(w_in_l):
    off = np.cumsum((0, 256, 256, 512, 512, 32, 1024, 256, 256, 1536, 512, 8, 8))
    gq, gk, gv, gr, glr, aq, ak, av, dqkv, dz, da, db = [
        w_in_l[:, int(off[i]):int(off[i + 1])] for i in range(12)]
    main = jnp.concatenate([aq, ak, av, gq, gk, gv, gr, dqkv, dz], axis=1).astype(BF16)
    pad = jnp.zeros((D_MODEL, LANES - 48), w_in_l.dtype)
    small = jnp.concatenate([glr, da, db, pad], axis=1).astype(BF16)
    return main, small


def _lane_row(vals, offset):
    return jnp.zeros((1, LANES), F32).at[0, offset:offset + vals.shape[0]].set(vals.astype(F32))


def kernel(x_prompt, x_sample, cache_k, cache_v, state_gla, state_dn, c, c_ctx, norm1_g, norm2_g,
           w_mod, b_mod, w_in, gla_w2, gla_b, gla_norm_g, q_norm_g, k_norm_g, dn_conv, dn_a_log,
           dn_dt_bias, dn_norm_g, w_out, w_ff1, w_ff2):
    n_ctx, n_lat = x_prompt.shape[0], x_sample.shape[0]
    depth = w_in.shape[0]
    n_ctx_rows = n_ctx * SEQ
    assert n_ctx_rows % DEC_SEQ == 0 and n_lat <= 7
    lat_blk0 = n_ctx_rows // DEC_SEQ

    xs = [x_prompt.reshape(n_ctx_rows, D_MODEL), x_sample.reshape(n_lat * DEC_SEQ, D_MODEL)]
    cond8 = jnp.concatenate([c_ctx[None, :], c, jnp.zeros((7 - n_lat, D_MODEL), F32)], axis=0)
    mods = _modulation(cond8, w_mod, b_mod).reshape(depth, 8, N_MOD, D_MODEL)

    cos, sin = _rope_tables()
    gla_consts = _gla_consts()
    ck = cache_k.reshape(n_lat, depth, PAST_LEN, ATT_HKV * HEAD_DIM)
    cv = cache_v.reshape(n_lat, depth, PAST_LEN, ATT_HKV * HEAD_DIM)

    nks, nvs, sgs, sds = [], [], [], []
    for l in range(depth):
        w_main, w_small = _prep_w_in(w_in[l])
        p_main, p_small = _inproj(xs, mods[l], norm1_g[l][None, :], w_main, w_small, n_ctx_rows)

        qg, kg = q_norm_g[l][None, :], k_norm_g[l][None, :]
        oa_c, nk, nv = _att_ctx(p_main, qg, kg, n_ctx)
        oa_l = _att_lat(p_main, qg, kg, cos, sin, ck, cv, l, n_lat, lat_blk0)

        w2p = jnp.zeros((2, LANES, GLA_H * GLA_DK), F32)
        w2p = w2p.at[0, 0:GLA_LR].set(gla_w2[l, 0]).at[1, GLA_LR:2 * GLA_LR].set(gla_w2[l, 1])
        gbias = gla_b[l][:, None, :]
        gng = gla_norm_g[l][None, :]
        og_c, sg = _gla(p_main, p_small, w2p, gbias, gng, gla_consts, None, l, SEQ, n_ctx, 0)
        og_l, _ = _gla(p_main, p_small, w2p, gbias, gng, gla_consts, state_gla, l,
                       DEC_SEQ, n_lat, lat_blk0)

        alog_row = _lane_row(dn_a_log[l].reshape(-1), S_DA)
        dtb_row = _lane_row(dn_dt_bias[l].reshape(-1), S_DA)
        dng = dn_norm_g[l][None, :]
        od_c, sd = _dn(p_main, p_small, dn_conv[l], alog_row, dtb_row, dng, None, l,
                       SEQ, n_ctx, 0)
        od_l, _ = _dn(p_main, p_small, dn_conv[l], alog_row, dtb_row, dng, state_dn, l,
                      DEC_SEQ, n_lat, lat_blk0)

        x1, h2 = _outproj(xs, [og_c, og_l], [oa_c, oa_l], [od_c, od_l], mods[l],
                          norm2_g[l][None, :], w_out[l].astype(BF16), n_ctx_rows)
        xs = _ffn(h2, x1, mods[l], w_ff1[l].astype(BF16), w_ff2[l].astype(BF16), n_ctx_rows,
                  split_out=(l == depth - 1))

        nks.append(nk.reshape(n_ctx, SEQ, ATT_HKV, HEAD_DIM))
        nvs.append(nv.reshape(n_ctx, SEQ, ATT_HKV, HEAD_DIM))
        sgs.append(sg)
        sds.append(sd)

    y_prompt = xs[0].reshape(n_ctx, SEQ, D_MODEL)
    y_sample = xs[1].reshape(n_lat, DEC_SEQ, D_MODEL)
    return (y_prompt, y_sample, jnp.stack(nks, axis=1), jnp.stack(nvs, axis=1),
            jnp.stack(sgs, axis=1), jnp.stack(sds, axis=1))
```

```python
import functools
import math

import numpy as np
import jax
import jax.numpy as jnp
from jax import lax
from jax.experimental import pallas as pl
from jax.experimental.pallas import tpu as pltpu

F32 = jnp.float32
BF16 = jnp.bfloat16

D_MODEL = 2048
SEQ = 256
DEC_SEQ = 1024
PAST_LEN = 256
GRID_W = 64
HEAD_DIM = 128
EPS = 1e-6
GLA_H = 4
GLA_DK = 64
GLA_DV = 128
GLA_LR = 16
GLA_GATE_NORM = 16.0
ATT_HQ = 8
ATT_HKV = 2
ROPE_THETA = 10000.0
DN_H = 4
DN_DK = 128
DN_DV = 128
DN_CONV = 3
CHUNK = 64
FF = 4 * D_MODEL
N_MOD = 6
LANES = 128

P_AQ = 0
P_AKV = 1024
P_GLA = 1536
P_DN = 3072
P_DZ = 4608
P_MAIN = 5120
S_DA = 32
S_DB = 40

VMEM_LIMIT = 56 * 1024 * 1024


def _cparams(sem):
    return pltpu.CompilerParams(dimension_semantics=sem, vmem_limit_bytes=VMEM_LIMIT)


def _bdot(a, b):
    return jnp.dot(a.astype(BF16), b.astype(BF16), preferred_element_type=F32)


def _bdot_nt(a, b):
    return lax.dot_general(a.astype(BF16), b.astype(BF16), (((1,), (1,)), ((), ())),
                           preferred_element_type=F32)


def _bdot_tn(a, b):
    return lax.dot_general(a.astype(BF16), b.astype(BF16), (((0,), (0,)), ((), ())),
                           preferred_element_type=F32)


def _split3(x):
    hi = x.astype(BF16)
    r = x - hi.astype(F32)
    mid = r.astype(BF16)
    lo = (r - mid.astype(F32)).astype(BF16)
    return hi, mid, lo


def _dot_exact_lhs(lhs_bf16, x):
    hi, mid, lo = _split3(x)
    d = functools.partial(jnp.dot, preferred_element_type=F32)
    return d(lhs_bf16, hi) + d(lhs_bf16, mid) + d(lhs_bf16, lo)


def _dot_exact_lhs_wide(lhs_bf16, x):
    n = x.shape[1]
    hi = x.astype(BF16)
    mid = (x - hi.astype(F32)).astype(BF16)
    r = jnp.dot(lhs_bf16, jnp.concatenate([hi, mid], axis=1), preferred_element_type=F32)
    return r[:, 0:n] + r[:, n:2 * n]


def _dot3(a, b):
    ah, am, _ = _split3(a)
    bh, bm, _ = _split3(b)
    d = functools.partial(jnp.dot, preferred_element_type=F32)
    return d(ah, bh) + (d(ah, bm) + d(am, bh))


def _silu(x):
    return x * jax.nn.sigmoid(x)


def _softplus(x):
    return jnp.maximum(x, 0.0) + jnp.log1p(jnp.exp(-jnp.abs(x)))


def _rms(x, g):
    return x * lax.rsqrt(jnp.mean(x * x, axis=-1, keepdims=True) + EPS) * g


def _mods_kernel(c_ref, w_ref, b_ref, o_ref):
    o_ref[...] = _bdot(_silu(c_ref[...]), w_ref[...]) + b_ref[...]


def _modulation(cond8, w_mod, b_mod):
    depth = w_mod.shape[0]
    n_out = N_MOD * D_MODEL
    tn = 1024
    return pl.pallas_call(
        _mods_kernel,
        grid=(depth, n_out // tn),
        in_specs=[
            pl.BlockSpec((8, D_MODEL), lambda l, j: (0, 0)),
            pl.BlockSpec((None, D_MODEL, tn), lambda l, j: (l, 0, j)),
            pl.BlockSpec((None, 1, tn), lambda l, j: (l, 0, j)),
        ],
        out_specs=pl.BlockSpec((None, 8, tn), lambda l, j: (l, 0, j)),
        out_shape=jax.ShapeDtypeStruct((depth, 8, n_out), F32),
        compiler_params=_cparams(("arbitrary", "arbitrary")),
        name="modulation",
    )(cond8, w_mod, b_mod.reshape(depth, 1, n_out))


def _group_of_tile(i, tm, n_ctx_rows):
    n_ctx_tiles = n_ctx_rows // tm
    per_lat = DEC_SEQ // tm
    return jnp.where(i < n_ctx_tiles, 0, 1 + (i - n_ctx_tiles) // per_lat)


def _row_specs(arrays, width, tm, nct):
    if len(arrays) == 1:
        return [pl.BlockSpec((tm, width), lambda i, *_: (i, 0))]
    return [pl.BlockSpec((tm, width), lambda i, *_: (jnp.minimum(i, nct - 1), 0)),
            pl.BlockSpec((tm, width), lambda i, *_: (jnp.maximum(i - nct, 0), 0))]


def _row_load(refs, nct):
    if len(refs) == 1:
        return refs[0][...]
    return jnp.where(pl.program_id(0) < nct, refs[0][...], refs[1][...])


def _inproj_kernel(nct, n_x, *refs):
    x_refs = refs[:n_x]
    mod_ref, g_ref, w_ref, ws_ref, p_ref, ps_ref, h_sc = refs[n_x:]

    @pl.when(pl.program_id(1) == 0)
    def _():
        m = mod_ref[...]
        h = _rms(_row_load(x_refs, nct), g_ref[...]) * (1.0 + m[1:2]) + m[0:1]
        hb = h.astype(BF16)
        h_sc[...] = hb
        ps_ref[...] = jnp.dot(hb, ws_ref[...], preferred_element_type=F32)

    p_ref[...] = jnp.dot(h_sc[...], w_ref[...], preferred_element_type=F32)


def _inproj(xs, mods_l, g1, w_main, w_small, layer, n_ctx_rows, tm=1024, tn=512):
    rows = sum(x.shape[0] for x in xs)
    nct = n_ctx_rows // tm
    grp = functools.partial(_group_of_tile, tm=tm, n_ctx_rows=n_ctx_rows)
    return pl.pallas_call(
        functools.partial(_inproj_kernel, nct, len(xs)),
        grid=(rows // tm, P_MAIN // tn),
        in_specs=_row_specs(xs, D_MODEL, tm, nct) + [
            pl.BlockSpec((None, N_MOD, D_MODEL), lambda i, j: (grp(i), 0, 0)),
            pl.BlockSpec((1, D_MODEL), lambda i, j: (0, 0)),
            pl.BlockSpec((None, D_MODEL, tn), lambda i, j: (layer, 0, j)),
            pl.BlockSpec((None, D_MODEL, LANES), lambda i, j: (layer, 0, 0)),
        ],
        out_specs=[
            pl.BlockSpec((tm, tn), lambda i, j: (i, j)),
            pl.BlockSpec((tm, LANES), lambda i, j: (i, 0)),
        ],
        out_shape=[
            jax.ShapeDtypeStruct((rows, P_MAIN), F32),
            jax.ShapeDtypeStruct((rows, LANES), F32),
        ],
        scratch_shapes=[pltpu.VMEM((tm, D_MODEL), BF16)],
        compiler_params=_cparams(("arbitrary", "arbitrary")),
        name="inproj",
    )(*xs, mods_l, g1, w_main, w_small)


def _outproj_kernel(nct, n_x, *refs):
    x_refs = refs[:n_x]
    (ogc_ref, ogl_ref, oac_ref, oal_ref, odc_ref, odl_ref, mod_ref, g_ref, w_ref,
     x1_ref, h2_ref) = refs[n_x:]
    m = mod_ref[...]
    d = functools.partial(jnp.dot, preferred_element_type=F32)
    mix = (d(_row_load((ogc_ref, ogl_ref), nct), w_ref[0:512, :])
           + d(_row_load((oac_ref, oal_ref), nct), w_ref[512:1536, :])
           + d(_row_load((odc_ref, odl_ref), nct), w_ref[1536:2048, :]))
    x1 = _row_load(x_refs, nct) + m[2:3] * mix
    x1_ref[...] = x1
    h2_ref[...] = (_rms(x1, g_ref[...]) * (1.0 + m[4:5]) + m[3:4]).astype(BF16)


def _outproj(xs, o_gla, o_att, o_dn, mods_l, g2, w_out, layer, n_ctx_rows, tm=512):
    rows = sum(x.shape[0] for x in xs)
    nct = n_ctx_rows // tm
    grp = functools.partial(_group_of_tile, tm=tm, n_ctx_rows=n_ctx_rows)
    return pl.pallas_call(
        functools.partial(_outproj_kernel, nct, len(xs)),
        grid=(rows // tm,),
        in_specs=(_row_specs(xs, D_MODEL, tm, nct) + _row_specs(o_gla, 512, tm, nct)
                  + _row_specs(o_att, 1024, tm, nct) + _row_specs(o_dn, 512, tm, nct) + [
                      pl.BlockSpec((None, N_MOD, D_MODEL), lambda i: (grp(i), 0, 0)),
                      pl.BlockSpec((1, D_MODEL), lambda i: (0, 0)),
                      pl.BlockSpec((None, D_MODEL, D_MODEL), lambda i: (layer, 0, 0),
                                   pipeline_mode=pl.Buffered(1)),
                  ]),
        out_specs=[
            pl.BlockSpec((tm, D_MODEL), lambda i: (i, 0)),
            pl.BlockSpec((tm, D_MODEL), lambda i: (i, 0)),
        ],
        out_shape=[
            jax.ShapeDtypeStruct((rows, D_MODEL), F32),
            jax.ShapeDtypeStruct((rows, D_MODEL), BF16),
        ],
        compiler_params=_cparams(("arbitrary",)),
        name="outproj",
    )(*xs, *o_gla, *o_att, *o_dn, mods_l, g2, w_out)


def _ffn_kernel(nct, n_out, h_ref, x1_ref, mod_ref, w1_ref, w2_ref, *refs):
    y_refs, acc_sc = refs[:n_out], refs[n_out]
    i, j = pl.program_id(0), pl.program_id(1)
    last = pl.num_programs(1) - 1
    tf = w1_ref.shape[1]

    @pl.when(j == 0)
    def _():
        acc_sc[...] = jnp.zeros(acc_sc.shape, F32)

    h = h_ref[...]
    for half in range(2):
        cols = slice(half * (tf // 2), (half + 1) * (tf // 2))
        t = jnp.dot(h, w1_ref[:, cols], preferred_element_type=F32)
        t = jnp.square(jnp.maximum(t, 0.0))
        acc_sc[...] += jnp.dot(t.astype(BF16), w2_ref[cols, :], preferred_element_type=F32)

    def result():
        return x1_ref[...] + mod_ref[5:6, :] * acc_sc[...]

    if n_out == 1:
        @pl.when(j == last)
        def _():
            y_refs[0][...] = result()
    else:
        @pl.when((j == last) & (i < nct))
        def _():
            y_refs[0][...] = result()

        @pl.when((j == last) & (i >= nct))
        def _():
            y_refs[1][...] = result()


def _ffn(h2, x1, mods_l, w1, w2, layer, n_ctx_rows, split_out, tm=512, tf=1024):
    rows = x1.shape[0]
    nct = n_ctx_rows // tm
    grp = functools.partial(_group_of_tile, tm=tm, n_ctx_rows=n_ctx_rows)
    assert FF // tf >= 2
    if split_out:
        out_specs = [pl.BlockSpec((tm, D_MODEL), lambda i, j: (jnp.minimum(i, nct - 1), 0)),
                     pl.BlockSpec((tm, D_MODEL), lambda i, j: (jnp.maximum(i - nct, 0), 0))]
        out_shape = [jax.ShapeDtypeStruct((n_ctx_rows, D_MODEL), F32),
                     jax.ShapeDtypeStruct((rows - n_ctx_rows, D_MODEL), F32)]
    else:
        out_specs = [pl.BlockSpec((tm, D_MODEL), lambda i, j: (i, 0))]
        out_shape = [jax.ShapeDtypeStruct((rows, D_MODEL), F32)]
    return pl.pallas_call(
        functools.partial(_ffn_kernel, nct, len(out_specs)),
        grid=(rows // tm, FF // tf),
        in_specs=[
            pl.BlockSpec((tm, D_MODEL), lambda i, j: (i, 0)),
            pl.BlockSpec((tm, D_MODEL), lambda i, j: (i, 0)),
            pl.BlockSpec((None, N_MOD, D_MODEL), lambda i, j: (grp(i), 0, 0)),
            pl.BlockSpec((None, D_MODEL, tf), lambda i, j: (layer, 0, j)),
            pl.BlockSpec((None, tf, D_MODEL), lambda i, j: (layer, j, 0)),
        ],
        out_specs=out_specs,
        out_shape=out_shape,
        scratch_shapes=[pltpu.VMEM((tm, D_MODEL), F32)],
        compiler_params=_cparams(("arbitrary", "arbitrary")),
        name="ffn",
    )(h2, x1, mods_l, w1, w2)


def _rope_tables():
    half = HEAD_DIM // 2
    pos = np.arange(DEC_SEQ)
    row = (pos // GRID_W).astype(np.float32)
    col = (pos % GRID_W).astype(np.float32)
    inv = (ROPE_THETA ** (-np.arange(0, half, 2, dtype=np.float32) / half)).astype(np.float32)
    ar = row[:, None] * inv[None, :]
    ac = col[:, None] * inv[None, :]
    cos = np.concatenate([np.cos(ar), np.cos(ar), np.cos(ac), np.cos(ac)], axis=1)
    sin = np.concatenate([-np.sin(ar), np.sin(ar), -np.sin(ac), np.sin(ac)], axis=1)
    return jnp.asarray(cos, F32), jnp.asarray(sin, F32)


def _rope(x, cos, sin):
    lane = lax.broadcasted_iota(jnp.int32, x.shape, 1)
    first = (lane % 64) < 32
    partner = jnp.where(first, pltpu.roll(x, 96, 1), pltpu.roll(x, 32, 1))
    return x * cos + partner * sin


ATT_G = ATT_HQ // ATT_HKV


def _softmax_pv(q, kb, vb):
    s = lax.dot_general(q.astype(BF16), kb, (((1,), (1,)), ((), ())), preferred_element_type=F32)
    m = jnp.max(s, axis=-1, keepdims=True)
    p = jnp.exp(s - m)
    l = jnp.sum(p, axis=-1, keepdims=True)
    return jnp.dot(p.astype(BF16), vb, preferred_element_type=F32) / l


def _group_attention(q_heads, kb, vb, store):
    for g, q in enumerate(q_heads):
        store(g, _softmax_pv(q, kb, vb).astype(BF16))


def _att_ctx_kernel(aq_ref, akv_ref, qg_ref, kg_ref, o_ref, nk_ref, nv_ref):
    qg = qg_ref[...] * (HEAD_DIM ** -0.5)
    kg = kg_ref[...]
    for hk in range(ATT_HKV):
        ks = slice(hk * HEAD_DIM, (hk + 1) * HEAD_DIM)
        kn = _rms(akv_ref[:, ks], kg)
        v = akv_ref[:, 256 + hk * HEAD_DIM:256 + (hk + 1) * HEAD_DIM]
        nk_ref[:, ks] = kn
        nv_ref[:, ks] = v

        def head_cols(g, hk=hk):
            return slice((hk * ATT_G + g) * HEAD_DIM, (hk * ATT_G + g + 1) * HEAD_DIM)

        def store(g, val):
            o_ref[:, head_cols(g)] = val

        qs = [_rms(aq_ref[:, head_cols(g)], qg) for g in range(ATT_G)]
        _group_attention(qs, kn.astype(BF16), v.astype(BF16), store)


def _att_lat_kernel(aq_ref, akv_ref, qg_ref, kg_ref, cos_ref, sin_ref, ck_ref, cv_ref, o_ref,
                    k_sc, v_sc):
    qg = qg_ref[...] * (HEAD_DIM ** -0.5)
    kg = kg_ref[...]
    qb = 256
    for hk in range(ATT_HKV):
        ks = slice(hk * HEAD_DIM, (hk + 1) * HEAD_DIM)
        k_sc[hk, 0:PAST_LEN, :] = ck_ref[:, ks].astype(BF16)
        v_sc[hk, 0:PAST_LEN, :] = cv_ref[:, ks].astype(BF16)
        kn = _rope(_rms(akv_ref[:, ks], kg), cos_ref[...], sin_ref[...])
        k_sc[hk, PAST_LEN:, :] = kn.astype(BF16)
        v_sc[hk, PAST_LEN:, :] = akv_ref[:, 256 + hk * HEAD_DIM:256 + (hk + 1) * HEAD_DIM].astype(BF16)

    def body(r, carry):
        r0 = pl.multiple_of(r * qb, qb)
        cos = cos_ref[pl.ds(r0, qb), :]
        sin = sin_ref[pl.ds(r0, qb), :]
        for hk in range(ATT_HKV):
            def head_cols(g, hk=hk):
                return slice((hk * ATT_G + g) * HEAD_DIM, (hk * ATT_G + g + 1) * HEAD_DIM)

            def store(g, val):
                o_ref[pl.ds(r0, qb), head_cols(g)] = val

            qs = [_rope(_rms(aq_ref[pl.ds(r0, qb), head_cols(g)], qg), cos, sin)
                  for g in range(ATT_G)]
            _group_attention(qs, k_sc[hk], v_sc[hk], store)
        return carry

    lax.fori_loop(0, DEC_SEQ // qb, body, 0)


def _att_ctx(p_main, qg, kg, n_seq):
    L = SEQ
    return pl.pallas_call(
        _att_ctx_kernel,
        grid=(n_seq,),
        in_specs=[
            pl.BlockSpec((L, 1024), lambda b: (b, P_AQ // 1024)),
            pl.BlockSpec((L, 512), lambda b: (b, P_AKV // 512)),
            pl.BlockSpec((1, HEAD_DIM), lambda b: (0, 0)),
            pl.BlockSpec((1, HEAD_DIM), lambda b: (0, 0)),
        ],
        out_specs=[
            pl.BlockSpec((L, 1024), lambda b: (b, 0)),
            pl.BlockSpec((None, L, 256), lambda b: (b, 0, 0)),
            pl.BlockSpec((None, L, 256), lambda b: (b, 0, 0)),
        ],
        out_shape=[
            jax.ShapeDtypeStruct((n_seq * L, 1024), BF16),
            jax.ShapeDtypeStruct((n_seq, L, 256), F32),
            jax.ShapeDtypeStruct((n_seq, L, 256), F32),
        ],
        compiler_params=_cparams(("arbitrary",)),
        name="att_ctx",
    )(p_main, p_main, qg, kg)


def _att_lat(p_main, qg, kg, cos, sin, ck, cv, layer, n_seq, row_blk0):
    L = DEC_SEQ
    return pl.pallas_call(
        _att_lat_kernel,
        grid=(n_seq,),
        in_specs=[
            pl.BlockSpec((L, 1024), lambda b: (row_blk0 + b, P_AQ // 1024)),
            pl.BlockSpec((L, 512), lambda b: (row_blk0 + b, P_AKV // 512)),
            pl.BlockSpec((1, HEAD_DIM), lambda b: (0, 0)),
            pl.BlockSpec((1, HEAD_DIM), lambda b: (0, 0)),
            pl.BlockSpec((L, HEAD_DIM), lambda b: (0, 0)),
            pl.BlockSpec((L, HEAD_DIM), lambda b: (0, 0)),
            pl.BlockSpec((None, None, PAST_LEN, 256), lambda b: (b, layer, 0, 0)),
            pl.BlockSpec((None, None, PAST_LEN, 256), lambda b: (b, layer, 0, 0)),
        ],
        out_specs=pl.BlockSpec((L, 1024), lambda b: (b, 0)),
        out_shape=jax.ShapeDtypeStruct((n_seq * L, 1024), BF16),
        scratch_shapes=[
            pltpu.VMEM((ATT_HKV, PAST_LEN + L, HEAD_DIM), BF16),
            pltpu.VMEM((ATT_HKV, PAST_LEN + L, HEAD_DIM), BF16),
        ],
        compiler_params=_cparams(("arbitrary",)),
        name="att_lat",
    )(p_main, p_main, qg, kg, cos, sin, ck, cv)


_GLA_LEVELS = (32, 16, 8, 4, 2, 1)


def _gla_consts():
    C = CHUNK
    i = np.arange(C)[:, None]
    t = np.arange(C)[None, :]
    mats = [t <= i, t > i]
    masks = [i == t]
    for s in _GLA_LEVELS:
        m = (i // s) * s
        if s > 1:
            mats.append((t > m) & (t <= i))
        mats.append((t > i) & (t <= m + s))
        masks.append((i // (2 * s) == t // (2 * s)) & ((i // s) % 2 == 1) & ((t // s) % 2 == 0))
    fwd = np.concatenate(mats, 0)
    bwd = np.concatenate([mm[::-1, ::-1] for mm in mats], 0)
    mstack = np.stack([fwd, bwd]).astype(np.float32)
    mf = np.stack([np.concatenate([mm, mm], 0) for mm in masks])
    mb = np.stack([np.concatenate([mm[::-1, ::-1], mm[::-1, ::-1]], 0) for mm in masks])
    lmask = np.stack([mf, mb]).astype(np.float32)
    return jnp.asarray(mstack, BF16), jnp.asarray(lmask, F32)


def _gla_kernel(L, has_state, *refs):
    if has_state:
        (pg_ref, ps_ref, w2_ref, gb_ref, ng_ref, ms_ref, lm_ref, s0_ref,
         o_ref, so_ref, la_sc, of_sc, ob_sc, st_sc) = refs
    else:
        (pg_ref, ps_ref, w2_ref, gb_ref, ng_ref, ms_ref, lm_ref,
         o_ref, so_ref, la_sc, of_sc, ob_sc, st_sc) = refs
    n = L // CHUNK
    C = CHUNK

    ps = ps_ref[...]
    for z in range(2):
        x = _dot3(ps, w2_ref[z]) + gb_ref[z]
        la_sc[z] = (jnp.minimum(x, 0.0) - jnp.log1p(jnp.exp(-jnp.abs(x)))) * (1.0 / GLA_GATE_NORM)

    if has_state:
        for z in range(2):
            for p in range(2):
                pair = jnp.concatenate([s0_ref[z, 2 * p], s0_ref[z, 2 * p + 1]], axis=0)
                st_sc[z, p] = jnp.transpose(pair)
    else:
        st_sc[...] = jnp.zeros(st_sc.shape, F32)

    lane = lax.broadcasted_iota(jnp.int32, (C, LANES), 1)
    first = lane < 64

    def stack_heads(t):
        return jnp.concatenate([jnp.where(first, t, 0.0), jnp.where(first, 0.0, t)], axis=0)

    chains = [(z, p) for z in range(2) for p in range(2)]
    nch = len(chains)

    def body(c, carry):
        rows = [pl.ds(pl.multiple_of(c * C, C), C), pl.ds(pl.multiple_of((n - 1 - c) * C, C), C)]
        last = [C - 1, 0]
        q = [pg_ref[rows[z], p * LANES:(p + 1) * LANES] * (GLA_DK ** -0.5) for z, p in chains]
        k = [pg_ref[rows[z], 256 + p * LANES:256 + (p + 1) * LANES] for z, p in chains]
        e = [jnp.exp(_dot_exact_lhs_wide(ms_ref[z], la_sc[z, rows[z], p * LANES:(p + 1) * LANES]))
             for z, p in chains]
        st = [st_sc[z, p] for z, p in chains]
        a2 = [_bdot_nt(stack_heads(q[i]), k[i]) * lm_ref[z, 0] for i, (z, p) in enumerate(chains)]
        blk = 2
        for li, s in enumerate(_GLA_LEVELS):
            if s > 1:
                qs = [q[i] * e[i][blk * C:(blk + 1) * C] for i in range(nch)]
                blk += 1
            else:
                qs = q
            ks = [k[i] * e[i][blk * C:(blk + 1) * C] for i in range(nch)]
            blk += 1
            a2 = [a2[i] + _bdot_nt(stack_heads(qs[i]), ks[i]) * lm_ref[z, li + 1]
                  for i, (z, p) in enumerate(chains)]
        inter = [_bdot_nt(stack_heads(q[i] * e[i][0:C]), st[i]) for i in range(nch)]
        kl = [k[i] * e[i][C:2 * C] for i in range(nch)]
        for i, (z, p) in enumerate(chains):
            osc = of_sc if z == 0 else ob_sc
            zs = []
            for hh in range(2):
                h = 2 * p + hh
                v = pg_ref[rows[z], 512 + h * LANES:512 + (h + 1) * LANES]
                o = _bdot(a2[i][hh * C:(hh + 1) * C], v) + inter[i][hh * C:(hh + 1) * C]
                osc[rows[z], h * LANES:(h + 1) * LANES] = o
                zs.append(_bdot_tn(v, kl[i]))
            st_sc[z, p] = (st[i] * e[i][last[z]:last[z] + 1, :]
                           + jnp.where(first[0:1], zs[0], zs[1]))
        return carry

    lax.fori_loop(0, n, body, 0)

    ng = ng_ref[...]

    def fin(c, carry):
        r0 = pl.multiple_of(c * C, C)
        rows = pl.ds(r0, C)
        for h in range(GLA_H):
            cols = slice(h * LANES, (h + 1) * LANES)
            o = of_sc[rows, cols] + ob_sc[rows, cols]
            gate = _silu(pg_ref[rows, 1024 + h * LANES:1024 + (h + 1) * LANES])
            o_ref[rows, cols] = (_rms(o, ng) * gate).astype(BF16)
        return carry

    lax.fori_loop(0, n, fin, 0)
    for z in range(2):
        for p in range(2):
            pair = jnp.transpose(st_sc[z, p])
            so_ref[z, 2 * p] = pair[0:GLA_DK]
            so_ref[z, 2 * p + 1] = pair[GLA_DK:2 * GLA_DK]


def _gla(p_main, p_small, w2p, gbias, ng, consts, s0, layer, L, n_seq, row_blk0):
    mstack, lmask = consts
    has_state = s0 is not None
    state_blk = (None, 2, GLA_H, GLA_DK, GLA_DV)
    in_specs = [
        pl.BlockSpec((L, 1536), lambda b: (row_blk0 + b, P_GLA // 1536)),
        pl.BlockSpec((L, LANES), lambda b: (row_blk0 + b, 0)),
        pl.BlockSpec((2, LANES, 256), lambda b: (0, 0, 0)),
        pl.BlockSpec((2, 1, 256), lambda b: (0, 0, 0)),
        pl.BlockSpec((1, GLA_DV), lambda b: (0, 0)),
        pl.BlockSpec(mstack.shape, lambda b: (0, 0, 0)),
        pl.BlockSpec(lmask.shape, lambda b: (0, 0, 0, 0)),
    ]
    args = [p_main, p_small, w2p, gbias, ng, mstack, lmask]
    if has_state:
        in_specs.append(pl.BlockSpec((None,) + state_blk, lambda b: (b, layer, 0, 0, 0, 0)))
        args.append(s0)
    return pl.pallas_call(
        functools.partial(_gla_kernel, L, has_state),
        grid=(n_seq,),
        in_specs=in_specs,
        out_specs=[
            pl.BlockSpec((L, 512), lambda b: (b, 0)),
            pl.BlockSpec(state_blk, lambda b: (b, 0, 0, 0, 0)),
        ],
        out_shape=[
            jax.ShapeDtypeStruct((n_seq * L, 512), BF16),
            jax.ShapeDtypeStruct((n_seq, 2, GLA_H, GLA_DK, GLA_DV), F32),
        ],
        scratch_shapes=[
            pltpu.VMEM((2, L, 256), F32),
            pltpu.VMEM((L, 512), F32),
            pltpu.VMEM((L, 512), F32),
            pltpu.VMEM((2, 2, GLA_DV, LANES), F32),
        ],
        compiler_params=_cparams(("arbitrary",)),
        name="gla_lat" if has_state else "gla_ctx",
    )(*args)


def _dn_kernel(L, has_state, *refs):
    if has_state:
        (pd_ref, dz_ref, ps_ref, cw_ref, al_ref, dtb_ref, ng_ref, s0_ref,
         o_ref, so_ref, qkv_sc, g_sc, bb_sc, of_sc, ob_sc, s_sc) = refs
    else:
        (pd_ref, dz_ref, ps_ref, cw_ref, al_ref, dtb_ref, ng_ref,
         o_ref, so_ref, qkv_sc, g_sc, bb_sc, of_sc, ob_sc, s_sc) = refs
    n = L // CHUNK
    C = CHUNK

    rowi = lax.broadcasted_iota(jnp.int32, (L, LANES), 0)
    not_first = rowi > 0
    not_last = rowi < L - 1

    def conv_tile(j, carry):
        c0 = pl.multiple_of(j * LANES, LANES)
        x = pd_ref[:, pl.ds(c0, LANES)]
        w = cw_ref[:, pl.ds(c0, LANES)]
        prev = jnp.where(not_first, pltpu.roll(x, 1, 0), 0.0)
        nxt = jnp.where(not_last, pltpu.roll(x, L - 1, 0), 0.0)
        y = _silu(prev * w[0:1] + x * w[1:2] + nxt * w[2:3])
        inv = lax.rsqrt(jnp.sum(y * y, axis=-1, keepdims=True) + EPS)
        scale = jnp.where(j < DN_H, inv * (DN_DK ** -0.5), jnp.where(j < 2 * DN_H, inv, 1.0))
        qkv_sc[:, pl.ds(c0, LANES)] = y * scale
        return carry

    lax.fori_loop(0, 3 * DN_H, conv_tile, 0)

    ps = ps_ref[...]
    lane_l = lax.broadcasted_iota(jnp.int32, (L, LANES), 1)
    g_sc[...] = -jnp.exp(al_ref[...]) * _softplus(ps + dtb_ref[...])
    b_all = jax.nn.sigmoid(ps)
    for idx in range(2 * DN_H):
        bcol = jnp.sum(jnp.where(lane_l == S_DB + idx, b_all, 0.0), axis=-1, keepdims=True)
        bb_sc[idx] = jnp.broadcast_to(bcol, (L, LANES))

    if has_state:
        s_sc[...] = s0_ref[...]
    else:
        s_sc[...] = jnp.zeros(s_sc.shape, F32)

    ri = lax.broadcasted_iota(jnp.int32, (C, C), 0)
    ci = lax.broadcasted_iota(jnp.int32, (C, C), 1)
    lane_c =lax.broadcasted_iota(jnp.int32, (C, LANES), 1)

    incl = [ci <= ri, ci >= ri]
    strict = [ci < ri, ci > ri]
    m_b = [m.astype(BF16) for m in incl]
    chains = [(z, h) for z in range(2) for h in range(DN_H)]
    nch = len(chains)

    def body(c, carry):
        rows = [pl.ds(pl.multiple_of(c * C, C), C), pl.ds(pl.multiple_of((n - 1 - c) * C, C), C)]
        last = [C - 1, 0]

        def ld(base, z, h):
            return qkv_sc[rows[z], base + h * LANES:base + (h + 1) * LANES]

        q = [ld(0, z, h) for z, h in chains]
        k = [ld(512, z, h) for z, h in chains]
        v = [ld(1024, z, h) for z, h in chains]
        bb = [bb_sc[z * DN_H + h, rows[z], :] for z, h in chains]
        gc = [_dot_exact_lhs(m_b[z], g_sc[rows[z], :]) for z in range(2)]
        gct = [jnp.transpose(jnp.concatenate([gc[z], gc[z]], axis=0)) for z in range(2)]
        gi = [jnp.broadcast_to(jnp.sum(jnp.where(lane_c == S_DA + z * DN_H + h, gc[z], 0.0),
                                       axis=-1, keepdims=True), (C, LANES)) for z, h in chains]
        gj = [jnp.broadcast_to(gct[z][S_DA + z * DN_H + h:S_DA + z * DN_H + h + 1, 0:C], (C, C))
              for z, h in chains]
        gam = [jnp.where(incl[z], jnp.exp(jnp.where(incl[z], gi[i][:, 0:C] - gj[i], 0.0)), 0.0)
               for i, (z, h) in enumerate(chains)]
        kb = [k[i] * bb[i] for i in range(nch)]
        kq = [_bdot_nt(jnp.concatenate([kb[i], q[i]], axis=0), k[i]) for i in range(nch)]
        a = [jnp.where(strict[z], kq[i][0:C] * gam[i], 0.0) for i, (z, h) in enumerate(chains)]
        att = [kq[i][C:2 * C] * gam[i] for i in range(nch)]
        nt = [-a[i] for i in range(nch)]
        pw = a
        for _ in range(5):
            pw = [_dot3(pw[i], pw[i]) for i in range(nch)]
            nt = [nt[i] + pw[i] + _dot3(nt[i], pw[i]) for i in range(nch)]
        eg = [jnp.exp(gi[i]) for i in range(nch)]
        rhs = [jnp.concatenate([v[i] * bb[i], kb[i] * eg[i]], axis=1) for i in range(nch)]
        uw = [rhs[i] + _bdot(nt[i], rhs[i]) for i in range(nch)]
        s = [s_sc[z, h] for z, h in chains]
        ws = [_bdot(jnp.concatenate([uw[i][:, LANES:], q[i] * eg[i]], axis=0), s[i])
              for i in range(nch)]
        v_new = [uw[i][:, 0:LANES] - ws[i][0:C] for i in range(nch)]
        o = [ws[i][C:2 * C] + _bdot(att[i], v_new[i]) for i in range(nch)]
        for i, (z, h) in enumerate(chains):
            osc = of_sc if z == 0 else ob_sc
            osc[rows[z], h * LANES:(h + 1) * LANES] = o[i]
        gt = [gi[i][last[z]:last[z] + 1, :] for i, (z, h) in enumerate(chains)]
        upd = [_bdot_tn(k[i] * jnp.exp(gt[i] - gi[i]), v_new[i]) for i in range(nch)]
        for i, (z, h) in enumerate(chains):
            s_sc[z, h] = s[i] * jnp.exp(gt[i]) + upd[i]
        return carry

    lax.fori_loop(0, n, body, 0)

    ng = ng_ref[...]

    def fin(c, carry):
        r0 = pl.multiple_of(c * C, C)
        rows = pl.ds(r0, C)
        for h in range(DN_H):
            cols = slice(h * LANES, (h + 1) * LANES)
            o = of_sc[rows, cols] + ob_sc[rows, cols]
            o_ref[rows, cols] = (_rms(o, ng) * _silu(dz_ref[rows, cols])).astype(BF16)
        return carry

    lax.fori_loop(0, n, fin, 0)
    so_ref[...] = s_sc[...]


def _dn(p_main, p_small, conv_w, alog_row, dtb_row, ng, s0, layer, L, n_seq, row_blk0):
    has_state = s0 is not None
    in_specs = [
        pl.BlockSpec((L, 1536), lambda b: (row_blk0 + b, P_DN // 1536)),
        pl.BlockSpec((L, 512), lambda b: (row_blk0 + b, P_DZ // 512)),
        pl.BlockSpec((L, LANES), lambda b: (row_blk0 + b, 0)),
        pl.BlockSpec((DN_CONV, 3 * DN_H * DN_DK), lambda b: (0, 0)),
        pl.BlockSpec((1, LANES), lambda b: (0, 0)),
        pl.BlockSpec((1, LANES), lambda b: (0, 0)),
        pl.BlockSpec((1, DN_DV), lambda b: (0, 0)),
    ]
    args = [p_main, p_main, p_small, conv_w, alog_row, dtb_row, ng]
    if has_state:
        in_specs.append(pl.BlockSpec((None, None, 2, DN_H, DN_DK, DN_DV),
                                     lambda b: (b, layer, 0, 0, 0, 0)))
        args.append(s0)
    return pl.pallas_call(
        functools.partial(_dn_kernel, L, has_state),
        grid=(n_seq,),
        in_specs=in_specs,
        out_specs=[
            pl.BlockSpec((L, 512), lambda b: (b, 0)),
            pl.BlockSpec((None, 2, DN_H, DN_DK, DN_DV), lambda b: (b, 0, 0, 0, 0)),
        ],
        out_shape=[
            jax.ShapeDtypeStruct((n_seq * L, 512), BF16),
            jax.ShapeDtypeStruct((n_seq, 2, DN_H, DN_DK, DN_DV), F32),
        ],
        scratch_shapes=[
            pltpu.VMEM((L, 1536), F32),
            pltpu.VMEM((L, LANES), F32),
            pltpu.VMEM((2 * DN_H, L, LANES), F32),
            pltpu.VMEM((L, 512), F32),
            pltpu.VMEM((L, 512), F32),
            pltpu.VMEM((2, DN_H, DN_DK, DN_DV), F32),
        ],
        compiler_params=_cparams(("arbitrary",)),
        name="dn_lat" if has_state else "dn_ctx",
    )(*args)


def _prep_w_in(w_in):
    off = np.cumsum((0, 256, 256, 512, 512, 32, 1024, 256, 256, 1536, 512, 8, 8))
    gq, gk, gv, gr, glr, aq, ak, av, dqkv, dz, da, db = [
        w_in[:, :, int(off[i]):int(off[i + 1])] for i in range(12)]
    main = jnp.concatenate([aq, ak, av, gq, gk, gv, gr, dqkv, dz], axis=2).astype(BF16)
    pad = jnp.zeros(w_in.shape[:2] + (LANES - 48,), w_in.dtype)
    small = jnp.concatenate([glr, da, db, pad], axis=2).astype(BF16)
    return main, small


def _lane_row(vals, offset):
    return jnp.zeros((1, LANES), F32).at[0, offset:offset + vals.shape[0]].set(vals.astype(F32))


def kernel(x_prompt, x_sample, cache_k, cache_v, state_gla, state_dn, c, c_ctx, norm1_g, norm2_g,
           w_mod, b_mod, w_in, gla_w2, gla_b, gla_norm_g, q_norm_g, k_norm_g, dn_conv, dn_a_log,
           dn_dt_bias, dn_norm_g, w_out, w_ff1, w_ff2):
    n_ctx, n_lat = x_prompt.shape[0], x_sample.shape[0]
    depth = w_in.shape[0]
    n_ctx_rows = n_ctx * SEQ
    assert n_ctx_rows % DEC_SEQ == 0 and n_lat <= 7
    lat_blk0 = n_ctx_rows // DEC_SEQ

    xs = [x_prompt.reshape(n_ctx_rows, D_MODEL), x_sample.reshape(n_lat * DEC_SEQ, D_MODEL)]
    cond8 = jnp.concatenate([c_ctx[None, :], c, jnp.zeros((7 - n_lat, D_MODEL), F32)], axis=0)
    mods = _modulation(cond8, w_mod, b_mod).reshape(depth, 8, N_MOD, D_MODEL)

    cos, sin = _rope_tables()
    gla_consts = _gla_consts()
    ck = cache_k.reshape(n_lat, depth, PAST_LEN, ATT_HKV * HEAD_DIM)
    cv = cache_v.reshape(n_lat, depth, PAST_LEN, ATT_HKV * HEAD_DIM)

    w_main, w_small = _prep_w_in(w_in)
    w_out_b, w_ff1_b, w_ff2_b = w_out.astype(BF16), w_ff1.astype(BF16), w_ff2.astype(BF16)

    nks, nvs, sgs, sds = [], [], [], []
    for l in range(depth):
        p_main, p_small = _inproj(xs, mods[l], norm1_g[l][None, :], w_main, w_small, l,
                                  n_ctx_rows)

        qg, kg = q_norm_g[l][None, :], k_norm_g[l][None, :]
        oa_c, nk, nv = _att_ctx(p_main, qg, kg, n_ctx)
        oa_l = _att_lat(p_main, qg, kg, cos, sin, ck, cv, l, n_lat, lat_blk0)

        w2p = jnp.zeros((2, LANES, GLA_H * GLA_DK), F32)
        w2p = w2p.at[0, 0:GLA_LR].set(gla_w2[l, 0]).at[1, GLA_LR:2 * GLA_LR].set(gla_w2[l, 1])
        gbias = gla_b[l][:, None, :]
        gng = gla_norm_g[l][None, :]
        og_c, sg = _gla(p_main, p_small, w2p, gbias, gng, gla_consts, None, l, SEQ, n_ctx, 0)
        og_l, _ = _gla(p_main, p_small, w2p, gbias, gng, gla_consts, state_gla, l,
                       DEC_SEQ, n_lat, lat_blk0)

        alog_row = _lane_row(dn_a_log[l].reshape(-1), S_DA)
        dtb_row = _lane_row(dn_dt_bias[l].reshape(-1), S_DA)
        dng = dn_norm_g[l][None, :]
        od_c, sd = _dn(p_main, p_small, dn_conv[l], alog_row, dtb_row, dng, None, l,
                       SEQ, n_ctx, 0)
        od_l, _ = _dn(p_main, p_small, dn_conv[l], alog_row, dtb_row, dng, state_dn, l,
                      DEC_SEQ, n_lat, lat_blk0)

        x1, h2 = _outproj(xs, [og_c, og_l], [oa_c, oa_l], [od_c, od_l], mods[l],
                          norm2_g[l][None, :], w_out_b, l, n_ctx_rows)
        xs = _ffn(h2, x1, mods[l], w_ff1_b, w_ff2_b, l, n_ctx_rows,
                  split_out=(l == depth - 1))

        nks.append(nk.reshape(n_ctx, SEQ, ATT_HKV, HEAD_DIM))
        nvs.append(nv.reshape(n_ctx, SEQ, ATT_HKV, HEAD_DIM))
        sgs.append(sg)
        sds.append(sd)

    y_prompt = xs[0].reshape(n_ctx, SEQ, D_MODEL)
    y_sample = xs[1].reshape(n_lat, DEC_SEQ, D_MODEL)
    return (y_prompt, y_sample, jnp.stack(nks, axis=1), jnp.stack(nvs, axis=1),
            jnp.stack(sgs, axis=1), jnp.stack(sds, axis=1))
```

```python
import functools
import math

import numpy as np
import jax
import jax.numpy as jnp
from jax import lax
from jax.experimental import pallas as pl
from jax.experimental.pallas import tpu as pltpu

F32 = jnp.float32
BF16 = jnp.bfloat16

D_MODEL = 2048
SEQ = 256
DEC_SEQ = 1024
PAST_LEN = 256
GRID_W = 64
HEAD_DIM = 128
EPS = 1e-6
GLA_H = 4
GLA_DK = 64
GLA_DV = 128
GLA_LR = 16
GLA_GATE_NORM = 16.0
ATT_HQ = 8
ATT_HKV = 2
ROPE_THETA = 10000.0
DN_H = 4
DN_DK = 128
DN_DV = 128
DN_CONV = 3
CHUNK = 64
FF = 4 * D_MODEL
N_MOD = 6
LANES = 128

P_AQ = 0
P_AKV = 1024
P_GLA = 1536
P_DN = 3072
P_DZ = 4608
P_MAIN = 5120
S_DA = 32
S_DB = 40

VMEM_LIMIT = 56 * 1024 * 1024


def _cparams(sem):
    return pltpu.CompilerParams(dimension_semantics=sem, vmem_limit_bytes=VMEM_LIMIT)


def _bdot(a, b):
    return jnp.dot(a.astype(BF16), b.astype(BF16), preferred_element_type=F32)


def _bdot_nt(a, b):
    return lax.dot_general(a.astype(BF16), b.astype(BF16), (((1,), (1,)), ((), ())),
                           preferred_element_type=F32)


def _bdot_tn(a, b):
    return lax.dot_general(a.astype(BF16), b.astype(BF16), (((0,), (0,)), ((), ())),
                           preferred_element_type=F32)


def _split3(x):
    hi = x.astype(BF16)
    r = x - hi.astype(F32)
    mid = r.astype(BF16)
    lo = (r - mid.astype(F32)).astype(BF16)
    return hi, mid, lo


def _dot_exact_lhs(lhs_bf16, x):
    hi, mid, lo = _split3(x)
    d = functools.partial(jnp.dot, preferred_element_type=F32)
    return d(lhs_bf16, hi) + d(lhs_bf16, mid) + d(lhs_bf16, lo)


def _dot_exact_lhs_wide(lhs_bf16, x):
    n = x.shape[1]
    hi = x.astype(BF16)
    mid = (x - hi.astype(F32)).astype(BF16)
    r = jnp.dot(lhs_bf16, jnp.concatenate([hi, mid], axis=1), preferred_element_type=F32)
    return r[:, 0:n] + r[:, n:2 * n]


def _dot3(a, b):
    ah, am, _ = _split3(a)
    bh, bm, _ = _split3(b)
    d = functools.partial(jnp.dot, preferred_element_type=F32)
    return d(ah, bh) + (d(ah, bm) + d(am, bh))


def _silu(x):
    return x * jax.nn.sigmoid(x)


def _softplus(x):
    return jnp.maximum(x, 0.0) + jnp.log1p(jnp.exp(-jnp.abs(x)))


def _rms(x, g):
    return x * lax.rsqrt(jnp.mean(x * x, axis=-1, keepdims=True) + EPS) * g


def _mods_kernel(c_ref, w_ref, b_ref, o_ref):
    o_ref[...] = _bdot(_silu(c_ref[...]), w_ref[...]) + b_ref[...]


def _modulation(cond8, w_mod, b_mod):
    depth = w_mod.shape[0]
    n_out = N_MOD * D_MODEL
    tn = 1024
    return pl.pallas_call(
        _mods_kernel,
        grid=(depth, n_out // tn),
        in_specs=[
            pl.BlockSpec((8, D_MODEL), lambda l, j: (0, 0)),
            pl.BlockSpec((None, D_MODEL, tn), lambda l, j: (l, 0, j)),
            pl.BlockSpec((None, 1, tn), lambda l, j: (l, 0, j)),
        ],
        out_specs=pl.BlockSpec((None, 8, tn), lambda l, j: (l, 0, j)),
        out_shape=jax.ShapeDtypeStruct((depth, 8, n_out), F32),
        compiler_params=_cparams(("arbitrary", "arbitrary")),
        name="modulation",
    )(cond8, w_mod, b_mod.reshape(depth, 1, n_out))


def _group_of_tile(i, tm, n_ctx_rows):
    n_ctx_tiles = n_ctx_rows // tm
    per_lat = DEC_SEQ // tm
    return jnp.where(i < n_ctx_tiles, 0, 1 + (i - n_ctx_tiles) // per_lat)


def _row_specs(arrays, width, tm, nct):
    if len(arrays) == 1:
        return [pl.BlockSpec((tm, width), lambda i, *_: (i, 0))]
    return [pl.BlockSpec((tm, width), lambda i, *_: (jnp.minimum(i, nct - 1), 0)),
            pl.BlockSpec((tm, width), lambda i, *_: (jnp.maximum(i - nct, 0), 0))]


def _row_load(refs, nct):
    if len(refs) == 1:
        return refs[0][...]
    return jnp.where(pl.program_id(0) < nct, refs[0][...], refs[1][...])


def _inproj_kernel(nct, n_x, *refs):
    x_refs = refs[:n_x]
    mod_ref, g_ref, w_ref, ws_ref, p_ref, ps_ref, h_sc = refs[n_x:]

    @pl.when(pl.program_id(1) == 0)
    def _():
        m = mod_ref[...]
        h = _rms(_row_load(x_refs, nct), g_ref[...]) * (1.0 + m[1:2]) + m[0:1]
        hb = h.astype(BF16)
        h_sc[...] = hb
        ps_ref[...] = jnp.dot(hb, ws_ref[...], preferred_element_type=F32)

    p_ref[...] = jnp.dot(h_sc[...], w_ref[...], preferred_element_type=F32)


def _inproj(xs, mods_l, g1, w_main, w_small, layer, n_ctx_rows, tm=1024, tn=512):
    rows = sum(x.shape[0] for x in xs)
    nct = n_ctx_rows // tm
    grp = functools.partial(_group_of_tile, tm=tm, n_ctx_rows=n_ctx_rows)
    return pl.pallas_call(
        functools.partial(_inproj_kernel, nct, len(xs)),
        grid=(rows // tm, P_MAIN // tn),
        in_specs=_row_specs(xs, D_MODEL, tm, nct) + [
            pl.BlockSpec((None, N_MOD, D_MODEL), lambda i, j: (grp(i), 0, 0)),
            pl.BlockSpec((1, D_MODEL), lambda i, j: (0, 0)),
            pl.BlockSpec((None, D_MODEL, tn), lambda i, j: (layer, 0, j)),
            pl.BlockSpec((None, D_MODEL, LANES), lambda i, j: (layer, 0, 0)),
        ],
        out_specs=[
            pl.BlockSpec((tm, tn), lambda i, j: (i, j)),
            pl.BlockSpec((tm, LANES), lambda i, j: (i, 0)),
        ],
        out_shape=[
            jax.ShapeDtypeStruct((rows, P_MAIN), F32),
            jax.ShapeDtypeStruct((rows, LANES), F32),
        ],
        scratch_shapes=[pltpu.VMEM((tm, D_MODEL), BF16)],
        compiler_params=_cparams(("arbitrary", "arbitrary")),
        name="inproj",
    )(*xs, mods_l, g1, w_main, w_small)


def _outproj_kernel(nct, n_x, *refs):
    x_refs = refs[:n_x]
    (ogc_ref, ogl_ref, oac_ref, oal_ref, odc_ref, odl_ref, mod_ref, g_ref, w_ref,
     x1_ref, h2_ref) = refs[n_x:]
    m = mod_ref[...]
    d = functools.partial(jnp.dot, preferred_element_type=F32)
    mix = (d(_row_load((ogc_ref, ogl_ref), nct), w_ref[0:512, :])
           + d(_row_load((oac_ref, oal_ref), nct), w_ref[512:1536, :])
           + d(_row_load((odc_ref, odl_ref), nct), w_ref[1536:2048, :]))
    x1 = _row_load(x_refs, nct) + m[2:3] * mix
    x1_ref[...] = x1
    h2_ref[...] = (_rms(x1, g_ref[...]) * (1.0 + m[4:5]) + m[3:4]).astype(BF16)


def _outproj(xs, o_gla, o_att, o_dn, mods_l, g2, w_out, layer, n_ctx_rows, tm=512):
    rows = sum(x.shape[0] for x in xs)
    nct = n_ctx_rows // tm
    grp = functools.partial(_group_of_tile, tm=tm, n_ctx_rows=n_ctx_rows)
    return pl.pallas_call(
        functools.partial(_outproj_kernel, nct, len(xs)),
        grid=(rows // tm,),
        in_specs=(_row_specs(xs, D_MODEL, tm, nct) + _row_specs(o_gla, 512, tm, nct)
                  + _row_specs(o_att, 1024, tm, nct) + _row_specs(o_dn, 512, tm, nct) + [
                      pl.BlockSpec((None, N_MOD, D_MODEL), lambda i: (grp(i), 0, 0)),
                      pl.BlockSpec((1, D_MODEL), lambda i: (0, 0)),
                      pl.BlockSpec((None, D_MODEL, D_MODEL), lambda i: (layer, 0, 0),
                                   pipeline_mode=pl.Buffered(1)),
                  ]),
        out_specs=[
            pl.BlockSpec((tm, D_MODEL), lambda i: (i, 0)),
            pl.BlockSpec((tm, D_MODEL), lambda i: (i, 0)),
        ],
        out_shape=[
            jax.ShapeDtypeStruct((rows, D_MODEL), F32),
            jax.ShapeDtypeStruct((rows, D_MODEL), BF16),
        ],
        compiler_params=_cparams(("arbitrary",)),
        name="outproj",
    )(*xs, *o_gla, *o_att, *o_dn, mods_l, g2, w_out)


FFN_TM = 1024
FFN_OUT_CHUNK = 512


def _ffn_kernel(h_ref, x1_ref, mod_ref, w1_ref, w2_ref, y_ref):
    j = pl.program_id(1)
    tf = w1_ref.shape[1]

    @pl.when(j == 0)
    def _():
        y_ref[...] = jnp.zeros(y_ref.shape, F32)

    h = h_ref[...]
    for half in range(2):
        cols = slice(half * (tf // 2), (half + 1) * (tf // 2))
        t = jnp.dot(h, w1_ref[:, cols], preferred_element_type=F32)
        t = jnp.square(jnp.maximum(t, 0.0)).astype(BF16)
        for nc in range(D_MODEL // FFN_OUT_CHUNK):
            ncs = slice(nc * FFN_OUT_CHUNK, (nc + 1) * FFN_OUT_CHUNK)
            y_ref[:, ncs] += jnp.dot(t, w2_ref[cols, ncs], preferred_element_type=F32)

    @pl.when(j == pl.num_programs(1) - 1)
    def _():
        y_ref[...] = x1_ref[...] + mod_ref[5:6, :] * y_ref[...]


def _ffn(h2, x1, mods_l, w1, w2, layer, n_ctx_rows, tile0, n_tiles, tm=1024, tf=1024):
    grp = functools.partial(_group_of_tile, tm=tm, n_ctx_rows=n_ctx_rows)
    once = pl.Buffered(1)
    return pl.pallas_call(
        _ffn_kernel,
        grid=(n_tiles, FF // tf),
        in_specs=[
            pl.BlockSpec((tm, D_MODEL), lambda i, j: (tile0 + i, 0), pipeline_mode=once),
            pl.BlockSpec((tm, D_MODEL), lambda i, j: (tile0 + i, 0), pipeline_mode=once),
            pl.BlockSpec((None, N_MOD, D_MODEL), lambda i, j: (grp(tile0 + i), 0, 0)),
            pl.BlockSpec((None, D_MODEL, tf), lambda i, j: (layer, 0, j)),
            pl.BlockSpec((None, tf, D_MODEL), lambda i, j: (layer, j, 0)),
        ],
        out_specs=pl.BlockSpec((tm, D_MODEL), lambda i, j: (i, 0)),
        out_shape=jax.ShapeDtypeStruct((n_tiles * tm, D_MODEL), F32),
        compiler_params=_cparams(("arbitrary", "arbitrary")),
        name="ffn",
    )(h2, x1, mods_l, w1, w2)


def _rope_tables():
    half = HEAD_DIM // 2
    pos = np.arange(DEC_SEQ)
    row = (pos // GRID_W).astype(np.float32)
    col = (pos % GRID_W).astype(np.float32)
    inv = (ROPE_THETA ** (-np.arange(0, half, 2, dtype=np.float32) / half)).astype(np.float32)
    ar = row[:, None] * inv[None, :]
    ac = col[:, None] * inv[None, :]
    cos = np.concatenate([np.cos(ar), np.cos(ar), np.cos(ac), np.cos(ac)], axis=1)
    sin = np.concatenate([-np.sin(ar), np.sin(ar), -np.sin(ac), np.sin(ac)], axis=1)
    return jnp.asarray(cos, F32), jnp.asarray(sin, F32)


def _rope(x, cos, sin):
    lane = lax.broadcasted_iota(jnp.int32, x.shape, 1)
    first = (lane % 64) < 32
    partner = jnp.where(first, pltpu.roll(x, 96, 1), pltpu.roll(x, 32, 1))
    return x * cos + partner * sin


ATT_G = ATT_HQ // ATT_HKV


def _softmax_pv(q, kb, vb):
    s = lax.dot_general(q.astype(BF16), kb, (((1,), (1,)), ((), ())), preferred_element_type=F32)
    m = jnp.max(s, axis=-1, keepdims=True)
    p = jnp.exp(s - m)
    l = jnp.sum(p, axis=-1, keepdims=True)
    return jnp.dot(p.astype(BF16), vb, preferred_element_type=F32) / l


def _group_attention(q_heads, kb, vb, store):
    for g, q in enumerate(q_heads):
        store(g, _softmax_pv(q, kb, vb).astype(BF16))


def _att_ctx_kernel(aq_ref, akv_ref, qg_ref, kg_ref, o_ref, nk_ref, nv_ref):
    qg = qg_ref[...] * (HEAD_DIM ** -0.5)
    kg = kg_ref[...]
    for hk in range(ATT_HKV):
        ks = slice(hk * HEAD_DIM, (hk + 1) * HEAD_DIM)
        kn = _rms(akv_ref[:, ks], kg)
        v = akv_ref[:, 256 + hk * HEAD_DIM:256 + (hk + 1) * HEAD_DIM]
        nk_ref[:, ks] = kn
        nv_ref[:, ks] = v

        def head_cols(g, hk=hk):
            return slice((hk * ATT_G + g) * HEAD_DIM, (hk * ATT_G + g + 1) * HEAD_DIM)

        def store(g, val):
            o_ref[:, head_cols(g)] = val

        qs = [_rms(aq_ref[:, head_cols(g)], qg) for g in range(ATT_G)]
        _group_attention(qs, kn.astype(BF16), v.astype(BF16), store)


def _att_lat_kernel(aq_ref, akv_ref, qg_ref, kg_ref, cos_ref, sin_ref, ck_ref, cv_ref, o_ref,
                    k_sc, v_sc):
    qg = qg_ref[...] * (HEAD_DIM ** -0.5)
    kg = kg_ref[...]
    qb = 256
    for hk in range(ATT_HKV):
        ks = slice(hk * HEAD_DIM, (hk + 1) * HEAD_DIM)
        k_sc[hk, 0:PAST_LEN, :] = ck_ref[:, ks].astype(BF16)
        v_sc[hk, 0:PAST_LEN, :] = cv_ref[:, ks].astype(BF16)
        kn = _rope(_rms(akv_ref[:, ks], kg), cos_ref[...], sin_ref[...])
        k_sc[hk, PAST_LEN:, :] = kn.astype(BF16)
        v_sc[hk, PAST_LEN:, :] = akv_ref[:, 256 + hk * HEAD_DIM:256 + (hk + 1) * HEAD_DIM].astype(BF16)

    def body(r, carry):
        r0 = pl.multiple_of(r * qb, qb)
        cos = cos_ref[pl.ds(r0, qb), :]
        sin = sin_ref[pl.ds(r0, qb), :]
        for hk in range(ATT_HKV):
            def head_cols(g, hk=hk):
                return slice((hk * ATT_G + g) * HEAD_DIM, (hk * ATT_G + g + 1) * HEAD_DIM)

            def store(g, val):
                o_ref[pl.ds(r0, qb), head_cols(g)] = val

            qs = [_rope(_rms(aq_ref[pl.ds(r0, qb), head_cols(g)], qg), cos, sin)
                  for g in range(ATT_G)]
            _group_attention(qs, k_sc[hk], v_sc[hk], store)
        return carry

    lax.fori_loop(0, DEC_SEQ // qb, body, 0)


def _att_ctx(p_main, qg, kg, n_seq):
    L = SEQ
    return pl.pallas_call(
        _att_ctx_kernel,
        grid=(n_seq,),
        in_specs=[
            pl.BlockSpec((L, 1024), lambda b: (b, P_AQ // 1024)),
            pl.BlockSpec((L, 512), lambda b: (b, P_AKV // 512)),
            pl.BlockSpec((1, HEAD_DIM), lambda b: (0, 0)),
            pl.BlockSpec((1, HEAD_DIM), lambda b: (0, 0)),
        ],
        out_specs=[
            pl.BlockSpec((L, 1024), lambda b: (b, 0)),
            pl.BlockSpec((None, L, 256), lambda b: (b, 0, 0)),
            pl.BlockSpec((None, L, 256), lambda b: (b, 0, 0)),
        ],
        out_shape=[
            jax.ShapeDtypeStruct((n_seq * L, 1024), BF16),
            jax.ShapeDtypeStruct((n_seq, L, 256), F32),
            jax.ShapeDtypeStruct((n_seq, L, 256), F32),
        ],
        compiler_params=_cparams(("arbitrary",)),
        name="att_ctx",
    )(p_main, p_main, qg, kg)


def _att_lat(p_main, qg, kg, cos, sin, ck, cv, layer, n_seq, row_blk0):
    L = DEC_SEQ
    return pl.pallas_call(
        _att_lat_kernel,
        grid=(n_seq,),
        in_specs=[
            pl.BlockSpec((L, 1024), lambda b: (row_blk0 + b, P_AQ // 1024)),
            pl.BlockSpec((L, 512), lambda b: (row_blk0 + b, P_AKV // 512)),
            pl.BlockSpec((1, HEAD_DIM), lambda b: (0, 0)),
            pl.BlockSpec((1, HEAD_DIM), lambda b: (0, 0)),
            pl.BlockSpec((L, HEAD_DIM), lambda b: (0, 0)),
            pl.BlockSpec((L, HEAD_DIM), lambda b: (0, 0)),
            pl.BlockSpec((None, None, PAST_LEN, 256), lambda b: (b, layer, 0, 0)),
            pl.BlockSpec((None, None, PAST_LEN, 256), lambda b: (b, layer, 0, 0)),
        ],
        out_specs=pl.BlockSpec((L, 1024), lambda b: (b, 0)),
        out_shape=jax.ShapeDtypeStruct((n_seq * L, 1024), BF16),
        scratch_shapes=[
            pltpu.VMEM((ATT_HKV, PAST_LEN + L, HEAD_DIM), BF16),
            pltpu.VMEM((ATT_HKV, PAST_LEN + L, HEAD_DIM), BF16),
        ],
        compiler_params=_cparams(("arbitrary",)),
        name="att_lat",
    )(p_main, p_main, qg, kg, cos, sin, ck, cv)


_GLA_LEVELS = (32, 16, 8, 4, 2, 1)


def _gla_consts():
    C = CHUNK
    i = np.arange(C)[:, None]
    t = np.arange(C)[None, :]
    mats = [t <= i, t > i]
    masks = [i == t]
    for s in _GLA_LEVELS:
        m = (i // s) * s
        if s > 1:
            mats.append((t > m) & (t <= i))
        mats.append((t > i) & (t <= m + s))
        masks.append((i // (2 * s) == t // (2 * s)) & ((i // s) % 2 == 1) & ((t // s) % 2 == 0))
    fwd = np.concatenate(mats, 0)
    bwd = np.concatenate([mm[::-1, ::-1] for mm in mats], 0)
    mstack = np.stack([fwd, bwd]).astype(np.float32)
    mf = np.stack([np.concatenate([mm, mm], 0) for mm in masks])
    mb = np.stack([np.concatenate([mm[::-1, ::-1], mm[::-1, ::-1]], 0) for mm in masks])
    lmask = np.stack([mf, mb]).astype(np.float32)
    return jnp.asarray(mstack, BF16), jnp.asarray(lmask, F32)


def _gla_kernel(L, has_state, *refs):
    if has_state:
        (pg_ref, ps_ref, w2_ref, gb_ref, ng_ref, ms_ref, lm_ref, s0_ref,
         o_ref, so_ref, la_sc, of_sc, ob_sc, st_sc) = refs
    else:
        (pg_ref, ps_ref, w2_ref, gb_ref, ng_ref, ms_ref, lm_ref,
         o_ref, so_ref, la_sc, of_sc, ob_sc, st_sc) = refs
    n = L // CHUNK
    C = CHUNK

    ps = ps_ref[...]
    for z in range(2):
        x = _dot3(ps, w2_ref[z]) + gb_ref[z]
        la_sc[z] = (jnp.minimum(x, 0.0) - jnp.log1p(jnp.exp(-jnp.abs(x)))) * (1.0 / GLA_GATE_NORM)

    if has_state:
        for z in range(2):
            for p in range(2):
                pair = jnp.concatenate([s0_ref[z, 2 * p], s0_ref[z, 2 * p + 1]], axis=0)
                st_sc[z, p] = jnp.transpose(pair)
    else:
        st_sc[...] = jnp.zeros(st_sc.shape, F32)

    lane = lax.broadcasted_iota(jnp.int32, (C, LANES), 1)
    first = lane < 64

    def stack_heads(t):
        return jnp.concatenate([jnp.where(first, t, 0.0), jnp.where(first, 0.0, t)], axis=0)

    chains = [(z, p) for z in range(2) for p in range(2)]
    nch = len(chains)

    def body(c, carry):
        rows = [pl.ds(pl.multiple_of(c * C, C), C), pl.ds(pl.multiple_of((n - 1 - c) * C, C), C)]
        last = [C - 1, 0]
        q = [pg_ref[rows[z], p * LANES:(p + 1) * LANES] * (GLA_DK ** -0.5) for z, p in chains]
        k = [pg_ref[rows[z], 256 + p * LANES:256 + (p + 1) * LANES] for z, p in chains]
        e = [jnp.exp(_dot_exact_lhs_wide(ms_ref[z], la_sc[z, rows[z], p * LANES:(p + 1) * LANES]))
             for z, p in chains]
        st = [st_sc[z, p] for z, p in chains]
        a2 = [_bdot_nt(stack_heads(q[i]), k[i]) * lm_ref[z, 0] for i, (z, p) in enumerate(chains)]
        blk = 2
        for li, s in enumerate(_GLA_LEVELS):
            if s > 1:
                qs = [q[i] * e[i][blk * C:(blk + 1) * C] for i in range(nch)]
                blk += 1
            else:
                qs = q
            ks = [k[i] * e[i][blk * C:(blk + 1) * C] for i in range(nch)]
            blk += 1
            a2 = [a2[i] + _bdot_nt(stack_heads(qs[i]), ks[i]) * lm_ref[z, li + 1]
                  for i, (z, p) in enumerate(chains)]
        inter = [_bdot_nt(stack_heads(q[i] * e[i][0:C]), st[i]) for i in range(nch)]
        kl = [k[i] * e[i][C:2 * C] for i in range(nch)]
        for i, (z, p) in enumerate(chains):
            osc = of_sc if z == 0 else ob_sc
            zs = []
            for hh in range(2):
                h = 2 * p + hh
                v = pg_ref[rows[z], 512 + h * LANES:512 + (h + 1) * LANES]
                o = _bdot(a2[i][hh * C:(hh + 1) * C], v) + inter[i][hh * C:(hh + 1) * C]
                osc[rows[z], h * LANES:(h + 1) * LANES] = o
                zs.append(_bdot_tn(v, kl[i]))
            st_sc[z, p] = (st[i] * e[i][last[z]:last[z] + 1, :]
                           + jnp.where(first[0:1], zs[0], zs[1]))
        return carry

    lax.fori_loop(0, n, body, 0)

    ng = ng_ref[...]

    def fin(c, carry):
        r0 = pl.multiple_of(c * C, C)
        rows = pl.ds(r0, C)
        for h in range(GLA_H):
            cols = slice(h * LANES, (h + 1) * LANES)
            o = of_sc[rows, cols] + ob_sc[rows, cols]
            gate = _silu(pg_ref[rows, 1024 + h * LANES:1024 + (h + 1) * LANES])
            o_ref[rows, cols] = (_rms(o, ng) * gate).astype(BF16)
        return carry

    lax.fori_loop(0, n, fin, 0)
    for z in range(2):
        for p in range(2):
            pair = jnp.transpose(st_sc[z, p])
            so_ref[z, 2 * p] = pair[0:GLA_DK]
            so_ref[z, 2 * p + 1] = pair[GLA_DK:2 * GLA_DK]


def _gla(p_main, p_small, w2p, gbias, ng, consts, s0, layer, L, n_seq, row_blk0):
    mstack, lmask = consts
    has_state = s0 is not None
    state_blk = (None, 2, GLA_H, GLA_DK, GLA_DV)
    in_specs = [
        pl.BlockSpec((L, 1536), lambda b: (row_blk0 + b, P_GLA // 1536)),
        pl.BlockSpec((L, LANES), lambda b: (row_blk0 + b, 0)),
        pl.BlockSpec((2, LANES, 256), lambda b: (0, 0, 0)),
        pl.BlockSpec((2, 1, 256), lambda b: (0, 0, 0)),
        pl.BlockSpec((1, GLA_DV), lambda b: (0, 0)),
        pl.BlockSpec(mstack.shape, lambda b: (0, 0, 0)),
        pl.BlockSpec(lmask.shape, lambda b: (0, 0, 0, 0)),
    ]
    args = [p_main, p_small, w2p, gbias, ng, mstack, lmask]
    if has_state:
        in_specs.append(pl.BlockSpec((None,) + state_blk, lambda b: (b, layer, 0, 0, 0, 0)))
        args.append(s0)
    return pl.pallas_call(
        functools.partial(_gla_kernel, L, has_state),
        grid=(n_seq,),
        in_specs=in_specs,
        out_specs=[
            pl.BlockSpec((L, 512), lambda b: (b, 0)),
            pl.BlockSpec(state_blk, lambda b: (b, 0, 0, 0, 0)),
        ],
        out_shape=[
            jax.ShapeDtypeStruct((n_seq * L, 512), BF16),
            jax.ShapeDtypeStruct((n_seq, 2, GLA_H, GLA_DK, GLA_DV), F32),
        ],
        scratch_shapes=[
            pltpu.VMEM((2, L, 256), F32),
            pltpu.VMEM((L, 512), F32),
            pltpu.VMEM((L, 512), F32),
            pltpu.VMEM((2, 2, GLA_DV, LANES), F32),
        ],
        compiler_params=_cparams(("arbitrary",)),
        name="gla_lat" if has_state else "gla_ctx",
    )(*args)


def _dn_kernel(L, has_state, *refs):
    if has_state:
        (pd_ref, dz_ref, ps_ref, cw_ref, al_ref, dtb_ref, ng_ref, s0_ref,
         o_ref, so_ref, qkv_sc, g_sc, bb_sc, of_sc, ob_sc, s_sc) = refs
    else:
        (pd_ref, dz_ref, ps_ref, cw_ref, al_ref, dtb_ref, ng_ref,
         o_ref, so_ref, qkv_sc, g_sc, bb_sc, of_sc, ob_sc, s_sc) = refs
    n = L // CHUNK
    C = CHUNK

    rowi = lax.broadcasted_iota(jnp.int32, (L, LANES), 0)
    not_first = rowi > 0
    not_last = rowi < L - 1

    def conv_tile(j, carry):
        c0 = pl.multiple_of(j * LANES, LANES)
        x = pd_ref[:, pl.ds(c0, LANES)]
        w = cw_ref[:, pl.ds(c0, LANES)]
        prev = jnp.where(not_first, pltpu.roll(x, 1, 0), 0.0)
        nxt = jnp.where(not_last, pltpu.roll(x, L - 1, 0), 0.0)
        y = _silu(prev * w[0:1] + x * w[1:2] + nxt * w[2:3])
        inv = lax.rsqrt(jnp.sum(y * y, axis=-1, keepdims=True) + EPS)
        scale = jnp.where(j < DN_H, inv * (DN_DK ** -0.5), jnp.where(j < 2 * DN_H, inv, 1.0))
        qkv_sc[:, pl.ds(c0, LANES)] = y * scale
        return carry

    lax.fori_loop(0, 3 * DN_H, conv_tile, 0)

    ps = ps_ref[...]
    lane_l = lax.broadcasted_iota(jnp.int32, (L, LANES), 1)
    g_sc[...] = -jnp.exp(al_ref[...]) * _softplus(ps + dtb_ref[...])
    b_all = jax.nn.sigmoid(ps)
    for idx in range(2 * DN_H):
        bcol = jnp.sum(jnp.where(lane_l == S_DB + idx, b_all, 0.0), axis=-1, keepdims=True)
        bb_sc[idx] = jnp.broadcast_to(bcol, (L, LANES))

    if has_state:
        s_sc[...] = s0_ref[...]
    else:
        s_sc[...] = jnp.zeros(s_sc.shape, F32)

    ri = lax.broadcasted_iota(jnp.int32, (C, LANES), 0)
    lane_c = lax.broadcasted_iota(jnp.int32, (C, LANES), 1)
    ci = lane_c & (C - 1)
    first = lane_c < C
    incl = [ci <= ri, ci >= ri]
    strict = [ci < ri, ci > ri]
    rs = lax.broadcasted_iota(jnp.int32, (C, C), 0)
    cs = lax.broadcasted_iota(jnp.int32, (C, C), 1)
    m_b = [(cs <= rs).astype(BF16), (cs >= rs).astype(BF16)]
    chains = [(z, h) for z in range(2) for h in range(DN_H)]
    nch = len(chains)
    pairs = [(z, p) for z in range(2) for p in range(DN_H // 2)]
    npr = len(pairs)

    def stack_heads(t):
        return jnp.concatenate([jnp.where(first, t, 0.0), jnp.where(first, 0.0, t)], axis=0)

    def hi_mid(x):
        hi = x.astype(BF16).astype(F32)
        return hi, x - hi

    def pair_products(lhs_list, x):
        xh, xm = hi_mid(x)
        rhs = jnp.concatenate([stack_heads(xh), stack_heads(xm)], axis=1).astype(BF16)
        parts = []
        for l in lhs_list:
            parts.extend(hi_mid(l))
        r = jnp.dot(jnp.concatenate(parts, axis=0).astype(BF16), rhs, preferred_element_type=F32)
        out = []
        for i in range(len(lhs_list)):
            blk = r[2 * C * i:2 * C * i + C] + r[2 * C * i + C:2 * C * (i + 1)]
            out.append(blk[:, 0:LANES] + blk[:, LANES:2 * LANES])
        return out

    def body(c, carry):
        rows = [pl.ds(pl.multiple_of(c * C, C), C), pl.ds(pl.multiple_of((n - 1 - c) * C, C), C)]
        last = [C - 1, 0]

        def ld(base, z, h):
            return qkv_sc[rows[z], base + h * LANES:base + (h + 1) * LANES]

        q = [ld(0, z, h) for z, h in chains]
        k = [ld(512, z, h) for z, h in chains]
        v = [ld(1024, z, h) for z, h in chains]
        bb = [bb_sc[z * DN_H + h, rows[z], :] for z, h in chains]
        gc = [_dot_exact_lhs(m_b[z], g_sc[rows[z], :]) for z in range(2)]
        gct = [jnp.transpose(jnp.concatenate([gc[z], gc[z]], axis=0)) for z in range(2)]
        gi = [jnp.broadcast_to(jnp.sum(jnp.where(lane_c == S_DA + z * DN_H + h, gc[z], 0.0),
                                       axis=-1, keepdims=True), (C, LANES)) for z, h in chains]
        gjrow = [gct[z][S_DA + z * DN_H + h:S_DA + z * DN_H + h + 1, :] for z, h in chains]
        heads = [(z * DN_H + 2 * p, z * DN_H + 2 * p + 1) for z, p in pairs]
        gam = []
        for (z, p), (i0, i1) in zip(pairs, heads):
            diff = jnp.where(first, gi[i0], gi[i1]) - jnp.where(first[0:1], gjrow[i0], gjrow[i1])
            gam.append(jnp.where(incl[z], jnp.exp(jnp.where(incl[z], diff, 0.0)), 0.0))
        kb = [k[i] * bb[i] for i in range(nch)]
        kq = [_bdot_nt(jnp.concatenate([kb[i0], q[i0], kb[i1], q[i1]], axis=0),
                       jnp.concatenate([k[i0], k[i1]], axis=0)) for i0, i1 in heads]
        a = [jnp.where(strict[z], jnp.where(first, kq[j][0:C], kq[j][2 * C:3 * C]) * gam[j], 0.0)
             for j, (z, p) in enumerate(pairs)]
        att = [jnp.where(first, kq[j][C:2 * C], kq[j][3 * C:4 * C]) * gam[j] for j in range(npr)]
        nt = [-a[j] for j in range(npr)]
        pw = [pair_products([a[j]], a[j])[0] for j in range(npr)]
        for step in range(5):
            if step < 4:
                both = [pair_products([nt[j], pw[j]], pw[j]) for j in range(npr)]
                nt = [nt[j] + pw[j] + both[j][0] for j in range(npr)]
                pw = [both[j][1] for j in range(npr)]
            else:
                nt = [nt[j] + pw[j] + pair_products([nt[j]], pw[j])[0] for j in range(npr)]
        eg = [jnp.exp(gi[i]) for i in range(nch)]
        rhs = [jnp.concatenate([v[i] * bb[i], kb[i] * eg[i]], axis=1) for i in range(nch)]
        corr = [_bdot(stack_heads(nt[j]), jnp.concatenate([rhs[i0], rhs[i1]], axis=0))
                for j, (i0, i1) in enumerate(heads)]
        uw = [None] * nch
        for j, (i0, i1) in enumerate(heads):
            uw[i0] = rhs[i0] + corr[j][0:C]
            uw[i1] = rhs[i1] + corr[j][C:2 * C]
        s = [s_sc[z, h] for z, h in chains]
        ws = [_bdot(jnp.concatenate([uw[i][:, LANES:], q[i] * eg[i]], axis=0), s[i])
              for i in range(nch)]
        v_new = [uw[i][:, 0:LANES] - ws[i][0:C] for i in range(nch)]
        av = [_bdot(stack_heads(att[j]), jnp.concatenate([v_new[i0], v_new[i1]], axis=0))
              for j, (i0, i1) in enumerate(heads)]
        o = [None] * nch
        for j, (i0, i1) in enumerate(heads):
            o[i0] = ws[i0][C:2 * C] + av[j][0:C]
            o[i1] = ws[i1][C:2 * C] + av[j][C:2 * C]
        for i, (z, h) in enumerate(chains):
            osc = of_sc if z == 0 else ob_sc
            osc[rows[z], h * LANES:(h + 1) * LANES] = o[i]
        gt = [gi[i][last[z]:last[z] + 1, :] for i, (z, h) in enumerate(chains)]
        upd = [_bdot_tn(k[i] * jnp.exp(gt[i] - gi[i]), v_new[i]) for i in range(nch)]
        for i, (z, h) in enumerate(chains):
            s_sc[z, h] = s[i] * jnp.exp(gt[i]) + upd[i]
        return carry

    lax.fori_loop(0, n, body, 0)

    ng = ng_ref[...]

    def fin(c, carry):
        r0 = pl.multiple_of(c * C, C)
        rows = pl.ds(r0, C)
        for h in range(DN_H):
            cols = slice(h * LANES, (h + 1) * LANES)
            o = of_sc[rows, cols] + ob_sc[rows, cols]
            o_ref[rows, cols] = (_rms(o, ng) * _silu(dz_ref[rows, cols])).astype(BF16)
        return carry

    lax.fori_loop(0, n, fin, 0)
    so_ref[...] = s_sc[...]


def _dn(p_main, p_small, conv_w, alog_row, dtb_row, ng, s0, layer, L, n_seq, row_blk0):
    has_state = s0 is not None
    in_specs = [
        pl.BlockSpec((L, 1536), lambda b: (row_blk0 + b, P_DN // 1536)),
        pl.BlockSpec((L, 512), lambda b: (row_blk0 + b, P_DZ // 512)),
        pl.BlockSpec((L, LANES), lambda b: (row_blk0 + b, 0)),
        pl.BlockSpec((DN_CONV, 3 * DN_H * DN_DK), lambda b: (0, 0)),
        pl.BlockSpec((1, LANES), lambda b: (0, 0)),
        pl.BlockSpec((1, LANES), lambda b: (0, 0)),
        pl.BlockSpec((1, DN_DV), lambda b: (0, 0)),
    ]
    args = [p_main, p_main, p_small, conv_w, alog_row, dtb_row, ng]
    if has_state:
        in_specs.append(pl.BlockSpec((None, None, 2, DN_H, DN_DK, DN_DV),
                                     lambda b: (b, layer, 0, 0, 0, 0)))
        args.append(s0)
    return pl.pallas_call(
        functools.partial(_dn_kernel, L, has_state),
        grid=(n_seq,),
        in_specs=in_specs,
        out_specs=[
            pl.BlockSpec((L, 512), lambda b: (b, 0)),
            pl.BlockSpec((None, 2, DN_H, DN_DK, DN_DV), lambda b: (b, 0, 0, 0, 0)),
        ],
        out_shape=[
            jax.ShapeDtypeStruct((n_seq * L, 512), BF16),
            jax.ShapeDtypeStruct((n_seq, 2, DN_H, DN_DK, DN_DV), F32),
        ],
        scratch_shapes=[
            pltpu.VMEM((L, 1536), F32),
            pltpu.VMEM((L, LANES), F32),
            pltpu.VMEM((2 * DN_H, L, LANES), F32),
            pltpu.VMEM((L, 512), F32),
            pltpu.VMEM((L, 512), F32),
            pltpu.VMEM((2, DN_H, DN_DK, DN_DV), F32),
        ],
        compiler_params=_cparams(("arbitrary",)),
        name="dn_lat" if has_state else "dn_ctx",
    )(*args)


def _prep_w_in(w_in):
    off = np.cumsum((0, 256, 256, 512, 512, 32, 1024, 256, 256, 1536, 512, 8, 8))
    gq, gk, gv, gr, glr, aq, ak, av, dqkv, dz, da, db = [
        w_in[:, :, int(off[i]):int(off[i + 1])] for i in range(12)]
    main = jnp.concatenate([aq, ak, av, gq, gk, gv, gr, dqkv, dz], axis=2).astype(BF16)
    pad = jnp.zeros(w_in.shape[:2] + (LANES - 48,), w_in.dtype)
    small = jnp.concatenate([glr, da, db, pad], axis=2).astype(BF16)
    return main, small


def _lane_row(vals, offset):
    return jnp.zeros((1, LANES), F32).at[0, offset:offset + vals.shape[0]].set(vals.astype(F32))


def kernel(x_prompt, x_sample, cache_k, cache_v, state_gla, state_dn, c, c_ctx, norm1_g, norm2_g,
           w_mod, b_mod, w_in, gla_w2, gla_b, gla_norm_g, q_norm_g, k_norm_g, dn_conv, dn_a_log,
           dn_dt_bias, dn_norm_g, w_out, w_ff1, w_ff2):
    n_ctx, n_lat = x_prompt.shape[0], x_sample.shape[0]
    depth = w_in.shape[0]
    n_ctx_rows = n_ctx * SEQ
    assert n_ctx_rows % DEC_SEQ == 0 and n_lat <= 7
    lat_blk0 = n_ctx_rows // DEC_SEQ

    xs = [x_prompt.reshape(n_ctx_rows, D_MODEL), x_sample.reshape(n_lat * DEC_SEQ, D_MODEL)]
    cond8 = jnp.concatenate([c_ctx[None, :], c, jnp.zeros((7 - n_lat, D_MODEL), F32)], axis=0)
    mods = _modulation(cond8, w_mod, b_mod).reshape(depth, 8, N_MOD, D_MODEL)

    cos, sin = _rope_tables()
    gla_consts = _gla_consts()
    ck = cache_k.reshape(n_lat, depth, PAST_LEN, ATT_HKV * HEAD_DIM)
    cv = cache_v.reshape(n_lat, depth, PAST_LEN, ATT_HKV * HEAD_DIM)

    w_main, w_small = _prep_w_in(w_in)
    w_out_b, w_ff1_b, w_ff2_b = w_out.astype(BF16), w_ff1.astype(BF16), w_ff2.astype(BF16)

    nks, nvs, sgs, sds = [], [], [], []
    for l in range(depth):
        p_main, p_small = _inproj(xs, mods[l], norm1_g[l][None, :], w_main, w_small, l,
                                  n_ctx_rows)

        qg, kg = q_norm_g[l][None, :], k_norm_g[l][None, :]
        oa_c, nk, nv = _att_ctx(p_main, qg, kg, n_ctx)
        oa_l = _att_lat(p_main, qg, kg, cos, sin, ck, cv, l, n_lat, lat_blk0)

        w2p = jnp.zeros((2, LANES, GLA_H * GLA_DK), F32)
        w2p = w2p.at[0, 0:GLA_LR].set(gla_w2[l, 0]).at[1, GLA_LR:2 * GLA_LR].set(gla_w2[l, 1])
        gbias = gla_b[l][:, None, :]
        gng = gla_norm_g[l][None, :]
        og_c, sg = _gla(p_main, p_small, w2p, gbias, gng, gla_consts, None, l, SEQ, n_ctx, 0)
        og_l, _ = _gla(p_main, p_small, w2p, gbias, gng, gla_consts, state_gla, l,
                       DEC_SEQ, n_lat, lat_blk0)

        alog_row = _lane_row(dn_a_log[l].reshape(-1), S_DA)
        dtb_row = _lane_row(dn_dt_bias[l].reshape(-1), S_DA)
        dng = dn_norm_g[l][None, :]
        od_c, sd = _dn(p_main, p_small, dn_conv[l], alog_row, dtb_row, dng, None, l,
                       SEQ, n_ctx, 0)
        od_l, _ = _dn(p_main, p_small, dn_conv[l], alog_row, dtb_row, dng, state_dn, l,
                      DEC_SEQ, n_lat, lat_blk0)

        x1, h2 = _outproj(xs, [og_c, og_l], [oa_c, oa_l], [od_c, od_l], mods[l],
                          norm2_g[l][None, :], w_out_b, l, n_ctx_rows)
        n_tiles, nct = x1.shape[0] // FFN_TM, n_ctx_rows // FFN_TM
        ffn = functools.partial(_ffn, h2, x1, mods[l], w_ff1_b, w_ff2_b, l, n_ctx_rows, tm=FFN_TM)
        if l < depth - 1:
            xs = [ffn(0, n_tiles)]
        else:
            xs = [ffn(0, nct), ffn(nct, n_tiles - nct)]

        nks.append(nk.reshape(n_ctx, SEQ, ATT_HKV, HEAD_DIM))
        nvs.append(nv.reshape(n_ctx, SEQ, ATT_HKV, HEAD_DIM))
        sgs.append(sg)
        sds.append(sd)

    y_prompt = xs[0].reshape(n_ctx, SEQ, D_MODEL)
    y_sample = xs[1].reshape(n_lat, DEC_SEQ, D_MODEL)
    return (y_prompt, y_sample, jnp.stack(nks, axis=1), jnp.stack(nvs, axis=1),
            jnp.stack(sgs, axis=1), jnp.stack(sds, axis=1))
```

```python
import functools
import math

import numpy as np
import jax
import jax.numpy as jnp
from jax import lax
from jax.experimental import pallas as pl
from jax.experimental.pallas import tpu as pltpu

F32 = jnp.float32
BF16 = jnp.bfloat16

D_MODEL = 2048
SEQ = 256
DEC_SEQ = 1024
PAST_LEN = 256
GRID_W = 64
HEAD_DIM = 128
EPS = 1e-6
GLA_H = 4
GLA_DK = 64
GLA_DV = 128
GLA_LR = 16
GLA_GATE_NORM = 16.0
ATT_HQ = 8
ATT_HKV = 2
ROPE_THETA = 10000.0
DN_H = 4
DN_DK = 128
DN_DV = 128
DN_CONV = 3
CHUNK = 64
FF = 4 * D_MODEL
N_MOD = 6
LANES = 128

P_AQ = 0
P_AKV = 1024
P_GLA = 1536
P_DN = 3072
P_DZ = 4608
P_MAIN = 5120
S_DA = 32
S_DB = 40

VMEM_LIMIT = 56 * 1024 * 1024


def _cparams(sem):
    return pltpu.CompilerParams(dimension_semantics=sem, vmem_limit_bytes=VMEM_LIMIT)


def _bdot(a, b):
    return jnp.dot(a.astype(BF16), b.astype(BF16), preferred_element_type=F32)


def _bdot_nt(a, b):
    return lax.dot_general(a.astype(BF16), b.astype(BF16), (((1,), (1,)), ((), ())),
                           preferred_element_type=F32)


def _bdot_tn(a, b):
    return lax.dot_general(a.astype(BF16), b.astype(BF16), (((0,), (0,)), ((), ())),
                           preferred_element_type=F32)


def _split3(x):
    hi = x.astype(BF16)
    r = x - hi.astype(F32)
    mid = r.astype(BF16)
    lo = (r - mid.astype(F32)).astype(BF16)
    return hi, mid, lo


def _dot_exact_lhs(lhs_bf16, x):
    hi, mid, lo = _split3(x)
    d = functools.partial(jnp.dot, preferred_element_type=F32)
    return d(lhs_bf16, hi) + d(lhs_bf16, mid) + d(lhs_bf16, lo)


def _dot_exact_lhs_wide(lhs_bf16, x):
    n = x.shape[1]
    hi = x.astype(BF16)
    mid = (x - hi.astype(F32)).astype(BF16)
    r = jnp.dot(lhs_bf16, jnp.concatenate([hi, mid], axis=1), preferred_element_type=F32)
    return r[:, 0:n] + r[:, n:2 * n]


def _dot3(a, b):
    ah, am, _ = _split3(a)
    bh, bm, _ = _split3(b)
    d = functools.partial(jnp.dot, preferred_element_type=F32)
    return d(ah, bh) + (d(ah, bm) + d(am, bh))


def _silu(x):
    return x * jax.nn.sigmoid(x)


def _softplus(x):
    return jnp.maximum(x, 0.0) + jnp.log1p(jnp.exp(-jnp.abs(x)))


def _rms(x, g):
    return x * lax.rsqrt(jnp.mean(x * x, axis=-1, keepdims=True) + EPS) * g


def _mods_kernel(c_ref, w_ref, b_ref, o_ref):
    o_ref[...] = _bdot(_silu(c_ref[...]), w_ref[...]) + b_ref[...]


def _modulation(cond8, w_mod, b_mod):
    depth = w_mod.shape[0]
    n_out = N_MOD * D_MODEL
    tn = 1024
    return pl.pallas_call(
        _mods_kernel,
        grid=(depth, n_out // tn),
        in_specs=[
            pl.BlockSpec((8, D_MODEL), lambda l, j: (0, 0)),
            pl.BlockSpec((None, D_MODEL, tn), lambda l, j: (l, 0, j)),
            pl.BlockSpec((None, 1, tn), lambda l, j: (l, 0, j)),
        ],
        out_specs=pl.BlockSpec((None, 8, tn), lambda l, j: (l, 0, j)),
        out_shape=jax.ShapeDtypeStruct((depth, 8, n_out), F32),
        compiler_params=_cparams(("arbitrary", "arbitrary")),
        name="modulation",
    )(cond8, w_mod, b_mod.reshape(depth, 1, n_out))


def _group_of_tile(i, tm, n_ctx_rows):
    n_ctx_tiles = n_ctx_rows // tm
    per_lat = DEC_SEQ // tm
    return jnp.where(i < n_ctx_tiles, 0, 1 + (i - n_ctx_tiles) // per_lat)


def _row_specs(arrays, width, tm, nct):
    if len(arrays) == 1:
        return [pl.BlockSpec((tm, width), lambda i, *_: (i, 0))]
    return [pl.BlockSpec((tm, width), lambda i, *_: (jnp.minimum(i, nct - 1), 0)),
            pl.BlockSpec((tm, width), lambda i, *_: (jnp.maximum(i - nct, 0), 0))]


def _row_load(refs, nct):
    if len(refs) == 1:
        return refs[0][...]
    return jnp.where(pl.program_id(0) < nct, refs[0][...], refs[1][...])


def _inproj_kernel(nct, n_x, *refs):
    x_refs = refs[:n_x]
    mod_ref, g_ref, w_ref, ws_ref, p_ref, ps_ref, h_sc = refs[n_x:]

    @pl.when(pl.program_id(1) == 0)
    def _():
        m = mod_ref[...]
        h = _rms(_row_load(x_refs, nct), g_ref[...]) * (1.0 + m[1:2]) + m[0:1]
        hb = h.astype(BF16)
        h_sc[...] = hb
        ps_ref[...] = jnp.dot(hb, ws_ref[...], preferred_element_type=F32)

    p_ref[...] = jnp.dot(h_sc[...], w_ref[...], preferred_element_type=F32)


def _inproj(xs, mods_l, g1, w_main, w_small, layer, n_ctx_rows, tm=1024, tn=512):
    rows = sum(x.shape[0] for x in xs)
    nct = n_ctx_rows // tm
    grp = functools.partial(_group_of_tile, tm=tm, n_ctx_rows=n_ctx_rows)
    return pl.pallas_call(
        functools.partial(_inproj_kernel, nct, len(xs)),
        grid=(rows // tm, P_MAIN // tn),
        in_specs=_row_specs(xs, D_MODEL, tm, nct) + [
            pl.BlockSpec((None, N_MOD, D_MODEL), lambda i, j: (grp(i), 0, 0)),
            pl.BlockSpec((1, D_MODEL), lambda i, j: (0, 0)),
            pl.BlockSpec((None, D_MODEL, tn), lambda i, j: (layer, 0, j)),
            pl.BlockSpec((None, D_MODEL, LANES), lambda i, j: (layer, 0, 0)),
        ],
        out_specs=[
            pl.BlockSpec((tm, tn), lambda i, j: (i, j)),
            pl.BlockSpec((tm, LANES), lambda i, j: (i, 0)),
        ],
        out_shape=[
            jax.ShapeDtypeStruct((rows, P_MAIN), F32),
            jax.ShapeDtypeStruct((rows, LANES), F32),
        ],
        scratch_shapes=[pltpu.VMEM((tm, D_MODEL), BF16)],
        compiler_params=_cparams(("arbitrary", "arbitrary")),
        name="inproj",
    )(*xs, mods_l, g1, w_main, w_small)


def _outproj_kernel(nct, n_x, *refs):
    x_refs = refs[:n_x]
    (ogc_ref, ogl_ref, oac_ref, oal_ref, odc_ref, odl_ref, mod_ref, g_ref, w_ref,
     x1_ref, h2_ref) = refs[n_x:]
    m = mod_ref[...]
    d = functools.partial(jnp.dot, preferred_element_type=F32)
    mix = (d(_row_load((ogc_ref, ogl_ref), nct), w_ref[0:512, :])
           + d(_row_load((oac_ref, oal_ref), nct), w_ref[512:1536, :])
           + d(_row_load((odc_ref, odl_ref), nct), w_ref[1536:2048, :]))
    x1 = _row_load(x_refs, nct) + m[2:3] * mix
    x1_ref[...] = x1
    h2_ref[...] = (_rms(x1, g_ref[...]) * (1.0 + m[4:5]) + m[3:4]).astype(BF16)


def _outproj(xs, o_gla, o_att, o_dn, mods_l, g2, w_out, layer, n_ctx_rows, tm=512):
    rows = sum(x.shape[0] for x in xs)
    nct = n_ctx_rows // tm
    grp = functools.partial(_group_of_tile, tm=tm, n_ctx_rows=n_ctx_rows)
    return pl.pallas_call(
        functools.partial(_outproj_kernel, nct, len(xs)),
        grid=(rows // tm,),
        in_specs=(_row_specs(xs, D_MODEL, tm, nct) + _row_specs(o_gla, 512, tm, nct)
                  + _row_specs(o_att, 1024, tm, nct) + _row_specs(o_dn, 512, tm, nct) + [
                      pl.BlockSpec((None, N_MOD, D_MODEL), lambda i: (grp(i), 0, 0)),
                      pl.BlockSpec((1, D_MODEL), lambda i: (0, 0)),
                      pl.BlockSpec((None, D_MODEL, D_MODEL), lambda i: (layer, 0, 0),
                                   pipeline_mode=pl.Buffered(1)),
                  ]),
        out_specs=[
            pl.BlockSpec((tm, D_MODEL), lambda i: (i, 0)),
            pl.BlockSpec((tm, D_MODEL), lambda i: (i, 0)),
        ],
        out_shape=[
            jax.ShapeDtypeStruct((rows, D_MODEL), F32),
            jax.ShapeDtypeStruct((rows, D_MODEL), BF16),
        ],
        compiler_params=_cparams(("arbitrary",)),
        name="outproj",
    )(*xs, *o_gla, *o_att, *o_dn, mods_l, g2, w_out)


FFN_TM = 512
FFN_OUT_CHUNK = 512


def _ffn_kernel(h_ref, x1_ref, mod_ref, w1_ref, w2_ref, y_ref):
    j = pl.program_id(1)
    tf = w1_ref.shape[1]

    @pl.when(j == 0)
    def _():
        y_ref[...] = jnp.zeros(y_ref.shape, F32)

    h = h_ref[...]
    for half in range(2):
        cols = slice(half * (tf // 2), (half + 1) * (tf // 2))
        t = jnp.dot(h, w1_ref[:, cols], preferred_element_type=F32)
        t = jnp.square(jnp.maximum(t, 0.0)).astype(BF16)
        for nc in range(D_MODEL // FFN_OUT_CHUNK):
            ncs = slice(nc * FFN_OUT_CHUNK, (nc + 1) * FFN_OUT_CHUNK)
            y_ref[:, ncs] += jnp.dot(t, w2_ref[cols, ncs], preferred_element_type=F32)

    @pl.when(j == pl.num_programs(1) - 1)
    def _():
        y_ref[...] = x1_ref[...] + mod_ref[5:6, :] * y_ref[...]


def _ffn(h2, x1, mods_l, w1, w2, layer, n_ctx_rows, tile0, n_tiles, tm, tf=1024):
    grp = functools.partial(_group_of_tile, tm=tm, n_ctx_rows=n_ctx_rows)
    return pl.pallas_call(
        _ffn_kernel,
        grid=(n_tiles, FF // tf),
        in_specs=[
            pl.BlockSpec((tm, D_MODEL), lambda i, j: (tile0 + i, 0)),
            pl.BlockSpec((tm, D_MODEL), lambda i, j: (tile0 + i, 0)),
            pl.BlockSpec((None, N_MOD, D_MODEL), lambda i, j: (grp(tile0 + i), 0, 0)),
            pl.BlockSpec((None, D_MODEL, tf), lambda i, j: (layer, 0, j)),
            pl.BlockSpec((None, tf, D_MODEL), lambda i, j: (layer, j, 0)),
        ],
        out_specs=pl.BlockSpec((tm, D_MODEL), lambda i, j: (i, 0)),
        out_shape=jax.ShapeDtypeStruct((n_tiles * tm, D_MODEL), F32),
        compiler_params=_cparams(("arbitrary", "arbitrary")),
        name="ffn",
    )(h2, x1, mods_l, w1, w2)


def _rope_tables():
    half = HEAD_DIM // 2
    pos = np.arange(DEC_SEQ)
    row = (pos // GRID_W).astype(np.float32)
    col = (pos % GRID_W).astype(np.float32)
    inv = (ROPE_THETA ** (-np.arange(0, half, 2, dtype=np.float32) / half)).astype(np.float32)
    ar = row[:, None] * inv[None, :]
    ac = col[:, None] * inv[None, :]
    cos = np.concatenate([np.cos(ar), np.cos(ar), np.cos(ac), np.cos(ac)], axis=1)
    sin = np.concatenate([-np.sin(ar), np.sin(ar), -np.sin(ac), np.sin(ac)], axis=1)
    return jnp.asarray(cos, F32), jnp.asarray(sin, F32)


def _rope(x, cos, sin):
    lane = lax.broadcasted_iota(jnp.int32, x.shape, 1)
    first = (lane % 64) < 32
    partner = jnp.where(first, pltpu.roll(x, 96, 1), pltpu.roll(x, 32, 1))
    return x * cos + partner * sin


ATT_G = ATT_HQ // ATT_HKV


def _softmax_pv(q, kb, vb):
    s = lax.dot_general(q.astype(BF16), kb, (((1,), (1,)), ((), ())), preferred_element_type=F32)
    m = jnp.max(s, axis=-1, keepdims=True)
    p = jnp.exp(s - m)
    l = jnp.sum(p, axis=-1, keepdims=True)
    return jnp.dot(p.astype(BF16), vb, preferred_element_type=F32) / l


def _group_attention(q_heads, kb, vb, store):
    for g, q in enumerate(q_heads):
        store(g, _softmax_pv(q, kb, vb).astype(BF16))


def _att_ctx_kernel(aq_ref, akv_ref, qg_ref, kg_ref, o_ref, nk_ref, nv_ref):
    qg = qg_ref[...] * (HEAD_DIM ** -0.5)
    kg = kg_ref[...]
    for hk in range(ATT_HKV):
        ks = slice(hk * HEAD_DIM, (hk + 1) * HEAD_DIM)
        kn = _rms(akv_ref[:, ks], kg)
        v = akv_ref[:, 256 + hk * HEAD_DIM:256 + (hk + 1) * HEAD_DIM]
        nk_ref[:, ks] = kn
        nv_ref[:, ks] = v

        def head_cols(g, hk=hk):
            return slice((hk * ATT_G + g) * HEAD_DIM, (hk * ATT_G + g + 1) * HEAD_DIM)

        def store(g, val):
            o_ref[:, head_cols(g)] = val

        qs = [_rms(aq_ref[:, head_cols(g)], qg) for g in range(ATT_G)]
        _group_attention(qs, kn.astype(BF16), v.astype(BF16), store)


def _att_lat_kernel(aq_ref, akv_ref, qg_ref, kg_ref, cos_ref, sin_ref, ck_ref, cv_ref, o_ref,
                    k_sc, v_sc):
    qg = qg_ref[...] * (HEAD_DIM ** -0.5)
    kg = kg_ref[...]
    qb = 256
    for hk in range(ATT_HKV):
        ks = slice(hk * HEAD_DIM, (hk + 1) * HEAD_DIM)
        k_sc[hk, 0:PAST_LEN, :] = ck_ref[:, ks].astype(BF16)
        v_sc[hk, 0:PAST_LEN, :] = cv_ref[:, ks].astype(BF16)
        kn = _rope(_rms(akv_ref[:, ks], kg), cos_ref[...], sin_ref[...])
        k_sc[hk, PAST_LEN:, :] = kn.astype(BF16)
        v_sc[hk, PAST_LEN:, :] = akv_ref[:, 256 + hk * HEAD_DIM:256 + (hk + 1) * HEAD_DIM].astype(BF16)

    def body(r, carry):
        r0 = pl.multiple_of(r * qb, qb)
        cos = cos_ref[pl.ds(r0, qb), :]
        sin = sin_ref[pl.ds(r0, qb), :]
        for hk in range(ATT_HKV):
            def head_cols(g, hk=hk):
                return slice((hk * ATT_G + g) * HEAD_DIM, (hk * ATT_G + g + 1) * HEAD_DIM)

            def store(g, val):
                o_ref[pl.ds(r0, qb), head_cols(g)] = val

            qs = [_rope(_rms(aq_ref[pl.ds(r0, qb), head_cols(g)], qg), cos, sin)
                  for g in range(ATT_G)]
            _group_attention(qs, k_sc[hk], v_sc[hk], store)
        return carry

    lax.fori_loop(0, DEC_SEQ // qb, body, 0)


def _att_ctx(p_main, qg, kg, n_seq):
    L = SEQ
    return pl.pallas_call(
        _att_ctx_kernel,
        grid=(n_seq,),
        in_specs=[
            pl.BlockSpec((L, 1024), lambda b: (b, P_AQ // 1024)),
            pl.BlockSpec((L, 512), lambda b: (b, P_AKV // 512)),
            pl.BlockSpec((1, HEAD_DIM), lambda b: (0, 0)),
            pl.BlockSpec((1, HEAD_DIM), lambda b: (0, 0)),
        ],
        out_specs=[
            pl.BlockSpec((L, 1024), lambda b: (b, 0)),
            pl.BlockSpec((None, L, 256), lambda b: (b, 0, 0)),
            pl.BlockSpec((None, L, 256), lambda b: (b, 0, 0)),
        ],
        out_shape=[
            jax.ShapeDtypeStruct((n_seq * L, 1024), BF16),
            jax.ShapeDtypeStruct((n_seq, L, 256), F32),
            jax.ShapeDtypeStruct((n_seq, L, 256), F32),
        ],
        compiler_params=_cparams(("arbitrary",)),
        name="att_ctx",
    )(p_main, p_main, qg, kg)


def _att_lat(p_main, qg, kg, cos, sin, ck, cv, layer, n_seq, row_blk0):
    L = DEC_SEQ
    return pl.pallas_call(
        _att_lat_kernel,
        grid=(n_seq,),
        in_specs=[
            pl.BlockSpec((L, 1024), lambda b: (row_blk0 + b, P_AQ // 1024)),
            pl.BlockSpec((L, 512), lambda b: (row_blk0 + b, P_AKV // 512)),
            pl.BlockSpec((1, HEAD_DIM), lambda b: (0, 0)),
            pl.BlockSpec((1, HEAD_DIM), lambda b: (0, 0)),
            pl.BlockSpec((L, HEAD_DIM), lambda b: (0, 0)),
            pl.BlockSpec((L, HEAD_DIM), lambda b: (0, 0)),
            pl.BlockSpec((None, None, PAST_LEN, 256), lambda b: (b, layer, 0, 0)),
            pl.BlockSpec((None, None, PAST_LEN, 256), lambda b: (b, layer, 0, 0)),
        ],
        out_specs=pl.BlockSpec((L, 1024), lambda b: (b, 0)),
        out_shape=jax.ShapeDtypeStruct((n_seq * L, 1024), BF16),
        scratch_shapes=[
            pltpu.VMEM((ATT_HKV, PAST_LEN + L, HEAD_DIM), BF16),
            pltpu.VMEM((ATT_HKV, PAST_LEN + L, HEAD_DIM), BF16),
        ],
        compiler_params=_cparams(("arbitrary",)),
        name="att_lat",
    )(p_main, p_main, qg, kg, cos, sin, ck, cv)


def _seqs_per_step(L, n_seq):
    return 2 if (L <= SEQ and n_seq % 2 == 0) else 1


_GLA_LEVELS = (32, 16, 8, 4, 2, 1)


def _gla_consts():
    C = CHUNK
    i = np.arange(C)[:, None]
    t = np.arange(C)[None, :]
    mats = [t <= i, t > i]
    masks = [i == t]
    for s in _GLA_LEVELS:
        m = (i // s) * s
        if s > 1:
            mats.append((t > m) & (t <= i))
        mats.append((t > i) & (t <= m + s))
        masks.append((i // (2 * s) == t // (2 * s)) & ((i // s) % 2 == 1) & ((t // s) % 2 == 0))
    fwd = np.concatenate(mats, 0)
    bwd = np.concatenate([mm[::-1, ::-1] for mm in mats], 0)
    mstack = np.stack([fwd, bwd]).astype(np.float32)
    mf = np.stack([np.concatenate([mm, mm], 0) for mm in masks])
    mb = np.stack([np.concatenate([mm[::-1, ::-1], mm[::-1, ::-1]], 0) for mm in masks])
    lmask = np.stack([mf, mb]).astype(np.float32)
    return jnp.asarray(mstack, BF16), jnp.asarray(lmask, F32)


def _gla_kernel(L, nb, has_state, *refs):
    if has_state:
        (pg_ref, ps_ref, w2_ref, gb_ref, ng_ref, ms_ref, lm_ref, s0_ref,
         o_ref, so_ref, la_sc, of_sc, ob_sc, st_sc) = refs
    else:
        (pg_ref, ps_ref, w2_ref, gb_ref, ng_ref, ms_ref, lm_ref,
         o_ref, so_ref, la_sc, of_sc, ob_sc, st_sc) = refs
    n = L // CHUNK
    C = CHUNK

    ps = ps_ref[...]
    for z in range(2):
        x = _dot3(ps, w2_ref[z]) + gb_ref[z]
        la_sc[z] = (jnp.minimum(x, 0.0) - jnp.log1p(jnp.exp(-jnp.abs(x)))) * (1.0 / GLA_GATE_NORM)

    if has_state:
        for sq in range(nb):
            for z in range(2):
                for p in range(2):
                    pair = jnp.concatenate([s0_ref[sq, z, 2 * p], s0_ref[sq, z, 2 * p + 1]], axis=0)
                    st_sc[sq, z, p] = jnp.transpose(pair)
    else:
        st_sc[...] = jnp.zeros(st_sc.shape, F32)

    lane = lax.broadcasted_iota(jnp.int32, (C, LANES), 1)
    first = lane < 64

    def stack_heads(t):
        return jnp.concatenate([jnp.where(first, t, 0.0), jnp.where(first, 0.0, t)], axis=0)

    chains = [(sq, z, p) for sq in range(nb) for z in range(2) for p in range(2)]
    nch = len(chains)

    def body(c, carry):
        def chunk_rows(sq, z):
            cc = c if z == 0 else n - 1 - c
            return pl.ds(pl.multiple_of(sq * L + cc * C, C), C)

        rows = [chunk_rows(sq, z) for sq, z, p in chains]
        last = [C - 1, 0]
        q = [pg_ref[rows[i], p * LANES:(p + 1) * LANES] * (GLA_DK ** -0.5)
             for i, (sq, z, p) in enumerate(chains)]
        k = [pg_ref[rows[i], 256 + p * LANES:256 + (p + 1) * LANES]
             for i, (sq, z, p) in enumerate(chains)]
        e = [jnp.exp(_dot_exact_lhs_wide(ms_ref[z], la_sc[z, rows[i], p * LANES:(p + 1) * LANES]))
             for i, (sq, z, p) in enumerate(chains)]
        st = [st_sc[sq, z, p] for sq, z, p in chains]
        a2 = [_bdot_nt(stack_heads(q[i]), k[i]) * lm_ref[z, 0]
              for i, (sq, z, p) in enumerate(chains)]
        blk = 2
        for li, s in enumerate(_GLA_LEVELS):
            if s > 1:
                qs = [q[i] * e[i][blk * C:(blk + 1) * C] for i in range(nch)]
                blk += 1
            else:
                qs = q
            ks = [k[i] * e[i][blk * C:(blk + 1) * C] for i in range(nch)]
            blk += 1
            a2 = [a2[i] + _bdot_nt(stack_heads(qs[i]), ks[i]) * lm_ref[z, li + 1]
                  for i, (sq, z, p) in enumerate(chains)]
        inter = [_bdot_nt(stack_heads(q[i] * e[i][0:C]), st[i]) for i in range(nch)]
        kl = [k[i] * e[i][C:2 * C] for i in range(nch)]
        for i, (sq, z, p) in enumerate(chains):
            osc = of_sc if z == 0 else ob_sc
            zs = []
            for hh in range(2):
                h = 2 * p + hh
                v = pg_ref[rows[i], 512 + h * LANES:512 + (h + 1) * LANES]
                o = _bdot(a2[i][hh * C:(hh + 1) * C], v) + inter[i][hh * C:(hh + 1) * C]
                osc[rows[i], h * LANES:(h + 1) * LANES] = o
                zs.append(_bdot_tn(v, kl[i]))
            st_sc[sq, z, p] = (st[i] * e[i][last[z]:last[z] + 1, :]
                               + jnp.where(first[0:1], zs[0], zs[1]))
        return carry

    lax.fori_loop(0, n, body, 0)

    ng = ng_ref[...]

    def fin(c, carry):
        r0 = pl.multiple_of(c * C, C)
        rows = pl.ds(r0, C)
        for h in range(GLA_H):
            cols = slice(h * LANES, (h + 1) * LANES)
            o = of_sc[rows, cols] + ob_sc[rows, cols]
            gate = _silu(pg_ref[rows, 1024 + h * LANES:1024 + (h + 1) * LANES])
            o_ref[rows, cols] = (_rms(o, ng) * gate).astype(BF16)
        return carry

    lax.fori_loop(0, nb * n, fin, 0)
    for sq in range(nb):
        for z in range(2):
            for p in range(2):
                pair = jnp.transpose(st_sc[sq, z, p])
                so_ref[sq, z, 2 * p] = pair[0:GLA_DK]
                so_ref[sq, z, 2 * p + 1] = pair[GLA_DK:2 * GLA_DK]


def _gla(p_main, p_small, w2p, gbias, ng, consts, s0, layer, L, n_seq, row_blk0):
    mstack, lmask = consts
    has_state = s0 is not None
    nb = _seqs_per_step(L, n_seq)
    assert row_blk0 % nb == 0
    blk0 = row_blk0 // nb
    state_blk = (nb, 2, GLA_H, GLA_DK, GLA_DV)
    in_specs = [
        pl.BlockSpec((nb * L, 1536), lambda b: (blk0 + b, P_GLA // 1536)),
        pl.BlockSpec((nb * L, LANES), lambda b: (blk0 + b, 0)),
        pl.BlockSpec((2, LANES, 256), lambda b: (0, 0, 0)),
        pl.BlockSpec((2, 1, 256), lambda b: (0, 0, 0)),
        pl.BlockSpec((1, GLA_DV), lambda b: (0, 0)),
        pl.BlockSpec(mstack.shape, lambda b: (0, 0, 0)),
        pl.BlockSpec(lmask.shape, lambda b: (0, 0, 0, 0)),
    ]
    args = [p_main, p_small, w2p, gbias, ng, mstack, lmask]
    if has_state:
        in_specs.append(pl.BlockSpec((nb, None) + state_blk[1:], lambda b: (b, layer, 0, 0, 0, 0)))
        args.append(s0)
    return pl.pallas_call(
        functools.partial(_gla_kernel, L, nb, has_state),
        grid=(n_seq // nb,),
        in_specs=in_specs,
        out_specs=[
            pl.BlockSpec((nb * L, 512), lambda b: (b, 0)),
            pl.BlockSpec(state_blk, lambda b: (b, 0, 0, 0, 0)),
        ],
        out_shape=[
            jax.ShapeDtypeStruct((n_seq * L, 512), BF16),
            jax.ShapeDtypeStruct((n_seq, 2, GLA_H, GLA_DK, GLA_DV), F32),
        ],
        scratch_shapes=[
            pltpu.VMEM((2, nb * L, 256), F32),
            pltpu.VMEM((nb * L, 512), F32),
            pltpu.VMEM((nb * L, 512), F32),
            pltpu.VMEM((nb, 2, 2, GLA_DV, LANES), F32),
        ],
        compiler_params=_cparams(("arbitrary",)),
        name="gla_lat" if has_state else "gla_ctx",
    )(*args)


def _dn_kernel(L, nb, has_state, *refs):
    if has_state:
        (pd_ref, dz_ref, ps_ref, cw_ref, al_ref, dtb_ref, ng_ref, s0_ref,
         o_ref, so_ref, qkv_sc, g_sc, bb_sc, of_sc, ob_sc, s_sc) = refs
    else:
        (pd_ref, dz_ref, ps_ref, cw_ref, al_ref, dtb_ref, ng_ref,
         o_ref, so_ref, qkv_sc, g_sc, bb_sc, of_sc, ob_sc, s_sc) = refs
    n = L // CHUNK
    C = CHUNK

    R = nb * L
    pos = lax.broadcasted_iota(jnp.int32, (R, LANES), 0) & (L - 1)
    not_first = pos > 0
    not_last = pos < L - 1

    def conv_tile(j, carry):
        c0 = pl.multiple_of(j * LANES, LANES)
        x = pd_ref[:, pl.ds(c0, LANES)]
        w = cw_ref[:, pl.ds(c0, LANES)]
        prev = jnp.where(not_first, pltpu.roll(x, 1, 0), 0.0)
        nxt = jnp.where(not_last, pltpu.roll(x, R - 1, 0), 0.0)
        y = _silu(prev * w[0:1] + x * w[1:2] + nxt * w[2:3])
        inv = lax.rsqrt(jnp.sum(y * y, axis=-1, keepdims=True) + EPS)
        scale = jnp.where(j < DN_H, inv * (DN_DK ** -0.5), jnp.where(j < 2 * DN_H, inv, 1.0))
        qkv_sc[:, pl.ds(c0, LANES)] = y * scale
        return carry

    lax.fori_loop(0, 3 * DN_H, conv_tile, 0)

    ps = ps_ref[...]
    lane_l = lax.broadcasted_iota(jnp.int32, (R, LANES), 1)
    g_sc[...] = -jnp.exp(al_ref[...]) * _softplus(ps + dtb_ref[...])
    b_all = jax.nn.sigmoid(ps)
    for idx in range(2 * DN_H):
        bcol = jnp.sum(jnp.where(lane_l == S_DB + idx, b_all, 0.0), axis=-1, keepdims=True)
        bb_sc[idx] = jnp.broadcast_to(bcol, (R, LANES))

    if has_state:
        s_sc[...] = s0_ref[...]
    else:
        s_sc[...] = jnp.zeros(s_sc.shape, F32)

    ri = lax.broadcasted_iota(jnp.int32, (C, LANES), 0)
    lane_c = lax.broadcasted_iota(jnp.int32, (C, LANES), 1)
    ci = lane_c & (C - 1)
    first = lane_c < C
    incl = [ci <= ri, ci >= ri]
    strict = [ci < ri, ci > ri]
    rs = lax.broadcasted_iota(jnp.int32, (C, C), 0)
    cs = lax.broadcasted_iota(jnp.int32, (C, C), 1)
    m_b = [(cs <= rs).astype(BF16), (cs >= rs).astype(BF16)]
    chains = [(sq, z, h) for sq in range(nb) for z in range(2) for h in range(DN_H)]
    nch = len(chains)
    pairs = [(sq, z, p) for sq in range(nb) for z in range(2) for p in range(DN_H // 2)]
    npr = len(pairs)

    def stack_heads(t):
        return jnp.concatenate([jnp.where(first, t, 0.0), jnp.where(first, 0.0, t)], axis=0)

    def hi_mid(x):
        hi = x.astype(BF16).astype(F32)
        return hi, x - hi

    def pair_products(lhs_list, x):
        xh, xm = hi_mid(x)
        rhs = jnp.concatenate([stack_heads(xh), stack_heads(xm)], axis=1).astype(BF16)
        parts = []
        for l in lhs_list:
            parts.extend(hi_mid(l))
        r = jnp.dot(jnp.concatenate(parts, axis=0).astype(BF16), rhs, preferred_element_type=F32)
        out = []
        for i in range(len(lhs_list)):
            blk = r[2 * C * i:2 * C * i + C] + r[2 * C * i + C:2 * C * (i + 1)]
            out.append(blk[:, 0:LANES] + blk[:, LANES:2 * LANES])
        return out

    def body(c, carry):
        def chunk_rows(sq, z):
            cc = c if z == 0 else n - 1 - c
            return pl.ds(pl.multiple_of(sq * L + cc * C, C), C)

        rows = [chunk_rows(sq, z) for sq in range(nb) for z in range(2)]
        last = [C - 1, 0]

        def ld(base, sq, z, h):
            return qkv_sc[rows[2 * sq + z], base + h * LANES:base + (h + 1) * LANES]

        q = [ld(0, *ch) for ch in chains]
        k = [ld(512, *ch) for ch in chains]
        v = [ld(1024, *ch) for ch in chains]
        bb = [bb_sc[z * DN_H + h, rows[2 * sq + z], :] for sq, z, h in chains]
        gc = [_dot_exact_lhs(m_b[z], g_sc[rows[2 * sq + z], :])
              for sq in range(nb) for z in range(2)]
        gct = [jnp.transpose(jnp.concatenate([g, g], axis=0)) for g in gc]
        gi = [jnp.broadcast_to(jnp.sum(jnp.where(lane_c == S_DA + z * DN_H + h, gc[2 * sq + z], 0.0),
                                       axis=-1, keepdims=True), (C, LANES)) for sq, z, h in chains]
        gjrow = [gct[2 * sq + z][S_DA + z * DN_H + h:S_DA + z * DN_H + h + 1, :]
                 for sq, z, h in chains]
        heads = [((2 * sq + z) * DN_H + 2 * p, (2 * sq + z) * DN_H + 2 * p + 1)
                 for sq, z, p in pairs]
        gam = []
        for (sq, z, p), (i0, i1) in zip(pairs, heads):
            diff = jnp.where(first, gi[i0], gi[i1]) - jnp.where(first[0:1], gjrow[i0], gjrow[i1])
            gam.append(jnp.where(incl[z], jnp.exp(jnp.where(incl[z], diff, 0.0)), 0.0))
        kb = [k[i] * bb[i] for i in range(nch)]
        kq = [_bdot_nt(jnp.concatenate([kb[i0], q[i0], kb[i1], q[i1]], axis=0),
                       jnp.concatenate([k[i0], k[i1]], axis=0)) for i0, i1 in heads]
        a = [jnp.where(strict[z], jnp.where(first, kq[j][0:C], kq[j][2 * C:3 * C]) * gam[j], 0.0)
             for j, (sq, z, p) in enumerate(pairs)]
        att = [jnp.where(first, kq[j][C:2 * C], kq[j][3 * C:4 * C]) * gam[j] for j in range(npr)]
        nt = [-a[j] for j in range(npr)]
        pw = [pair_products([a[j]], a[j])[0] for j in range(npr)]
        for step in range(5):
            if step < 4:
                both = [pair_products([nt[j], pw[j]], pw[j]) for j in range(npr)]
                nt = [nt[j] + pw[j] + both[j][0] for j in range(npr)]
                pw = [both[j][1] for j in range(npr)]
            else:
                nt = [nt[j] + pw[j] + pair_products([nt[j]], pw[j])[0] for j in range(npr)]
        eg = [jnp.exp(gi[i]) for i in range(nch)]
        rhs = [jnp.concatenate([v[i] * bb[i], kb[i] * eg[i]], axis=1) for i in range(nch)]
        corr = [_bdot(stack_heads(nt[j]), jnp.concatenate([rhs[i0], rhs[i1]], axis=0))
                for j, (i0, i1) in enumerate(heads)]
        uw = [None] * nch
        for j, (i0, i1) in enumerate(heads):
            uw[i0] = rhs[i0] + corr[j][0:C]
            uw[i1] = rhs[i1] + corr[j][C:2 * C]
        s = [s_sc[sq, z, h] for sq, z, h in chains]
        ws = [_bdot(jnp.concatenate([uw[i][:, LANES:], q[i] * eg[i]], axis=0), s[i])
              for i in range(nch)]
        v_new = [uw[i][:, 0:LANES] - ws[i][0:C] for i in range(nch)]
        av = [_bdot(stack_heads(att[j]), jnp.concatenate([v_new[i0], v_new[i1]], axis=0))
              for j, (i0, i1) in enumerate(heads)]
        o = [None] * nch
        for j, (i0, i1) in enumerate(heads):
            o[i0] = ws[i0][C:2 * C] + av[j][0:C]
            o[i1] = ws[i1][C:2 * C] + av[j][C:2 * C]
        for i, (sq, z, h) in enumerate(chains):
            osc = of_sc if z == 0 else ob_sc
            osc[rows[2 * sq + z], h * LANES:(h + 1) * LANES] = o[i]
        gt = [gi[i][last[z]:last[z] + 1, :] for i, (sq, z, h) in enumerate(chains)]
        upd = [_bdot_tn(k[i] * jnp.exp(gt[i] - gi[i]), v_new[i]) for i in range(nch)]
        for i, (sq, z, h) in enumerate(chains):
            s_sc[sq, z, h] = s[i] * jnp.exp(gt[i]) + upd[i]
        return carry

    lax.fori_loop(0, n, body, 0)

    ng = ng_ref[...]

    def fin(c, carry):
        r0 = pl.multiple_of(c * C, C)
        rows = pl.ds(r0, C)
        for h in range(DN_H):
            cols = slice(h * LANES, (h + 1) * LANES)
            o = of_sc[rows, cols] + ob_sc[rows, cols]
            o_ref[rows, cols] = (_rms(o, ng) * _silu(dz_ref[rows, cols])).astype(BF16)
        return carry

    lax.fori_loop(0, nb * n, fin, 0)
    so_ref[...] = s_sc[...]


def _dn(p_main, p_small, conv_w, alog_row, dtb_row, ng, s0, layer, L, n_seq, row_blk0):
    has_state = s0 is not None
    nb = _seqs_per_step(L, n_seq)
    assert row_blk0 % nb == 0
    row_blk0, L_blk = row_blk0 // nb, nb * L
    in_specs = [
        pl.BlockSpec((L_blk, 1536), lambda b: (row_blk0 + b, P_DN // 1536)),
        pl.BlockSpec((L_blk, 512), lambda b: (row_blk0 + b, P_DZ // 512)),
        pl.BlockSpec((L_blk, LANES), lambda b: (row_blk0 + b, 0)),
        pl.BlockSpec((DN_CONV, 3 * DN_H * DN_DK), lambda b: (0, 0)),
        pl.BlockSpec((1, LANES), lambda b: (0, 0)),
        pl.BlockSpec((1, LANES), lambda b: (0, 0)),
        pl.BlockSpec((1, DN_DV), lambda b: (0, 0)),
    ]
    args = [p_main, p_main, p_small, conv_w, alog_row, dtb_row, ng]
    if has_state:
        in_specs.append(pl.BlockSpec((nb, None, 2, DN_H, DN_DK, DN_DV),
                                     lambda b: (b, layer, 0, 0, 0, 0)))
        args.append(s0)
    return pl.pallas_call(
        functools.partial(_dn_kernel, L, nb, has_state),
        grid=(n_seq // nb,),
        in_specs=in_specs,
        out_specs=[
            pl.BlockSpec((L_blk, 512), lambda b: (b, 0)),
            pl.BlockSpec((nb, 2, DN_H, DN_DK, DN_DV), lambda b: (b, 0, 0, 0, 0)),
        ],
        out_shape=[
            jax.ShapeDtypeStruct((n_seq * L, 512), BF16),
            jax.ShapeDtypeStruct((n_seq, 2, DN_H, DN_DK, DN_DV), F32),
        ],
        scratch_shapes=[
            pltpu.VMEM((L_blk, 1536), F32),
            pltpu.VMEM((L_blk, LANES), F32),
            pltpu.VMEM((2 * DN_H, L_blk, LANES), F32),
            pltpu.VMEM((L_blk, 512), F32),
            pltpu.VMEM((L_blk, 512), F32),
            pltpu.VMEM((nb, 2, DN_H, DN_DK, DN_DV), F32),
        ],
        compiler_params=_cparams(("arbitrary",)),
        name="dn_lat" if has_state else "dn_ctx",
    )(*args)


def _prep_w_in(w_in):
    off = np.cumsum((0, 256, 256, 512, 512, 32, 1024, 256, 256, 1536, 512, 8, 8))
    gq, gk, gv, gr, glr, aq, ak, av, dqkv, dz, da, db = [
        w_in[:, :, int(off[i]):int(off[i + 1])] for i in range(12)]
    main = jnp.concatenate([aq, ak, av, gq, gk, gv, gr, dqkv, dz], axis=2).astype(BF16)
    pad = jnp.zeros(w_in.shape[:2] + (LANES - 48,), w_in.dtype)
    small = jnp.concatenate([glr, da, db, pad], axis=2).astype(BF16)
    return main, small


def _lane_row(vals, offset):
    return jnp.zeros((1, LANES), F32).at[0, offset:offset + vals.shape[0]].set(vals.astype(F32))


def kernel(x_prompt, x_sample, cache_k, cache_v, state_gla, state_dn, c, c_ctx, norm1_g, norm2_g,
           w_mod, b_mod, w_in, gla_w2, gla_b, gla_norm_g, q_norm_g, k_norm_g, dn_conv, dn_a_log,
           dn_dt_bias, dn_norm_g, w_out, w_ff1, w_ff2):
    n_ctx, n_lat = x_prompt.shape[0], x_sample.shape[0]
    depth = w_in.shape[0]
    n_ctx_rows = n_ctx * SEQ
    assert n_ctx_rows % DEC_SEQ == 0 and n_lat <= 7
    lat_blk0 = n_ctx_rows // DEC_SEQ

    xs = [x_prompt.reshape(n_ctx_rows, D_MODEL), x_sample.reshape(n_lat * DEC_SEQ, D_MODEL)]
    cond8 = jnp.concatenate([c_ctx[None, :], c, jnp.zeros((7 - n_lat, D_MODEL), F32)], axis=0)
    mods = _modulation(cond8, w_mod, b_mod).reshape(depth, 8, N_MOD, D_MODEL)

    cos, sin = _rope_tables()
    gla_consts = _gla_consts()
    ck = cache_k.reshape(n_lat, depth, PAST_LEN, ATT_HKV * HEAD_DIM)
    cv = cache_v.reshape(n_lat, depth, PAST_LEN, ATT_HKV * HEAD_DIM)

    w_main, w_small = _prep_w_in(w_in)
    w_out_b, w_ff1_b, w_ff2_b = w_out.astype(BF16), w_ff1.astype(BF16), w_ff2.astype(BF16)

    nks, nvs, sgs, sds = [], [], [], []
    for l in range(depth):
        p_main, p_small = _inproj(xs, mods[l], norm1_g[l][None, :], w_main, w_small, l,
                                  n_ctx_rows)

        qg, kg = q_norm_g[l][None, :], k_norm_g[l][None, :]
        oa_c, nk, nv = _att_ctx(p_main, qg, kg, n_ctx)
        oa_l = _att_lat(p_main, qg, kg, cos, sin, ck, cv, l, n_lat, lat_blk0)

        w2p = jnp.zeros((2, LANES, GLA_H * GLA_DK), F32)
        w2p = w2p.at[0, 0:GLA_LR].set(gla_w2[l, 0]).at[1, GLA_LR:2 * GLA_LR].set(gla_w2[l, 1])
        gbias = gla_b[l][:, None, :]
        gng = gla_norm_g[l][None, :]
        og_c, sg = _gla(p_main, p_small, w2p, gbias, gng, gla_consts, None, l, SEQ, n_ctx, 0)
        og_l, _ = _gla(p_main, p_small, w2p, gbias, gng, gla_consts, state_gla, l,
                       DEC_SEQ, n_lat, lat_blk0)

        alog_row = _lane_row(dn_a_log[l].reshape(-1), S_DA)
        dtb_row = _lane_row(dn_dt_bias[l].reshape(-1), S_DA)
        dng = dn_norm_g[l][None, :]
        od_c, sd = _dn(p_main, p_small, dn_conv[l], alog_row, dtb_row, dng, None, l,
                       SEQ, n_ctx, 0)
        od_l, _ = _dn(p_main, p_small, dn_conv[l], alog_row, dtb_row, dng, state_dn, l,
                      DEC_SEQ, n_lat, lat_blk0)

        x1, h2 = _outproj(xs, [og_c, og_l], [oa_c, oa_l], [od_c, od_l], mods[l],
                          norm2_g[l][None, :], w_out_b, l, n_ctx_rows)
        n_tiles, nct = x1.shape[0] // FFN_TM, n_ctx_rows // FFN_TM
        ffn = functools.partial(_ffn, h2, x1, mods[l], w_ff1_b, w_ff2_b, l, n_ctx_rows, tm=FFN_TM)
        if l < depth - 1:
            xs = [ffn(0, n_tiles)]
        else:
            xs = [ffn(0, nct), ffn(nct, n_tiles - nct)]

        nks.append(nk.reshape(n_ctx, SEQ, ATT_HKV, HEAD_DIM))
        nvs.append(nv.reshape(n_ctx, SEQ, ATT_HKV, HEAD_DIM))
        sgs.append(sg)
        sds.append(sd)

    y_prompt = xs[0].reshape(n_ctx, SEQ, D_MODEL)
    y_sample = xs[1].reshape(n_lat, DEC_SEQ, D_MODEL)
    return (y_prompt, y_sample, jnp.stack(nks, axis=1), jnp.stack(nvs, axis=1),
            jnp.stack(sgs, axis=1), jnp.stack(sds, axis=1))
```

```python
import functools
import math

import numpy as np
import jax
import jax.numpy as jnp
from jax import lax
from jax.experimental import pallas as pl
from jax.experimental.pallas import tpu as pltpu

F32 = jnp.float32
BF16 = jnp.bfloat16

D_MODEL = 2048
SEQ = 256
DEC_SEQ = 1024
PAST_LEN = 256
GRID_W = 64
HEAD_DIM = 128
EPS = 1e-6
GLA_H = 4
GLA_DK = 64
GLA_DV = 128
GLA_LR = 16
GLA_GATE_NORM = 16.0
ATT_HQ = 8
ATT_HKV = 2
ROPE_THETA = 10000.0
DN_H = 4
DN_DK = 128
DN_DV = 128
DN_CONV = 3
CHUNK = 64
FF = 4 * D_MODEL
N_MOD = 6
LANES = 128

P_AQ = 0
P_AKV = 1024
P_GLA = 1536
P_DN = 3072
P_DZ = 4608
P_MAIN = 5120
S_DA = 32
S_DB = 40

VMEM_LIMIT = 56 * 1024 * 1024


def _cparams(sem):
    return pltpu.CompilerParams(dimension_semantics=sem, vmem_limit_bytes=VMEM_LIMIT)


def _bdot(a, b):
    return jnp.dot(a.astype(BF16), b.astype(BF16), preferred_element_type=F32)


def _bdot_nt(a, b):
    return lax.dot_general(a.astype(BF16), b.astype(BF16), (((1,), (1,)), ((), ())),
                           preferred_element_type=F32)


def _bdot_tn(a, b):
    return lax.dot_general(a.astype(BF16), b.astype(BF16), (((0,), (0,)), ((), ())),
                           preferred_element_type=F32)


def _split3(x):
    hi = x.astype(BF16)
    r = x - hi.astype(F32)
    mid = r.astype(BF16)
    lo = (r - mid.astype(F32)).astype(BF16)
    return hi, mid, lo


def _dot_exact_lhs(lhs_bf16, x):
    hi, mid, lo = _split3(x)
    d = functools.partial(jnp.dot, preferred_element_type=F32)
    return d(lhs_bf16, hi) + d(lhs_bf16, mid) + d(lhs_bf16, lo)


def _dot_exact_lhs_wide(lhs_bf16, x):
    n = x.shape[1]
    hi = x.astype(BF16)
    mid = (x - hi.astype(F32)).astype(BF16)
    r = jnp.dot(lhs_bf16, jnp.concatenate([hi, mid], axis=1), preferred_element_type=F32)
    return r[:, 0:n] + r[:, n:2 * n]


def _dot3(a, b):
    ah, am, _ = _split3(a)
    bh, bm, _ = _split3(b)
    d = functools.partial(jnp.dot, preferred_element_type=F32)
    return d(ah, bh) + (d(ah, bm) + d(am, bh))


def _sigmoid(x):
    return 0.5 * jnp.tanh(0.5 * x) + 0.5


def _silu(x):
    return x * _sigmoid(x)


def _softplus(x):
    return jnp.maximum(x, 0.0) + jnp.log1p(jnp.exp(-jnp.abs(x)))


def _rms(x, g):
    return x * lax.rsqrt(jnp.mean(x * x, axis=-1, keepdims=True) + EPS) * g


def _mods_kernel(c_ref, w_ref, b_ref, o_ref):
    o_ref[...] = _bdot(_silu(c_ref[...]), w_ref[...]) + b_ref[...]


def _modulation(cond8, w_mod, b_mod):
    depth = w_mod.shape[0]
    n_out = N_MOD * D_MODEL
    tn = 1024
    return pl.pallas_call(
        _mods_kernel,
        grid=(depth, n_out // tn),
        in_specs=[
            pl.BlockSpec((8, D_MODEL), lambda l, j: (0, 0)),
            pl.BlockSpec((None, D_MODEL, tn), lambda l, j: (l, 0, j)),
            pl.BlockSpec((None, 1, tn), lambda l, j: (l, 0, j)),
        ],
        out_specs=pl.BlockSpec((None, 8, tn), lambda l, j: (l, 0, j)),
        out_shape=jax.ShapeDtypeStruct((depth, 8, n_out), F32),
        compiler_params=_cparams(("arbitrary", "arbitrary")),
        name="modulation",
    )(cond8, w_mod, b_mod.reshape(depth, 1, n_out))


def _group_of_tile(i, tm, n_ctx_rows):
    n_ctx_tiles = n_ctx_rows // tm
    per_lat = DEC_SEQ // tm
    return jnp.where(i < n_ctx_tiles, 0, 1 + (i - n_ctx_tiles) // per_lat)


def _row_specs(arrays, width, tm, nct):
    if len(arrays) == 1:
        return [pl.BlockSpec((tm, width), lambda i, *_: (i, 0))]
    return [pl.BlockSpec((tm, width), lambda i, *_: (jnp.minimum(i, nct - 1), 0)),
            pl.BlockSpec((tm, width), lambda i, *_: (jnp.maximum(i - nct, 0), 0))]


def _row_load(refs, nct):
    if len(refs) == 1:
        return refs[0][...]
    return jnp.where(pl.program_id(0) < nct, refs[0][...], refs[1][...])


INPROJ_TM = 1024


def _inproj_kernel(x_ref, mod_ref, g_ref, w_ref, ws_ref, p_ref, ps_ref, h_sc):
    @pl.when(pl.program_id(1) == 0)
    def _():
        m = mod_ref[...]
        h = _rms(x_ref[...], g_ref[...]) * (1.0 + m[1:2]) + m[0:1]
        hb = h.astype(BF16)
        h_sc[...] = hb
        ps_ref[...] = jnp.dot(hb, ws_ref[...], preferred_element_type=F32)

    h = h_sc[...]
    half = w_ref.shape[1] // 2
    for piece in range(2):
        cols = slice(piece * half, (piece + 1) * half)
        p_ref[:, cols] = jnp.dot(h, w_ref[:, cols], preferred_element_type=F32)


def _inproj(x, mods_l, g1, w_main, w_small, layer, n_ctx_rows, tile0, n_tiles, grp_tile0, tn=1024):
    tm = INPROJ_TM
    grp = functools.partial(_group_of_tile, tm=tm, n_ctx_rows=n_ctx_rows)
    return pl.pallas_call(
        _inproj_kernel,
        grid=(n_tiles, P_MAIN // tn),
        in_specs=[
            pl.BlockSpec((tm, D_MODEL), lambda i, j: (tile0 + i, 0)),
            pl.BlockSpec((None, N_MOD, D_MODEL), lambda i, j: (grp(grp_tile0 + i), 0, 0)),
            pl.BlockSpec((1, D_MODEL), lambda i, j: (0, 0)),
            pl.BlockSpec((None, D_MODEL, tn), lambda i, j: (layer, 0, j)),
            pl.BlockSpec((None, D_MODEL, LANES), lambda i, j: (layer, 0, 0)),
        ],
        out_specs=[
            pl.BlockSpec((tm, tn), lambda i, j: (i, j)),
            pl.BlockSpec((tm, LANES), lambda i, j: (i, 0)),
        ],
        out_shape=[
            jax.ShapeDtypeStruct((n_tiles * tm, P_MAIN), F32),
            jax.ShapeDtypeStruct((n_tiles * tm, LANES), F32),
        ],
        scratch_shapes=[pltpu.VMEM((tm, D_MODEL), BF16)],
        compiler_params=_cparams(("arbitrary", "arbitrary")),
        name="inproj",
    )(x, mods_l, g1, w_main, w_small)


def _outproj_kernel(nct, n_x, *refs):
    x_refs = refs[:n_x]
    (ogc_ref, ogl_ref, oac_ref, oal_ref, odc_ref, odl_ref, mod_ref, g_ref, w_ref,
     x1_ref, h2_ref) = refs[n_x:]
    m = mod_ref[...]
    d = functools.partial(jnp.dot, preferred_element_type=F32)
    mix = (d(_row_load((ogc_ref, ogl_ref), nct), w_ref[0:512, :])
           + d(_row_load((oac_ref, oal_ref), nct), w_ref[512:1536, :])
           + d(_row_load((odc_ref, odl_ref), nct), w_ref[1536:2048, :]))
    x1 = _row_load(x_refs, nct) + m[2:3] * mix
    x1_ref[...] = x1
    h2_ref[...] = (_rms(x1, g_ref[...]) * (1.0 + m[4:5]) + m[3:4]).astype(BF16)


def _outproj(xs, o_gla, o_att, o_dn, mods_l, g2, w_out, layer, n_ctx_rows, tm=512):
    rows = sum(x.shape[0] for x in xs)
    nct = n_ctx_rows // tm
    grp = functools.partial(_group_of_tile, tm=tm, n_ctx_rows=n_ctx_rows)
    return pl.pallas_call(
        functools.partial(_outproj_kernel, nct, len(xs)),
        grid=(rows // tm,),
        in_specs=(_row_specs(xs, D_MODEL, tm, nct) + _row_specs(o_gla, 512, tm, nct)
                  + _row_specs(o_att, 1024, tm, nct) + _row_specs(o_dn, 512, tm, nct) + [
                      pl.BlockSpec((None, N_MOD, D_MODEL), lambda i: (grp(i), 0, 0)),
                      pl.BlockSpec((1, D_MODEL), lambda i: (0, 0)),
                      pl.BlockSpec((None, D_MODEL, D_MODEL), lambda i: (layer, 0, 0),
                                   pipeline_mode=pl.Buffered(1)),
                  ]),
        out_specs=[
            pl.BlockSpec((tm, D_MODEL), lambda i: (i, 0)),
            pl.BlockSpec((tm, D_MODEL), lambda i: (i, 0)),
        ],
        out_shape=[
            jax.ShapeDtypeStruct((rows, D_MODEL), F32),
            jax.ShapeDtypeStruct((rows, D_MODEL), BF16),
        ],
        compiler_params=_cparams(("arbitrary",)),
        name="outproj",
    )(*xs, *o_gla, *o_att, *o_dn, mods_l, g2, w_out)


FFN_TM = 512
FFN_OUT_CHUNK = 512


def _ffn_kernel(h_ref, x1_ref, mod_ref, w1_ref, w2_ref, y_ref):
    j = pl.program_id(1)
    tf = w1_ref.shape[1]

    @pl.when(j == 0)
    def _():
        y_ref[...] = jnp.zeros(y_ref.shape, F32)

    h = h_ref[...]
    for half in range(2):
        cols = slice(half * (tf // 2), (half + 1) * (tf // 2))
        t = jnp.dot(h, w1_ref[:, cols], preferred_element_type=F32)
        t = jnp.square(jnp.maximum(t, 0.0)).astype(BF16)
        for nc in range(D_MODEL // FFN_OUT_CHUNK):
            ncs = slice(nc * FFN_OUT_CHUNK, (nc + 1) * FFN_OUT_CHUNK)
            y_ref[:, ncs] += jnp.dot(t, w2_ref[cols, ncs], preferred_element_type=F32)

    @pl.when(j == pl.num_programs(1) - 1)
    def _():
        y_ref[...] = x1_ref[...] + mod_ref[5:6, :] * y_ref[...]


def _ffn(h2, x1, mods_l, w1, w2, layer, n_ctx_rows, tile0, n_tiles, tm, tf=1024):
    grp = functools.partial(_group_of_tile, tm=tm, n_ctx_rows=n_ctx_rows)
    return pl.pallas_call(
        _ffn_kernel,
        grid=(n_tiles, FF // tf),
        in_specs=[
            pl.BlockSpec((tm, D_MODEL), lambda i, j: (tile0 + i, 0)),
            pl.BlockSpec((tm, D_MODEL), lambda i, j: (tile0 + i, 0)),
            pl.BlockSpec((None, N_MOD, D_MODEL), lambda i, j: (grp(tile0 + i), 0, 0)),
            pl.BlockSpec((None, D_MODEL, tf), lambda i, j: (layer, 0, j)),
            pl.BlockSpec((None, tf, D_MODEL), lambda i, j: (layer, j, 0)),
        ],
        out_specs=pl.BlockSpec((tm, D_MODEL), lambda i, j: (i, 0)),
        out_shape=jax.ShapeDtypeStruct((n_tiles * tm, D_MODEL), F32),
        compiler_params=_cparams(("arbitrary", "arbitrary")),
        name="ffn",
    )(h2, x1, mods_l, w1, w2)


def _rope_tables():
    half = HEAD_DIM // 2
    pos = np.arange(DEC_SEQ)
    row = (pos // GRID_W).astype(np.float32)
    col = (pos % GRID_W).astype(np.float32)
    inv = (ROPE_THETA ** (-np.arange(0, half, 2, dtype=np.float32) / half)).astype(np.float32)
    ar = row[:, None] * inv[None, :]
    ac = col[:, None] * inv[None, :]
    cos = np.concatenate([np.cos(ar), np.cos(ar), np.cos(ac), np.cos(ac)], axis=1)
    sin = np.concatenate([-np.sin(ar), np.sin(ar), -np.sin(ac), np.sin(ac)], axis=1)
    return jnp.asarray(cos, F32), jnp.asarray(sin, F32)


def _rope(x, cos, sin):
    lane = lax.broadcasted_iota(jnp.int32, x.shape, 1)
    first = (lane % 64) < 32
    partner = jnp.where(first, pltpu.roll(x, 96, 1), pltpu.roll(x, 32, 1))
    return x * cos + partner * sin


ATT_G = ATT_HQ // ATT_HKV


def _softmax_pv(q, kb, vb):
    s = lax.dot_general(q.astype(BF16), kb, (((1,), (1,)), ((), ())), preferred_element_type=F32)
    m = jnp.max(s, axis=-1, keepdims=True)
    p = jnp.exp(s - m)
    l = jnp.sum(p, axis=-1, keepdims=True)
    return jnp.dot(p.astype(BF16), vb, preferred_element_type=F32) / l


def _group_attention(q_heads, kb, vb, store):
    for g, q in enumerate(q_heads):
        store(g, _softmax_pv(q, kb, vb).astype(BF16))


def _att_ctx_kernel(aq_ref, akv_ref, qg_ref, kg_ref, o_ref, nk_ref, nv_ref):
    qg = qg_ref[...] * (HEAD_DIM ** -0.5)
    kg = kg_ref[...]
    for hk in range(ATT_HKV):
        ks = slice(hk * HEAD_DIM, (hk + 1) * HEAD_DIM)
        kn = _rms(akv_ref[:, ks], kg)
        v = akv_ref[:, 256 + hk * HEAD_DIM:256 + (hk + 1) * HEAD_DIM]
        nk_ref[:, ks] = kn
        nv_ref[:, ks] = v

        def head_cols(g, hk=hk):
            return slice((hk * ATT_G + g) * HEAD_DIM, (hk * ATT_G + g + 1) * HEAD_DIM)

        def store(g, val):
            o_ref[:, head_cols(g)] = val

        qs = [_rms(aq_ref[:, head_cols(g)], qg) for g in range(ATT_G)]
        _group_attention(qs, kn.astype(BF16), v.astype(BF16), store)


def _att_lat_kernel(aq_ref, akv_ref, qg_ref, kg_ref, cos_ref, sin_ref, ck_ref, cv_ref, o_ref,
                    k_sc, v_sc):
    qg = qg_ref[...] * (HEAD_DIM ** -0.5)
    kg = kg_ref[...]
    qb = 256
    for hk in range(ATT_HKV):
        ks = slice(hk * HEAD_DIM, (hk + 1) * HEAD_DIM)
        k_sc[hk, 0:PAST_LEN, :] = ck_ref[:, ks].astype(BF16)
        v_sc[hk, 0:PAST_LEN, :] = cv_ref[:, ks].astype(BF16)
        kn = _rope(_rms(akv_ref[:, ks], kg), cos_ref[...], sin_ref[...])
        k_sc[hk, PAST_LEN:, :] = kn.astype(BF16)
        v_sc[hk, PAST_LEN:, :] = akv_ref[:, 256 + hk * HEAD_DIM:256 + (hk + 1) * HEAD_DIM].astype(BF16)

    def body(r, carry):
        r0 = pl.multiple_of(r * qb, qb)
        cos = cos_ref[pl.ds(r0, qb), :]
        sin = sin_ref[pl.ds(r0, qb), :]
        for hk in range(ATT_HKV):
            def head_cols(g, hk=hk):
                return slice((hk * ATT_G + g) * HEAD_DIM, (hk * ATT_G + g + 1) * HEAD_DIM)

            def store(g, val):
                o_ref[pl.ds(r0, qb), head_cols(g)] = val

            qs = [_rope(_rms(aq_ref[pl.ds(r0, qb), head_cols(g)], qg), cos, sin)
                  for g in range(ATT_G)]
            _group_attention(qs, k_sc[hk], v_sc[hk], store)
        return carry

    lax.fori_loop(0, DEC_SEQ // qb, body, 0)


def _att_ctx(p_main, qg, kg, n_seq):
    L = SEQ
    return pl.pallas_call(
        _att_ctx_kernel,
        grid=(n_seq,),
        in_specs=[
            pl.BlockSpec((L, 1024), lambda b: (b, P_AQ // 1024)),
            pl.BlockSpec((L, 512), lambda b: (b, P_AKV // 512)),
            pl.BlockSpec((1, HEAD_DIM), lambda b: (0, 0)),
            pl.BlockSpec((1, HEAD_DIM), lambda b: (0, 0)),
        ],
        out_specs=[
            pl.BlockSpec((L, 1024), lambda b: (b, 0)),
            pl.BlockSpec((None, L, 256), lambda b: (b, 0, 0)),
            pl.BlockSpec((None, L, 256), lambda b: (b, 0, 0)),
        ],
        out_shape=[
            jax.ShapeDtypeStruct((n_seq * L, 1024), BF16),
            jax.ShapeDtypeStruct((n_seq, L, 256), F32),
            jax.ShapeDtypeStruct((n_seq, L, 256), F32),
        ],
        compiler_params=_cparams(("arbitrary",)),
        name="att_ctx",
    )(p_main, p_main, qg, kg)


def _att_lat(p_main, qg, kg, cos, sin, ck, cv, layer, n_seq, row_blk0):
    L = DEC_SEQ
    return pl.pallas_call(
        _att_lat_kernel,
        grid=(n_seq,),
        in_specs=[
            pl.BlockSpec((L, 1024), lambda b: (row_blk0 + b, P_AQ // 1024)),
            pl.BlockSpec((L, 512), lambda b: (row_blk0 + b, P_AKV // 512)),
            pl.BlockSpec((1, HEAD_DIM), lambda b: (0, 0)),
            pl.BlockSpec((1, HEAD_DIM), lambda b: (0, 0)),
            pl.BlockSpec((L, HEAD_DIM), lambda b: (0, 0)),
            pl.BlockSpec((L, HEAD_DIM), lambda b: (0, 0)),
            pl.BlockSpec((None, None, PAST_LEN, 256), lambda b: (b, layer, 0, 0)),
            pl.BlockSpec((None, None, PAST_LEN, 256), lambda b: (b, layer, 0, 0)),
        ],
        out_specs=pl.BlockSpec((L, 1024), lambda b: (b, 0)),
        out_shape=jax.ShapeDtypeStruct((n_seq * L, 1024), BF16),
        scratch_shapes=[
            pltpu.VMEM((ATT_HKV, PAST_LEN + L, HEAD_DIM), BF16),
            pltpu.VMEM((ATT_HKV, PAST_LEN + L, HEAD_DIM), BF16),
        ],
        compiler_params=_cparams(("arbitrary",)),
        name="att_lat",
    )(p_main, p_main, qg, kg, cos, sin, ck, cv)


def _seqs_per_step(L, n_seq):
    return 2 if (L <= SEQ and n_seq % 2 == 0) else 1


_GLA_LEVELS = (32, 16, 8, 4, 2, 1)


def _gla_consts():
    C = CHUNK
    i = np.arange(C)[:, None]
    t = np.arange(C)[None, :]
    mats = [t <= i, t > i]
    masks = [i == t]
    for s in _GLA_LEVELS:
        m = (i // s) * s
        if s > 1:
            mats.append((t > m) & (t <= i))
        mats.append((t > i) & (t <= m + s))
        masks.append((i // (2 * s) == t // (2 * s)) & ((i // s) % 2 == 1) & ((t // s) % 2 == 0))
    fwd = np.concatenate(mats, 0)
    bwd = np.concatenate([mm[::-1, ::-1] for mm in mats], 0)
    mstack = np.stack([fwd, bwd]).astype(np.float32)
    mf = np.stack([np.concatenate([mm, mm], 0) for mm in masks])
    mb = np.stack([np.concatenate([mm[::-1, ::-1], mm[::-1, ::-1]], 0) for mm in masks])
    lmask = np.stack([mf, mb]).astype(np.float32)
    return jnp.asarray(mstack, BF16), jnp.asarray(lmask, F32)


def _gla_kernel(L, nb, has_state, *refs):
    if has_state:
        (pg_ref, ps_ref, w2_ref, gb_ref, ng_ref, ms_ref, lm_ref, s0_ref,
         o_ref, so_ref, la_sc, of_sc, ob_sc, st_sc) = refs
    else:
        (pg_ref, ps_ref, w2_ref, gb_ref, ng_ref, ms_ref, lm_ref,
         o_ref, so_ref, la_sc, of_sc, ob_sc, st_sc) = refs
    n = L // CHUNK
    C = CHUNK

    ps = ps_ref[...]
    for z in range(2):
        x = _dot3(ps, w2_ref[z]) + gb_ref[z]
        la_sc[z] = (jnp.minimum(x, 0.0) - jnp.log1p(jnp.exp(-jnp.abs(x)))) * (1.0 / GLA_GATE_NORM)

    if has_state:
        for sq in range(nb):
            for z in range(2):
                for p in range(2):
                    pair = jnp.concatenate([s0_ref[sq, z, 2 * p], s0_ref[sq, z, 2 * p + 1]], axis=0)
                    st_sc[sq, z, p] = jnp.transpose(pair)
    else:
        st_sc[...] = jnp.zeros(st_sc.shape, F32)

    lane = lax.broadcasted_iota(jnp.int32, (C, LANES), 1)
    first = lane < 64

    def stack_heads(t):
        return jnp.concatenate([jnp.where(first, t, 0.0), jnp.where(first, 0.0, t)], axis=0)

    chains = [(sq, z, p) for sq in range(nb) for z in range(2) for p in range(2)]
    nch = len(chains)

    def body(c, carry):
        def chunk_rows(sq, z):
            cc = c if z == 0 else n - 1 - c
            return pl.ds(pl.multiple_of(sq * L + cc * C, C), C)

        rows = [chunk_rows(sq, z) for sq, z, p in chains]
        last = [C - 1, 0]
        q = [pg_ref[rows[i], p * LANES:(p + 1) * LANES] * (GLA_DK ** -0.5)
             for i, (sq, z, p) in enumerate(chains)]
        k = [pg_ref[rows[i], 256 + p * LANES:256 + (p + 1) * LANES]
             for i, (sq, z, p) in enumerate(chains)]
        e = [jnp.exp(_dot_exact_lhs_wide(ms_ref[z], la_sc[z, rows[i], p * LANES:(p + 1) * LANES]))
             for i, (sq, z, p) in enumerate(chains)]
        st = [st_sc[sq, z, p] for sq, z, p in chains]
        a2 = [_bdot_nt(stack_heads(q[i]), k[i]) * lm_ref[z, 0]
              for i, (sq, z, p) in enumerate(chains)]
        blk = 2
        for li, s in enumerate(_GLA_LEVELS):
            if s > 1:
                qs = [q[i] * e[i][blk * C:(blk + 1) * C] for i in range(nch)]
                blk += 1
            else:
                qs = q
            ks = [k[i] * e[i][blk * C:(blk + 1) * C] for i in range(nch)]
            blk += 1
            a2 = [a2[i] + _bdot_nt(stack_heads(qs[i]), ks[i]) * lm_ref[z, li + 1]
                  for i, (sq, z, p) in enumerate(chains)]
        inter = [_bdot_nt(stack_heads(q[i] * e[i][0:C]), st[i]) for i in range(nch)]
        kl = [k[i] * e[i][C:2 * C] for i in range(nch)]
        for i, (sq, z, p) in enumerate(chains):
            osc = of_sc if z == 0 else ob_sc
            zs = []
            for hh in range(2):
                h = 2 * p + hh
                v = pg_ref[rows[i], 512 + h * LANES:512 + (h + 1) * LANES]
                o = _bdot(a2[i][hh * C:(hh + 1) * C], v) + inter[i][hh * C:(hh + 1) * C]
                osc[rows[i], h * LANES:(h + 1) * LANES] = o
                zs.append(_bdot_tn(v, kl[i]))
            st_sc[sq, z, p] = (st[i] * e[i][last[z]:last[z] + 1, :]
                               + jnp.where(first[0:1], zs[0], zs[1]))
        return carry

    lax.fori_loop(0, n, body, 0)

    ng = ng_ref[...]

    def fin(c, carry):
        r0 = pl.multiple_of(c * C, C)
        rows = pl.ds(r0, C)
        for h in range(GLA_H):
            cols = slice(h * LANES, (h + 1) * LANES)
            o = of_sc[rows, cols] + ob_sc[rows, cols]
            gate = _silu(pg_ref[rows, 1024 + h * LANES:1024 + (h + 1) * LANES])
            o_ref[rows, cols] = (_rms(o, ng) * gate).astype(BF16)
        return carry

    lax.fori_loop(0, nb * n, fin, 0)
    for sq in range(nb):
        for z in range(2):
            for p in range(2):
                pair = jnp.transpose(st_sc[sq, z, p])
                so_ref[sq, z, 2 * p] = pair[0:GLA_DK]
                so_ref[sq, z, 2 * p + 1] = pair[GLA_DK:2 * GLA_DK]


def _gla(p_main, p_small, w2p, gbias, ng, consts, s0, layer, L, n_seq, row_blk0):
    mstack, lmask = consts
    has_state = s0 is not None
    nb = _seqs_per_step(L, n_seq)
    assert row_blk0 % nb == 0
    blk0 = row_blk0 // nb
    state_blk = (nb, 2, GLA_H, GLA_DK, GLA_DV)
    in_specs = [
        pl.BlockSpec((nb * L, 1536), lambda b: (blk0 + b, P_GLA // 1536)),
        pl.BlockSpec((nb * L, LANES), lambda b: (blk0 + b, 0)),
        pl.BlockSpec((2, LANES, 256), lambda b: (0, 0, 0)),
        pl.BlockSpec((2, 1, 256), lambda b: (0, 0, 0)),
        pl.BlockSpec((1, GLA_DV), lambda b: (0, 0)),
        pl.BlockSpec(mstack.shape, lambda b: (0, 0, 0)),
        pl.BlockSpec(lmask.shape, lambda b: (0, 0, 0, 0)),
    ]
    args = [p_main, p_small, w2p, gbias, ng, mstack, lmask]
    if has_state:
        in_specs.append(pl.BlockSpec((nb, None) + state_blk[1:], lambda b: (b, layer, 0, 0, 0, 0)))
        args.append(s0)
    return pl.pallas_call(
        functools.partial(_gla_kernel, L, nb, has_state),
        grid=(n_seq // nb,),
        in_specs=in_specs,
        out_specs=[
            pl.BlockSpec((nb * L, 512), lambda b: (b, 0)),
            pl.BlockSpec(state_blk, lambda b: (b, 0, 0, 0, 0)),
        ],
        out_shape=[
            jax.ShapeDtypeStruct((n_seq * L, 512), BF16),
            jax.ShapeDtypeStruct((n_seq, 2, GLA_H, GLA_DK, GLA_DV), F32),
        ],
        scratch_shapes=[
            pltpu.VMEM((2, nb * L, 256), F32),
            pltpu.VMEM((nb * L, 512), F32),
            pltpu.VMEM((nb * L, 512), F32),
            pltpu.VMEM((nb, 2, 2, GLA_DV, LANES), F32),
        ],
        compiler_params=_cparams(("arbitrary",)),
        name="gla_lat" if has_state else "gla_ctx",
    )(*args)


def _dn_kernel(L, nb, has_state, *refs):
    if has_state:
        (pd_ref, dz_ref, ps_ref, cw_ref, al_ref, dtb_ref, ng_ref, s0_ref,
         o_ref, so_ref, qkv_sc, g_sc, bb_sc, of_sc, ob_sc, s_sc) = refs
    else:
        (pd_ref, dz_ref, ps_ref, cw_ref, al_ref, dtb_ref, ng_ref,
         o_ref, so_ref, qkv_sc, g_sc, bb_sc, of_sc, ob_sc, s_sc) = refs
    n = L // CHUNK
    C = CHUNK

    R = nb * L
    pos = lax.broadcasted_iota(jnp.int32, (R, LANES), 0) & (L - 1)
    not_first = pos > 0
    not_last = pos < L - 1

    def conv_tile(j, carry):
        c0 = pl.multiple_of(j * LANES, LANES)
        x = pd_ref[:, pl.ds(c0, LANES)]
        w = cw_ref[:, pl.ds(c0, LANES)]
        prev = jnp.where(not_first, pltpu.roll(x, 1, 0), 0.0)
        nxt = jnp.where(not_last, pltpu.roll(x, R - 1, 0), 0.0)
        y = _silu(prev * w[0:1] + x * w[1:2] + nxt * w[2:3])
        inv = lax.rsqrt(jnp.sum(y * y, axis=-1, keepdims=True) + EPS)
        scale = jnp.where(j < DN_H, inv * (DN_DK ** -0.5), jnp.where(j < 2 * DN_H, inv, 1.0))
        qkv_sc[:, pl.ds(c0, LANES)] = y * scale
        return carry

    lax.fori_loop(0, 3 * DN_H, conv_tile, 0)

    ps = ps_ref[...]
    lane_l = lax.broadcasted_iota(jnp.int32, (R, LANES), 1)
    g_sc[...] = -jnp.exp(al_ref[...]) * _softplus(ps + dtb_ref[...])
    b_all = _sigmoid(ps)
    for idx in range(2 * DN_H):
        bcol = jnp.sum(jnp.where(lane_l == S_DB + idx, b_all, 0.0), axis=-1, keepdims=True)
        bb_sc[idx] = jnp.broadcast_to(bcol, (R, LANES))

    if has_state:
        s_sc[...] = s0_ref[...]
    else:
        s_sc[...] = jnp.zeros(s_sc.shape, F32)

    ri = lax.broadcasted_iota(jnp.int32, (C, LANES), 0)
    lane_c = lax.broadcasted_iota(jnp.int32, (C, LANES), 1)
    ci = lane_c & (C - 1)
    first = lane_c < C
    incl = [ci <= ri, ci >= ri]
    strict = [ci < ri, ci > ri]
    rs = lax.broadcasted_iota(jnp.int32, (C, C), 0)
    cs = lax.broadcasted_iota(jnp.int32, (C, C), 1)
    m_b = [(cs <= rs).astype(BF16), (cs >= rs).astype(BF16)]
    chains = [(sq, z, h) for sq in range(nb) for z in range(2) for h in range(DN_H)]
    nch = len(chains)
    pairs = [(sq, z, p) for sq in range(nb) for z in range(2) for p in range(DN_H // 2)]
    npr = len(pairs)

    def stack_heads(t):
        return jnp.concatenate([jnp.where(first, t, 0.0), jnp.where(first, 0.0, t)], axis=0)

    def hi_mid(x):
        hi = x.astype(BF16).astype(F32)
        return hi, x - hi

    def pair_products(lhs_list, x):
        xh, xm = hi_mid(x)
        rhs = jnp.concatenate([stack_heads(xh), stack_heads(xm)], axis=1).astype(BF16)
        parts = []
        for l in lhs_list:
            parts.extend(hi_mid(l))
        r = jnp.dot(jnp.concatenate(parts, axis=0).astype(BF16), rhs, preferred_element_type=F32)
        out = []
        for i in range(len(lhs_list)):
            blk = r[2 * C * i:2 * C * i + C] + r[2 * C * i + C:2 * C * (i + 1)]
            out.append(blk[:, 0:LANES] + blk[:, LANES:2 * LANES])
        return out

    def body(c, carry):
        def chunk_rows(sq, z):
            cc = c if z == 0 else n - 1 - c
            return pl.ds(pl.multiple_of(sq * L + cc * C, C), C)

        rows = [chunk_rows(sq, z) for sq in range(nb) for z in range(2)]
        last = [C - 1, 0]

        def ld(base, sq, z, h):
            return qkv_sc[rows[2 * sq + z], base + h * LANES:base + (h + 1) * LANES]

        q = [ld(0, *ch) for ch in chains]
        k = [ld(512, *ch) for ch in chains]
        v = [ld(1024, *ch) for ch in chains]
        bb = [bb_sc[z * DN_H + h, rows[2 * sq + z], :] for sq, z, h in chains]
        gc = [_dot_exact_lhs(m_b[z], g_sc[rows[2 * sq + z], :])
              for sq in range(nb) for z in range(2)]
        gct = [jnp.transpose(jnp.concatenate([g, g], axis=0)) for g in gc]
        gi = [jnp.broadcast_to(jnp.sum(jnp.where(lane_c == S_DA + z * DN_H + h, gc[2 * sq + z], 0.0),
                                       axis=-1, keepdims=True), (C, LANES)) for sq, z, h in chains]
        gjrow = [gct[2 * sq + z][S_DA + z * DN_H + h:S_DA + z * DN_H + h + 1, :]
                 for sq, z, h in chains]
        heads = [((2 * sq + z) * DN_H + 2 * p, (2 * sq + z) * DN_H + 2 * p + 1)
                 for sq, z, p in pairs]
        gam = []
        for (sq, z, p), (i0, i1) in zip(pairs, heads):
            diff = jnp.where(first, gi[i0], gi[i1]) - jnp.where(first[0:1], gjrow[i0], gjrow[i1])
            gam.append(jnp.where(incl[z], jnp.exp(jnp.where(incl[z], diff, 0.0)), 0.0))
        kb = [k[i] * bb[i] for i in range(nch)]
        kq = [_bdot_nt(jnp.concatenate([kb[i0], q[i0], kb[i1], q[i1]], axis=0),
                       jnp.concatenate([k[i0], k[i1]], axis=0)) for i0, i1 in heads]
        a = [jnp.where(strict[z], jnp.where(first, kq[j][0:C], kq[j][2 * C:3 * C]) * gam[j], 0.0)
             for j, (sq, z, p) in enumerate(pairs)]
        att = [jnp.where(first, kq[j][C:2 * C], kq[j][3 * C:4 * C]) * gam[j] for j in range(npr)]
        nt = [-a[j] for j in range(npr)]
        pw = [pair_products([a[j]], a[j])[0] for j in range(npr)]
        for step in range(5):
            if step < 4:
                both = [pair_products([nt[j], pw[j]], pw[j]) for j in range(npr)]
                nt = [nt[j] + pw[j] + both[j][0] for j in range(npr)]
                pw = [both[j][1] for j in range(npr)]
            else:
                nt = [nt[j] + pw[j] + pair_products([nt[j]], pw[j])[0] for j in range(npr)]
        eg = [jnp.exp(gi[i]) for i in range(nch)]
        rhs = [jnp.concatenate([v[i] * bb[i], kb[i] * eg[i]], axis=1) for i in range(nch)]
        corr = [_bdot(stack_heads(nt[j]), jnp.concatenate([rhs[i0], rhs[i1]], axis=0))
                for j, (i0, i1) in enumerate(heads)]
        uw = [None] * nch
        for j, (i0, i1) in enumerate(heads):
            uw[i0] = rhs[i0] + corr[j][0:C]
            uw[i1] = rhs[i1] + corr[j][C:2 * C]
        s = [s_sc[sq, z, h] for sq, z, h in chains]
        ws = [_bdot(jnp.concatenate([uw[i][:, LANES:], q[i] * eg[i]], axis=0), s[i])
              for i in range(nch)]
        v_new = [uw[i][:, 0:LANES] - ws[i][0:C] for i in range(nch)]
        av = [_bdot(stack_heads(att[j]), jnp.concatenate([v_new[i0], v_new[i1]], axis=0))
              for j, (i0, i1) in enumerate(heads)]
        o = [None] * nch
        for j, (i0, i1) in enumerate(heads):
            o[i0] = ws[i0][C:2 * C] + av[j][0:C]
            o[i1] = ws[i1][C:2 * C] + av[j][C:2 * C]
        for i, (sq, z, h) in enumerate(chains):
            osc = of_sc if z == 0 else ob_sc
            osc[rows[2 * sq + z], h * LANES:(h + 1) * LANES] = o[i]
        gt = [gi[i][last[z]:last[z] + 1, :] for i, (sq, z, h) in enumerate(chains)]
        upd = [_bdot_tn(k[i] * jnp.exp(gt[i] - gi[i]), v_new[i]) for i in range(nch)]
        for i, (sq, z, h) in enumerate(chains):
            s_sc[sq, z, h] = s[i] * jnp.exp(gt[i]) + upd[i]
        return carry

    lax.fori_loop(0, n, body, 0)

    ng = ng_ref[...]

    def fin(c, carry):
        r0 = pl.multiple_of(c * C, C)
        rows = pl.ds(r0, C)
        for h in range(DN_H):
            cols = slice(h * LANES, (h + 1) * LANES)
            o = of_sc[rows, cols] + ob_sc[rows, cols]
            o_ref[rows, cols] = (_rms(o, ng) * _silu(dz_ref[rows, cols])).astype(BF16)
        return carry

    lax.fori_loop(0, nb * n, fin, 0)
    so_ref[...] = s_sc[...]


def _dn(p_main, p_small, conv_w, alog_row, dtb_row, ng, s0, layer, L, n_seq, row_blk0):
    has_state = s0 is not None
    nb = _seqs_per_step(L, n_seq)
    assert row_blk0 % nb == 0
    row_blk0, L_blk = row_blk0 // nb, nb * L
    in_specs = [
        pl.BlockSpec((L_blk, 1536), lambda b: (row_blk0 + b, P_DN // 1536)),
        pl.BlockSpec((L_blk, 512), lambda b: (row_blk0 + b, P_DZ // 512)),
        pl.BlockSpec((L_blk, LANES), lambda b: (row_blk0 + b, 0)),
        pl.BlockSpec((DN_CONV, 3 * DN_H * DN_DK), lambda b: (0, 0)),
        pl.BlockSpec((1, LANES), lambda b: (0, 0)),
        pl.BlockSpec((1, LANES), lambda b: (0, 0)),
        pl.BlockSpec((1, DN_DV), lambda b: (0, 0)),
    ]
    args = [p_main, p_main, p_small, conv_w, alog_row, dtb_row, ng]
    if has_state:
        in_specs.append(pl.BlockSpec((nb, None, 2, DN_H, DN_DK, DN_DV),
                                     lambda b: (b, layer, 0, 0, 0, 0)))
        args.append(s0)
    return pl.pallas_call(
        functools.partial(_dn_kernel, L, nb, has_state),
        grid=(n_seq // nb,),
        in_specs=in_specs,
        out_specs=[
            pl.BlockSpec((L_blk, 512), lambda b: (b, 0)),
            pl.BlockSpec((nb, 2, DN_H, DN_DK, DN_DV), lambda b: (b, 0, 0, 0, 0)),
        ],
        out_shape=[
            jax.ShapeDtypeStruct((n_seq * L, 512), BF16),
            jax.ShapeDtypeStruct((n_seq, 2, DN_H, DN_DK, DN_DV), F32),
        ],
        scratch_shapes=[
            pltpu.VMEM((L_blk, 1536), F32),
            pltpu.VMEM((L_blk, LANES), F32),
            pltpu.VMEM((2 * DN_H, L_blk, LANES), F32),
            pltpu.VMEM((L_blk, 512), F32),
            pltpu.VMEM((L_blk, 512), F32),
            pltpu.VMEM((nb, 2, DN_H, DN_DK, DN_DV), F32),
        ],
        compiler_params=_cparams(("arbitrary",)),
        name="dn_lat" if has_state else "dn_ctx",
    )(*args)


_W_IN_SPLITS = (256, 256, 512, 512, 32, 1024, 256, 256, 1536, 512, 8, 8)


def _regroup_kernel(w_ref, main_ref, small_ref):
    off = np.cumsum((0,) + _W_IN_SPLITS)
    w = w_ref[...]
    gq, gk, gv, gr, glr, aq, ak, av, dqkv, dz, da, db = [
        w[:, int(off[i]):int(off[i + 1])] for i in range(12)]
    main_ref[...] = jnp.concatenate([aq, ak, av, gq, gk, gv, gr, dqkv, dz], axis=1).astype(BF16)
    pad = jnp.zeros((w.shape[0], LANES - 48), F32)
    small_ref[...] = jnp.concatenate([glr, da, db, pad], axis=1).astype(BF16)


def _prep_w_in(w_in, tr=256):
    depth, _, in_cols = w_in.shape
    return pl.pallas_call(
        _regroup_kernel,
        grid=(depth, D_MODEL // tr),
        in_specs=[pl.BlockSpec((None, tr, in_cols), lambda l, r: (l, r, 0))],
        out_specs=[
            pl.BlockSpec((None, tr, P_MAIN), lambda l, r: (l, r, 0)),
            pl.BlockSpec((None, tr, LANES), lambda l, r: (l, r, 0)),
        ],
        out_shape=[
            jax.ShapeDtypeStruct((depth, D_MODEL, P_MAIN), BF16),
            jax.ShapeDtypeStruct((depth, D_MODEL, LANES), BF16),
        ],
        compiler_params=_cparams(("arbitrary", "arbitrary")),
        name="w_in_regroup",
    )(w_in)


def _lane_row(vals, offset):
    return jnp.zeros((1, LANES), F32).at[0, offset:offset + vals.shape[0]].set(vals.astype(F32))


def kernel(x_prompt, x_sample, cache_k, cache_v, state_gla, state_dn, c, c_ctx, norm1_g, norm2_g,
           w_mod, b_mod, w_in, gla_w2, gla_b, gla_norm_g, q_norm_g, k_norm_g, dn_conv, dn_a_log,
           dn_dt_bias, dn_norm_g, w_out, w_ff1, w_ff2):
    n_ctx, n_lat = x_prompt.shape[0], x_sample.shape[0]
    depth = w_in.shape[0]
    n_ctx_rows = n_ctx * SEQ
    assert n_ctx_rows % DEC_SEQ == 0 and n_lat <= 7

    xs = [x_prompt.reshape(n_ctx_rows, D_MODEL), x_sample.reshape(n_lat * DEC_SEQ, D_MODEL)]
    cond8 = jnp.concatenate([c_ctx[None, :], c, jnp.zeros((7 - n_lat, D_MODEL), F32)], axis=0)
    mods = _modulation(cond8, w_mod, b_mod).reshape(depth, 8, N_MOD, D_MODEL)

    cos, sin = _rope_tables()
    gla_consts = _gla_consts()
    ck = cache_k.reshape(n_lat, depth, PAST_LEN, ATT_HKV * HEAD_DIM)
    cv = cache_v.reshape(n_lat, depth, PAST_LEN, ATT_HKV * HEAD_DIM)

    w_main, w_small = _prep_w_in(w_in)
    w_out_b, w_ff1_b, w_ff2_b = w_out.astype(BF16), w_ff1.astype(BF16), w_ff2.astype(BF16)

    nks, nvs, sgs, sds = [], [], [], []
    for l in range(depth):
        tc, tl = n_ctx_rows // INPROJ_TM, n_lat * DEC_SEQ // INPROJ_TM
        inproj = functools.partial(_inproj, mods_l=mods[l], g1=norm1_g[l][None, :], w_main=w_main,
                                   w_small=w_small, layer=l, n_ctx_rows=n_ctx_rows)
        pc, psc = inproj(xs[0], tile0=0, n_tiles=tc, grp_tile0=0)
        if len(xs) == 2:
            pl_, psl = inproj(xs[1], tile0=0, n_tiles=tl, grp_tile0=tc)
        else:
            pl_, psl = inproj(xs[0], tile0=tc, n_tiles=tl, grp_tile0=tc)

        qg, kg = q_norm_g[l][None, :], k_norm_g[l][None, :]
        oa_c, nk, nv = _att_ctx(pc, qg, kg, n_ctx)
        oa_l = _att_lat(pl_, qg, kg, cos, sin, ck, cv, l, n_lat, 0)

        w2p = jnp.zeros((2, LANES, GLA_H * GLA_DK), F32)
        w2p = w2p.at[0, 0:GLA_LR].set(gla_w2[l, 0]).at[1, GLA_LR:2 * GLA_LR].set(gla_w2[l, 1])
        gbias = gla_b[l][:, None, :]
        gng = gla_norm_g[l][None, :]
        og_c, sg = _gla(pc, psc, w2p, gbias, gng, gla_consts, None, l, SEQ, n_ctx, 0)
        og_l, _ = _gla(pl_, psl, w2p, gbias, gng, gla_consts, state_gla, l, DEC_SEQ, n_lat, 0)

        alog_row = _lane_row(dn_a_log[l].reshape(-1), S_DA)
        dtb_row = _lane_row(dn_dt_bias[l].reshape(-1), S_DA)
        dng = dn_norm_g[l][None, :]
        od_c, sd = _dn(pc, psc, dn_conv[l], alog_row, dtb_row, dng, None, l, SEQ, n_ctx, 0)
        od_l, _ = _dn(pl_, psl, dn_conv[l], alog_row, dtb_row, dng, state_dn, l, DEC_SEQ, n_lat, 0)

        x1, h2 = _outproj(xs, [og_c, og_l], [oa_c, oa_l], [od_c, od_l], mods[l],
                          norm2_g[l][None, :], w_out_b, l, n_ctx_rows)
        n_tiles, nct = x1.shape[0] // FFN_TM, n_ctx_rows // FFN_TM
        ffn = functools.partial(_ffn, h2, x1, mods[l], w_ff1_b, w_ff2_b, l, n_ctx_rows, tm=FFN_TM)
        if l < depth - 1:
            xs = [ffn(0, n_tiles)]
        else:
            xs = [ffn(0, nct), ffn(nct, n_tiles - nct)]

        nks.append(nk.reshape(n_ctx, SEQ, ATT_HKV, HEAD_DIM))
        nvs.append(nv.reshape(n_ctx, SEQ, ATT_HKV, HEAD_DIM))
        sgs.append(sg)
        sds.append(sd)

    y_prompt = xs[0].reshape(n_ctx, SEQ, D_MODEL)
    y_sample = xs[1].reshape(n_lat, DEC_SEQ, D_MODEL)
    return (y_prompt, y_sample, jnp.stack(nks, axis=1), jnp.stack(nvs, axis=1),
            jnp.stack(sgs, axis=1), jnp.stack(sds, axis=1))
```

```python
import functools
import math

import numpy as np
import jax
import jax.numpy as jnp
from jax import lax
from jax.experimental import pallas as pl
from jax.experimental.pallas import tpu as pltpu

F32 = jnp.float32
BF16 = jnp.bfloat16

D_MODEL = 2048
SEQ = 256
DEC_SEQ = 1024
PAST_LEN = 256
GRID_W = 64
HEAD_DIM = 128
EPS = 1e-6
GLA_H = 4
GLA_DK = 64
GLA_DV = 128
GLA_LR = 16
GLA_GATE_NORM = 16.0
ATT_HQ = 8
ATT_HKV = 2
ROPE_THETA = 10000.0
DN_H = 4
DN_DK = 128
DN_DV = 128
DN_CONV = 3
CHUNK = 64
FF = 4 * D_MODEL
N_MOD = 6
LANES = 128

P_AQ = 0
P_AKV = 1024
P_GLA = 1536
P_DN = 3072
P_DZ = 4608
P_MAIN = 5120
S_DA = 32
S_DB = 40

VMEM_LIMIT = 56 * 1024 * 1024


def _cparams(sem):
    return pltpu.CompilerParams(dimension_semantics=sem, vmem_limit_bytes=VMEM_LIMIT)


def _bdot(a, b):
    return jnp.dot(a.astype(BF16), b.astype(BF16), preferred_element_type=F32)


def _bdot_nt(a, b):
    return lax.dot_general(a.astype(BF16), b.astype(BF16), (((1,), (1,)), ((), ())),
                           preferred_element_type=F32)


def _bdot_tn(a, b):
    return lax.dot_general(a.astype(BF16), b.astype(BF16), (((0,), (0,)), ((), ())),
                           preferred_element_type=F32)


def _split3(x):
    hi = x.astype(BF16)
    r = x - hi.astype(F32)
    mid = r.astype(BF16)
    lo = (r - mid.astype(F32)).astype(BF16)
    return hi, mid, lo


def _dot_exact_lhs(lhs_bf16, x):
    hi, mid, lo = _split3(x)
    d = functools.partial(jnp.dot, preferred_element_type=F32)
    return d(lhs_bf16, hi) + d(lhs_bf16, mid) + d(lhs_bf16, lo)


def _dot_exact_lhs_wide(lhs_bf16, x):
    n = x.shape[1]
    hi = x.astype(BF16)
    mid = (x - hi.astype(F32)).astype(BF16)
    r = jnp.dot(lhs_bf16, jnp.concatenate([hi, mid], axis=1), preferred_element_type=F32)
    return r[:, 0:n] + r[:, n:2 * n]


def _dot3(a, b):
    ah, am, _ = _split3(a)
    bh, bm, _ = _split3(b)
    d = functools.partial(jnp.dot, preferred_element_type=F32)
    return d(ah, bh) + (d(ah, bm) + d(am, bh))


def _sigmoid(x):
    return 0.5 * jnp.tanh(0.5 * x) + 0.5


def _silu(x):
    return x * _sigmoid(x)


def _softplus(x):
    return jnp.maximum(x, 0.0) + jnp.log1p(jnp.exp(-jnp.abs(x)))


def _rms(x, g):
    return x * lax.rsqrt(jnp.mean(x * x, axis=-1, keepdims=True) + EPS) * g


def _mods_kernel(c_ref, w_ref, b_ref, o_ref):
    o_ref[...] = _bdot(_silu(c_ref[...]), w_ref[...]) + b_ref[...]


def _modulation(cond8, w_mod, b_mod):
    depth = w_mod.shape[0]
    n_out = N_MOD * D_MODEL
    tn = 1024
    return pl.pallas_call(
        _mods_kernel,
        grid=(depth, n_out // tn),
        in_specs=[
            pl.BlockSpec((8, D_MODEL), lambda l, j: (0, 0)),
            pl.BlockSpec((None, D_MODEL, tn), lambda l, j: (l, 0, j)),
            pl.BlockSpec((None, 1, tn), lambda l, j: (l, 0, j)),
        ],
        out_specs=pl.BlockSpec((None, 8, tn), lambda l, j: (l, 0, j)),
        out_shape=jax.ShapeDtypeStruct((depth, 8, n_out), F32),
        compiler_params=_cparams(("arbitrary", "arbitrary")),
        name="modulation",
    )(cond8, w_mod, b_mod.reshape(depth, 1, n_out))


def _group_of_tile(i, tm, n_ctx_rows):
    n_ctx_tiles = n_ctx_rows // tm
    per_lat = DEC_SEQ // tm
    return jnp.where(i < n_ctx_tiles, 0, 1 + (i - n_ctx_tiles) // per_lat)


def _row_specs(arrays, width, tm, nct):
    if len(arrays) == 1:
        return [pl.BlockSpec((tm, width), lambda i, *_: (i, 0))]
    return [pl.BlockSpec((tm, width), lambda i, *_: (jnp.minimum(i, nct - 1), 0)),
            pl.BlockSpec((tm, width), lambda i, *_: (jnp.maximum(i - nct, 0), 0))]


def _row_load(refs, nct):
    if len(refs) == 1:
        return refs[0][...]
    return jnp.where(pl.program_id(0) < nct, refs[0][...], refs[1][...])


INPROJ_TM = 1024


def _inproj_kernel(x_ref, mod_ref, g_ref, w_ref, ws_ref, p_ref, ps_ref, h_sc):
    @pl.when(pl.program_id(1) == 0)
    def _():
        m = mod_ref[...]
        h = _rms(x_ref[...], g_ref[...]) * (1.0 + m[1:2]) + m[0:1]
        hb = h.astype(BF16)
        h_sc[...] = hb
        ps_ref[...] = jnp.dot(hb, ws_ref[...], preferred_element_type=F32)

    h = h_sc[...]
    half = w_ref.shape[1] // 2
    for piece in range(2):
        cols = slice(piece * half, (piece + 1) * half)
        p_ref[:, cols] = jnp.dot(h, w_ref[:, cols], preferred_element_type=F32)


def _inproj(x, mods_l, g1, w_main, w_small, layer, n_ctx_rows, tile0, n_tiles, grp_tile0, tn=1024):
    tm = INPROJ_TM
    grp = functools.partial(_group_of_tile, tm=tm, n_ctx_rows=n_ctx_rows)
    return pl.pallas_call(
        _inproj_kernel,
        grid=(n_tiles, P_MAIN // tn),
        in_specs=[
            pl.BlockSpec((tm, D_MODEL), lambda i, j: (tile0 + i, 0)),
            pl.BlockSpec((None, N_MOD, D_MODEL), lambda i, j: (grp(grp_tile0 + i), 0, 0)),
            pl.BlockSpec((1, D_MODEL), lambda i, j: (0, 0)),
            pl.BlockSpec((None, D_MODEL, tn), lambda i, j: (layer, 0, j)),
            pl.BlockSpec((None, D_MODEL, LANES), lambda i, j: (layer, 0, 0)),
        ],
        out_specs=[
            pl.BlockSpec((tm, tn), lambda i, j: (i, j)),
            pl.BlockSpec((tm, LANES), lambda i, j: (i, 0)),
        ],
        out_shape=[
            jax.ShapeDtypeStruct((n_tiles * tm, P_MAIN), F32),
            jax.ShapeDtypeStruct((n_tiles * tm, LANES), F32),
        ],
        scratch_shapes=[pltpu.VMEM((tm, D_MODEL), BF16)],
        compiler_params=_cparams(("arbitrary", "arbitrary")),
        name="inproj",
    )(x, mods_l, g1, w_main, w_small)


def _outproj_kernel(nct, n_x, *refs):
    x_refs = refs[:n_x]
    (ogc_ref, ogl_ref, oac_ref, oal_ref, odc_ref, odl_ref, mod_ref, g_ref, w_ref,
     x1_ref, h2_ref) = refs[n_x:]
    m = mod_ref[...]
    d = functools.partial(jnp.dot, preferred_element_type=F32)
    mix = (d(_row_load((ogc_ref, ogl_ref), nct), w_ref[0:512, :])
           + d(_row_load((oac_ref, oal_ref), nct), w_ref[512:1536, :])
           + d(_row_load((odc_ref, odl_ref), nct), w_ref[1536:2048, :]))
    x1 = _row_load(x_refs, nct) + m[2:3] * mix
    x1_ref[...] = x1
    h2_ref[...] = (_rms(x1, g_ref[...]) * (1.0 + m[4:5]) + m[3:4]).astype(BF16)


def _outproj(xs, o_gla, o_att, o_dn, mods_l, g2, w_out, layer, n_ctx_rows, tm=512):
    rows = sum(x.shape[0] for x in xs)
    nct = n_ctx_rows // tm
    grp = functools.partial(_group_of_tile, tm=tm, n_ctx_rows=n_ctx_rows)
    return pl.pallas_call(
        functools.partial(_outproj_kernel, nct, len(xs)),
        grid=(rows // tm,),
        in_specs=(_row_specs(xs, D_MODEL, tm, nct) + _row_specs(o_gla, 512, tm, nct)
                  + _row_specs(o_att, 1024, tm, nct) + _row_specs(o_dn, 512, tm, nct) + [
                      pl.BlockSpec((None, N_MOD, D_MODEL), lambda i: (grp(i), 0, 0)),
                      pl.BlockSpec((1, D_MODEL), lambda i: (0, 0)),
                      pl.BlockSpec((None, D_MODEL, D_MODEL), lambda i: (layer, 0, 0),
                                   pipeline_mode=pl.Buffered(1)),
                  ]),
        out_specs=[
            pl.BlockSpec((tm, D_MODEL), lambda i: (i, 0)),
            pl.BlockSpec((tm, D_MODEL), lambda i: (i, 0)),
        ],
        out_shape=[
            jax.ShapeDtypeStruct((rows, D_MODEL), F32),
            jax.ShapeDtypeStruct((rows, D_MODEL), BF16),
        ],
        compiler_params=_cparams(("arbitrary",)),
        name="outproj",
    )(*xs, *o_gla, *o_att, *o_dn, mods_l, g2, w_out)


FFN_TM = 512
FFN_OUT_CHUNK = 512


def _ffn_kernel(h_ref, x1_ref, mod_ref, w1_ref, w2_ref, y_ref):
    j = pl.program_id(1)
    tf = w1_ref.shape[1]

    @pl.when(j == 0)
    def _():
        y_ref[...] = jnp.zeros(y_ref.shape, F32)

    h = h_ref[...]
    for half in range(2):
        cols = slice(half * (tf // 2), (half + 1) * (tf // 2))
        t = jnp.dot(h, w1_ref[:, cols], preferred_element_type=F32)
        t = jnp.square(jnp.maximum(t, 0.0)).astype(BF16)
        for nc in range(D_MODEL // FFN_OUT_CHUNK):
            ncs = slice(nc * FFN_OUT_CHUNK, (nc + 1) * FFN_OUT_CHUNK)
            y_ref[:, ncs] += jnp.dot(t, w2_ref[cols, ncs], preferred_element_type=F32)

    @pl.when(j == pl.num_programs(1) - 1)
    def _():
        y_ref[...] = x1_ref[...] + mod_ref[5:6, :] * y_ref[...]


def _ffn(h2, x1, mods_l, w1, w2, layer, n_ctx_rows, tile0, n_tiles, tm, tf=1024):
    grp = functools.partial(_group_of_tile, tm=tm, n_ctx_rows=n_ctx_rows)
    return pl.pallas_call(
        _ffn_kernel,
        grid=(n_tiles, FF // tf),
        in_specs=[
            pl.BlockSpec((tm, D_MODEL), lambda i, j: (tile0 + i, 0)),
            pl.BlockSpec((tm, D_MODEL), lambda i, j: (tile0 + i, 0)),
            pl.BlockSpec((None, N_MOD, D_MODEL), lambda i, j: (grp(tile0 + i), 0, 0)),
            pl.BlockSpec((None, D_MODEL, tf), lambda i, j: (layer, 0, j)),
            pl.BlockSpec((None, tf, D_MODEL), lambda i, j: (layer, j, 0)),
        ],
        out_specs=pl.BlockSpec((tm, D_MODEL), lambda i, j: (i, 0)),
        out_shape=jax.ShapeDtypeStruct((n_tiles * tm, D_MODEL), F32),
        compiler_params=_cparams(("arbitrary", "arbitrary")),
        name="ffn",
    )(h2, x1, mods_l, w1, w2)


def _rope_tables():
    half = HEAD_DIM // 2
    pos = np.arange(DEC_SEQ)
    row = (pos // GRID_W).astype(np.float32)
    col = (pos % GRID_W).astype(np.float32)
    inv = (ROPE_THETA ** (-np.arange(0, half, 2, dtype=np.float32) / half)).astype(np.float32)
    ar = row[:, None] * inv[None, :]
    ac = col[:, None] * inv[None, :]
    cos = np.concatenate([np.cos(ar), np.cos(ar), np.cos(ac), np.cos(ac)], axis=1)
    sin = np.concatenate([-np.sin(ar), np.sin(ar), -np.sin(ac), np.sin(ac)], axis=1)
    return jnp.asarray(cos, F32), jnp.asarray(sin, F32)


def _rope(x, cos, sin):
    lane = lax.broadcasted_iota(jnp.int32, x.shape, 1)
    first = (lane % 64) < 32
    partner = jnp.where(first, pltpu.roll(x, 96, 1), pltpu.roll(x, 32, 1))
    return x * cos + partner * sin


ATT_G = ATT_HQ // ATT_HKV


def _softmax_pv(q, kb, vb):
    s = lax.dot_general(q.astype(BF16), kb, (((1,), (1,)), ((), ())), preferred_element_type=F32)
    m = jnp.max(s, axis=-1, keepdims=True)
    p = jnp.exp(s - m)
    l = jnp.sum(p, axis=-1, keepdims=True)
    return jnp.dot(p.astype(BF16), vb, preferred_element_type=F32) / l


def _group_attention(q_heads, kb, vb, store):
    for g, q in enumerate(q_heads):
        store(g, _softmax_pv(q, kb, vb).astype(BF16))


def _att_ctx_kernel(aq_ref, akv_ref, qg_ref, kg_ref, o_ref, nk_ref, nv_ref):
    qg = qg_ref[...] * (HEAD_DIM ** -0.5)
    kg = kg_ref[...]
    for hk in range(ATT_HKV):
        ks = slice(hk * HEAD_DIM, (hk + 1) * HEAD_DIM)
        kn = _rms(akv_ref[:, ks], kg)
        v = akv_ref[:, 256 + hk * HEAD_DIM:256 + (hk + 1) * HEAD_DIM]
        nk_ref[:, ks] = kn
        nv_ref[:, ks] = v

        def head_cols(g, hk=hk):
            return slice((hk * ATT_G + g) * HEAD_DIM, (hk * ATT_G + g + 1) * HEAD_DIM)

        def store(g, val):
            o_ref[:, head_cols(g)] = val

        qs = [_rms(aq_ref[:, head_cols(g)], qg) for g in range(ATT_G)]
        _group_attention(qs, kn.astype(BF16), v.astype(BF16), store)


def _att_lat_kernel(aq_ref, akv_ref, qg_ref, kg_ref, cos_ref, sin_ref, ck_ref, cv_ref, o_ref,
                    k_sc, v_sc):
    qg = qg_ref[...] * (HEAD_DIM ** -0.5)
    kg = kg_ref[...]
    qb = 256
    for hk in range(ATT_HKV):
        ks = slice(hk * HEAD_DIM, (hk + 1) * HEAD_DIM)
        k_sc[hk, 0:PAST_LEN, :] = ck_ref[:, ks].astype(BF16)
        v_sc[hk, 0:PAST_LEN, :] = cv_ref[:, ks].astype(BF16)
        kn = _rope(_rms(akv_ref[:, ks], kg), cos_ref[...], sin_ref[...])
        k_sc[hk, PAST_LEN:, :] = kn.astype(BF16)
        v_sc[hk, PAST_LEN:, :] = akv_ref[:, 256 + hk * HEAD_DIM:256 + (hk + 1) * HEAD_DIM].astype(BF16)

    def body(r, carry):
        r0 = pl.multiple_of(r * qb, qb)
        cos = cos_ref[pl.ds(r0, qb), :]
        sin = sin_ref[pl.ds(r0, qb), :]
        for hk in range(ATT_HKV):
            def head_cols(g, hk=hk):
                return slice((hk * ATT_G + g) * HEAD_DIM, (hk * ATT_G + g + 1) * HEAD_DIM)

            def store(g, val):
                o_ref[pl.ds(r0, qb), head_cols(g)] = val

            qs = [_rope(_rms(aq_ref[pl.ds(r0, qb), head_cols(g)], qg), cos, sin)
                  for g in range(ATT_G)]
            _group_attention(qs, k_sc[hk], v_sc[hk], store)
        return carry

    lax.fori_loop(0, DEC_SEQ // qb, body, 0)


def _att_ctx(p_main, qg, kg, n_seq):
    L = SEQ
    return pl.pallas_call(
        _att_ctx_kernel,
        grid=(n_seq,),
        in_specs=[
            pl.BlockSpec((L, 1024), lambda b: (b, P_AQ // 1024)),
            pl.BlockSpec((L, 512), lambda b: (b, P_AKV // 512)),
            pl.BlockSpec((1, HEAD_DIM), lambda b: (0, 0)),
            pl.BlockSpec((1, HEAD_DIM), lambda b: (0, 0)),
        ],
        out_specs=[
            pl.BlockSpec((L, 1024), lambda b: (b, 0)),
            pl.BlockSpec((None, L, 256), lambda b: (b, 0, 0)),
            pl.BlockSpec((None, L, 256), lambda b: (b, 0, 0)),
        ],
        out_shape=[
            jax.ShapeDtypeStruct((n_seq * L, 1024), BF16),
            jax.ShapeDtypeStruct((n_seq, L, 256), F32),
            jax.ShapeDtypeStruct((n_seq, L, 256), F32),
        ],
        compiler_params=_cparams(("arbitrary",)),
        name="att_ctx",
    )(p_main, p_main, qg, kg)


def _att_lat(p_main, qg, kg, cos, sin, ck, cv, layer, n_seq, row_blk0):
    L = DEC_SEQ
    return pl.pallas_call(
        _att_lat_kernel,
        grid=(n_seq,),
        in_specs=[
            pl.BlockSpec((L, 1024), lambda b: (row_blk0 + b, P_AQ // 1024)),
            pl.BlockSpec((L, 512), lambda b: (row_blk0 + b, P_AKV // 512)),
            pl.BlockSpec((1, HEAD_DIM), lambda b: (0, 0)),
            pl.BlockSpec((1, HEAD_DIM), lambda b: (0, 0)),
            pl.BlockSpec((L, HEAD_DIM), lambda b: (0, 0)),
            pl.BlockSpec((L, HEAD_DIM), lambda b: (0, 0)),
            pl.BlockSpec((None, None, PAST_LEN, 256), lambda b: (b, layer, 0, 0)),
            pl.BlockSpec((None, None, PAST_LEN, 256), lambda b: (b, layer, 0, 0)),
        ],
        out_specs=pl.BlockSpec((L, 1024), lambda b: (b, 0)),
        out_shape=jax.ShapeDtypeStruct((n_seq * L, 1024), BF16),
        scratch_shapes=[
            pltpu.VMEM((ATT_HKV, PAST_LEN + L, HEAD_DIM), BF16),
            pltpu.VMEM((ATT_HKV, PAST_LEN + L, HEAD_DIM), BF16),
        ],
        compiler_params=_cparams(("arbitrary",)),
        name="att_lat",
    )(p_main, p_main, qg, kg, cos, sin, ck, cv)


def _seqs_per_step(L, n_seq):
    return 2 if (L <= SEQ and n_seq % 2 == 0) else 1


_GLA_LEVELS = (32, 16, 8, 4, 2, 1)


def _gla_consts():
    C = CHUNK
    i = np.arange(C)[:, None]
    t = np.arange(C)[None, :]
    mats = [t <= i, t > i]
    masks = [i == t]
    for s in _GLA_LEVELS:
        m = (i // s) * s
        if s > 1:
            mats.append((t > m) & (t <= i))
        mats.append((t > i) & (t <= m + s))
        masks.append((i // (2 * s) == t // (2 * s)) & ((i // s) % 2 == 1) & ((t // s) % 2 == 0))
    fwd = np.concatenate(mats, 0)
    bwd = np.concatenate([mm[::-1, ::-1] for mm in mats], 0)
    mstack = np.stack([fwd, bwd]).astype(np.float32)
    mf = np.stack([np.concatenate([mm, mm], 0) for mm in masks])
    mb = np.stack([np.concatenate([mm[::-1, ::-1], mm[::-1, ::-1]], 0) for mm in masks])
    lmask = np.stack([mf, mb]).astype(np.float32)
    return jnp.asarray(mstack, BF16), jnp.asarray(lmask, F32)


def _gla_kernel(L, nb, has_state, *refs):
    if has_state:
        (pg_ref, ps_ref, w2_ref, gb_ref, ng_ref, ms_ref, lm_ref, s0_ref,
         o_ref, so_ref, la_sc, of_sc, ob_sc, st_sc) = refs
    else:
        (pg_ref, ps_ref, w2_ref, gb_ref, ng_ref, ms_ref, lm_ref,
         o_ref, so_ref, la_sc, of_sc, ob_sc, st_sc) = refs
    n = L // CHUNK
    C = CHUNK

    ps = ps_ref[...]
    for z in range(2):
        x = _dot3(ps, w2_ref[z]) + gb_ref[z]
        la_sc[z] = (jnp.minimum(x, 0.0) - jnp.log1p(jnp.exp(-jnp.abs(x)))) * (1.0 / GLA_GATE_NORM)

    if has_state:
        for sq in range(nb):
            for z in range(2):
                for p in range(2):
                    pair = jnp.concatenate([s0_ref[sq, z, 2 * p], s0_ref[sq, z, 2 * p + 1]], axis=0)
                    st_sc[sq, z, p] = jnp.transpose(pair)
    else:
        st_sc[...] = jnp.zeros(st_sc.shape, F32)

    lane = lax.broadcasted_iota(jnp.int32, (C, LANES), 1)
    first = lane < 64

    def stack_heads(t):
        return jnp.concatenate([jnp.where(first, t, 0.0), jnp.where(first, 0.0, t)], axis=0)

    chains = [(sq, z, p) for sq in range(nb) for z in range(2) for p in range(2)]
    nch = len(chains)

    def body(c, carry):
        def chunk_rows(sq, z):
            cc = c if z == 0 else n - 1 - c
            return pl.ds(pl.multiple_of(sq * L + cc * C, C), C)

        rows = [chunk_rows(sq, z) for sq, z, p in chains]
        last = [C - 1, 0]
        q = [pg_ref[rows[i], p * LANES:(p + 1) * LANES] * (GLA_DK ** -0.5)
             for i, (sq, z, p) in enumerate(chains)]
        k = [pg_ref[rows[i], 256 + p * LANES:256 + (p + 1) * LANES]
             for i, (sq, z, p) in enumerate(chains)]
        e = [jnp.exp(_dot_exact_lhs_wide(ms_ref[z], la_sc[z, rows[i], p * LANES:(p + 1) * LANES]))
             for i, (sq, z, p) in enumerate(chains)]
        st = [st_sc[sq, z, p] for sq, z, p in chains]
        a2 = [_bdot_nt(stack_heads(q[i]), k[i]) * lm_ref[z, 0]
              for i, (sq, z, p) in enumerate(chains)]
        blk = 2
        for li, s in enumerate(_GLA_LEVELS):
            if s > 1:
                qs = [q[i] * e[i][blk * C:(blk + 1) * C] for i in range(nch)]
                blk += 1
            else:
                qs = q
            ks = [k[i] * e[i][blk * C:(blk + 1) * C] for i in range(nch)]
            blk += 1
            a2 = [a2[i] + _bdot_nt(stack_heads(qs[i]), ks[i]) * lm_ref[z, li + 1]
                  for i, (sq, z, p) in enumerate(chains)]
        inter = [_bdot_nt(stack_heads(q[i] * e[i][0:C]), st[i]) for i in range(nch)]
        kl = [k[i] * e[i][C:2 * C] for i in range(nch)]
        for i, (sq, z, p) in enumerate(chains):
            osc = of_sc if z == 0 else ob_sc
            zs = []
            for hh in range(2):
                h = 2 * p + hh
                v = pg_ref[rows[i], 512 + h * LANES:512 + (h + 1) * LANES]
                o = _bdot(a2[i][hh * C:(hh + 1) * C], v) + inter[i][hh * C:(hh + 1) * C]
                osc[rows[i], h * LANES:(h + 1) * LANES] = o
                zs.append(_bdot_tn(v, kl[i]))
            st_sc[sq, z, p] = (st[i] * e[i][last[z]:last[z] + 1, :]
                               + jnp.where(first[0:1], zs[0], zs[1]))
        return carry

    lax.fori_loop(0, n, body, 0)

    ng = ng_ref[...]

    def fin(c, carry):
        r0 = pl.multiple_of(c * C, C)
        rows = pl.ds(r0, C)
        for h in range(GLA_H):
            cols = slice(h * LANES, (h + 1) * LANES)
            o = of_sc[rows, cols] + ob_sc[rows, cols]
            gate = _silu(pg_ref[rows, 1024 + h * LANES:1024 + (h + 1) * LANES])
            o_ref[rows, cols] = (_rms(o, ng) * gate).astype(BF16)
        return carry

    lax.fori_loop(0, nb * n, fin, 0)
    for sq in range(nb):
        for z in range(2):
            for p in range(2):
                pair = jnp.transpose(st_sc[sq, z, p])
                so_ref[sq, z, 2 * p] = pair[0:GLA_DK]
                so_ref[sq, z, 2 * p + 1] = pair[GLA_DK:2 * GLA_DK]


def _gla(p_main, p_small, w2p, gbias, ng, consts, s0, layer, L, n_seq, row_blk0):
    mstack, lmask = consts
    has_state = s0 is not None
    nb = _seqs_per_step(L, n_seq)
    assert row_blk0 % nb == 0
    blk0 = row_blk0 // nb
    state_blk = (nb, 2, GLA_H, GLA_DK, GLA_DV)
    in_specs = [
        pl.BlockSpec((nb * L, 1536), lambda b: (blk0 + b, P_GLA // 1536)),
        pl.BlockSpec((nb * L, LANES), lambda b: (blk0 + b, 0)),
        pl.BlockSpec((2, LANES, 256), lambda b: (0, 0, 0)),
        pl.BlockSpec((2, 1, 256), lambda b: (0, 0, 0)),
        pl.BlockSpec((1, GLA_DV), lambda b: (0, 0)),
        pl.BlockSpec(mstack.shape, lambda b: (0, 0, 0)),
        pl.BlockSpec(lmask.shape, lambda b: (0, 0, 0, 0)),
    ]
    args = [p_main, p_small, w2p, gbias, ng, mstack, lmask]
    if has_state:
        in_specs.append(pl.BlockSpec((nb, None) + state_blk[1:], lambda b: (b, layer, 0, 0, 0, 0)))
        args.append(s0)
    return pl.pallas_call(
        functools.partial(_gla_kernel, L, nb, has_state),
        grid=(n_seq // nb,),
        in_specs=in_specs,
        out_specs=[
            pl.BlockSpec((nb * L, 512), lambda b: (b, 0)),
            pl.BlockSpec(state_blk, lambda b: (b, 0, 0, 0, 0)),
        ],
        out_shape=[
            jax.ShapeDtypeStruct((n_seq * L, 512), BF16),
            jax.ShapeDtypeStruct((n_seq, 2, GLA_H, GLA_DK, GLA_DV), F32),
        ],
        scratch_shapes=[
            pltpu.VMEM((2, nb * L, 256), F32),
            pltpu.VMEM((nb * L, 512), F32),
            pltpu.VMEM((nb * L, 512), F32),
            pltpu.VMEM((nb, 2, 2, GLA_DV, LANES), F32),
        ],
        compiler_params=_cparams(("arbitrary",)),
        name="gla_lat" if has_state else "gla_ctx",
    )(*args)


def _dn_kernel(L, nb, has_state, *refs):
    if has_state:
        (pd_ref, dz_ref, ps_ref, cw_ref, al_ref, dtb_ref, ng_ref, s0_ref,
         o_ref, so_ref, qkv_sc, g_sc, bb_sc, of_sc, ob_sc, s_sc, prep_sc, att_sc, dec_sc) = refs
    else:
        (pd_ref, dz_ref, ps_ref, cw_ref, al_ref, dtb_ref, ng_ref,
         o_ref, so_ref, qkv_sc, g_sc, bb_sc, of_sc, ob_sc, s_sc, prep_sc, att_sc, dec_sc) = refs
    n = L // CHUNK
    C = CHUNK

    R = nb * L
    pos = lax.broadcasted_iota(jnp.int32, (R, LANES), 0) & (L - 1)
    not_first = pos > 0
    not_last = pos < L - 1

    def conv_tile(j, carry):
        c0 = pl.multiple_of(j * LANES, LANES)
        x = pd_ref[:, pl.ds(c0, LANES)]
        w = cw_ref[:, pl.ds(c0, LANES)]
        prev = jnp.where(not_first, pltpu.roll(x, 1, 0), 0.0)
        nxt = jnp.where(not_last, pltpu.roll(x, R - 1, 0), 0.0)
        y = _silu(prev * w[0:1] + x * w[1:2] + nxt * w[2:3])
        inv = lax.rsqrt(jnp.sum(y * y, axis=-1, keepdims=True) + EPS)
        scale = jnp.where(j < DN_H, inv * (DN_DK ** -0.5), jnp.where(j < 2 * DN_H, inv, 1.0))
        qkv_sc[:, pl.ds(c0, LANES)] = y * scale
        return carry

    lax.fori_loop(0, 3 * DN_H, conv_tile, 0)

    ps = ps_ref[...]
    lane_l = lax.broadcasted_iota(jnp.int32, (R, LANES), 1)
    g_sc[...] = -jnp.exp(al_ref[...]) * _softplus(ps + dtb_ref[...])
    b_all = _sigmoid(ps)
    for idx in range(2 * DN_H):
        bcol = jnp.sum(jnp.where(lane_l == S_DB + idx, b_all, 0.0), axis=-1, keepdims=True)
        bb_sc[idx] = jnp.broadcast_to(bcol, (R, LANES))

    if has_state:
        s_sc[...] = s0_ref[...]
    else:
        s_sc[...] = jnp.zeros(s_sc.shape, F32)

    ri = lax.broadcasted_iota(jnp.int32, (C, LANES), 0)
    lane_c = lax.broadcasted_iota(jnp.int32, (C, LANES), 1)
    ci = lane_c & (C - 1)
    first = lane_c < C
    incl = [ci <= ri, ci >= ri]
    strict = [ci < ri, ci > ri]
    rs = lax.broadcasted_iota(jnp.int32, (C, C), 0)
    cs = lax.broadcasted_iota(jnp.int32, (C, C), 1)
    m_b = [(cs <= rs).astype(BF16), (cs >= rs).astype(BF16)]
    nu = 2 * nb
    chains = [(sq, z, h) for sq in range(nu) for z in range(2) for h in range(DN_H)]
    nch = len(chains)
    pairs = [(sq, z, p) for sq in range(nu) for z in range(2) for p in range(DN_H // 2)]
    npr = len(pairs)

    def stack_heads(t):
        return jnp.concatenate([jnp.where(first, t, 0.0), jnp.where(first, 0.0, t)], axis=0)

    def hi_mid(x):
        hi = x.astype(BF16).astype(F32)
        return hi, x - hi

    def pair_products(lhs_list, x):
        xh, xm = hi_mid(x)
        rhs = jnp.concatenate([stack_heads(xh), stack_heads(xm)], axis=1).astype(BF16)
        parts = []
        for l in lhs_list:
            parts.extend(hi_mid(l))
        r = jnp.dot(jnp.concatenate(parts, axis=0).astype(BF16), rhs, preferred_element_type=F32)
        out = []
        for i in range(len(lhs_list)):
            blk = r[2 * C * i:2 * C * i + C] + r[2 * C * i + C:2 * C * (i + 1)]
            out.append(blk[:, 0:LANES] + blk[:, LANES:2 * LANES])
        return out

    heads = [((2 * sq + z) * DN_H + 2 * p, (2 * sq + z) * DN_H + 2 * p + 1)
             for sq, z, p in pairs]

    def chunk_rows(cs):
        def one(sq, z):
            c = cs[sq // nb]
            start = (sq % nb) * L + (c if z == 0 else n - 1 - c) * C
            return pl.ds(start if isinstance(start, int) else pl.multiple_of(start, C), C)
        return [one(sq, z) for sq in range(nu) for z in range(2)]

    def prepare_products(cs):
        rows = chunk_rows(cs)
        last = [C - 1, 0]

        def ld(base, sq, z, h):
            return qkv_sc[rows[2 * sq + z], base + h * LANES:base + (h + 1) * LANES]

        q = [ld(0, *ch) for ch in chains]
        k = [ld(512, *ch) for ch in chains]
        v = [ld(1024, *ch) for ch in chains]
        bb = [bb_sc[z * DN_H + h, rows[2 * sq + z], :] for sq, z, h in chains]
        gc = [_dot_exact_lhs(m_b[z], g_sc[rows[2 * sq + z], :])
              for sq in range(nu) for z in range(2)]
        gct =[jnp.transpose(jnp.concatenate([g, g], axis=0)) for g in gc]
        gi = [jnp.broadcast_to(jnp.sum(jnp.where(lane_c == S_DA + z * DN_H + h, gc[2 * sq + z], 0.0),
                                       axis=-1, keepdims=True), (C, LANES)) for sq, z, h in chains]
        gjrow = [gct[2 * sq + z][S_DA + z * DN_H + h:S_DA + z * DN_H + h + 1, :]
                 for sq, z, h in chains]
        gam = []
        for (sq, z, p), (i0, i1) in zip(pairs, heads):
            diff = jnp.where(first, gi[i0], gi[i1]) - jnp.where(first[0:1], gjrow[i0], gjrow[i1])
            gam.append(jnp.where(incl[z], jnp.exp(jnp.where(incl[z], diff, 0.0)), 0.0))
        kb = [k[i] * bb[i] for i in range(nch)]
        kq = [_bdot_nt(jnp.concatenate([kb[i0], q[i0], kb[i1], q[i1]], axis=0),
                       jnp.concatenate([k[i0], k[i1]], axis=0)) for i0, i1 in heads]
        gt = [gi[i][last[z]:last[z] + 1, :] for i, (sq, z, h) in enumerate(chains)]
        return q, k, v, bb, kb, gi, gt, gam, kq

    def prepare_finish(parts):
        q, k, v, bb, kb, gi, gt, gam, kq = parts
        a = [jnp.where(strict[z], jnp.where(first, kq[j][0:C], kq[j][2 * C:3 * C]) * gam[j], 0.0)
             for j, (sq, z, p) in enumerate(pairs)]
        att = [jnp.where(first, kq[j][C:2 * C], kq[j][3 * C:4 * C]) * gam[j] for j in range(npr)]
        nt = [-a[j] for j in range(npr)]
        pw = [pair_products([a[j]], a[j])[0] for j in range(npr)]
        for step in range(5):
            if step < 4:
                both = [pair_products([nt[j], pw[j]], pw[j]) for j in range(npr)]
                nt = [nt[j] + pw[j] + both[j][0] for j in range(npr)]
                pw = [both[j][1] for j in range(npr)]
            else:
                nt = [nt[j] + pw[j] + pair_products([nt[j]], pw[j])[0] for j in range(npr)]
        eg = [jnp.exp(gi[i]) for i in range(nch)]
        rhs = [jnp.concatenate([v[i] * bb[i], kb[i] * eg[i]], axis=1) for i in range(nch)]
        corr = [_bdot(stack_heads(nt[j]), jnp.concatenate([rhs[i0], rhs[i1]], axis=0))
                for j, (i0, i1) in enumerate(heads)]
        for j, (i0, i1) in enumerate(heads):
            for half, i in enumerate((i0, i1)):
                uw = rhs[i] + corr[j][half * C:(half + 1) * C]
                prep_sc[i, 0] = uw[:, 0:LANES]
                prep_sc[i, 1] = uw[:, LANES:]
                prep_sc[i, 2] = q[i] * eg[i]
                prep_sc[i, 3] = k[i] * jnp.exp(gt[i] - gi[i])
                dec_sc[i] = jnp.broadcast_to(jnp.exp(gt[i]), (8, LANES))
            att_sc[j] = att[j]

    nch1, npr1 = nch // 2, npr // 2

    def apply_state_products(slot, s):
        return [_bdot(jnp.concatenate([prep_sc[slot * nch1 + i, 1], prep_sc[slot * nch1 + i, 2]],
                                      axis=0), s[i]) for i in range(nch1)]

    def apply_finish(rows, slot, s, ws):
        base = slot * nch1
        v_new = [prep_sc[base + i, 0] - ws[i][0:C] for i in range(nch1)]
        av = []
        for j in range(npr1):
            i0, i1 = heads[slot * npr1 + j]
            av.append(_bdot(stack_heads(att_sc[slot * npr1 + j]),
                            jnp.concatenate([v_new[i0 - base], v_new[i1 - base]], axis=0)))
        upd = [_bdot_tn(prep_sc[base + i, 3], v_new[i]) for i in range(nch1)]
        for j in range(npr1):
            for half, ig in enumerate(heads[slot * npr1 + j]):
                sq, z, h = chains[ig]
                osc = of_sc if z == 0 else ob_sc
                osc[rows[2 * sq + z], h * LANES:(h + 1) * LANES] = (
                    ws[ig - base][C:2 * C] + av[j][half * C:(half + 1) * C])
        return [s[i] * dec_sc[base + i, 0:1, :] + upd[i] for i in range(nch1)]

    def apply_two(cs, parts_fn):
        rows = chunk_rows(cs)
        s = [s_sc[sq, z, h] for sq, z, h in chains[0:nch1]]
        ws = apply_state_products(0, s)
        parts = parts_fn()
        s = apply_finish(rows, 0, s, ws)
        ws = apply_state_products(1, s)
        s = apply_finish(rows, 1, s, ws)
        for i, (sq, z, h) in enumerate(chains[0:nch1]):
            s_sc[sq, z, h] = s[i]
        return parts

    prepare_finish(prepare_products((0, 1)))

    def body(it, carry):
        c = 2 * it
        parts = apply_two((c, c + 1), lambda: prepare_products((c + 2, c + 3)))
        prepare_finish(parts)
        return carry

    lax.fori_loop(0, n // 2 - 1, body, 0)
    apply_two((n - 2, n - 1), lambda: None)

    ng = ng_ref[...]

    def fin(c, carry):
        r0 = pl.multiple_of(c * C, C)
        rows = pl.ds(r0, C)
        for h in range(DN_H):
            cols = slice(h * LANES, (h + 1) * LANES)
            o = of_sc[rows, cols] + ob_sc[rows, cols]
            o_ref[rows, cols] = (_rms(o, ng) * _silu(dz_ref[rows, cols])).astype(BF16)
        return carry

    lax.fori_loop(0, nb * n, fin, 0)
    so_ref[...] = s_sc[...]


def _dn(p_main, p_small, conv_w, alog_row, dtb_row, ng, s0, layer, L, n_seq, row_blk0):
    has_state = s0 is not None
    nb = _seqs_per_step(L, n_seq)
    assert row_blk0 % nb == 0
    row_blk0, L_blk = row_blk0 // nb, nb * L
    in_specs = [
        pl.BlockSpec((L_blk, 1536), lambda b: (row_blk0 + b, P_DN // 1536)),
        pl.BlockSpec((L_blk, 512), lambda b: (row_blk0 + b, P_DZ // 512)),
        pl.BlockSpec((L_blk, LANES), lambda b: (row_blk0 + b, 0)),
        pl.BlockSpec((DN_CONV, 3 * DN_H * DN_DK), lambda b: (0, 0)),
        pl.BlockSpec((1, LANES), lambda b: (0, 0)),
        pl.BlockSpec((1, LANES), lambda b: (0, 0)),
        pl.BlockSpec((1, DN_DV), lambda b: (0, 0)),
    ]
    args = [p_main, p_main, p_small, conv_w, alog_row, dtb_row, ng]
    if has_state:
        in_specs.append(pl.BlockSpec((nb, None, 2, DN_H, DN_DK, DN_DV),
                                     lambda b: (b, layer, 0, 0, 0, 0)))
        args.append(s0)
    return pl.pallas_call(
        functools.partial(_dn_kernel, L, nb, has_state),
        grid=(n_seq // nb,),
        in_specs=in_specs,
        out_specs=[
            pl.BlockSpec((L_blk, 512), lambda b: (b, 0)),
            pl.BlockSpec((nb, 2, DN_H, DN_DK, DN_DV), lambda b: (b, 0, 0, 0, 0)),
        ],
        out_shape=[
            jax.ShapeDtypeStruct((n_seq * L, 512), BF16),
            jax.ShapeDtypeStruct((n_seq, 2, DN_H, DN_DK, DN_DV), F32),
        ],
        scratch_shapes=[
            pltpu.VMEM((L_blk, 1536), F32),
            pltpu.VMEM((L_blk, LANES), F32),
            pltpu.VMEM((2 * DN_H, L_blk, LANES), F32),
            pltpu.VMEM((L_blk, 512), F32),
            pltpu.VMEM((L_blk, 512), F32),
            pltpu.VMEM((nb, 2, DN_H, DN_DK, DN_DV), F32),
            pltpu.VMEM((2 * nb * 2 * DN_H, 4, CHUNK, LANES), F32),
            pltpu.VMEM((2 * nb * DN_H, CHUNK, LANES), F32),
            pltpu.VMEM((2 * nb * 2 * DN_H, 8, LANES), F32),
        ],
        compiler_params=_cparams(("arbitrary",)),
        name="dn_lat" if has_state else "dn_ctx",
    )(*args)


_W_IN_SPLITS = (256, 256, 512, 512, 32, 1024, 256, 256, 1536, 512, 8, 8)


def _regroup_kernel(wt_ref, main_ref, small_ref):
    off = np.cumsum((0,) + _W_IN_SPLITS)
    gq, gk, gv, gr, glr, aq, ak, av, dqkv, dz, da, db = [
        wt_ref[int(off[i]):int(off[i + 1]), :] for i in range(12)]
    main_t = jnp.concatenate([aq, ak, av, gq, gk, gv, gr, dqkv, dz], axis=0)
    main_ref[...] = jnp.transpose(main_t).astype(BF16)
    pad = jnp.zeros((LANES - 48, wt_ref.shape[1]), F32)
    small_t = jnp.concatenate([glr, da, db, pad], axis=0)
    small_ref[...] = jnp.transpose(small_t).astype(BF16)


def _prep_w_in(w_in, tr=256):
    depth, _, in_cols = w_in.shape
    return pl.pallas_call(
        _regroup_kernel,
        grid=(depth, D_MODEL // tr),
        in_specs=[pl.BlockSpec((None, in_cols, tr), lambda l, r: (l, 0, r))],
        out_specs=[
            pl.BlockSpec((None, tr, P_MAIN), lambda l, r: (l, r, 0)),
            pl.BlockSpec((None, tr, LANES), lambda l, r: (l, r, 0)),
        ],
        out_shape=[
            jax.ShapeDtypeStruct((depth, D_MODEL, P_MAIN), BF16),
            jax.ShapeDtypeStruct((depth, D_MODEL, LANES), BF16),
        ],
        compiler_params=_cparams(("arbitrary", "arbitrary")),
        name="w_in_regroup",
    )(jnp.swapaxes(w_in, 1, 2))


def _lane_row(vals, offset):
    return jnp.zeros((1, LANES), F32).at[0, offset:offset + vals.shape[0]].set(vals.astype(F32))


def kernel(x_prompt, x_sample, cache_k, cache_v, state_gla, state_dn, c, c_ctx, norm1_g, norm2_g,
           w_mod, b_mod, w_in, gla_w2, gla_b, gla_norm_g, q_norm_g, k_norm_g, dn_conv, dn_a_log,
           dn_dt_bias, dn_norm_g, w_out, w_ff1, w_ff2):
    n_ctx, n_lat = x_prompt.shape[0], x_sample.shape[0]
    depth = w_in.shape[0]
    n_ctx_rows = n_ctx * SEQ
    assert n_ctx_rows % DEC_SEQ == 0 and n_lat <= 7

    xs = [x_prompt.reshape(n_ctx_rows, D_MODEL), x_sample.reshape(n_lat * DEC_SEQ, D_MODEL)]
    cond8 = jnp.concatenate([c_ctx[None, :], c, jnp.zeros((7 - n_lat, D_MODEL), F32)], axis=0)
    mods = _modulation(cond8, w_mod, b_mod).reshape(depth, 8, N_MOD, D_MODEL)

    cos, sin = _rope_tables()
    gla_consts = _gla_consts()
    ck = cache_k.reshape(n_lat, depth, PAST_LEN, ATT_HKV * HEAD_DIM)
    cv = cache_v.reshape(n_lat, depth, PAST_LEN, ATT_HKV * HEAD_DIM)

    w_main, w_small = _prep_w_in(w_in)
    w_out_b, w_ff1_b, w_ff2_b = w_out.astype(BF16), w_ff1.astype(BF16), w_ff2.astype(BF16)

    nks, nvs, sgs, sds = [], [], [], []
    for l in range(depth):
        tc, tl = n_ctx_rows // INPROJ_TM, n_lat * DEC_SEQ // INPROJ_TM
        inproj = functools.partial(_inproj, mods_l=mods[l], g1=norm1_g[l][None, :], w_main=w_main,
                                   w_small=w_small, layer=l, n_ctx_rows=n_ctx_rows)
        pc, psc = inproj(xs[0], tile0=0, n_tiles=tc, grp_tile0=0)
        if len(xs) == 2:
            pl_, psl = inproj(xs[1], tile0=0, n_tiles=tl, grp_tile0=tc)
        else:
            pl_, psl = inproj(xs[0], tile0=tc, n_tiles=tl, grp_tile0=tc)

        qg, kg = q_norm_g[l][None, :], k_norm_g[l][None, :]
        oa_c, nk, nv = _att_ctx(pc, qg, kg, n_ctx)
        oa_l = _att_lat(pl_, qg, kg, cos, sin, ck, cv, l, n_lat, 0)

        w2p = jnp.zeros((2, LANES, GLA_H * GLA_DK), F32)
        w2p = w2p.at[0, 0:GLA_LR].set(gla_w2[l, 0]).at[1, GLA_LR:2 * GLA_LR].set(gla_w2[l, 1])
        gbias = gla_b[l][:, None, :]
        gng = gla_norm_g[l][None, :]
        og_c, sg = _gla(pc, psc, w2p, gbias, gng, gla_consts, None, l, SEQ, n_ctx, 0)
        og_l, _ = _gla(pl_, psl, w2p, gbias, gng, gla_consts, state_gla, l, DEC_SEQ, n_lat, 0)

        alog_row = _lane_row(dn_a_log[l].reshape(-1), S_DA)
        dtb_row = _lane_row(dn_dt_bias[l].reshape(-1), S_DA)
        dng = dn_norm_g[l][None, :]
        od_c, sd = _dn(pc, psc, dn_conv[l], alog_row, dtb_row, dng, None, l, SEQ, n_ctx, 0)
        od_l, _ = _dn(pl_, psl, dn_conv[l], alog_row, dtb_row, dng, state_dn, l, DEC_SEQ, n_lat, 0)

        x1, h2 = _outproj(xs, [og_c, og_l], [oa_c, oa_l], [od_c, od_l], mods[l],
                          norm2_g[l][None, :], w_out_b, l, n_ctx_rows)
        n_tiles, nct = x1.shape[0] // FFN_TM, n_ctx_rows // FFN_TM
        ffn = functools.partial(_ffn, h2, x1, mods[l], w_ff1_b, w_ff2_b, l, n_ctx_rows, tm=FFN_TM)
        if l < depth - 1:
            xs = [ffn(0, n_tiles)]
        else:
            xs = [ffn(0, nct), ffn(nct, n_tiles - nct)]

        nks.append(nk.reshape(n_ctx, SEQ, ATT_HKV, HEAD_DIM))
        nvs.append(nv.reshape(n_ctx, SEQ, ATT_HKV, HEAD_DIM))
        sgs.append(sg)
        sds.append(sd)

    y_prompt = xs[0].reshape(n_ctx, SEQ, D_MODEL)
    y_sample = xs[1].reshape(n_lat, DEC_SEQ, D_MODEL)
    return (y_prompt, y_sample, jnp.stack(nks, axis=1), jnp.stack(nvs, axis=1),
            jnp.stack(sgs, axis=1), jnp.stack(sds, axis=1))
```

```python
import functools
import math

import numpy as np
import jax
import jax.numpy as jnp
from jax import lax
from jax.experimental import pallas as pl
from jax.experimental.pallas import tpu as pltpu

F32 = jnp.float32
BF16 = jnp.bfloat16

D_MODEL = 2048
SEQ = 256
DEC_SEQ = 1024
PAST_LEN = 256
GRID_W = 64
HEAD_DIM = 128
EPS = 1e-6
GLA_H = 4
GLA_DK = 64
GLA_DV = 128
GLA_LR = 16
GLA_GATE_NORM = 16.0
ATT_HQ = 8
ATT_HKV = 2
ROPE_THETA = 10000.0
DN_H = 4
DN_DK = 128
DN_DV = 128
DN_CONV = 3
CHUNK = 64
FF = 4 * D_MODEL
N_MOD = 6
LANES = 128

P_AQ = 0
P_AKV = 1024
P_GLA = 1536
P_DN = 3072
P_DZ = 4608
P_MAIN = 5120
S_DA = 32
S_DB = 40

VMEM_LIMIT = 56 * 1024 * 1024


def _cparams(sem):
    return pltpu.CompilerParams(dimension_semantics=sem, vmem_limit_bytes=VMEM_LIMIT)


def _bdot(a, b):
    return jnp.dot(a.astype(BF16), b.astype(BF16), preferred_element_type=F32)


def _bdot_nt(a, b):
    return lax.dot_general(a.astype(BF16), b.astype(BF16), (((1,), (1,)), ((), ())),
                           preferred_element_type=F32)


def _bdot_tn(a, b):
    return lax.dot_general(a.astype(BF16), b.astype(BF16), (((0,), (0,)), ((), ())),
                           preferred_element_type=F32)


def _split3(x):
    hi = x.astype(BF16)
    r = x - hi.astype(F32)
    mid = r.astype(BF16)
    lo = (r - mid.astype(F32)).astype(BF16)
    return hi, mid, lo


def _dot_exact_lhs(lhs_bf16, x):
    hi, mid, lo = _split3(x)
    d = functools.partial(jnp.dot, preferred_element_type=F32)
    return d(lhs_bf16, hi) + d(lhs_bf16, mid) + d(lhs_bf16, lo)


def _dot_exact_lhs_wide(lhs_bf16, x):
    n = x.shape[1]
    hi = x.astype(BF16)
    mid = (x - hi.astype(F32)).astype(BF16)
    r = jnp.dot(lhs_bf16, jnp.concatenate([hi, mid], axis=1), preferred_element_type=F32)
    return r[:, 0:n] + r[:, n:2 * n]


def _dot3(a, b):
    ah, am, _ = _split3(a)
    bh, bm, _ = _split3(b)
    d = functools.partial(jnp.dot, preferred_element_type=F32)
    return d(ah, bh) + (d(ah, bm) + d(am, bh))


def _sigmoid(x):
    return 0.5 * jnp.tanh(0.5 * x) + 0.5


def _silu(x):
    return x * _sigmoid(x)


def _softplus(x):
    return jnp.maximum(x, 0.0) + jnp.log1p(jnp.exp(-jnp.abs(x)))


def _rms(x, g):
    return x * lax.rsqrt(jnp.mean(x * x, axis=-1, keepdims=True) + EPS) * g


def _mods_kernel(c_ref, w_ref, b_ref, o_ref):
    o_ref[...] = _bdot(_silu(c_ref[...]), w_ref[...]) + b_ref[...]


def _modulation(cond8, w_mod, b_mod):
    depth = w_mod.shape[0]
    n_out = N_MOD * D_MODEL
    tn = 1024
    return pl.pallas_call(
        _mods_kernel,
        grid=(depth, n_out // tn),
        in_specs=[
            pl.BlockSpec((8, D_MODEL), lambda l, j: (0, 0)),
            pl.BlockSpec((None, D_MODEL, tn), lambda l, j: (l, 0, j)),
            pl.BlockSpec((None, 1, tn), lambda l, j: (l, 0, j)),
        ],
        out_specs=pl.BlockSpec((None, 8, tn), lambda l, j: (l, 0, j)),
        out_shape=jax.ShapeDtypeStruct((depth, 8, n_out), F32),
        compiler_params=_cparams(("arbitrary", "arbitrary")),
        name="modulation",
    )(cond8, w_mod, b_mod.reshape(depth, 1, n_out))


def _group_of_tile(i, tm, n_ctx_rows):
    n_ctx_tiles = n_ctx_rows // tm
    per_lat = DEC_SEQ // tm
    return jnp.where(i < n_ctx_tiles, 0, 1 + (i - n_ctx_tiles) // per_lat)


def _row_specs(arrays, width, tm, nct):
    if len(arrays) == 1:
        return [pl.BlockSpec((tm, width), lambda i, *_: (i, 0))]
    return [pl.BlockSpec((tm, width), lambda i, *_: (jnp.minimum(i, nct - 1), 0)),
            pl.BlockSpec((tm, width), lambda i, *_: (jnp.maximum(i - nct, 0), 0))]


def _row_load(refs, nct, rows=slice(None)):
    if len(refs) == 1:
        return refs[0][rows, :]
    return jnp.where(pl.program_id(0) < nct, refs[0][rows, :], refs[1][rows, :])


INPROJ_TM = 1024


def _inproj_kernel(x_ref, mod_ref, g_ref, w_ref, ws_ref, p_ref, ps_ref, h_sc):
    @pl.when(pl.program_id(1) == 0)
    def _():
        m = mod_ref[...]
        h = _rms(x_ref[...], g_ref[...]) * (1.0 + m[1:2]) + m[0:1]
        hb = h.astype(BF16)
        h_sc[...] = hb
        ps_ref[...] = jnp.dot(hb, ws_ref[...], preferred_element_type=F32)

    h = h_sc[...]
    half = w_ref.shape[1] // 2
    for piece in range(2):
        cols = slice(piece * half, (piece + 1) * half)
        p_ref[:, cols] = jnp.dot(h, w_ref[:, cols], preferred_element_type=F32)


def _inproj(x, mods_l, g1, w_main, w_small, layer, n_ctx_rows, tile0, n_tiles, grp_tile0, tn=1024):
    tm = INPROJ_TM
    grp = functools.partial(_group_of_tile, tm=tm, n_ctx_rows=n_ctx_rows)
    return pl.pallas_call(
        _inproj_kernel,
        grid=(n_tiles, P_MAIN // tn),
        in_specs=[
            pl.BlockSpec((tm, D_MODEL), lambda i, j: (tile0 + i, 0)),
            pl.BlockSpec((None, N_MOD, D_MODEL), lambda i, j: (grp(grp_tile0 + i), 0, 0)),
            pl.BlockSpec((1, D_MODEL), lambda i, j: (0, 0)),
            pl.BlockSpec((None, D_MODEL, tn), lambda i, j: (layer, 0, j)),
            pl.BlockSpec((None, D_MODEL, LANES), lambda i, j: (layer, 0, 0)),
        ],
        out_specs=[
            pl.BlockSpec((tm, tn), lambda i, j: (i, j)),
            pl.BlockSpec((tm, LANES), lambda i, j: (i, 0)),
        ],
        out_shape=[
            jax.ShapeDtypeStruct((n_tiles * tm, P_MAIN), F32),
            jax.ShapeDtypeStruct((n_tiles * tm, LANES), F32),
        ],
        scratch_shapes=[pltpu.VMEM((tm, D_MODEL), BF16)],
        compiler_params=_cparams(("arbitrary", "arbitrary")),
        name="inproj",
    )(x, mods_l, g1, w_main, w_small)


def _outproj_kernel(nct, n_x, *refs):
    x_refs = refs[:n_x]
    (ogc_ref, ogl_ref, oac_ref, oal_ref, odc_ref, odl_ref, mod_ref, g_ref, w_ref,
     x1_ref, h2_ref) = refs[n_x:]
    m = mod_ref[...]
    d = functools.partial(jnp.dot, preferred_element_type=F32)
    half = x1_ref.shape[0] // 2
    for r in range(2):
        rows = slice(r * half, (r + 1) * half)
        mix = (d(_row_load((ogc_ref, ogl_ref), nct, rows), w_ref[0:512, :])
               + d(_row_load((oac_ref, oal_ref), nct, rows), w_ref[512:1536, :])
               + d(_row_load((odc_ref, odl_ref), nct, rows), w_ref[1536:2048, :]))
        x1 = _row_load(x_refs, nct, rows) + m[2:3] * mix
        x1_ref[rows, :] = x1
        h2_ref[rows, :] = (_rms(x1, g_ref[...]) * (1.0 + m[4:5]) + m[3:4]).astype(BF16)


def _outproj(xs, o_gla, o_att, o_dn, mods_l, g2, w_out, layer, n_ctx_rows, tm=512):
    rows = sum(x.shape[0] for x in xs)
    nct = n_ctx_rows // tm
    grp = functools.partial(_group_of_tile, tm=tm, n_ctx_rows=n_ctx_rows)
    return pl.pallas_call(
        functools.partial(_outproj_kernel, nct, len(xs)),
        grid=(rows // tm,),
        in_specs=(_row_specs(xs, D_MODEL, tm, nct) + _row_specs(o_gla, 512, tm, nct)
                  + _row_specs(o_att, 1024, tm, nct) + _row_specs(o_dn, 512, tm, nct) + [
                      pl.BlockSpec((None, N_MOD, D_MODEL), lambda i: (grp(i), 0, 0)),
                      pl.BlockSpec((1, D_MODEL), lambda i: (0, 0)),
                      pl.BlockSpec((None, D_MODEL, D_MODEL), lambda i: (layer, 0, 0),
                                   pipeline_mode=pl.Buffered(1)),
                  ]),
        out_specs=[
            pl.BlockSpec((tm, D_MODEL), lambda i: (i, 0)),
            pl.BlockSpec((tm, D_MODEL), lambda i: (i, 0)),
        ],
        out_shape=[
            jax.ShapeDtypeStruct((rows, D_MODEL), F32),
            jax.ShapeDtypeStruct((rows, D_MODEL), BF16),
        ],
        compiler_params=_cparams(("arbitrary",)),
        name="outproj",
    )(*xs, *o_gla, *o_att, *o_dn, mods_l, g2, w_out)


FFN_TM = 512
FFN_OUT_CHUNK = 512


def _ffn_kernel(h_ref, x1_ref, mod_ref, w1_ref, w2_ref, y_ref):
    j = pl.program_id(1)
    tf = w1_ref.shape[1]

    @pl.when(j == 0)
    def _():
        y_ref[...] = jnp.zeros(y_ref.shape, F32)

    h = h_ref[...]
    for half in range(2):
        cols = slice(half * (tf // 2), (half + 1) * (tf // 2))
        t = jnp.dot(h, w1_ref[:, cols], preferred_element_type=F32)
        t = jnp.square(jnp.maximum(t, 0.0)).astype(BF16)
        for nc in range(D_MODEL // FFN_OUT_CHUNK):
            ncs = slice(nc * FFN_OUT_CHUNK, (nc + 1) * FFN_OUT_CHUNK)
            y_ref[:, ncs] += jnp.dot(t, w2_ref[cols, ncs], preferred_element_type=F32)

    @pl.when(j == pl.num_programs(1) - 1)
    def _():
        y_ref[...] = x1_ref[...] + mod_ref[5:6, :] * y_ref[...]


def _ffn(h2, x1, mods_l, w1, w2, layer, n_ctx_rows, tile0, n_tiles, tm, tf=1024):
    grp = functools.partial(_group_of_tile, tm=tm, n_ctx_rows=n_ctx_rows)
    return pl.pallas_call(
        _ffn_kernel,
        grid=(n_tiles, FF // tf),
        in_specs=[
            pl.BlockSpec((tm, D_MODEL), lambda i, j: (tile0 + i, 0)),
            pl.BlockSpec((tm, D_MODEL), lambda i, j: (tile0 + i, 0)),
            pl.BlockSpec((None, N_MOD, D_MODEL), lambda i, j: (grp(tile0 + i), 0, 0)),
            pl.BlockSpec((None, D_MODEL, tf), lambda i, j: (layer, 0, j)),
            pl.BlockSpec((None, tf, D_MODEL), lambda i, j: (layer, j, 0)),
        ],
        out_specs=pl.BlockSpec((tm, D_MODEL), lambda i, j: (i, 0)),
        out_shape=jax.ShapeDtypeStruct((n_tiles * tm, D_MODEL), F32),
        compiler_params=_cparams(("arbitrary", "arbitrary")),
        name="ffn",
    )(h2, x1, mods_l, w1, w2)


def _rope_tables():
    half = HEAD_DIM // 2
    pos = np.arange(DEC_SEQ)
    row = (pos // GRID_W).astype(np.float32)
    col = (pos % GRID_W).astype(np.float32)
    inv = (ROPE_THETA ** (-np.arange(0, half, 2, dtype=np.float32) / half)).astype(np.float32)
    ar = row[:, None] * inv[None, :]
    ac = col[:, None] * inv[None, :]
    cos = np.concatenate([np.cos(ar), np.cos(ar), np.cos(ac), np.cos(ac)], axis=1)
    sin = np.concatenate([-np.sin(ar), np.sin(ar), -np.sin(ac), np.sin(ac)], axis=1)
    return jnp.asarray(cos, F32), jnp.asarray(sin, F32)


def _rope(x, cos, sin):
    lane = lax.broadcasted_iota(jnp.int32, x.shape, 1)
    first = (lane % 64) < 32
    partner = jnp.where(first, pltpu.roll(x, 96, 1), pltpu.roll(x, 32, 1))
    return x * cos + partner * sin


ATT_G = ATT_HQ // ATT_HKV


def _softmax_pv(q, kb, vb):
    s = lax.dot_general(q.astype(BF16), kb, (((1,), (1,)), ((), ())), preferred_element_type=F32)
    m = jnp.max(s, axis=-1, keepdims=True)
    p = jnp.exp(s - m)
    l = jnp.sum(p, axis=-1, keepdims=True)
    return jnp.dot(p.astype(BF16), vb, preferred_element_type=F32) / l


def _group_attention(q_heads, kb, vb, store):
    for g, q in enumerate(q_heads):
        store(g, _softmax_pv(q, kb, vb).astype(BF16))


def _att_ctx_kernel(aq_ref, akv_ref, qg_ref, kg_ref, o_ref, nk_ref, nv_ref):
    qg = qg_ref[...] * (HEAD_DIM ** -0.5)
    kg = kg_ref[...]
    for hk in range(ATT_HKV):
        ks = slice(hk * HEAD_DIM, (hk + 1) * HEAD_DIM)
        kn = _rms(akv_ref[:, ks], kg)
        v = akv_ref[:, 256 + hk * HEAD_DIM:256 + (hk + 1) * HEAD_DIM]
        nk_ref[:, ks] = kn
        nv_ref[:, ks] = v

        def head_cols(g, hk=hk):
            return slice((hk * ATT_G + g) * HEAD_DIM, (hk * ATT_G + g + 1) * HEAD_DIM)

        def store(g, val):
            o_ref[:, head_cols(g)] = val

        qs = [_rms(aq_ref[:, head_cols(g)], qg) for g in range(ATT_G)]
        _group_attention(qs, kn.astype(BF16), v.astype(BF16), store)


def _att_lat_kernel(aq_ref, akv_ref, qg_ref, kg_ref, cos_ref, sin_ref, ck_ref, cv_ref, o_ref,
                    k_sc, v_sc):
    qg = qg_ref[...] * (HEAD_DIM ** -0.5)
    kg = kg_ref[...]
    qb = 256
    for hk in range(ATT_HKV):
        ks = slice(hk * HEAD_DIM, (hk + 1) * HEAD_DIM)
        k_sc[hk, 0:PAST_LEN, :] = ck_ref[:, ks].astype(BF16)
        v_sc[hk, 0:PAST_LEN, :] = cv_ref[:, ks].astype(BF16)
        kn = _rope(_rms(akv_ref[:, ks], kg), cos_ref[...], sin_ref[...])
        k_sc[hk, PAST_LEN:, :] = kn.astype(BF16)
        v_sc[hk, PAST_LEN:, :] = akv_ref[:, 256 + hk * HEAD_DIM:256 + (hk + 1) * HEAD_DIM].astype(BF16)

    def body(r, carry):
        r0 = pl.multiple_of(r * qb, qb)
        cos = cos_ref[pl.ds(r0, qb), :]
        sin = sin_ref[pl.ds(r0, qb), :]
        for hk in range(ATT_HKV):
            def head_cols(g, hk=hk):
                return slice((hk * ATT_G + g) * HEAD_DIM, (hk * ATT_G + g + 1) * HEAD_DIM)

            def store(g, val):
                o_ref[pl.ds(r0, qb), head_cols(g)] = val

            qs = [_rope(_rms(aq_ref[pl.ds(r0, qb), head_cols(g)], qg), cos, sin)
                  for g in range(ATT_G)]
            _group_attention(qs, k_sc[hk], v_sc[hk], store)
        return carry

    lax.fori_loop(0, DEC_SEQ // qb, body, 0)


def _att_ctx(p_main, qg, kg, n_seq):
    L = SEQ
    return pl.pallas_call(
        _att_ctx_kernel,
        grid=(n_seq,),
        in_specs=[
            pl.BlockSpec((L, 1024), lambda b: (b, P_AQ // 1024)),
            pl.BlockSpec((L, 512), lambda b: (b, P_AKV // 512)),
            pl.BlockSpec((1, HEAD_DIM), lambda b: (0, 0)),
            pl.BlockSpec((1, HEAD_DIM), lambda b: (0, 0)),
        ],
        out_specs=[
            pl.BlockSpec((L, 1024), lambda b: (b, 0)),
            pl.BlockSpec((None, L, 256), lambda b: (b, 0, 0)),
            pl.BlockSpec((None, L, 256), lambda b: (b, 0, 0)),
        ],
        out_shape=[
            jax.ShapeDtypeStruct((n_seq * L, 1024), BF16),
            jax.ShapeDtypeStruct((n_seq, L, 256), F32),
            jax.ShapeDtypeStruct((n_seq, L, 256), F32),
        ],
        compiler_params=_cparams(("arbitrary",)),
        name="att_ctx",
    )(p_main, p_main, qg, kg)


def _att_lat(p_main, qg, kg, cos, sin, ck, cv, layer, n_seq, row_blk0):
    L = DEC_SEQ
    return pl.pallas_call(
        _att_lat_kernel,
        grid=(n_seq,),
        in_specs=[
            pl.BlockSpec((L, 1024), lambda b: (row_blk0 + b, P_AQ // 1024)),
            pl.BlockSpec((L, 512), lambda b: (row_blk0 + b, P_AKV // 512)),
            pl.BlockSpec((1, HEAD_DIM), lambda b: (0, 0)),
            pl.BlockSpec((1, HEAD_DIM), lambda b: (0, 0)),
            pl.BlockSpec((L, HEAD_DIM), lambda b: (0, 0)),
            pl.BlockSpec((L, HEAD_DIM), lambda b: (0, 0)),
            pl.BlockSpec((None, None, PAST_LEN, 256), lambda b: (b, layer, 0, 0)),
            pl.BlockSpec((None, None, PAST_LEN, 256), lambda b: (b, layer, 0, 0)),
        ],
        out_specs=pl.BlockSpec((L, 1024), lambda b: (b, 0)),
        out_shape=jax.ShapeDtypeStruct((n_seq * L, 1024), BF16),
        scratch_shapes=[
            pltpu.VMEM((ATT_HKV, PAST_LEN + L, HEAD_DIM), BF16),
            pltpu.VMEM((ATT_HKV, PAST_LEN + L, HEAD_DIM), BF16),
        ],
        compiler_params=_cparams(("arbitrary",)),
        name="att_lat",
    )(p_main, p_main, qg, kg, cos, sin, ck, cv)


def _seqs_per_step(L, n_seq):
    return 2 if (L <= SEQ and n_seq % 2 == 0) else 1


_GLA_LEVELS = (32, 16, 8, 4, 2, 1)


def _gla_consts():
    C = CHUNK
    i = np.arange(C)[:, None]
    t = np.arange(C)[None, :]
    mats = [t <= i, t > i]
    masks = [i == t]
    for s in _GLA_LEVELS:
        m = (i // s) * s
        if s > 1:
            mats.append((t > m) & (t <= i))
        mats.append((t > i) & (t <= m + s))
        masks.append((i // (2 * s) == t // (2 * s)) & ((i // s) % 2 == 1) & ((t // s) % 2 == 0))
    fwd = np.concatenate(mats, 0)
    bwd = np.concatenate([mm[::-1, ::-1] for mm in mats], 0)
    mstack = np.stack([fwd, bwd]).astype(np.float32)
    mf = np.stack([np.concatenate([mm, mm], 0) for mm in masks])
    mb = np.stack([np.concatenate([mm[::-1, ::-1], mm[::-1, ::-1]], 0) for mm in masks])
    lmask = np.stack([mf, mb]).astype(np.float32)
    return jnp.asarray(mstack, BF16), jnp.asarray(lmask, F32)


def _gla_kernel(L, nb, has_state, *refs):
    if has_state:
        (pg_ref, ps_ref, w2_ref, gb_ref, ng_ref, ms_ref, lm_ref, s0_ref,
         o_ref, so_ref, la_sc, of_sc, ob_sc, st_sc) = refs
    else:
        (pg_ref, ps_ref, w2_ref, gb_ref, ng_ref, ms_ref, lm_ref,
         o_ref, so_ref, la_sc, of_sc, ob_sc, st_sc) = refs
    n = L // CHUNK
    C = CHUNK

    ps = ps_ref[...]
    for z in range(2):
        x = _dot3(ps, w2_ref[z]) + gb_ref[z]
        la_sc[z] = (jnp.minimum(x, 0.0) - jnp.log1p(jnp.exp(-jnp.abs(x)))) * (1.0 / GLA_GATE_NORM)

    if has_state:
        for sq in range(nb):
            for z in range(2):
                for p in range(2):
                    pair = jnp.concatenate([s0_ref[sq, z, 2 * p], s0_ref[sq, z, 2 * p + 1]], axis=0)
                    st_sc[sq, z, p] = jnp.transpose(pair)
    else:
        st_sc[...] = jnp.zeros(st_sc.shape, F32)

    lane = lax.broadcasted_iota(jnp.int32, (C, LANES), 1)
    first = lane < 64

    def stack_heads(t):
        return jnp.concatenate([jnp.where(first, t, 0.0), jnp.where(first, 0.0, t)], axis=0)

    chains = [(t, sq, z, p) for t in range(2) for sq in range(nb) for z in range(2) for p in range(2)]
    nch = len(chains)
    nch1 = nch // 2

    def body(it, carry):
        def chunk_rows(t, sq, z):
            c = 2 * it + t
            cc = c if z == 0 else n - 1 - c
            return pl.ds(pl.multiple_of(sq * L + cc * C, C), C)

        rows = [chunk_rows(t, sq, z) for t, sq, z, p in chains]
        last = [C - 1, 0]
        q = [pg_ref[rows[i], p * LANES:(p + 1) * LANES] * (GLA_DK ** -0.5)
             for i, (t, sq, z, p) in enumerate(chains)]
        k = [pg_ref[rows[i], 256 + p * LANES:256 + (p + 1) * LANES]
             for i, (t, sq, z, p) in enumerate(chains)]
        e = [jnp.exp(_dot_exact_lhs_wide(ms_ref[z], la_sc[z, rows[i], p * LANES:(p + 1) * LANES]))
             for i, (t, sq, z, p) in enumerate(chains)]
        a2 = [_bdot_nt(stack_heads(q[i]), k[i]) * lm_ref[z, 0]
              for i, (t, sq, z, p) in enumerate(chains)]
        blk = 2
        for li, s in enumerate(_GLA_LEVELS):
            if s > 1:
                qs = [q[i] * e[i][blk * C:(blk + 1) * C] for i in range(nch)]
                blk += 1
            else:
                qs = q
            ks = [k[i] * e[i][blk * C:(blk + 1) * C] for i in range(nch)]
            blk += 1
            a2 = [a2[i] + _bdot_nt(stack_heads(qs[i]), ks[i]) * lm_ref[z, li + 1]
                  for i, (t, sq, z, p) in enumerate(chains)]
        kl = [k[i] * e[i][C:2 * C] for i in range(nch)]
        v = [[pg_ref[rows[i], 512 + (2 * p + hh) * LANES:512 + (2 * p + hh + 1) * LANES]
              for hh in range(2)] for i, (t, sq, z, p) in enumerate(chains)]
        intra = [[_bdot(a2[i][hh * C:(hh + 1) * C], v[i][hh]) for hh in range(2)]
                 for i in range(nch)]
        zs = [[_bdot_tn(v[i][hh], kl[i]) for hh in range(2)] for i in range(nch)]
        qe = [stack_heads(q[i] * e[i][0:C]) for i in range(nch)]
        st = [st_sc[sq, z, p] for t, sq, z, p in chains[0:nch1]]
        for i, (t, sq, z, p) in enumerate(chains):
            j = i % nch1
            inter = _bdot_nt(qe[i], st[j])
            osc = of_sc if z == 0 else ob_sc
            for hh in range(2):
                h = 2 * p + hh
                osc[rows[i], h * LANES:(h + 1) * LANES] = intra[i][hh] + inter[hh * C:(hh + 1) * C]
            st[j] = (st[j] * e[i][last[z]:last[z] + 1, :]
                     + jnp.where(first[0:1], zs[i][0], zs[i][1]))
        for j, (t, sq, z, p) in enumerate(chains[0:nch1]):
            st_sc[sq, z, p] = st[j]
        return carry

    lax.fori_loop(0, n // 2, body, 0)

    ng = ng_ref[...]

    def fin(c, carry):
        r0 = pl.multiple_of(c * C, C)
        rows = pl.ds(r0, C)
        for h in range(GLA_H):
            cols = slice(h * LANES, (h + 1) * LANES)
            o = of_sc[rows, cols] + ob_sc[rows, cols]
            gate = _silu(pg_ref[rows, 1024 + h * LANES:1024 + (h + 1) * LANES])
            o_ref[rows, cols] = (_rms(o, ng) * gate).astype(BF16)
        return carry

    lax.fori_loop(0, nb * n, fin, 0)
    for sq in range(nb):
        for z in range(2):
            for p in range(2):
                pair = jnp.transpose(st_sc[sq, z, p])
                so_ref[sq, z, 2 * p] = pair[0:GLA_DK]
                so_ref[sq, z, 2 * p + 1] = pair[GLA_DK:2 * GLA_DK]


def _gla(p_main, p_small, w2p, gbias, ng, consts, s0, layer, L, n_seq, row_blk0):
    mstack, lmask = consts
    has_state = s0 is not None
    nb = _seqs_per_step(L, n_seq)
    assert row_blk0 % nb == 0
    blk0 = row_blk0 // nb
    state_blk = (nb, 2, GLA_H, GLA_DK, GLA_DV)
    in_specs = [
        pl.BlockSpec((nb * L, 1536), lambda b: (blk0 + b, P_GLA // 1536)),
        pl.BlockSpec((nb * L, LANES), lambda b: (blk0 + b, 0)),
        pl.BlockSpec((2, LANES, 256), lambda b: (0, 0, 0)),
        pl.BlockSpec((2, 1, 256), lambda b: (0, 0, 0)),
        pl.BlockSpec((1, GLA_DV), lambda b: (0, 0)),
        pl.BlockSpec(mstack.shape, lambda b: (0, 0, 0)),
        pl.BlockSpec(lmask.shape, lambda b: (0, 0, 0, 0)),
    ]
    args = [p_main, p_small, w2p, gbias, ng, mstack, lmask]
    if has_state:
        in_specs.append(pl.BlockSpec((nb, None) + state_blk[1:], lambda b: (b, layer, 0, 0, 0, 0)))
        args.append(s0)
    return pl.pallas_call(
        functools.partial(_gla_kernel, L, nb, has_state),
        grid=(n_seq // nb,),
        in_specs=in_specs,
        out_specs=[
            pl.BlockSpec((nb * L, 512), lambda b: (b, 0)),
            pl.BlockSpec(state_blk, lambda b: (b, 0, 0, 0, 0)),
        ],
        out_shape=[
            jax.ShapeDtypeStruct((n_seq * L, 512), BF16),
            jax.ShapeDtypeStruct((n_seq, 2, GLA_H, GLA_DK, GLA_DV), F32),
        ],
        scratch_shapes=[
            pltpu.VMEM((2, nb * L, 256), F32),
            pltpu.VMEM((nb * L, 512), F32),
            pltpu.VMEM((nb * L, 512), F32),
            pltpu.VMEM((nb, 2, 2, GLA_DV, LANES), F32),
        ],
        compiler_params=_cparams(("arbitrary",)),
        name="gla_lat" if has_state else "gla_ctx",
    )(*args)


def _dn_kernel(L, nb, has_state, *refs):
    if has_state:
        (pd_ref, dz_ref, ps_ref, cw_ref, al_ref, dtb_ref, ng_ref, s0_ref,
         o_ref, so_ref, qkv_sc, g_sc, bb_sc, of_sc, ob_sc, s_sc, prep_sc, att_sc, dec_sc) = refs
    else:
        (pd_ref, dz_ref, ps_ref, cw_ref, al_ref, dtb_ref, ng_ref,
         o_ref, so_ref, qkv_sc, g_sc, bb_sc, of_sc, ob_sc, s_sc, prep_sc, att_sc, dec_sc) = refs
    n = L // CHUNK
    C = CHUNK

    R = nb * L
    pos = lax.broadcasted_iota(jnp.int32, (R, LANES), 0) & (L - 1)
    not_first = pos > 0
    not_last = pos < L - 1

    def conv_silu(j):
        c0 = pl.multiple_of(j * LANES, LANES)
        x = pd_ref[:, pl.ds(c0, LANES)]
        w = cw_ref[:, pl.ds(c0, LANES)]
        prev = jnp.where(not_first, pltpu.roll(x, 1, 0), 0.0)
        nxt = jnp.where(not_last, pltpu.roll(x, R - 1, 0), 0.0)
        return c0, _silu(prev * w[0:1] + x * w[1:2] + nxt * w[2:3])

    def qk_tile(j, carry):
        c0, y = conv_silu(j)
        inv = lax.rsqrt(jnp.sum(y * y, axis=-1, keepdims=True) + EPS)
        qkv_sc[:, pl.ds(c0, LANES)] = y * (inv * jnp.where(j < DN_H, DN_DK ** -0.5, 1.0))
        return carry

    def v_tile(j, carry):
        c0, y = conv_silu(j)
        qkv_sc[:, pl.ds(c0, LANES)] = y
        return carry

    lax.fori_loop(0, 2 * DN_H, qk_tile, 0)
    lax.fori_loop(2 * DN_H, 3 * DN_H, v_tile, 0)

    ps = ps_ref[...]
    lane_l = lax.broadcasted_iota(jnp.int32, (R, LANES), 1)
    g_sc[...] = -jnp.exp(al_ref[...]) * _softplus(ps + dtb_ref[...])
    b_all = _sigmoid(ps)
    for idx in range(2 * DN_H):
        bcol = jnp.sum(jnp.where(lane_l == S_DB + idx, b_all, 0.0), axis=-1, keepdims=True)
        bb_sc[idx] = jnp.broadcast_to(bcol, (R, LANES))

    if has_state:
        s_sc[...] = s0_ref[...]
    else:
        s_sc[...] = jnp.zeros(s_sc.shape, F32)

    ri = lax.broadcasted_iota(jnp.int32, (C, LANES), 0)
    lane_c = lax.broadcasted_iota(jnp.int32, (C, LANES), 1)
    ci = lane_c & (C - 1)
    first = lane_c < C
    incl = [ci <= ri, ci >= ri]
    strict = [ci < ri, ci > ri]
    rs = lax.broadcasted_iota(jnp.int32, (C, C), 0)
    cs = lax.broadcasted_iota(jnp.int32, (C, C), 1)
    m_b = [(cs <= rs).astype(BF16), (cs >= rs).astype(BF16)]
    nu = 2 * nb
    chains = [(sq, z, h) for sq in range(nu) for z in range(2) for h in range(DN_H)]
    nch = len(chains)
    pairs = [(sq, z, p) for sq in range(nu) for z in range(2) for p in range(DN_H // 2)]
    npr = len(pairs)

    def stack_heads(t):
        return jnp.concatenate([jnp.where(first, t, 0.0), jnp.where(first, 0.0, t)], axis=0)

    def hi_mid(x):
        hi = x.astype(BF16).astype(F32)
        return hi, x - hi

    def pair_products(lhs_list, x):
        xh, xm = hi_mid(x)
        rhs = jnp.concatenate([stack_heads(xh), stack_heads(xm)], axis=1).astype(BF16)
        parts = []
        for l in lhs_list:
            parts.extend(hi_mid(l))
        r = jnp.dot(jnp.concatenate(parts, axis=0).astype(BF16), rhs, preferred_element_type=F32)
        out = []
        for i in range(len(lhs_list)):
            blk = r[2 * C * i:2 * C * i + C] + r[2 * C * i + C:2 * C * (i + 1)]
            out.append(blk[:, 0:LANES] + blk[:, LANES:2 * LANES])
        return out

    heads = [((2 * sq + z) * DN_H + 2 * p, (2 * sq + z) * DN_H + 2 * p + 1)
             for sq, z, p in pairs]

    def chunk_rows(cs):
        def one(sq, z):
            c = cs[sq // nb]
            start = (sq % nb) * L + (c if z == 0 else n - 1 - c) * C
            return pl.ds(start if isinstance(start, int) else pl.multiple_of(start, C), C)
        return [one(sq, z) for sq in range(nu) for z in range(2)]

    def prepare_products(cs):
        rows = chunk_rows(cs)
        last = [C - 1, 0]

        def ld(base, sq, z, h):
            return qkv_sc[rows[2 * sq + z], base + h * LANES:base + (h + 1) * LANES]

        q = [ld(0, *ch) for ch in chains]
        k = [ld(512, *ch) for ch in chains]
        v = [ld(1024, *ch) for ch in chains]
        bb = [bb_sc[z * DN_H + h, rows[2 * sq + z], :] for sq, z, h in chains]
        gc = [_dot_exact_lhs(m_b[z], g_sc[rows[2 * sq + z], :])
              for sq in range(nu) for z in range(2)]
        gct =[jnp.transpose(jnp.concatenate([g, g], axis=0)) for g in gc]
        gi = [jnp.broadcast_to(jnp.sum(jnp.where(lane_c == S_DA + z * DN_H + h, gc[2 * sq + z], 0.0),
                                       axis=-1, keepdims=True), (C, LANES)) for sq, z, h in chains]
        gjrow = [gct[2 * sq + z][S_DA + z * DN_H + h:S_DA + z * DN_H + h + 1, :]
                 for sq, z, h in chains]
        gam = []
        for (sq, z, p), (i0, i1) in zip(pairs, heads):
            diff = jnp.where(first, gi[i0], gi[i1]) - jnp.where(first[0:1], gjrow[i0], gjrow[i1])
            gam.append(jnp.where(incl[z], jnp.exp(jnp.where(incl[z], diff, 0.0)), 0.0))
        kb = [k[i] * bb[i] for i in range(nch)]
        kq = [_bdot_nt(jnp.concatenate([kb[i0], q[i0], kb[i1], q[i1]], axis=0),
                       jnp.concatenate([k[i0], k[i1]], axis=0)) for i0, i1 in heads]
        gt = [gi[i][last[z]:last[z] + 1, :] for i, (sq, z, h) in enumerate(chains)]
        return q, k, v, bb, kb, gi, gt, gam, kq

    def prepare_finish(parts):
        q, k, v, bb, kb, gi, gt, gam, kq = parts
        a = [jnp.where(strict[z], jnp.where(first, kq[j][0:C], kq[j][2 * C:3 * C]) * gam[j], 0.0)
             for j, (sq, z, p) in enumerate(pairs)]
        att = [jnp.where(first, kq[j][C:2 * C], kq[j][3 * C:4 * C]) * gam[j] for j in range(npr)]
        nt = [-a[j] for j in range(npr)]
        pw = [pair_products([a[j]], a[j])[0] for j in range(npr)]
        for step in range(5):
            if step < 4:
                both = [pair_products([nt[j], pw[j]], pw[j]) for j in range(npr)]
                nt = [nt[j] + pw[j] + both[j][0] for j in range(npr)]
                pw = [both[j][1] for j in range(npr)]
            else:
                nt = [nt[j] + pw[j] + pair_products([nt[j]], pw[j])[0] for j in range(npr)]
        eg = [jnp.exp(gi[i]) for i in range(nch)]
        rhs = [jnp.concatenate([v[i] * bb[i], kb[i] * eg[i]], axis=1) for i in range(nch)]
        corr = [_bdot(stack_heads(nt[j]), jnp.concatenate([rhs[i0], rhs[i1]], axis=0))
                for j, (i0, i1) in enumerate(heads)]
        for j, (i0, i1) in enumerate(heads):
            for half, i in enumerate((i0, i1)):
                uw = rhs[i] + corr[j][half * C:(half + 1) * C]
                prep_sc[i, 0] = uw[:, 0:LANES]
                prep_sc[i, 1] = uw[:, LANES:]
                prep_sc[i, 2] = q[i] * eg[i]
                prep_sc[i, 3] = k[i] * jnp.exp(gt[i] - gi[i])
                dec_sc[i] = jnp.broadcast_to(jnp.exp(gt[i]), (8, LANES))
            att_sc[j] = att[j]

    nch1, npr1 = nch // 2, npr // 2

    def apply_state_products(slot, s):
        return [_bdot(jnp.concatenate([prep_sc[slot * nch1 + i, 1], prep_sc[slot * nch1 + i, 2]],
                                      axis=0), s[i]) for i in range(nch1)]

    def apply_finish(rows, slot, s, ws):
        base = slot * nch1
        v_new = [prep_sc[base + i, 0] - ws[i][0:C] for i in range(nch1)]
        av = []
        for j in range(npr1):
            i0, i1 = heads[slot * npr1 + j]
            av.append(_bdot(stack_heads(att_sc[slot * npr1 + j]),
                            jnp.concatenate([v_new[i0 - base], v_new[i1 - base]], axis=0)))
        upd = [_bdot_tn(prep_sc[base + i, 3], v_new[i]) for i in range(nch1)]
        for j in range(npr1):
            for half, ig in enumerate(heads[slot * npr1 + j]):
                sq, z, h = chains[ig]
                osc = of_sc if z == 0 else ob_sc
                osc[rows[2 * sq + z], h * LANES:(h + 1) * LANES] = (
                    ws[ig - base][C:2 * C] + av[j][half * C:(half + 1) * C])
        return [s[i] * dec_sc[base + i, 0:1, :] + upd[i] for i in range(nch1)]

    def apply_two(cs, parts_fn):
        rows = chunk_rows(cs)
        s = [s_sc[sq, z, h] for sq, z, h in chains[0:nch1]]
        ws = apply_state_products(0, s)
        parts = parts_fn()
        s = apply_finish(rows, 0, s, ws)
        ws = apply_state_products(1, s)
        s = apply_finish(rows, 1, s, ws)
        for i, (sq, z, h) in enumerate(chains[0:nch1]):
            s_sc[sq, z, h] = s[i]
        return parts

    prepare_finish(prepare_products((0, 1)))

    def body(it, carry):
        c = 2 * it
        parts = apply_two((c, c + 1), lambda: prepare_products((c + 2, c + 3)))
        prepare_finish(parts)
        return carry

    lax.fori_loop(0, n // 2 - 1, body, 0)
    apply_two((n - 2, n - 1), lambda: None)

    ng = ng_ref[...]

    def fin(c, carry):
        r0 = pl.multiple_of(c * C, C)
        rows = pl.ds(r0, C)
        for h in range(DN_H):
            cols = slice(h * LANES, (h + 1) * LANES)
            o = of_sc[rows, cols] + ob_sc[rows, cols]
            o_ref[rows, cols] = (_rms(o, ng) * _silu(dz_ref[rows, cols])).astype(BF16)
        return carry

    lax.fori_loop(0, nb * n, fin, 0)
    so_ref[...] = s_sc[...]


def _dn(p_main, p_small, conv_w, alog_row, dtb_row, ng, s0, layer, L, n_seq, row_blk0):
    has_state = s0 is not None
    nb = _seqs_per_step(L, n_seq)
    assert row_blk0 % nb == 0
    row_blk0, L_blk = row_blk0 // nb, nb * L
    in_specs = [
        pl.BlockSpec((L_blk, 1536), lambda b: (row_blk0 + b, P_DN // 1536)),
        pl.BlockSpec((L_blk, 512), lambda b: (row_blk0 + b, P_DZ // 512)),
        pl.BlockSpec((L_blk, LANES), lambda b: (row_blk0 + b, 0)),
        pl.BlockSpec((DN_CONV, 3 * DN_H * DN_DK), lambda b: (0, 0)),
        pl.BlockSpec((1, LANES), lambda b: (0, 0)),
        pl.BlockSpec((1, LANES), lambda b: (0, 0)),
        pl.BlockSpec((1, DN_DV), lambda b: (0, 0)),
    ]
    args = [p_main, p_main, p_small, conv_w, alog_row, dtb_row, ng]
    if has_state:
        in_specs.append(pl.BlockSpec((nb, None, 2, DN_H, DN_DK, DN_DV),
                                     lambda b: (b, layer, 0, 0, 0, 0)))
        args.append(s0)
    return pl.pallas_call(
        functools.partial(_dn_kernel, L, nb, has_state),
        grid=(n_seq // nb,),
        in_specs=in_specs,
        out_specs=[
            pl.BlockSpec((L_blk, 512), lambda b: (b, 0)),
            pl.BlockSpec((nb, 2, DN_H, DN_DK, DN_DV), lambda b: (b, 0, 0, 0, 0)),
        ],
        out_shape=[
            jax.ShapeDtypeStruct((n_seq * L, 512), BF16),
            jax.ShapeDtypeStruct((n_seq, 2, DN_H, DN_DK, DN_DV), F32),
        ],
        scratch_shapes=[
            pltpu.VMEM((L_blk, 1536), F32),
            pltpu.VMEM((L_blk, LANES), F32),
            pltpu.VMEM((2 * DN_H, L_blk, LANES), F32),
            pltpu.VMEM((L_blk, 512), F32),
            pltpu.VMEM((L_blk, 512), F32),
            pltpu.VMEM((nb, 2, DN_H, DN_DK, DN_DV), F32),
            pltpu.VMEM((2 * nb * 2 * DN_H, 4, CHUNK, LANES), F32),
            pltpu.VMEM((2 * nb * DN_H, CHUNK, LANES), F32),
            pltpu.VMEM((2 * nb * 2 * DN_H, 8, LANES), F32),
        ],
        compiler_params=_cparams(("arbitrary",)),
        name="dn_lat" if has_state else "dn_ctx",
    )(*args)


_W_IN_SPLITS = (256, 256, 512, 512, 32, 1024, 256, 256, 1536, 512, 8, 8)


def _regroup_kernel(wt_ref, main_ref, small_ref):
    off = np.cumsum((0,) + _W_IN_SPLITS)
    gq, gk, gv, gr, glr, aq, ak, av, dqkv, dz, da, db = [
        wt_ref[int(off[i]):int(off[i + 1]), :] for i in range(12)]
    main_t = jnp.concatenate([aq, ak, av, gq, gk, gv, gr, dqkv, dz], axis=0)
    main_ref[...] = jnp.transpose(main_t).astype(BF16)
    pad = jnp.zeros((LANES - 48, wt_ref.shape[1]), F32)
    small_t = jnp.concatenate([glr, da, db, pad], axis=0)
    small_ref[...] = jnp.transpose(small_t).astype(BF16)


def _prep_w_in(w_in, tr=256):
    depth, _, in_cols = w_in.shape
    return pl.pallas_call(
        _regroup_kernel,
        grid=(depth, D_MODEL // tr),
        in_specs=[pl.BlockSpec((None, in_cols, tr), lambda l, r: (l, 0, r))],
        out_specs=[
            pl.BlockSpec((None, tr, P_MAIN), lambda l, r: (l, r, 0)),
            pl.BlockSpec((None, tr, LANES), lambda l, r: (l, r, 0)),
        ],
        out_shape=[
            jax.ShapeDtypeStruct((depth, D_MODEL, P_MAIN), BF16),
            jax.ShapeDtypeStruct((depth, D_MODEL, LANES), BF16),
        ],
        compiler_params=_cparams(("arbitrary", "arbitrary")),
        name="w_in_regroup",
    )(jnp.swapaxes(w_in, 1, 2))


def _lane_row(vals, offset):
    return jnp.zeros((1, LANES), F32).at[0, offset:offset + vals.shape[0]].set(vals.astype(F32))


def kernel(x_prompt, x_sample, cache_k, cache_v, state_gla, state_dn, c, c_ctx, norm1_g, norm2_g,
           w_mod, b_mod, w_in, gla_w2, gla_b, gla_norm_g, q_norm_g, k_norm_g, dn_conv, dn_a_log,
           dn_dt_bias, dn_norm_g, w_out, w_ff1, w_ff2):
    n_ctx, n_lat = x_prompt.shape[0], x_sample.shape[0]
    depth = w_in.shape[0]
    n_ctx_rows = n_ctx * SEQ
    assert n_ctx_rows % DEC_SEQ == 0 and n_lat <= 7

    xs = [x_prompt.reshape(n_ctx_rows, D_MODEL), x_sample.reshape(n_lat * DEC_SEQ, D_MODEL)]
    cond8 = jnp.concatenate([c_ctx[None, :], c, jnp.zeros((7 - n_lat, D_MODEL), F32)], axis=0)
    mods = _modulation(cond8, w_mod, b_mod).reshape(depth, 8, N_MOD, D_MODEL)

    cos, sin = _rope_tables()
    gla_consts = _gla_consts()
    ck = cache_k.reshape(n_lat, depth, PAST_LEN, ATT_HKV * HEAD_DIM)
    cv = cache_v.reshape(n_lat, depth, PAST_LEN, ATT_HKV * HEAD_DIM)

    w_main, w_small = _prep_w_in(w_in)
    w_out_b, w_ff1_b, w_ff2_b = w_out.astype(BF16), w_ff1.astype(BF16), w_ff2.astype(BF16)

    nks, nvs, sgs, sds = [], [], [], []
    for l in range(depth):
        tc, tl = n_ctx_rows // INPROJ_TM, n_lat * DEC_SEQ // INPROJ_TM
        inproj = functools.partial(_inproj, mods_l=mods[l], g1=norm1_g[l][None, :], w_main=w_main,
                                   w_small=w_small, layer=l, n_ctx_rows=n_ctx_rows)
        pc, psc = inproj(xs[0], tile0=0, n_tiles=tc, grp_tile0=0)
        if len(xs) == 2:
            pl_, psl = inproj(xs[1], tile0=0, n_tiles=tl, grp_tile0=tc)
        else:
            pl_, psl = inproj(xs[0], tile0=tc, n_tiles=tl, grp_tile0=tc)

        qg, kg = q_norm_g[l][None, :], k_norm_g[l][None, :]
        oa_c, nk, nv = _att_ctx(pc, qg, kg, n_ctx)
        oa_l = _att_lat(pl_, qg, kg, cos, sin, ck, cv, l, n_lat, 0)

        w2p = jnp.zeros((2, LANES, GLA_H * GLA_DK), F32)
        w2p = w2p.at[0, 0:GLA_LR].set(gla_w2[l, 0]).at[1, GLA_LR:2 * GLA_LR].set(gla_w2[l, 1])
        gbias = gla_b[l][:, None, :]
        gng = gla_norm_g[l][None, :]
        og_c, sg = _gla(pc, psc, w2p, gbias, gng, gla_consts, None, l, SEQ, n_ctx, 0)
        og_l, _ = _gla(pl_, psl, w2p, gbias, gng, gla_consts, state_gla, l, DEC_SEQ, n_lat, 0)

        alog_row = _lane_row(dn_a_log[l].reshape(-1), S_DA)
        dtb_row = _lane_row(dn_dt_bias[l].reshape(-1), S_DA)
        dng = dn_norm_g[l][None, :]
        od_c, sd = _dn(pc, psc, dn_conv[l], alog_row, dtb_row, dng, None, l, SEQ, n_ctx, 0)
        od_l, _ = _dn(pl_, psl, dn_conv[l], alog_row, dtb_row, dng, state_dn, l, DEC_SEQ, n_lat, 0)

        x1, h2 = _outproj(xs, [og_c, og_l], [oa_c, oa_l], [od_c, od_l], mods[l],
                          norm2_g[l][None, :], w_out_b, l, n_ctx_rows)
        n_tiles, nct = x1.shape[0] // FFN_TM, n_ctx_rows // FFN_TM
        ffn = functools.partial(_ffn, h2, x1, mods[l], w_ff1_b, w_ff2_b, l, n_ctx_rows, tm=FFN_TM)
        if l < depth - 1:
            xs = [ffn(0, n_tiles)]
        else:
            xs = [ffn(0, nct), ffn(nct, n_tiles - nct)]

        nks.append(nk.reshape(n_ctx, SEQ, ATT_HKV, HEAD_DIM))
        nvs.append(nv.reshape(n_ctx, SEQ, ATT_HKV, HEAD_DIM))
        sgs.append(sg)
        sds.append(sd)

    y_prompt = xs[0].reshape(n_ctx, SEQ, D_MODEL)
    y_sample = xs[1].reshape(n_lat, DEC_SEQ, D_MODEL)
    return (y_prompt, y_sample, jnp.stack(nks, axis=1), jnp.stack(nvs, axis=1),
            jnp.stack(sgs, axis=1), jnp.stack(sds, axis=1))
```

```python
import functools
import math

import numpy as np
import jax
import jax.numpy as jnp
from jax import lax
from jax.experimental import pallas as pl
from jax.experimental.pallas import tpu as pltpu

F32 = jnp.float32
BF16 = jnp.bfloat16

D_MODEL = 2048
SEQ = 256
DEC_SEQ = 1024
PAST_LEN = 256
GRID_W = 64
HEAD_DIM = 128
EPS = 1e-6
GLA_H = 4
GLA_DK = 64
GLA_DV = 128
GLA_LR = 16
GLA_GATE_NORM = 16.0
ATT_HQ = 8
ATT_HKV = 2
ROPE_THETA = 10000.0
DN_H = 4
DN_DK = 128
DN_DV = 128
DN_CONV = 3
CHUNK = 64
FF = 4 * D_MODEL
N_MOD = 6
LANES = 128

P_AQ = 0
P_AKV = 1024
P_GLA = 1536
P_DN = 3072
P_DZ = 4608
P_MAIN = 5120
S_DA = 32
S_DB = 40

VMEM_LIMIT = 56 * 1024 * 1024


def _cparams(sem):
    return pltpu.CompilerParams(dimension_semantics=sem, vmem_limit_bytes=VMEM_LIMIT)


def _bdot(a, b):
    return jnp.dot(a.astype(BF16), b.astype(BF16), preferred_element_type=F32)


def _bdot_nt(a, b):
    return lax.dot_general(a.astype(BF16), b.astype(BF16), (((1,), (1,)), ((), ())),
                           preferred_element_type=F32)


def _bdot_tn(a, b):
    return lax.dot_general(a.astype(BF16), b.astype(BF16), (((0,), (0,)), ((), ())),
                           preferred_element_type=F32)


def _split3(x):
    hi = x.astype(BF16)
    r = x - hi.astype(F32)
    mid = r.astype(BF16)
    lo = (r - mid.astype(F32)).astype(BF16)
    return hi, mid, lo


def _dot_exact_lhs(lhs_bf16, x):
    hi, mid, lo = _split3(x)
    d = functools.partial(jnp.dot, preferred_element_type=F32)
    return d(lhs_bf16, hi) + d(lhs_bf16, mid) + d(lhs_bf16, lo)


def _dot_exact_lhs_wide(lhs_bf16, x):
    n = x.shape[1]
    hi = x.astype(BF16)
    mid = (x - hi.astype(F32)).astype(BF16)
    r = jnp.dot(lhs_bf16, jnp.concatenate([hi, mid], axis=1), preferred_element_type=F32)
    return r[:, 0:n] + r[:, n:2 * n]


def _dot3(a, b):
    ah, am, _ = _split3(a)
    bh, bm, _ = _split3(b)
    d = functools.partial(jnp.dot, preferred_element_type=F32)
    return d(ah, bh) + (d(ah, bm) + d(am, bh))


def _sigmoid(x):
    return 0.5 * jnp.tanh(0.5 * x) + 0.5


def _silu(x):
    return x * _sigmoid(x)


def _softplus(x):
    return jnp.maximum(x, 0.0) + jnp.log1p(jnp.exp(-jnp.abs(x)))


def _rms(x, g):
    return x * lax.rsqrt(jnp.mean(x * x, axis=-1, keepdims=True) + EPS) * g


def _mods_kernel(c_ref, w_ref, b_ref, o_ref):
    o_ref[...] = _bdot(_silu(c_ref[...]), w_ref[...]) + b_ref[...]


def _modulation(cond8, w_mod, b_mod):
    depth = w_mod.shape[0]
    n_out = N_MOD * D_MODEL
    tn = 1024
    return pl.pallas_call(
        _mods_kernel,
        grid=(depth, n_out // tn),
        in_specs=[
            pl.BlockSpec((8, D_MODEL), lambda l, j: (0, 0)),
            pl.BlockSpec((None, D_MODEL, tn), lambda l, j: (l, 0, j)),
            pl.BlockSpec((None, 1, tn), lambda l, j: (l, 0, j)),
        ],
        out_specs=pl.BlockSpec((None, 8, tn), lambda l, j: (l, 0, j)),
        out_shape=jax.ShapeDtypeStruct((depth, 8, n_out), F32),
        compiler_params=_cparams(("arbitrary", "arbitrary")),
        name="modulation",
    )(cond8, w_mod, b_mod.reshape(depth, 1, n_out))


def _group_of_tile(i, tm, n_ctx_rows):
    n_ctx_tiles = n_ctx_rows // tm
    per_lat = DEC_SEQ // tm
    return jnp.where(i < n_ctx_tiles, 0, 1 + (i - n_ctx_tiles) // per_lat)


def _row_specs(arrays, width, tm, nct):
    if len(arrays) == 1:
        return [pl.BlockSpec((tm, width), lambda i, *_: (i, 0))]
    return [pl.BlockSpec((tm, width), lambda i, *_: (jnp.minimum(i, nct - 1), 0)),
            pl.BlockSpec((tm, width), lambda i, *_: (jnp.maximum(i - nct, 0), 0))]


def _row_load(refs, nct, rows=slice(None)):
    if len(refs) == 1:
        return refs[0][rows, :]
    return jnp.where(pl.program_id(0) < nct, refs[0][rows, :], refs[1][rows, :])


INPROJ_TM = 1024


def _inproj_kernel(x_ref, mod_ref, g_ref, w_ref, ws_ref, p_ref, ps_ref, h_sc):
    @pl.when(pl.program_id(1) == 0)
    def _():
        m = mod_ref[...]
        h = _rms(x_ref[...], g_ref[...]) * (1.0 + m[1:2]) + m[0:1]
        hb = h.astype(BF16)
        h_sc[...] = hb
        ps_ref[...] = jnp.dot(hb, ws_ref[...], preferred_element_type=F32)

    h = h_sc[...]
    half = w_ref.shape[1] // 2
    for piece in range(2):
        cols = slice(piece * half, (piece + 1) * half)
        p_ref[:, cols] = jnp.dot(h, w_ref[:, cols], preferred_element_type=F32)


def _inproj(x, mods_l, g1, w_main, w_small, layer, n_ctx_rows, tile0, n_tiles, grp_tile0, tn=1024):
    tm = INPROJ_TM
    grp = functools.partial(_group_of_tile, tm=tm, n_ctx_rows=n_ctx_rows)
    return pl.pallas_call(
        _inproj_kernel,
        grid=(n_tiles, P_MAIN // tn),
        in_specs=[
            pl.BlockSpec((tm, D_MODEL), lambda i, j: (tile0 + i, 0)),
            pl.BlockSpec((None, N_MOD, D_MODEL), lambda i, j: (grp(grp_tile0 + i), 0, 0)),
            pl.BlockSpec((1, D_MODEL), lambda i, j: (0, 0)),
            pl.BlockSpec((None, D_MODEL, tn), lambda i, j: (layer, 0, j)),
            pl.BlockSpec((None, D_MODEL, LANES), lambda i, j: (layer, 0, 0)),
        ],
        out_specs=[
            pl.BlockSpec((tm, tn), lambda i, j: (i, j)),
            pl.BlockSpec((tm, LANES), lambda i, j: (i, 0)),
        ],
        out_shape=[
            jax.ShapeDtypeStruct((n_tiles * tm, P_MAIN), F32),
            jax.ShapeDtypeStruct((n_tiles * tm, LANES), F32),
        ],
        scratch_shapes=[pltpu.VMEM((tm, D_MODEL), BF16)],
        compiler_params=_cparams(("arbitrary", "arbitrary")),
        name="inproj",
    )(x, mods_l, g1, w_main, w_small)


def _outproj_kernel(nct, n_x, *refs):
    x_refs = refs[:n_x]
    (ogc_ref, ogl_ref, oac_ref, oal_ref, odc_ref, odl_ref, mod_ref, g_ref, w_ref,
     x1_ref, h2_ref) = refs[n_x:]
    m = mod_ref[...]
    d = functools.partial(jnp.dot, preferred_element_type=F32)
    half = x1_ref.shape[0] // 2
    for r in range(2):
        rows = slice(r * half, (r + 1) * half)
        mix = (d(_row_load((ogc_ref, ogl_ref), nct, rows), w_ref[0:512, :])
               + d(_row_load((oac_ref, oal_ref), nct, rows), w_ref[512:1536, :])
               + d(_row_load((odc_ref, odl_ref), nct, rows), w_ref[1536:2048, :]))
        x1 = _row_load(x_refs, nct, rows) + m[2:3] * mix
        x1_ref[rows, :] = x1
        h2_ref[rows, :] = (_rms(x1, g_ref[...]) * (1.0 + m[4:5]) + m[3:4]).astype(BF16)


def _outproj(xs, o_gla, o_att, o_dn, mods_l, g2, w_out, layer, n_ctx_rows, tm=512):
    rows = sum(x.shape[0] for x in xs)
    nct = n_ctx_rows // tm
    grp = functools.partial(_group_of_tile, tm=tm, n_ctx_rows=n_ctx_rows)
    return pl.pallas_call(
        functools.partial(_outproj_kernel, nct, len(xs)),
        grid=(rows // tm,),
        in_specs=(_row_specs(xs, D_MODEL, tm, nct) + _row_specs(o_gla, 512, tm, nct)
                  + _row_specs(o_att, 1024, tm, nct) + _row_specs(o_dn, 512, tm, nct) + [
                      pl.BlockSpec((None, N_MOD, D_MODEL), lambda i: (grp(i), 0, 0)),
                      pl.BlockSpec((1, D_MODEL), lambda i: (0, 0)),
                      pl.BlockSpec((None, D_MODEL, D_MODEL), lambda i: (layer, 0, 0),
                                   pipeline_mode=pl.Buffered(1)),
                  ]),
        out_specs=[
            pl.BlockSpec((tm, D_MODEL), lambda i: (i, 0)),
            pl.BlockSpec((tm, D_MODEL), lambda i: (i, 0)),
        ],
        out_shape=[
            jax.ShapeDtypeStruct((rows, D_MODEL), F32),
            jax.ShapeDtypeStruct((rows, D_MODEL), BF16),
        ],
        compiler_params=_cparams(("arbitrary",)),
        name="outproj",
    )(*xs, *o_gla, *o_att, *o_dn, mods_l, g2, w_out)


FFN_TM = 512
FFN_OUT_CHUNK = 512


def _ffn_kernel(h_ref, x1_ref, mod_ref, w1_ref, w2_ref, y_ref):
    j = pl.program_id(1)
    tf = w1_ref.shape[1]

    @pl.when(j == 0)
    def _():
        y_ref[...] = jnp.zeros(y_ref.shape, F32)

    h = h_ref[...]
    for half in range(2):
        cols = slice(half * (tf // 2), (half + 1) * (tf // 2))
        t = jnp.dot(h, w1_ref[:, cols], preferred_element_type=F32)
        t = jnp.square(jnp.maximum(t, 0.0)).astype(BF16)
        for nc in range(D_MODEL // FFN_OUT_CHUNK):
            ncs = slice(nc * FFN_OUT_CHUNK, (nc + 1) * FFN_OUT_CHUNK)
            y_ref[:, ncs] += jnp.dot(t, w2_ref[cols, ncs], preferred_element_type=F32)

    @pl.when(j == pl.num_programs(1) - 1)
    def _():
        y_ref[...] = x1_ref[...] + mod_ref[5:6, :] * y_ref[...]


def _ffn(h2, x1, mods_l, w1, w2, layer, n_ctx_rows, tile0, n_tiles, tm, tf=1024):
    grp = functools.partial(_group_of_tile, tm=tm, n_ctx_rows=n_ctx_rows)
    return pl.pallas_call(
        _ffn_kernel,
        grid=(n_tiles, FF // tf),
        in_specs=[
            pl.BlockSpec((tm, D_MODEL), lambda i, j: (tile0 + i, 0)),
            pl.BlockSpec((tm, D_MODEL), lambda i, j: (tile0 + i, 0)),
            pl.BlockSpec((None, N_MOD, D_MODEL), lambda i, j: (grp(tile0 + i), 0, 0)),
            pl.BlockSpec((None, D_MODEL, tf), lambda i, j: (layer, 0, j)),
            pl.BlockSpec((None, tf, D_MODEL), lambda i, j: (layer, j, 0)),
        ],
        out_specs=pl.BlockSpec((tm, D_MODEL), lambda i, j: (i, 0)),
        out_shape=jax.ShapeDtypeStruct((n_tiles * tm, D_MODEL), F32),
        compiler_params=_cparams(("arbitrary", "arbitrary")),
        name="ffn",
    )(h2, x1, mods_l, w1, w2)


def _rope_tables():
    half = HEAD_DIM // 2
    pos = np.arange(DEC_SEQ)
    row = (pos // GRID_W).astype(np.float32)
    col = (pos % GRID_W).astype(np.float32)
    inv = (ROPE_THETA ** (-np.arange(0, half, 2, dtype=np.float32) / half)).astype(np.float32)
    ar = row[:, None] * inv[None, :]
    ac = col[:, None] * inv[None, :]
    cos = np.concatenate([np.cos(ar), np.cos(ar), np.cos(ac), np.cos(ac)], axis=1)
    sin = np.concatenate([-np.sin(ar), np.sin(ar), -np.sin(ac), np.sin(ac)], axis=1)
    return jnp.asarray(cos, F32), jnp.asarray(sin, F32)


def _rope(x, cos, sin):
    lane = lax.broadcasted_iota(jnp.int32, x.shape, 1)
    first = (lane % 64) < 32
    partner = jnp.where(first, pltpu.roll(x, 96, 1), pltpu.roll(x, 32, 1))
    return x * cos + partner * sin


ATT_G = ATT_HQ // ATT_HKV


def _group_attention(q_heads, kb, vb, store):
    s = [lax.dot_general(q.astype(BF16), kb, (((1,), (1,)), ((), ())), preferred_element_type=F32)
         for q in q_heads]
    p = [jnp.exp(si - jnp.max(si, axis=-1, keepdims=True)) for si in s]
    l = [jnp.sum(pi, axis=-1, keepdims=True) for pi in p]
    for g in range(len(q_heads)):
        o = jnp.dot(p[g].astype(BF16), vb, preferred_element_type=F32) / l[g]
        store(g, o.astype(BF16))


def _att_ctx_kernel(aq_ref, akv_ref, qg_ref, kg_ref, o_ref, nk_ref, nv_ref):
    qg = qg_ref[...] * (HEAD_DIM ** -0.5)
    kg = kg_ref[...]
    for hk in range(ATT_HKV):
        ks = slice(hk * HEAD_DIM, (hk + 1) * HEAD_DIM)
        kn = _rms(akv_ref[:, ks], kg)
        v = akv_ref[:, 256 + hk * HEAD_DIM:256 + (hk + 1) * HEAD_DIM]
        nk_ref[:, ks] = kn
        nv_ref[:, ks] = v

        def head_cols(g, hk=hk):
            return slice((hk * ATT_G + g) * HEAD_DIM, (hk * ATT_G + g + 1) * HEAD_DIM)

        def store(g, val):
            o_ref[:, head_cols(g)] = val

        qs = [_rms(aq_ref[:, head_cols(g)], qg) for g in range(ATT_G)]
        _group_attention(qs, kn.astype(BF16), v.astype(BF16), store)


def _att_lat_kernel(aq_ref, akv_ref, qg_ref, kg_ref, cos_ref, sin_ref, ck_ref, cv_ref, o_ref,
                    k_sc, v_sc):
    qg = qg_ref[...] * (HEAD_DIM ** -0.5)
    kg = kg_ref[...]
    qb = 256
    for hk in range(ATT_HKV):
        ks = slice(hk * HEAD_DIM, (hk + 1) * HEAD_DIM)
        k_sc[hk, 0:PAST_LEN, :] = ck_ref[:, ks].astype(BF16)
        v_sc[hk, 0:PAST_LEN, :] = cv_ref[:, ks].astype(BF16)
        kn = _rope(_rms(akv_ref[:, ks], kg), cos_ref[...], sin_ref[...])
        k_sc[hk, PAST_LEN:, :] = kn.astype(BF16)
        v_sc[hk, PAST_LEN:, :] = akv_ref[:, 256 + hk * HEAD_DIM:256 + (hk + 1) * HEAD_DIM].astype(BF16)

    def body(r, carry):
        r0 = pl.multiple_of(r * qb, qb)
        cos = cos_ref[pl.ds(r0, qb), :]
        sin = sin_ref[pl.ds(r0, qb), :]
        for hk in range(ATT_HKV):
            def head_cols(g, hk=hk):
                return slice((hk * ATT_G + g) * HEAD_DIM, (hk * ATT_G + g + 1) * HEAD_DIM)

            def store(g, val):
                o_ref[pl.ds(r0, qb), head_cols(g)] = val

            qs = [_rope(_rms(aq_ref[pl.ds(r0, qb), head_cols(g)], qg), cos, sin)
                  for g in range(ATT_G)]
            _group_attention(qs, k_sc[hk], v_sc[hk], store)
        return carry

    lax.fori_loop(0, DEC_SEQ // qb, body, 0)


def _att_ctx(p_main, qg, kg, n_seq):
    L = SEQ
    return pl.pallas_call(
        _att_ctx_kernel,
        grid=(n_seq,),
        in_specs=[
            pl.BlockSpec((L, 1024), lambda b: (b, P_AQ // 1024)),
            pl.BlockSpec((L, 512), lambda b: (b, P_AKV // 512)),
            pl.BlockSpec((1, HEAD_DIM), lambda b: (0, 0)),
            pl.BlockSpec((1, HEAD_DIM), lambda b: (0, 0)),
        ],
        out_specs=[
            pl.BlockSpec((L, 1024), lambda b: (b, 0)),
            pl.BlockSpec((None, L, 256), lambda b: (b, 0, 0)),
            pl.BlockSpec((None, L, 256), lambda b: (b, 0, 0)),
        ],
        out_shape=[
            jax.ShapeDtypeStruct((n_seq * L, 1024), BF16),
            jax.ShapeDtypeStruct((n_seq, L, 256), F32),
            jax.ShapeDtypeStruct((n_seq, L, 256), F32),
        ],
        compiler_params=_cparams(("arbitrary",)),
        name="att_ctx",
    )(p_main, p_main, qg, kg)


def _att_lat(p_main, qg, kg, cos, sin, ck, cv, layer, n_seq, row_blk0):
    L = DEC_SEQ
    return pl.pallas_call(
        _att_lat_kernel,
        grid=(n_seq,),
        in_specs=[
            pl.BlockSpec((L, 1024), lambda b: (row_blk0 + b, P_AQ // 1024)),
            pl.BlockSpec((L, 512), lambda b: (row_blk0 + b, P_AKV // 512)),
            pl.BlockSpec((1, HEAD_DIM), lambda b: (0, 0)),
            pl.BlockSpec((1, HEAD_DIM), lambda b: (0, 0)),
            pl.BlockSpec((L, HEAD_DIM), lambda b: (0, 0)),
            pl.BlockSpec((L, HEAD_DIM), lambda b: (0, 0)),
            pl.BlockSpec((None, None, PAST_LEN, 256), lambda b: (b, layer, 0, 0)),
            pl.BlockSpec((None, None, PAST_LEN, 256), lambda b: (b, layer, 0, 0)),
        ],
        out_specs=pl.BlockSpec((L, 1024), lambda b: (b, 0)),
        out_shape=jax.ShapeDtypeStruct((n_seq * L, 1024), BF16),
        scratch_shapes=[
            pltpu.VMEM((ATT_HKV, PAST_LEN + L, HEAD_DIM), BF16),
            pltpu.VMEM((ATT_HKV, PAST_LEN + L, HEAD_DIM), BF16),
        ],
        compiler_params=_cparams(("arbitrary",)),
        name="att_lat",
    )(p_main, p_main, qg, kg, cos, sin, ck, cv)


def _seqs_per_step(L, n_seq):
    return 2 if (L <= SEQ and n_seq % 2 == 0) else 1


_GLA_LEVELS = (32, 16, 8, 4, 2, 1)


def _gla_consts():
    C = CHUNK
    i = np.arange(C)[:, None]
    t = np.arange(C)[None, :]
    mats = [t <= i, t > i]
    masks = [i == t]
    for s in _GLA_LEVELS:
        m = (i // s) * s
        if s > 1:
            mats.append((t > m) & (t <= i))
        mats.append((t > i) & (t <= m + s))
        masks.append((i // (2 * s) == t // (2 * s)) & ((i // s) % 2 == 1) & ((t // s) % 2 == 0))
    fwd = np.concatenate(mats, 0)
    bwd = np.concatenate([mm[::-1, ::-1] for mm in mats], 0)
    mstack = np.stack([fwd, bwd]).astype(np.float32)
    mf = np.stack([np.concatenate([mm, mm], 0) for mm in masks])
    mb = np.stack([np.concatenate([mm[::-1, ::-1], mm[::-1, ::-1]], 0) for mm in masks])
    lmask = np.stack([mf, mb]).astype(np.float32)
    return jnp.asarray(mstack, BF16), jnp.asarray(lmask, F32)


def _gla_kernel(L, nb, has_state, *refs):
    if has_state:
        (pg_ref, ps_ref, w2_ref, gb_ref, ng_ref, ms_ref, lm_ref, s0_ref,
         o_ref, so_ref, la_sc, of_sc, ob_sc, st_sc) = refs
    else:
        (pg_ref, ps_ref, w2_ref, gb_ref, ng_ref, ms_ref, lm_ref,
         o_ref, so_ref, la_sc, of_sc, ob_sc, st_sc) = refs
    n = L // CHUNK
    C = CHUNK

    ps = ps_ref[...]
    for z in range(2):
        x = _dot3(ps, w2_ref[z]) + gb_ref[z]
        la_sc[z] = (jnp.minimum(x, 0.0) - jnp.log1p(jnp.exp(-jnp.abs(x)))) * (1.0 / GLA_GATE_NORM)

    if has_state:
        for sq in range(nb):
            for z in range(2):
                for p in range(2):
                    pair = jnp.concatenate([s0_ref[sq, z, 2 * p], s0_ref[sq, z, 2 * p + 1]], axis=0)
                    st_sc[sq, z, p] = jnp.transpose(pair)
    else:
        st_sc[...] = jnp.zeros(st_sc.shape, F32)

    lane = lax.broadcasted_iota(jnp.int32, (C, LANES), 1)
    first = lane < 64

    def stack_heads(t):
        return jnp.concatenate([jnp.where(first, t, 0.0), jnp.where(first, 0.0, t)], axis=0)

    chains = [(t, sq, z, p) for t in range(2) for sq in range(nb) for z in range(2) for p in range(2)]
    nch = len(chains)
    nch1 = nch // 2

    def body(it, carry):
        def chunk_rows(t, sq, z):
            c = 2 * it + t
            cc = c if z == 0 else n - 1 - c
            return pl.ds(pl.multiple_of(sq * L + cc * C, C), C)

        rows = [chunk_rows(t, sq, z) for t, sq, z, p in chains]
        last = [C - 1, 0]
        q = [pg_ref[rows[i], p * LANES:(p + 1) * LANES] * (GLA_DK ** -0.5)
             for i, (t, sq, z, p) in enumerate(chains)]
        k = [pg_ref[rows[i], 256 + p * LANES:256 + (p + 1) * LANES]
             for i, (t, sq, z, p) in enumerate(chains)]
        e = [jnp.exp(_dot_exact_lhs_wide(ms_ref[z], la_sc[z, rows[i], p * LANES:(p + 1) * LANES]))
             for i, (t, sq, z, p) in enumerate(chains)]
        a2 = [_bdot_nt(stack_heads(q[i]), k[i]) * lm_ref[z, 0]
              for i, (t, sq, z, p) in enumerate(chains)]
        blk = 2
        for li, s in enumerate(_GLA_LEVELS):
            if s > 1:
                qs = [q[i] * e[i][blk * C:(blk + 1) * C] for i in range(nch)]
                blk += 1
            else:
                qs = q
            ks = [k[i] * e[i][blk * C:(blk + 1) * C] for i in range(nch)]
            blk += 1
            a2 = [a2[i] + _bdot_nt(stack_heads(qs[i]), ks[i]) * lm_ref[z, li + 1]
                  for i, (t, sq, z, p) in enumerate(chains)]
        kl = [k[i] * e[i][C:2 * C] for i in range(nch)]
        v = [[pg_ref[rows[i], 512 + (2 * p + hh) * LANES:512 + (2 * p + hh + 1) * LANES]
              for hh in range(2)] for i, (t, sq, z, p) in enumerate(chains)]
        intra = [[_bdot(a2[i][hh * C:(hh + 1) * C], v[i][hh]) for hh in range(2)]
                 for i in range(nch)]
        zs = [[_bdot_tn(v[i][hh], kl[i]) for hh in range(2)] for i in range(nch)]
        qe = [stack_heads(q[i] * e[i][0:C]) for i in range(nch)]
        st = [st_sc[sq, z, p] for t, sq, z, p in chains[0:nch1]]
        for i, (t, sq, z, p) in enumerate(chains):
            j = i % nch1
            inter = _bdot_nt(qe[i], st[j])
            osc = of_sc if z == 0 else ob_sc
            for hh in range(2):
                h = 2 * p + hh
                osc[rows[i], h * LANES:(h + 1) * LANES] = intra[i][hh] + inter[hh * C:(hh + 1) * C]
            st[j] = (st[j] * e[i][last[z]:last[z] + 1, :]
                     + jnp.where(first[0:1], zs[i][0], zs[i][1]))
        for j, (t, sq, z, p) in enumerate(chains[0:nch1]):
            st_sc[sq, z, p] = st[j]
        return carry

    lax.fori_loop(0, n // 2, body, 0)

    ng = ng_ref[...]

    def fin(c, carry):
        r0 = pl.multiple_of(c * C, C)
        rows = pl.ds(r0, C)
        for h in range(GLA_H):
            cols = slice(h * LANES, (h + 1) * LANES)
            o = of_sc[rows, cols] + ob_sc[rows, cols]
            gate = _silu(pg_ref[rows, 1024 + h * LANES:1024 + (h + 1) * LANES])
            o_ref[rows, cols] = (_rms(o, ng) * gate).astype(BF16)
        return carry

    lax.fori_loop(0, nb * n, fin, 0)
    for sq in range(nb):
        for z in range(2):
            for p in range(2):
                pair = jnp.transpose(st_sc[sq, z, p])
                so_ref[sq, z, 2 * p] = pair[0:GLA_DK]
                so_ref[sq, z, 2 * p + 1] = pair[GLA_DK:2 * GLA_DK]


def _gla(p_main, p_small, w2p, gbias, ng, consts, s0, layer, L, n_seq, row_blk0):
    mstack, lmask = consts
    has_state = s0 is not None
    nb = _seqs_per_step(L, n_seq)
    assert row_blk0 % nb == 0
    blk0 = row_blk0 // nb
    state_blk = (nb, 2, GLA_H, GLA_DK, GLA_DV)
    in_specs = [
        pl.BlockSpec((nb * L, 1536), lambda b: (blk0 + b, P_GLA // 1536)),
        pl.BlockSpec((nb * L, LANES), lambda b: (blk0 + b, 0)),
        pl.BlockSpec((2, LANES, 256), lambda b: (0, 0, 0)),
        pl.BlockSpec((2, 1, 256), lambda b: (0, 0, 0)),
        pl.BlockSpec((1, GLA_DV), lambda b: (0, 0)),
        pl.BlockSpec(mstack.shape, lambda b: (0, 0, 0)),
        pl.BlockSpec(lmask.shape, lambda b: (0, 0, 0, 0)),
    ]
    args = [p_main, p_small, w2p, gbias, ng, mstack, lmask]
    if has_state:
        in_specs.append(pl.BlockSpec((nb, None) + state_blk[1:], lambda b: (b, layer, 0, 0, 0, 0)))
        args.append(s0)
    return pl.pallas_call(
        functools.partial(_gla_kernel, L, nb, has_state),
        grid=(n_seq // nb,),
        in_specs=in_specs,
        out_specs=[
            pl.BlockSpec((nb * L, 512), lambda b: (b, 0)),
            pl.BlockSpec(state_blk, lambda b: (b, 0, 0, 0, 0)),
        ],
        out_shape=[
            jax.ShapeDtypeStruct((n_seq * L, 512), BF16),
            jax.ShapeDtypeStruct((n_seq, 2, GLA_H, GLA_DK, GLA_DV), F32),
        ],
        scratch_shapes=[
            pltpu.VMEM((2, nb * L, 256), F32),
            pltpu.VMEM((nb * L, 512), F32),
            pltpu.VMEM((nb * L, 512), F32),
            pltpu.VMEM((nb, 2, 2, GLA_DV, LANES), F32),
        ],
        compiler_params=_cparams(("arbitrary",)),
        name="gla_lat" if has_state else "gla_ctx",
    )(*args)


def _dn_kernel(L, nb, has_state, *refs):
    if has_state:
        (pd_ref, dz_ref, ps_ref, cw_ref, al_ref, dtb_ref, ng_ref, s0_ref,
         o_ref, so_ref, qkv_sc, g_sc, bb_sc, of_sc, ob_sc, s_sc, prep_sc, att_sc, dec_sc) = refs
    else:
        (pd_ref, dz_ref, ps_ref, cw_ref, al_ref, dtb_ref, ng_ref,
         o_ref, so_ref, qkv_sc, g_sc, bb_sc, of_sc, ob_sc, s_sc, prep_sc, att_sc, dec_sc) = refs
    n = L // CHUNK
    C = CHUNK

    R = nb * L
    pos = lax.broadcasted_iota(jnp.int32, (R, LANES), 0) & (L - 1)
    not_first = pos > 0
    not_last = pos < L - 1

    def conv_silu(j):
        c0 = pl.multiple_of(j * LANES, LANES)
        x = pd_ref[:, pl.ds(c0, LANES)]
        w = cw_ref[:, pl.ds(c0, LANES)]
        prev = jnp.where(not_first, pltpu.roll(x, 1, 0), 0.0)
        nxt = jnp.where(not_last, pltpu.roll(x, R - 1, 0), 0.0)
        return c0, _silu(prev * w[0:1] + x * w[1:2] + nxt * w[2:3])

    def qk_tile(j, carry):
        c0, y = conv_silu(j)
        inv = lax.rsqrt(jnp.sum(y * y, axis=-1, keepdims=True) + EPS)
        qkv_sc[:, pl.ds(c0, LANES)] = y * (inv * jnp.where(j < DN_H, DN_DK ** -0.5, 1.0))
        return carry

    def v_tile(j, carry):
        c0, y = conv_silu(j)
        qkv_sc[:, pl.ds(c0, LANES)] = y
        return carry

    lax.fori_loop(0, 2 * DN_H, qk_tile, 0)
    lax.fori_loop(2 * DN_H, 3 * DN_H, v_tile, 0)

    ps = ps_ref[...]
    lane_l = lax.broadcasted_iota(jnp.int32, (R, LANES), 1)
    g_sc[...] = -jnp.exp(al_ref[...]) * _softplus(ps + dtb_ref[...])
    b_all = _sigmoid(ps)
    for idx in range(2 * DN_H):
        bcol = jnp.sum(jnp.where(lane_l == S_DB + idx, b_all, 0.0), axis=-1, keepdims=True)
        bb_sc[idx] = jnp.broadcast_to(bcol, (R, LANES))

    if has_state:
        s_sc[...] = s0_ref[...]
    else:
        s_sc[...] = jnp.zeros(s_sc.shape, F32)

    ri = lax.broadcasted_iota(jnp.int32, (C, LANES), 0)
    lane_c = lax.broadcasted_iota(jnp.int32, (C, LANES), 1)
    ci = lane_c & (C - 1)
    first = lane_c < C
    incl = [ci <= ri, ci >= ri]
    strict = [ci < ri, ci > ri]
    rs = lax.broadcasted_iota(jnp.int32, (C, C), 0)
    cs = lax.broadcasted_iota(jnp.int32, (C, C), 1)
    m_b = [(cs <= rs).astype(BF16), (cs >= rs).astype(BF16)]
    nu = 2 * nb
    chains = [(sq, z, h) for sq in range(nu) for z in range(2) for h in range(DN_H)]
    nch = len(chains)
    pairs = [(sq, z, p) for sq in range(nu) for z in range(2) for p in range(DN_H // 2)]
    npr = len(pairs)

    def stack_heads(t):
        return jnp.concatenate([jnp.where(first, t, 0.0), jnp.where(first, 0.0, t)], axis=0)

    def hi_mid(x):
        hi = x.astype(BF16).astype(F32)
        return hi, x - hi

    def pair_products(lhs_list, x):
        xh, xm = hi_mid(x)
        rhs = jnp.concatenate([stack_heads(xh), stack_heads(xm)], axis=1).astype(BF16)
        parts = []
        for l in lhs_list:
            parts.extend(hi_mid(l))
        r = jnp.dot(jnp.concatenate(parts, axis=0).astype(BF16), rhs, preferred_element_type=F32)
        out = []
        for i in range(len(lhs_list)):
            blk = r[2 * C * i:2 * C * i + C] + r[2 * C * i + C:2 * C * (i + 1)]
            out.append(blk[:, 0:LANES] + blk[:, LANES:2 * LANES])
        return out

    heads = [((2 * sq + z) * DN_H + 2 * p, (2 * sq + z) * DN_H + 2 * p + 1)
             for sq, z, p in pairs]

    def chunk_rows(cs):
        def one(sq, z):
            c = cs[sq // nb]
            start = (sq % nb) * L + (c if z == 0 else n - 1 - c) * C
            return pl.ds(start if isinstance(start, int) else pl.multiple_of(start, C), C)
        return [one(sq, z) for sq in range(nu) for z in range(2)]

    def prepare_products(cs):
        rows = chunk_rows(cs)
        last = [C - 1, 0]

        def ld(base, sq, z, h):
            return qkv_sc[rows[2 * sq + z], base + h * LANES:base + (h + 1) * LANES]

        q = [ld(0, *ch) for ch in chains]
        k = [ld(512, *ch) for ch in chains]
        v = [ld(1024, *ch) for ch in chains]
        bb = [bb_sc[z * DN_H + h, rows[2 * sq + z], :] for sq, z, h in chains]
        gc = [_dot_exact_lhs(m_b[z], g_sc[rows[2 * sq + z], :])
              for sq in range(nu) for z in range(2)]
        gct =[jnp.transpose(jnp.concatenate([g, g], axis=0)) for g in gc]
        gi = [jnp.broadcast_to(jnp.sum(jnp.where(lane_c == S_DA + z * DN_H + h, gc[2 * sq + z], 0.0),
                                       axis=-1, keepdims=True), (C, LANES)) for sq, z, h in chains]
        gjrow = [gct[2 * sq + z][S_DA + z * DN_H + h:S_DA + z * DN_H + h + 1, :]
                 for sq, z, h in chains]
        gam = []
        for (sq, z, p), (i0, i1) in zip(pairs, heads):
            diff = jnp.where(first, gi[i0], gi[i1]) - jnp.where(first[0:1], gjrow[i0], gjrow[i1])
            gam.append(jnp.where(incl[z], jnp.exp(jnp.where(incl[z], diff, 0.0)), 0.0))
        kb = [k[i] * bb[i] for i in range(nch)]
        kq = [_bdot_nt(jnp.concatenate([kb[i0], q[i0], kb[i1], q[i1]], axis=0),
                       jnp.concatenate([k[i0], k[i1]], axis=0)) for i0, i1 in heads]
        gt = [gi[i][last[z]:last[z] + 1, :] for i, (sq, z, h) in enumerate(chains)]
        return q, k, v, bb, kb, gi, gt, gam, kq

    def prepare_finish(parts):
        q, k, v, bb, kb, gi, gt, gam, kq = parts
        a = [jnp.where(strict[z], jnp.where(first, kq[j][0:C], kq[j][2 * C:3 * C]) * gam[j], 0.0)
             for j, (sq, z, p) in enumerate(pairs)]
        att = [jnp.where(first, kq[j][C:2 * C], kq[j][3 * C:4 * C]) * gam[j] for j in range(npr)]
        nt = [-a[j] for j in range(npr)]
        pw = [pair_products([a[j]], a[j])[0] for j in range(npr)]
        for step in range(5):
            if step < 4:
                both = [pair_products([nt[j], pw[j]], pw[j]) for j in range(npr)]
                nt = [nt[j] + pw[j] + both[j][0] for j in range(npr)]
                pw = [both[j][1] for j in range(npr)]
            else:
                nt = [nt[j] + pw[j] + pair_products([nt[j]], pw[j])[0] for j in range(npr)]
        eg = [jnp.exp(gi[i]) for i in range(nch)]
        rhs = [jnp.concatenate([v[i] * bb[i], kb[i] * eg[i]], axis=1) for i in range(nch)]
        corr = [_bdot(stack_heads(nt[j]), jnp.concatenate([rhs[i0], rhs[i1]], axis=0))
                for j, (i0, i1) in enumerate(heads)]
        for j, (i0, i1) in enumerate(heads):
            for half, i in enumerate((i0, i1)):
                uw = rhs[i] + corr[j][half * C:(half + 1) * C]
                prep_sc[i, 0] = uw[:, 0:LANES]
                prep_sc[i, 1] = uw[:, LANES:]
                prep_sc[i, 2] = q[i] * eg[i]
                prep_sc[i, 3] = k[i] * jnp.exp(gt[i] - gi[i])
                dec_sc[i] = jnp.broadcast_to(jnp.exp(gt[i]), (8, LANES))
            att_sc[j] = att[j]

    nch1, npr1 = nch // 2, npr // 2

    def apply_state_products(slot, s):
        return [_bdot(jnp.concatenate([prep_sc[slot * nch1 + i, 1], prep_sc[slot * nch1 + i, 2]],
                                      axis=0), s[i]) for i in range(nch1)]

    def apply_finish(rows, slot, s, ws):
        base = slot * nch1
        v_new = [prep_sc[base + i, 0] - ws[i][0:C] for i in range(nch1)]
        av = []
        for j in range(npr1):
            i0, i1 = heads[slot * npr1 + j]
            av.append(_bdot(stack_heads(att_sc[slot * npr1 + j]),
                            jnp.concatenate([v_new[i0 - base], v_new[i1 - base]], axis=0)))
        upd = [_bdot_tn(prep_sc[base + i, 3], v_new[i]) for i in range(nch1)]
        for j in range(npr1):
            for half, ig in enumerate(heads[slot * npr1 + j]):
                sq, z, h = chains[ig]
                osc = of_sc if z == 0 else ob_sc
                osc[rows[2 * sq + z], h * LANES:(h + 1) * LANES] = (
                    ws[ig - base][C:2 * C] + av[j][half * C:(half + 1) * C])
        return [s[i] * dec_sc[base + i, 0:1, :] + upd[i] for i in range(nch1)]

    def apply_two(cs, parts_fn):
        rows = chunk_rows(cs)
        s = [s_sc[sq, z, h] for sq, z, h in chains[0:nch1]]
        ws = apply_state_products(0, s)
        parts = parts_fn()
        s = apply_finish(rows, 0, s, ws)
        ws = apply_state_products(1, s)
        s = apply_finish(rows, 1, s, ws)
        for i, (sq, z, h) in enumerate(chains[0:nch1]):
            s_sc[sq, z, h] = s[i]
        return parts

    prepare_finish(prepare_products((0, 1)))

    def body(it, carry):
        c = 2 * it
        parts = apply_two((c, c + 1), lambda: prepare_products((c + 2, c + 3)))
        prepare_finish(parts)
        return carry

    lax.fori_loop(0, n // 2 - 1, body, 0)
    apply_two((n - 2, n - 1), lambda: None)

    ng = ng_ref[...]

    def fin(c, carry):
        r0 = pl.multiple_of(c * C, C)
        rows = pl.ds(r0, C)
        for h in range(DN_H):
            cols = slice(h * LANES, (h + 1) * LANES)
            o = of_sc[rows, cols] + ob_sc[rows, cols]
            o_ref[rows, cols] = (_rms(o, ng) * _silu(dz_ref[rows, cols])).astype(BF16)
        return carry

    lax.fori_loop(0, nb * n, fin, 0)
    so_ref[...] = s_sc[...]


def _dn(p_main, p_small, conv_w, alog_row, dtb_row, ng, s0, layer, L, n_seq, row_blk0):
    has_state = s0 is not None
    nb = _seqs_per_step(L, n_seq)
    assert row_blk0 % nb == 0
    row_blk0, L_blk = row_blk0 // nb, nb * L
    in_specs = [
        pl.BlockSpec((L_blk, 1536), lambda b: (row_blk0 + b, P_DN // 1536)),
        pl.BlockSpec((L_blk, 512), lambda b: (row_blk0 + b, P_DZ // 512)),
        pl.BlockSpec((L_blk, LANES), lambda b: (row_blk0 + b, 0)),
        pl.BlockSpec((DN_CONV, 3 * DN_H * DN_DK), lambda b: (0, 0)),
        pl.BlockSpec((1, LANES), lambda b: (0, 0)),
        pl.BlockSpec((1, LANES), lambda b: (0, 0)),
        pl.BlockSpec((1, DN_DV), lambda b: (0, 0)),
    ]
    args = [p_main, p_main, p_small, conv_w, alog_row, dtb_row, ng]
    if has_state:
        in_specs.append(pl.BlockSpec((nb, None, 2, DN_H, DN_DK, DN_DV),
                                     lambda b: (b, layer, 0, 0, 0, 0)))
        args.append(s0)
    return pl.pallas_call(
        functools.partial(_dn_kernel, L, nb, has_state),
        grid=(n_seq // nb,),
        in_specs=in_specs,
        out_specs=[
            pl.BlockSpec((L_blk, 512), lambda b: (b, 0)),
            pl.BlockSpec((nb, 2, DN_H, DN_DK, DN_DV), lambda b: (b, 0, 0, 0, 0)),
        ],
        out_shape=[
            jax.ShapeDtypeStruct((n_seq * L, 512), BF16),
            jax.ShapeDtypeStruct((n_seq, 2, DN_H, DN_DK, DN_DV), F32),
        ],
        scratch_shapes=[
            pltpu.VMEM((L_blk, 1536), F32),
            pltpu.VMEM((L_blk, LANES), F32),
            pltpu.VMEM((2 * DN_H, L_blk, LANES), F32),
            pltpu.VMEM((L_blk, 512), F32),
            pltpu.VMEM((L_blk, 512), F32),
            pltpu.VMEM((nb, 2, DN_H, DN_DK, DN_DV), F32),
            pltpu.VMEM((2 * nb * 2 * DN_H, 4, CHUNK, LANES), F32),
            pltpu.VMEM((2 * nb * DN_H, CHUNK, LANES), F32),
            pltpu.VMEM((2 * nb * 2 * DN_H, 8, LANES), F32),
        ],
        compiler_params=_cparams(("arbitrary",)),
        name="dn_lat" if has_state else "dn_ctx",
    )(*args)


_W_IN_SPLITS = (256, 256, 512, 512, 32, 1024, 256, 256, 1536, 512, 8, 8)


def _regroup_kernel(wt_ref, main_ref, small_ref):
    off = np.cumsum((0,) + _W_IN_SPLITS)
    gq, gk, gv, gr, glr, aq, ak, av, dqkv, dz, da, db = [
        wt_ref[int(off[i]):int(off[i + 1]), :] for i in range(12)]
    main_t = jnp.concatenate([aq, ak, av, gq, gk, gv, gr, dqkv, dz], axis=0)
    main_ref[...] = jnp.transpose(main_t).astype(BF16)
    pad = jnp.zeros((LANES - 48, wt_ref.shape[1]), F32)
    small_t = jnp.concatenate([glr, da, db, pad], axis=0)
    small_ref[...] = jnp.transpose(small_t).astype(BF16)


def _prep_w_in(w_in, tr=256):
    depth, _, in_cols = w_in.shape
    return pl.pallas_call(
        _regroup_kernel,
        grid=(depth, D_MODEL // tr),
        in_specs=[pl.BlockSpec((None, in_cols, tr), lambda l, r: (l, 0, r))],
        out_specs=[
            pl.BlockSpec((None, tr, P_MAIN), lambda l, r: (l, r, 0)),
            pl.BlockSpec((None, tr, LANES), lambda l, r: (l, r, 0)),
        ],
        out_shape=[
            jax.ShapeDtypeStruct((depth, D_MODEL, P_MAIN), BF16),
            jax.ShapeDtypeStruct((depth, D_MODEL, LANES), BF16),
        ],
        compiler_params=_cparams(("arbitrary", "arbitrary")),
        name="w_in_regroup",
    )(jnp.swapaxes(w_in, 1, 2))


def _lane_row(vals, offset):
    return jnp.zeros((1, LANES), F32).at[0, offset:offset + vals.shape[0]].set(vals.astype(F32))


def kernel(x_prompt, x_sample, cache_k, cache_v, state_gla, state_dn, c, c_ctx, norm1_g, norm2_g,
           w_mod, b_mod, w_in, gla_w2, gla_b, gla_norm_g, q_norm_g, k_norm_g, dn_conv, dn_a_log,
           dn_dt_bias, dn_norm_g, w_out, w_ff1, w_ff2):
    n_ctx, n_lat = x_prompt.shape[0], x_sample.shape[0]
    depth = w_in.shape[0]
    n_ctx_rows = n_ctx * SEQ
    assert n_ctx_rows % DEC_SEQ == 0 and n_lat <= 7

    xs = [x_prompt.reshape(n_ctx_rows, D_MODEL), x_sample.reshape(n_lat * DEC_SEQ, D_MODEL)]
    cond8 = jnp.concatenate([c_ctx[None, :], c, jnp.zeros((7 - n_lat, D_MODEL), F32)], axis=0)
    mods = _modulation(cond8, w_mod, b_mod).reshape(depth, 8, N_MOD, D_MODEL)

    cos, sin = _rope_tables()
    gla_consts = _gla_consts()
    ck = cache_k.reshape(n_lat, depth, PAST_LEN, ATT_HKV * HEAD_DIM)
    cv = cache_v.reshape(n_lat, depth, PAST_LEN, ATT_HKV * HEAD_DIM)

    w_main, w_small = _prep_w_in(w_in)
    w_out_b, w_ff1_b, w_ff2_b = w_out.astype(BF16), w_ff1.astype(BF16), w_ff2.astype(BF16)

    nks, nvs, sgs, sds = [], [], [], []
    for l in range(depth):
        tc, tl = n_ctx_rows // INPROJ_TM, n_lat * DEC_SEQ // INPROJ_TM
        inproj = functools.partial(_inproj, mods_l=mods[l], g1=norm1_g[l][None, :], w_main=w_main,
                                   w_small=w_small, layer=l, n_ctx_rows=n_ctx_rows)
        pc, psc = inproj(xs[0], tile0=0, n_tiles=tc, grp_tile0=0)
        if len(xs) == 2:
            pl_, psl = inproj(xs[1], tile0=0, n_tiles=tl, grp_tile0=tc)
        else:
            pl_, psl = inproj(xs[0], tile0=tc, n_tiles=tl, grp_tile0=tc)

        qg, kg = q_norm_g[l][None, :], k_norm_g[l][None, :]
        oa_c, nk, nv = _att_ctx(pc, qg, kg, n_ctx)
        oa_l = _att_lat(pl_, qg, kg, cos, sin, ck, cv, l, n_lat, 0)

        w2p = jnp.zeros((2, LANES, GLA_H * GLA_DK), F32)
        w2p = w2p.at[0, 0:GLA_LR].set(gla_w2[l, 0]).at[1, GLA_LR:2 * GLA_LR].set(gla_w2[l, 1])
        gbias = gla_b[l][:, None, :]
        gng = gla_norm_g[l][None, :]
        og_c, sg = _gla(pc, psc, w2p, gbias, gng, gla_consts, None, l, SEQ, n_ctx, 0)
        og_l, _ = _gla(pl_, psl, w2p, gbias, gng, gla_consts, state_gla, l, DEC_SEQ, n_lat, 0)

        alog_row = _lane_row(dn_a_log[l].reshape(-1), S_DA)
        dtb_row = _lane_row(dn_dt_bias[l].reshape(-1), S_DA)
        dng = dn_norm_g[l][None, :]
        od_c, sd = _dn(pc, psc, dn_conv[l], alog_row, dtb_row, dng, None, l, SEQ, n_ctx, 0)
        od_l, _ = _dn(pl_, psl, dn_conv[l], alog_row, dtb_row, dng, state_dn, l, DEC_SEQ, n_lat, 0)

        x1, h2 = _outproj(xs, [og_c, og_l], [oa_c, oa_l], [od_c, od_l], mods[l],
                          norm2_g[l][None, :], w_out_b, l, n_ctx_rows)
        n_tiles, nct = x1.shape[0] // FFN_TM, n_ctx_rows // FFN_TM
        ffn = functools.partial(_ffn, h2, x1, mods[l], w_ff1_b, w_ff2_b, l, n_ctx_rows, tm=FFN_TM)
        if l < depth - 1:
            xs = [ffn(0, n_tiles)]
        else:
            xs = [ffn(0, nct), ffn(nct, n_tiles - nct)]

        nks.append(nk.reshape(n_ctx, SEQ, ATT_HKV, HEAD_DIM))
        nvs.append(nv.reshape(n_ctx, SEQ, ATT_HKV, HEAD_DIM))
        sgs.append(sg)
        sds.append(sd)

    y_prompt = xs[0].reshape(n_ctx, SEQ, D_MODEL)
    y_sample = xs[1].reshape(n_lat, DEC_SEQ, D_MODEL)
    return (y_prompt, y_sample, jnp.stack(nks, axis=1), jnp.stack(nvs, axis=1),
            jnp.stack(sgs, axis=1), jnp.stack(sds, axis=1))
```

```python
import functools
import math

import numpy as np
import jax
import jax.numpy as jnp
from jax import lax
from jax.experimental import pallas as pl
from jax.experimental.pallas import tpu as pltpu

F32 = jnp.float32
BF16 = jnp.bfloat16

D_MODEL = 2048
SEQ = 256
DEC_SEQ = 1024
PAST_LEN = 256
GRID_W = 64
HEAD_DIM = 128
EPS = 1e-6
GLA_H = 4
GLA_DK = 64
GLA_DV = 128
GLA_LR = 16
GLA_GATE_NORM = 16.0
ATT_HQ = 8
ATT_HKV = 2
ROPE_THETA = 10000.0
DN_H = 4
DN_DK = 128
DN_DV = 128
DN_CONV = 3
CHUNK = 64
FF = 4 * D_MODEL
N_MOD = 6
LANES = 128

P_AQ = 0
P_AKV = 1024
P_GLA = 1536
P_DN = 3072
P_DZ = 4608
P_MAIN = 5120
S_DA = 32
S_DB = 40

VMEM_LIMIT = 56 * 1024 * 1024


def _cparams(sem):
    return pltpu.CompilerParams(dimension_semantics=sem, vmem_limit_bytes=VMEM_LIMIT)


def _bdot(a, b):
    return jnp.dot(a.astype(BF16), b.astype(BF16), preferred_element_type=F32)


def _bdot_nt(a, b):
    return lax.dot_general(a.astype(BF16), b.astype(BF16), (((1,), (1,)), ((), ())),
                           preferred_element_type=F32)


def _bdot_tn(a, b):
    return lax.dot_general(a.astype(BF16), b.astype(BF16), (((0,), (0,)), ((), ())),
                           preferred_element_type=F32)


def _split3(x):
    hi = x.astype(BF16)
    r = x - hi.astype(F32)
    mid = r.astype(BF16)
    lo = (r - mid.astype(F32)).astype(BF16)
    return hi, mid, lo


def _dot_exact_lhs(lhs_bf16, x):
    hi, mid, lo = _split3(x)
    d = functools.partial(jnp.dot, preferred_element_type=F32)
    return d(lhs_bf16, hi) + d(lhs_bf16, mid) + d(lhs_bf16, lo)


def _dot_exact_lhs_wide(lhs_bf16, x):
    n = x.shape[1]
    hi = x.astype(BF16)
    mid = (x - hi.astype(F32)).astype(BF16)
    r = jnp.dot(lhs_bf16, jnp.concatenate([hi, mid], axis=1), preferred_element_type=F32)
    return r[:, 0:n] + r[:, n:2 * n]


def _dot3(a, b):
    ah, am, _ = _split3(a)
    bh, bm, _ = _split3(b)
    d = functools.partial(jnp.dot, preferred_element_type=F32)
    return d(ah, bh) + (d(ah, bm) + d(am, bh))


def _sigmoid(x):
    return 0.5 * jnp.tanh(0.5 * x) + 0.5


def _silu(x):
    return x * _sigmoid(x)


def _softplus(x):
    return jnp.maximum(x, 0.0) + jnp.log1p(jnp.exp(-jnp.abs(x)))


def _rms(x, g):
    return x * lax.rsqrt(jnp.mean(x * x, axis=-1, keepdims=True) + EPS) * g


def _mods_kernel(c_ref, w_ref, b_ref, o_ref):
    o_ref[...] = _bdot(_silu(c_ref[...]), w_ref[...]) + b_ref[...]


def _modulation(cond8, w_mod, b_mod3, depth):
    n_out = N_MOD * D_MODEL
    tn = 1024
    return pl.pallas_call(
        _mods_kernel,
        grid=(depth, n_out // tn),
        in_specs=[
            pl.BlockSpec((8, D_MODEL), lambda l, j: (0, 0)),
            pl.BlockSpec((None, D_MODEL, tn), lambda l, j: (l, 0, j)),
            pl.BlockSpec((None, 1, tn), lambda l, j: (l, 0, j)),
        ],
        out_specs=pl.BlockSpec((None, 8, tn), lambda l, j: (l, 0, j)),
        out_shape=jax.ShapeDtypeStruct((depth, 8, n_out), F32),
        compiler_params=_cparams(("arbitrary", "arbitrary")),
        name="modulation",
    )(cond8, w_mod, b_mod3)


def _group_of_tile(i, tm, n_ctx_rows):
    n_ctx_tiles = n_ctx_rows // tm
    per_lat = DEC_SEQ // tm
    return jnp.where(i < n_ctx_tiles, 0, 1 + (i - n_ctx_tiles) // per_lat)


def _row_specs(arrays, width, tm, nct):
    if len(arrays) == 1:
        return [pl.BlockSpec((tm, width), lambda i, *_: (i, 0))]
    return [pl.BlockSpec((tm, width), lambda i, *_: (jnp.minimum(i, nct - 1), 0)),
            pl.BlockSpec((tm, width), lambda i, *_: (jnp.maximum(i - nct, 0), 0))]


def _row_load(refs, nct, rows=slice(None)):
    if len(refs) == 1:
        return refs[0][rows, :]
    return jnp.where(pl.program_id(0) < nct, refs[0][rows, :], refs[1][rows, :])


INPROJ_TM = 1024


def _inproj_kernel(x_ref, mod_ref, g_ref, w_ref, ws_ref, p_ref, ps_ref, h_sc):
    @pl.when(pl.program_id(1) == 0)
    def _():
        m = mod_ref[...]
        h = _rms(x_ref[...], g_ref[...]) * (1.0 + m[1:2]) + m[0:1]
        hb = h.astype(BF16)
        h_sc[...] = hb
        ps_ref[...] = jnp.dot(hb, ws_ref[...], preferred_element_type=F32)

    h = h_sc[...]
    half = w_ref.shape[1] // 2
    for piece in range(2):
        cols = slice(piece * half, (piece + 1) * half)
        p_ref[:, cols] = jnp.dot(h, w_ref[:, cols], preferred_element_type=F32)


def _inproj(x, mods_l, g1, w_main, w_small, layer, n_ctx_rows, tile0, n_tiles, grp_tile0, tn=1024):
    tm = INPROJ_TM
    grp = functools.partial(_group_of_tile, tm=tm, n_ctx_rows=n_ctx_rows)
    return pl.pallas_call(
        _inproj_kernel,
        grid=(n_tiles, P_MAIN // tn),
        in_specs=[
            pl.BlockSpec((tm, D_MODEL), lambda i, j: (tile0 + i, 0)),
            pl.BlockSpec((None, N_MOD, D_MODEL), lambda i, j: (grp(grp_tile0 + i), 0, 0)),
            pl.BlockSpec((1, D_MODEL), lambda i, j: (0, 0)),
            pl.BlockSpec((None, D_MODEL, tn), lambda i, j: (layer, 0, j)),
            pl.BlockSpec((None, D_MODEL, LANES), lambda i, j: (layer, 0, 0)),
        ],
        out_specs=[
            pl.BlockSpec((tm, tn), lambda i, j: (i, j)),
            pl.BlockSpec((tm, LANES), lambda i, j: (i, 0)),
        ],
        out_shape=[
            jax.ShapeDtypeStruct((n_tiles * tm, P_MAIN), F32),
            jax.ShapeDtypeStruct((n_tiles * tm, LANES), F32),
        ],
        scratch_shapes=[pltpu.VMEM((tm, D_MODEL), BF16)],
        compiler_params=_cparams(("arbitrary", "arbitrary")),
        name="inproj",
    )(x, mods_l, g1, w_main, w_small)


def _outproj_kernel(nct, n_x, *refs):
    x_refs = refs[:n_x]
    (ogc_ref, ogl_ref, oac_ref, oal_ref, odc_ref, odl_ref, mod_ref, g_ref, w_ref,
     x1_ref, h2_ref) = refs[n_x:]
    m = mod_ref[...]
    d = functools.partial(jnp.dot, preferred_element_type=F32)
    half = x1_ref.shape[0] // 2
    for r in range(2):
        rows = slice(r * half, (r + 1) * half)
        mix = (d(_row_load((ogc_ref, ogl_ref), nct, rows), w_ref[0:512, :])
               + d(_row_load((oac_ref, oal_ref), nct, rows), w_ref[512:1536, :])
               + d(_row_load((odc_ref, odl_ref), nct, rows), w_ref[1536:2048, :]))
        x1 = _row_load(x_refs, nct, rows) + m[2:3] * mix
        x1_ref[rows, :] = x1
        h2_ref[rows, :] = (_rms(x1, g_ref[...]) * (1.0 + m[4:5]) + m[3:4]).astype(BF16)


def _outproj(xs, o_gla, o_att, o_dn, mods_l, g2, w_out, layer, n_ctx_rows, tm=512):
    rows = sum(x.shape[0] for x in xs)
    nct = n_ctx_rows // tm
    grp = functools.partial(_group_of_tile, tm=tm, n_ctx_rows=n_ctx_rows)
    return pl.pallas_call(
        functools.partial(_outproj_kernel, nct, len(xs)),
        grid=(rows // tm,),
        in_specs=(_row_specs(xs, D_MODEL, tm, nct) + _row_specs(o_gla, 512, tm, nct)
                  + _row_specs(o_att, 1024, tm, nct) + _row_specs(o_dn, 512, tm, nct) + [
                      pl.BlockSpec((None, N_MOD, D_MODEL), lambda i: (grp(i), 0, 0)),
                      pl.BlockSpec((1, D_MODEL), lambda i: (0, 0)),
                      pl.BlockSpec((None, D_MODEL, D_MODEL), lambda i: (layer, 0, 0),
                                   pipeline_mode=pl.Buffered(1)),
                  ]),
        out_specs=[
            pl.BlockSpec((tm, D_MODEL), lambda i: (i, 0)),
            pl.BlockSpec((tm, D_MODEL), lambda i: (i, 0)),
        ],
        out_shape=[
            jax.ShapeDtypeStruct((rows, D_MODEL), F32),
            jax.ShapeDtypeStruct((rows, D_MODEL), BF16),
        ],
        compiler_params=_cparams(("arbitrary",)),
        name="outproj",
    )(*xs, *o_gla, *o_att, *o_dn, mods_l, g2, w_out)


FFN_TM = 512
FFN_OUT_CHUNK = 512


def _ffn_kernel(n_cast, has_mod, h_ref, x1_ref, mod_ref, w1_ref, w2_ref, *refs):
    n_in = n_cast + (3 if has_mod else 0)
    y_ref = refs[n_in]
    for src, dst in zip(refs[:n_cast], refs[n_in + 1:n_in + 1 + n_cast]):
        dst[...] = src[...].astype(BF16)
    if has_mod:
        c_ref, wm_ref, bm_ref = refs[n_cast:n_in]
        refs[n_in + 1 + n_cast][...] = _bdot(_silu(c_ref[...]), wm_ref[...]) + bm_ref[...]
    j = pl.program_id(1)
    tf = w1_ref.shape[1]

    @pl.when(j == 0)
    def _():
        y_ref[...] = jnp.zeros(y_ref.shape, F32)

    h = h_ref[...]
    for half in range(2):
        cols = slice(half * (tf // 2), (half + 1) * (tf // 2))
        t = jnp.dot(h, w1_ref[:, cols], preferred_element_type=F32)
        t = jnp.square(jnp.maximum(t, 0.0)).astype(BF16)
        for nc in range(D_MODEL // FFN_OUT_CHUNK):
            ncs = slice(nc * FFN_OUT_CHUNK, (nc + 1) * FFN_OUT_CHUNK)
            y_ref[:, ncs] += jnp.dot(t, w2_ref[cols, ncs], preferred_element_type=F32)

    @pl.when(j == pl.num_programs(1) - 1)
    def _():
        y_ref[...] = x1_ref[...] + mod_ref[5:6, :] * y_ref[...]


def _ffn(h2, x1, mods_l, w1, w2, n_ctx_rows, tile0, n_tiles, tm, cast_next=(), mod_next=(),
         next_layer=0, tf=1024):
    grp = functools.partial(_group_of_tile, tm=tm, n_ctx_rows=n_ctx_rows)
    nj = FF // tf
    n_steps = n_tiles * nj
    cast_in, cast_out, cast_shape = [], [], []
    for wn in cast_next:
        rows, cols = wn.shape[1:]
        rb = rows // n_steps
        assert rb * n_steps == rows and rb % 16 == 0
        cast_in.append(pl.BlockSpec((None, rb, cols), lambda i, j: (next_layer, i * nj + j, 0)))
        cast_out.append(pl.BlockSpec((None, rb, cols), lambda i, j: (0, i * nj + j, 0)))
        cast_shape.append(jax.ShapeDtypeStruct((1, rows, cols), BF16))
    if mod_next:
        n_out = N_MOD * D_MODEL
        n_slabs = min(n_steps, n_out // LANES)
        cb = n_out // n_slabs
        assert cb * n_slabs == n_out and cb % LANES == 0

        def slab(i, j):
            return jnp.minimum(i * nj + j, n_slabs - 1)

        cast_in += [pl.BlockSpec((8, D_MODEL), lambda i, j: (0, 0)),
                    pl.BlockSpec((None, D_MODEL, cb), lambda i, j: (next_layer, 0, slab(i, j))),
                    pl.BlockSpec((None, 1, cb), lambda i, j: (next_layer, 0, slab(i, j)))]
        cast_out.append(pl.BlockSpec((8, cb), lambda i, j: (0, slab(i, j))))
        cast_shape.append(jax.ShapeDtypeStruct((8, n_out), F32))
    return pl.pallas_call(
        functools.partial(_ffn_kernel, len(cast_next), bool(mod_next)),
        grid=(n_tiles, nj),
        in_specs=[
            pl.BlockSpec((tm, D_MODEL), lambda i, j: (tile0 + i, 0)),
            pl.BlockSpec((tm, D_MODEL), lambda i, j: (tile0 + i, 0)),
            pl.BlockSpec((None, N_MOD, D_MODEL), lambda i, j: (grp(tile0 + i), 0, 0)),
            pl.BlockSpec((None, D_MODEL, tf), lambda i, j: (0, 0, j)),
            pl.BlockSpec((None, tf, D_MODEL), lambda i, j: (0, j, 0)),
        ] + cast_in,
        out_specs=[pl.BlockSpec((tm, D_MODEL), lambda i, j: (i, 0))] + cast_out,
        out_shape=[jax.ShapeDtypeStruct((n_tiles * tm, D_MODEL), F32)] + cast_shape,
        compiler_params=_cparams(("arbitrary", "arbitrary")),
        name="ffn",
    )(h2, x1, mods_l, w1, w2, *cast_next, *mod_next)


def _rope_tables():
    half = HEAD_DIM // 2
    pos = np.arange(DEC_SEQ)
    row = (pos // GRID_W).astype(np.float32)
    col = (pos % GRID_W).astype(np.float32)
    inv = (ROPE_THETA ** (-np.arange(0, half, 2, dtype=np.float32) / half)).astype(np.float32)
    ar = row[:, None] * inv[None, :]
    ac = col[:, None] * inv[None, :]
    cos = np.concatenate([np.cos(ar), np.cos(ar), np.cos(ac), np.cos(ac)], axis=1)
    sin = np.concatenate([-np.sin(ar), np.sin(ar), -np.sin(ac), np.sin(ac)], axis=1)
    return jnp.asarray(cos, F32), jnp.asarray(sin, F32)


def _rope(x, cos, sin):
    lane = lax.broadcasted_iota(jnp.int32, x.shape, 1)
    first = (lane % 64) < 32
    partner = jnp.where(first, pltpu.roll(x, 96, 1), pltpu.roll(x, 32, 1))
    return x * cos + partner * sin


ATT_G = ATT_HQ // ATT_HKV


def _group_attention(q_heads, kb, vb, store):
    s = [lax.dot_general(q.astype(BF16), kb, (((1,), (1,)), ((), ())), preferred_element_type=F32)
         for q in q_heads]
    p = [jnp.exp(si - jnp.max(si, axis=-1, keepdims=True)) for si in s]
    l = [jnp.sum(pi, axis=-1, keepdims=True) for pi in p]
    for g in range(len(q_heads)):
        o = jnp.dot(p[g].astype(BF16), vb, preferred_element_type=F32) / l[g]
        store(g, o.astype(BF16))


def _att_ctx_kernel(aq_ref, akv_ref, qg_ref, kg_ref, o_ref, nk_ref, nv_ref):
    qg = qg_ref[...] * (HEAD_DIM ** -0.5)
    kg = kg_ref[...]
    for hk in range(ATT_HKV):
        ks = slice(hk * HEAD_DIM, (hk + 1) * HEAD_DIM)
        kn = _rms(akv_ref[:, ks], kg)
        v = akv_ref[:, 256 + hk * HEAD_DIM:256 + (hk + 1) * HEAD_DIM]
        nk_ref[:, ks] = kn
        nv_ref[:, ks] = v

        def head_cols(g, hk=hk):
            return slice((hk * ATT_G + g) * HEAD_DIM, (hk * ATT_G + g + 1) * HEAD_DIM)

        def store(g, val):
            o_ref[:, head_cols(g)] = val

        qs = [_rms(aq_ref[:, head_cols(g)], qg) for g in range(ATT_G)]
        _group_attention(qs, kn.astype(BF16), v.astype(BF16), store)


def _att_lat_kernel(aq_ref, akv_ref, qg_ref, kg_ref, cos_ref, sin_ref, ck_ref, cv_ref, o_ref,
                    k_sc, v_sc):
    qg = qg_ref[...] * (HEAD_DIM ** -0.5)
    kg = kg_ref[...]
    qb = 256
    for hk in range(ATT_HKV):
        ks = slice(hk * HEAD_DIM, (hk + 1) * HEAD_DIM)
        k_sc[hk, 0:PAST_LEN, :] = ck_ref[:, ks].astype(BF16)
        v_sc[hk, 0:PAST_LEN, :] = cv_ref[:, ks].astype(BF16)
        kn = _rope(_rms(akv_ref[:, ks], kg), cos_ref[...], sin_ref[...])
        k_sc[hk, PAST_LEN:, :] = kn.astype(BF16)
        v_sc[hk, PAST_LEN:, :] = akv_ref[:, 256 + hk * HEAD_DIM:256 + (hk + 1) * HEAD_DIM].astype(BF16)

    def body(r, carry):
        r0 = pl.multiple_of(r * qb, qb)
        cos = cos_ref[pl.ds(r0, qb), :]
        sin = sin_ref[pl.ds(r0, qb), :]
        for hk in range(ATT_HKV):
            def head_cols(g, hk=hk):
                return slice((hk * ATT_G + g) * HEAD_DIM, (hk * ATT_G + g + 1) * HEAD_DIM)

            def store(g, val):
                o_ref[pl.ds(r0, qb), head_cols(g)] = val

            qs = [_rope(_rms(aq_ref[pl.ds(r0, qb), head_cols(g)], qg), cos, sin)
                  for g in range(ATT_G)]
            _group_attention(qs, k_sc[hk], v_sc[hk], store)
        return carry

    lax.fori_loop(0, DEC_SEQ // qb, body, 0)


def _att_ctx(p_main, qg, kg, n_seq):
    L = SEQ
    return pl.pallas_call(
        _att_ctx_kernel,
        grid=(n_seq,),
        in_specs=[
            pl.BlockSpec((L, 1024), lambda b: (b, P_AQ // 1024)),
            pl.BlockSpec((L, 512), lambda b: (b, P_AKV // 512)),
            pl.BlockSpec((1, HEAD_DIM), lambda b: (0, 0)),
            pl.BlockSpec((1, HEAD_DIM), lambda b: (0, 0)),
        ],
        out_specs=[
            pl.BlockSpec((L, 1024), lambda b: (b, 0)),
            pl.BlockSpec((None, L, 256), lambda b: (b, 0, 0)),
            pl.BlockSpec((None, L, 256), lambda b: (b, 0, 0)),
        ],
        out_shape=[
            jax.ShapeDtypeStruct((n_seq * L, 1024), BF16),
            jax.ShapeDtypeStruct((n_seq, L, 256), F32),
            jax.ShapeDtypeStruct((n_seq, L, 256), F32),
        ],
        compiler_params=_cparams(("arbitrary",)),
        name="att_ctx",
    )(p_main, p_main, qg, kg)


def _att_lat(p_main, qg, kg, cos, sin, ck, cv, layer, n_seq, row_blk0):
    L = DEC_SEQ
    return pl.pallas_call(
        _att_lat_kernel,
        grid=(n_seq,),
        in_specs=[
            pl.BlockSpec((L, 1024), lambda b: (row_blk0 + b, P_AQ // 1024)),
            pl.BlockSpec((L, 512), lambda b: (row_blk0 + b, P_AKV // 512)),
            pl.BlockSpec((1, HEAD_DIM), lambda b: (0, 0)),
            pl.BlockSpec((1, HEAD_DIM), lambda b: (0, 0)),
            pl.BlockSpec((L, HEAD_DIM), lambda b: (0, 0)),
            pl.BlockSpec((L, HEAD_DIM), lambda b: (0, 0)),
            pl.BlockSpec((None, None, PAST_LEN, 256), lambda b: (b, layer, 0, 0)),
            pl.BlockSpec((None, None, PAST_LEN, 256), lambda b: (b, layer, 0, 0)),
        ],
        out_specs=pl.BlockSpec((L, 1024), lambda b: (b, 0)),
        out_shape=jax.ShapeDtypeStruct((n_seq * L, 1024), BF16),
        scratch_shapes=[
            pltpu.VMEM((ATT_HKV, PAST_LEN + L, HEAD_DIM), BF16),
            pltpu.VMEM((ATT_HKV, PAST_LEN + L, HEAD_DIM), BF16),
        ],
        compiler_params=_cparams(("arbitrary",)),
        name="att_lat",
    )(p_main, p_main, qg, kg, cos, sin, ck, cv)


def _seqs_per_step(L, n_seq):
    return 2 if (L <= SEQ and n_seq % 2 == 0) else 1


_GLA_LEVELS = (32, 16, 8, 4, 2, 1)


def _gla_consts():
    C = CHUNK
    i = np.arange(C)[:, None]
    t = np.arange(C)[None, :]
    mats = [t <= i, t > i]
    masks = [i == t]
    for s in _GLA_LEVELS:
        m = (i // s) * s
        if s > 1:
            mats.append((t > m) & (t <= i))
        mats.append((t > i) & (t <= m + s))
        masks.append((i // (2 * s) == t // (2 * s)) & ((i // s) % 2 == 1) & ((t // s) % 2 == 0))
    fwd = np.concatenate(mats, 0)
    bwd = np.concatenate([mm[::-1, ::-1] for mm in mats], 0)
    mstack = np.stack([fwd, bwd]).astype(np.float32)
    mf = np.stack([np.concatenate([mm, mm], 0) for mm in masks])
    mb = np.stack([np.concatenate([mm[::-1, ::-1], mm[::-1, ::-1]], 0) for mm in masks])
    lmask = np.stack([mf, mb]).astype(np.float32)
    return jnp.asarray(mstack, BF16), jnp.asarray(lmask, F32)


def _gla_kernel(L, nb, has_state, *refs):
    if has_state:
        (pg_ref, ps_ref, w2_ref, gb_ref, ng_ref, ms_ref, lm_ref, s0_ref,
         o_ref, so_ref, la_sc, of_sc, ob_sc, st_sc) = refs
    else:
        (pg_ref, ps_ref, w2_ref, gb_ref, ng_ref, ms_ref, lm_ref,
         o_ref, so_ref, la_sc, of_sc, ob_sc, st_sc) = refs
    n = L // CHUNK
    C = CHUNK

    ps = ps_ref[...]
    for z in range(2):
        x = _dot3(ps, w2_ref[z]) + gb_ref[z]
        la_sc[z] = (jnp.minimum(x, 0.0) - jnp.log1p(jnp.exp(-jnp.abs(x)))) * (1.0 / GLA_GATE_NORM)

    if has_state:
        for sq in range(nb):
            for z in range(2):
                for p in range(2):
                    pair = jnp.concatenate([s0_ref[sq, z, 2 * p], s0_ref[sq, z, 2 * p + 1]], axis=0)
                    st_sc[sq, z, p] = jnp.transpose(pair)
    else:
        st_sc[...] = jnp.zeros(st_sc.shape, F32)

    lane = lax.broadcasted_iota(jnp.int32, (C, LANES), 1)
    first = lane < 64

    def stack_heads(t):
        return jnp.concatenate([jnp.where(first, t, 0.0), jnp.where(first, 0.0, t)], axis=0)

    chains = [(t, sq, z, p) for t in range(2) for sq in range(nb) for z in range(2) for p in range(2)]
    nch = len(chains)
    nch1 = nch // 2

    def body(it, carry):
        def chunk_rows(t, sq, z):
            c = 2 * it + t
            cc = c if z == 0 else n - 1 - c
            return pl.ds(pl.multiple_of(sq * L + cc * C, C), C)

        rows = [chunk_rows(t, sq, z) for t, sq, z, p in chains]
        last = [C - 1, 0]
        q = [pg_ref[rows[i], p * LANES:(p + 1) * LANES] * (GLA_DK ** -0.5)
             for i, (t, sq, z, p) in enumerate(chains)]
        k = [pg_ref[rows[i], 256 + p * LANES:256 + (p + 1) * LANES]
             for i, (t, sq, z, p) in enumerate(chains)]
        e = [jnp.exp(_dot_exact_lhs_wide(ms_ref[z], la_sc[z, rows[i], p * LANES:(p + 1) * LANES]))
             for i, (t, sq, z, p) in enumerate(chains)]
        a2 = [_bdot_nt(stack_heads(q[i]), k[i]) * lm_ref[z, 0]
              for i, (t, sq, z, p) in enumerate(chains)]
        blk = 2
        for li, s in enumerate(_GLA_LEVELS):
            if s > 1:
                qs = [q[i] * e[i][blk * C:(blk + 1) * C] for i in range(nch)]
                blk += 1
            else:
                qs = q
            ks = [k[i] * e[i][blk * C:(blk + 1) * C] for i in range(nch)]
            blk += 1
            a2 = [a2[i] + _bdot_nt(stack_heads(qs[i]), ks[i]) * lm_ref[z, li + 1]
                  for i, (t, sq, z, p) in enumerate(chains)]
        kl = [k[i] * e[i][C:2 * C] for i in range(nch)]
        v = [[pg_ref[rows[i], 512 + (2 * p + hh) * LANES:512 + (2 * p + hh + 1) * LANES]
              for hh in range(2)] for i, (t, sq, z, p) in enumerate(chains)]
        intra = [[_bdot(a2[i][hh * C:(hh + 1) * C], v[i][hh]) for hh in range(2)]
                 for i in range(nch)]
        zs = [[_bdot_tn(v[i][hh], kl[i]) for hh in range(2)] for i in range(nch)]
        qe = [stack_heads(q[i] * e[i][0:C]) for i in range(nch)]
        st = [st_sc[sq, z, p] for t, sq, z, p in chains[0:nch1]]
        for i, (t, sq, z, p) in enumerate(chains):
            j = i % nch1
            inter = _bdot_nt(qe[i], st[j])
            osc = of_sc if z == 0 else ob_sc
            for hh in range(2):
                h = 2 * p + hh
                osc[rows[i], h * LANES:(h + 1) * LANES] = intra[i][hh] + inter[hh * C:(hh + 1) * C]
            st[j] = (st[j] * e[i][last[z]:last[z] + 1, :]
                     + jnp.where(first[0:1], zs[i][0], zs[i][1]))
        for j, (t, sq, z, p) in enumerate(chains[0:nch1]):
            st_sc[sq, z, p] = st[j]
        return carry

    lax.fori_loop(0, n // 2, body, 0)

    ng = ng_ref[...]

    def fin(c, carry):
        r0 = pl.multiple_of(c * C, C)
        rows = pl.ds(r0, C)
        for h in range(GLA_H):
            cols = slice(h * LANES, (h + 1) * LANES)
            o = of_sc[rows, cols] + ob_sc[rows, cols]
            gate = _silu(pg_ref[rows, 1024 + h * LANES:1024 + (h + 1) * LANES])
            o_ref[rows, cols] = (_rms(o, ng) * gate).astype(BF16)
        return carry

    lax.fori_loop(0, nb * n, fin, 0)
    for sq in range(nb):
        for z in range(2):
            for p in range(2):
                pair = jnp.transpose(st_sc[sq, z, p])
                so_ref[sq, z, 2 * p] = pair[0:GLA_DK]
                so_ref[sq, z, 2 * p + 1] = pair[GLA_DK:2 * GLA_DK]


def _gla(p_main, p_small, w2p, gbias, ng, consts, s0, layer, L, n_seq, row_blk0):
    mstack, lmask = consts
    has_state = s0 is not None
    nb = _seqs_per_step(L, n_seq)
    assert row_blk0 % nb == 0
    blk0 = row_blk0 // nb
    state_blk = (nb, 2, GLA_H, GLA_DK, GLA_DV)
    in_specs = [
        pl.BlockSpec((nb * L, 1536), lambda b: (blk0 + b, P_GLA // 1536)),
        pl.BlockSpec((nb * L, LANES), lambda b: (blk0 + b, 0)),
        pl.BlockSpec((2, LANES, 256), lambda b: (0, 0, 0)),
        pl.BlockSpec((2, 1, 256), lambda b: (0, 0, 0)),
        pl.BlockSpec((1, GLA_DV), lambda b: (0, 0)),
        pl.BlockSpec(mstack.shape, lambda b: (0, 0, 0)),
        pl.BlockSpec(lmask.shape, lambda b: (0, 0, 0, 0)),
    ]
    args = [p_main, p_small, w2p, gbias, ng, mstack, lmask]
    if has_state:
        in_specs.append(pl.BlockSpec((nb, None) + state_blk[1:], lambda b: (b, layer, 0, 0, 0, 0)))
        args.append(s0)
    return pl.pallas_call(
        functools.partial(_gla_kernel, L, nb, has_state),
        grid=(n_seq // nb,),
        in_specs=in_specs,
        out_specs=[
            pl.BlockSpec((nb * L, 512), lambda b: (b, 0)),
            pl.BlockSpec(state_blk, lambda b: (b, 0, 0, 0, 0)),
        ],
        out_shape=[
            jax.ShapeDtypeStruct((n_seq * L, 512), BF16),
            jax.ShapeDtypeStruct((n_seq, 2, GLA_H, GLA_DK, GLA_DV), F32),
        ],
        scratch_shapes=[
            pltpu.VMEM((2, nb * L, 256), F32),
            pltpu.VMEM((nb * L, 512), F32),
            pltpu.VMEM((nb * L, 512), F32),
            pltpu.VMEM((nb, 2, 2, GLA_DV, LANES), F32),
        ],
        compiler_params=_cparams(("arbitrary",)),
        name="gla_lat" if has_state else "gla_ctx",
    )(*args)


def _dn_kernel(L, nb, has_state, *refs):
    if has_state:
        (pd_ref, dz_ref, ps_ref, cw_ref, al_ref, dtb_ref, ng_ref, s0_ref,
         o_ref, so_ref, qkv_sc, g_sc, bb_sc, of_sc, ob_sc, s_sc, prep_sc, att_sc, dec_sc) = refs
    else:
        (pd_ref, dz_ref, ps_ref, cw_ref, al_ref, dtb_ref, ng_ref,
         o_ref, so_ref, qkv_sc, g_sc, bb_sc, of_sc, ob_sc, s_sc, prep_sc, att_sc, dec_sc) = refs
    n = L // CHUNK
    C = CHUNK

    R = nb * L
    pos = lax.broadcasted_iota(jnp.int32, (R, LANES), 0) & (L - 1)
    not_first = pos > 0
    not_last = pos < L - 1

    def conv_silu(j):
        c0 = pl.multiple_of(j * LANES, LANES)
        x = pd_ref[:, pl.ds(c0, LANES)]
        w = cw_ref[:, pl.ds(c0, LANES)]
        prev = jnp.where(not_first, pltpu.roll(x, 1, 0), 0.0)
        nxt = jnp.where(not_last, pltpu.roll(x, R - 1, 0), 0.0)
        return c0, _silu(prev * w[0:1] + x * w[1:2] + nxt * w[2:3])

    def qk_tile(j, carry):
        c0, y = conv_silu(j)
        inv = lax.rsqrt(jnp.sum(y * y, axis=-1, keepdims=True) + EPS)
        qkv_sc[:, pl.ds(c0, LANES)] = y * (inv * jnp.where(j < DN_H, DN_DK ** -0.5, 1.0))
        return carry

    def v_tile(j, carry):
        c0, y = conv_silu(j)
        qkv_sc[:, pl.ds(c0, LANES)] = y
        return carry

    lax.fori_loop(0, 2 * DN_H, qk_tile, 0)
    lax.fori_loop(2 * DN_H, 3 * DN_H, v_tile, 0)

    ps = ps_ref[...]
    lane_l = lax.broadcasted_iota(jnp.int32, (R, LANES), 1)
    g_sc[...] = -jnp.exp(al_ref[...]) * _softplus(ps + dtb_ref[...])
    b_all = _sigmoid(ps)
    for idx in range(2 * DN_H):
        bcol = jnp.sum(jnp.where(lane_l == S_DB + idx, b_all, 0.0), axis=-1, keepdims=True)
        bb_sc[idx] = jnp.broadcast_to(bcol, (R, LANES))

    if has_state:
        s_sc[...] = s0_ref[...]
    else:
        s_sc[...] = jnp.zeros(s_sc.shape, F32)

    ri = lax.broadcasted_iota(jnp.int32, (C, LANES), 0)
    lane_c = lax.broadcasted_iota(jnp.int32, (C, LANES), 1)
    ci = lane_c & (C - 1)
    first = lane_c < C
    incl = [ci <= ri, ci >= ri]
    strict = [ci < ri, ci > ri]
    rs = lax.broadcasted_iota(jnp.int32, (C, C), 0)
    cs = lax.broadcasted_iota(jnp.int32, (C, C), 1)
    m_b = [(cs <= rs).astype(BF16), (cs >= rs).astype(BF16)]
    nu = 2 * nb
    chains = [(sq, z, h) for sq in range(nu) for z in range(2) for h in range(DN_H)]
    nch = len(chains)
    pairs = [(sq, z, p) for sq in range(nu) for z in range(2) for p in range(DN_H // 2)]
    npr = len(pairs)

    def stack_heads(t):
        return jnp.concatenate([jnp.where(first, t, 0.0), jnp.where(first, 0.0, t)], axis=0)

    def hi_mid(x):
        hi = x.astype(BF16).astype(F32)
        return hi, x - hi

    def pair_products(lhs_list, x):
        xh, xm = hi_mid(x)
        rhs = jnp.concatenate([stack_heads(xh), stack_heads(xm)], axis=1).astype(BF16)
        parts = []
        for l in lhs_list:
            parts.extend(hi_mid(l))
        r = jnp.dot(jnp.concatenate(parts, axis=0).astype(BF16), rhs, preferred_element_type=F32)
        out = []
        for i in range(len(lhs_list)):
            blk = r[2 * C * i:2 * C * i + C] + r[2 * C * i + C:2 * C * (i + 1)]
            out.append(blk[:, 0:LANES] + blk[:, LANES:2 * LANES])
        return out

    heads = [((2 * sq + z) * DN_H + 2 * p, (2 * sq + z) * DN_H + 2 * p + 1)
             for sq, z, p in pairs]

    def chunk_rows(cs):
        def one(sq, z):
            c = cs[sq // nb]
            start = (sq % nb) * L + (c if z == 0 else n - 1 - c) * C
            return pl.ds(start if isinstance(start, int) else pl.multiple_of(start, C), C)
        return [one(sq, z) for sq in range(nu) for z in range(2)]

    def prepare_products(cs):
        rows = chunk_rows(cs)
        last = [C - 1, 0]

        def ld(base, sq, z, h):
            return qkv_sc[rows[2 * sq + z], base + h * LANES:base + (h + 1) * LANES]

        q = [ld(0, *ch) for ch in chains]
        k = [ld(512, *ch) for ch in chains]
        v = [ld(1024, *ch) for ch in chains]
        bb = [bb_sc[z * DN_H + h, rows[2 * sq + z], :] for sq, z, h in chains]
        gc = [_dot_exact_lhs(m_b[z], g_sc[rows[2 * sq + z], :])
              for sq in range(nu) for z in range(2)]
        gct =[jnp.transpose(jnp.concatenate([g, g], axis=0)) for g in gc]
        gi = [jnp.broadcast_to(jnp.sum(jnp.where(lane_c == S_DA + z * DN_H + h, gc[2 * sq + z], 0.0),
                                       axis=-1, keepdims=True), (C, LANES)) for sq, z, h in chains]
        gjrow = [gct[2 * sq + z][S_DA + z * DN_H + h:S_DA + z * DN_H + h + 1, :]
                 for sq, z, h in chains]
        gam = []
        for (sq, z, p), (i0, i1) in zip(pairs, heads):
            diff = jnp.where(first, gi[i0], gi[i1]) - jnp.where(first[0:1], gjrow[i0], gjrow[i1])
            gam.append(jnp.where(incl[z], jnp.exp(jnp.where(incl[z], diff, 0.0)), 0.0))
        kb = [k[i] * bb[i] for i in range(nch)]
        kq = [_bdot_nt(jnp.concatenate([kb[i0], q[i0], kb[i1], q[i1]], axis=0),
                       jnp.concatenate([k[i0], k[i1]], axis=0)) for i0, i1 in heads]
        gt = [gi[i][last[z]:last[z] + 1, :] for i, (sq, z, h) in enumerate(chains)]
        return q, k, v, bb, kb, gi, gt, gam, kq

    def prepare_finish(parts):
        q, k, v, bb, kb, gi, gt, gam, kq = parts
        a = [jnp.where(strict[z], jnp.where(first, kq[j][0:C], kq[j][2 * C:3 * C]) * gam[j], 0.0)
             for j, (sq, z, p) in enumerate(pairs)]
        att = [jnp.where(first, kq[j][C:2 * C], kq[j][3 * C:4 * C]) * gam[j] for j in range(npr)]
        nt = [-a[j] for j in range(npr)]
        pw = [pair_products([a[j]], a[j])[0] for j in range(npr)]
        for step in range(5):
            if step < 4:
                both = [pair_products([nt[j], pw[j]], pw[j]) for j in range(npr)]
                nt = [nt[j] + pw[j] + both[j][0] for j in range(npr)]
                pw = [both[j][1] for j in range(npr)]
            else:
                nt = [nt[j] + pw[j] + pair_products([nt[j]], pw[j])[0] for j in range(npr)]
        eg = [jnp.exp(gi[i]) for i in range(nch)]
        rhs = [jnp.concatenate([v[i] * bb[i], kb[i] * eg[i]], axis=1) for i in range(nch)]
        corr = [_bdot(stack_heads(nt[j]), jnp.concatenate([rhs[i0], rhs[i1]], axis=0))
                for j, (i0, i1) in enumerate(heads)]
        for j, (i0, i1) in enumerate(heads):
            for half, i in enumerate((i0, i1)):
                uw = rhs[i] + corr[j][half * C:(half + 1) * C]
                prep_sc[i, 0] = uw[:, 0:LANES]
                prep_sc[i, 1] = uw[:, LANES:]
                prep_sc[i, 2] = q[i] * eg[i]
                prep_sc[i, 3] = k[i] * jnp.exp(gt[i] - gi[i])
                dec_sc[i] = jnp.broadcast_to(jnp.exp(gt[i]), (8, LANES))
            att_sc[j] = att[j]

    nch1, npr1 = nch // 2, npr // 2

    def apply_state_products(slot, s):
        return [_bdot(jnp.concatenate([prep_sc[slot * nch1 + i, 1], prep_sc[slot * nch1 + i, 2]],
                                      axis=0), s[i]) for i in range(nch1)]

    def apply_finish(rows, slot, s, ws):
        base = slot * nch1
        v_new = [prep_sc[base + i, 0] - ws[i][0:C] for i in range(nch1)]
        av = []
        for j in range(npr1):
            i0, i1 = heads[slot * npr1 + j]
            av.append(_bdot(stack_heads(att_sc[slot * npr1 + j]),
                            jnp.concatenate([v_new[i0 - base], v_new[i1 - base]], axis=0)))
        upd = [_bdot_tn(prep_sc[base + i, 3], v_new[i]) for i in range(nch1)]
        for j in range(npr1):
            for half, ig in enumerate(heads[slot * npr1 + j]):
                sq, z, h = chains[ig]
                osc = of_sc if z == 0 else ob_sc
                osc[rows[2 * sq + z], h * LANES:(h + 1) * LANES] = (
                    ws[ig - base][C:2 * C] + av[j][half * C:(half + 1) * C])
        return [s[i] * dec_sc[base + i, 0:1, :] + upd[i] for i in range(nch1)]

    def apply_two(cs, parts_fn):
        rows = chunk_rows(cs)
        s = [s_sc[sq, z, h] for sq, z, h in chains[0:nch1]]
        ws = apply_state_products(0, s)
        parts = parts_fn()
        s = apply_finish(rows, 0, s, ws)
        ws = apply_state_products(1, s)
        s = apply_finish(rows, 1, s, ws)
        for i, (sq, z, h) in enumerate(chains[0:nch1]):
            s_sc[sq, z, h] = s[i]
        return parts

    prepare_finish(prepare_products((0, 1)))

    def body(it, carry):
        c = 2 * it
        parts = apply_two((c, c + 1), lambda: prepare_products((c + 2, c + 3)))
        prepare_finish(parts)
        return carry

    lax.fori_loop(0, n // 2 - 1, body, 0)
    apply_two((n - 2, n - 1), lambda: None)

    ng = ng_ref[...]

    def fin(c, carry):
        r0 = pl.multiple_of(c * C, C)
        rows = pl.ds(r0, C)
        for h in range(DN_H):
            cols = slice(h * LANES, (h + 1) * LANES)
            o = of_sc[rows, cols] + ob_sc[rows, cols]
            o_ref[rows, cols] = (_rms(o, ng) * _silu(dz_ref[rows, cols])).astype(BF16)
        return carry

    lax.fori_loop(0, nb * n, fin, 0)
    so_ref[...] = s_sc[...]


def _dn(p_main, p_small, conv_w, alog_row, dtb_row, ng, s0, layer, L, n_seq, row_blk0):
    has_state = s0 is not None
    nb = _seqs_per_step(L, n_seq)
    assert row_blk0 % nb == 0
    row_blk0, L_blk = row_blk0 // nb, nb * L
    in_specs = [
        pl.BlockSpec((L_blk, 1536), lambda b: (row_blk0 + b, P_DN // 1536)),
        pl.BlockSpec((L_blk, 512), lambda b: (row_blk0 + b, P_DZ // 512)),
        pl.BlockSpec((L_blk, LANES), lambda b: (row_blk0 + b, 0)),
        pl.BlockSpec((DN_CONV, 3 * DN_H * DN_DK), lambda b: (0, 0)),
        pl.BlockSpec((1, LANES), lambda b: (0, 0)),
        pl.BlockSpec((1, LANES), lambda b: (0, 0)),
        pl.BlockSpec((1, DN_DV), lambda b: (0, 0)),
    ]
    args = [p_main, p_main, p_small, conv_w, alog_row, dtb_row, ng]
    if has_state:
        in_specs.append(pl.BlockSpec((nb, None, 2, DN_H, DN_DK, DN_DV),
                                     lambda b: (b, layer, 0, 0, 0, 0)))
        args.append(s0)
    return pl.pallas_call(
        functools.partial(_dn_kernel, L, nb, has_state),
        grid=(n_seq // nb,),
        in_specs=in_specs,
        out_specs=[
            pl.BlockSpec((L_blk, 512), lambda b: (b, 0)),
            pl.BlockSpec((nb, 2, DN_H, DN_DK, DN_DV), lambda b: (b, 0, 0, 0, 0)),
        ],
        out_shape=[
            jax.ShapeDtypeStruct((n_seq * L, 512), BF16),
            jax.ShapeDtypeStruct((n_seq, 2, DN_H, DN_DK, DN_DV), F32),
        ],
        scratch_shapes=[
            pltpu.VMEM((L_blk, 1536), F32),
            pltpu.VMEM((L_blk, LANES), F32),
            pltpu.VMEM((2 * DN_H, L_blk, LANES), F32),
            pltpu.VMEM((L_blk, 512), F32),
            pltpu.VMEM((L_blk, 512), F32),
            pltpu.VMEM((nb, 2, DN_H, DN_DK, DN_DV), F32),
            pltpu.VMEM((2 * nb * 2 * DN_H, 4, CHUNK, LANES), F32),
            pltpu.VMEM((2 * nb * DN_H, CHUNK, LANES), F32),
            pltpu.VMEM((2 * nb * 2 * DN_H, 8, LANES), F32),
        ],
        compiler_params=_cparams(("arbitrary",)),
        name="dn_lat" if has_state else "dn_ctx",
    )(*args)


_W_IN_SPLITS = (256, 256, 512, 512, 32, 1024, 256, 256, 1536, 512, 8, 8)


def _regroup_kernel(wt_ref, main_ref, small_ref):
    off = np.cumsum((0,) + _W_IN_SPLITS)
    gq, gk, gv, gr, glr, aq, ak, av, dqkv, dz, da, db = [
        wt_ref[int(off[i]):int(off[i + 1]), :] for i in range(12)]
    main_t = jnp.concatenate([aq, ak, av, gq, gk, gv, gr, dqkv, dz], axis=0)
    main_ref[...] = jnp.transpose(main_t).astype(BF16)
    pad = jnp.zeros((LANES - 48, wt_ref.shape[1]), F32)
    small_t = jnp.concatenate([glr, da, db, pad], axis=0)
    small_ref[...] = jnp.transpose(small_t).astype(BF16)


def _prep_w_in(w_in, tr=256):
    depth, _, in_cols = w_in.shape
    return pl.pallas_call(
        _regroup_kernel,
        grid=(depth, D_MODEL // tr),
        in_specs=[pl.BlockSpec((None, in_cols, tr), lambda l, r: (l, 0, r))],
        out_specs=[
            pl.BlockSpec((None, tr, P_MAIN), lambda l, r: (l, r, 0)),
            pl.BlockSpec((None, tr, LANES), lambda l, r: (l, r, 0)),
        ],
        out_shape=[
            jax.ShapeDtypeStruct((depth, D_MODEL, P_MAIN), BF16),
            jax.ShapeDtypeStruct((depth, D_MODEL, LANES), BF16),
        ],
        compiler_params=_cparams(("arbitrary", "arbitrary")),
        name="w_in_regroup",
    )(jnp.swapaxes(w_in, 1, 2))


def _lane_row(vals, offset):
    return jnp.zeros((1, LANES), F32).at[0, offset:offset + vals.shape[0]].set(vals.astype(F32))


def kernel(x_prompt, x_sample, cache_k, cache_v, state_gla, state_dn, c, c_ctx, norm1_g, norm2_g,
           w_mod, b_mod, w_in, gla_w2, gla_b, gla_norm_g, q_norm_g, k_norm_g, dn_conv, dn_a_log,
           dn_dt_bias, dn_norm_g, w_out, w_ff1, w_ff2):
    n_ctx, n_lat = x_prompt.shape[0], x_sample.shape[0]
    depth = w_in.shape[0]
    n_ctx_rows = n_ctx * SEQ
    assert n_ctx_rows % DEC_SEQ == 0 and n_lat <= 7

    xs = [x_prompt.reshape(n_ctx_rows, D_MODEL), x_sample.reshape(n_lat * DEC_SEQ, D_MODEL)]
    cond8 = jnp.concatenate([c_ctx[None, :], c, jnp.zeros((7 - n_lat, D_MODEL), F32)], axis=0)
    b_mod3 = b_mod.reshape(depth, 1, N_MOD * D_MODEL)
    mods_l = _modulation(cond8, w_mod, b_mod3, 1).reshape(8, N_MOD, D_MODEL)

    cos, sin = _rope_tables()
    gla_consts = _gla_consts()
    ck = cache_k.reshape(n_lat, depth, PAST_LEN, ATT_HKV * HEAD_DIM)
    cv = cache_v.reshape(n_lat, depth, PAST_LEN, ATT_HKV * HEAD_DIM)

    w_main, w_small = _prep_w_in(w_in)
    w_out_b, w_ff1_b, w_ff2_b = (w[0:1].astype(BF16) for w in (w_out, w_ff1, w_ff2))

    nks, nvs, sgs, sds = [], [], [], []
    for l in range(depth):
        tc, tl = n_ctx_rows // INPROJ_TM, n_lat * DEC_SEQ // INPROJ_TM
        inproj = functools.partial(_inproj, mods_l=mods_l, g1=norm1_g[l][None, :], w_main=w_main,
                                   w_small=w_small, layer=l, n_ctx_rows=n_ctx_rows)
        pc, psc = inproj(xs[0], tile0=0, n_tiles=tc, grp_tile0=0)
        if len(xs) == 2:
            pl_, psl = inproj(xs[1], tile0=0, n_tiles=tl, grp_tile0=tc)
        else:
            pl_, psl = inproj(xs[0], tile0=tc, n_tiles=tl, grp_tile0=tc)

        qg, kg = q_norm_g[l][None, :], k_norm_g[l][None, :]
        oa_c, nk, nv = _att_ctx(pc, qg, kg, n_ctx)
        oa_l = _att_lat(pl_, qg, kg, cos, sin, ck, cv, l, n_lat, 0)

        w2p = jnp.zeros((2, LANES, GLA_H * GLA_DK), F32)
        w2p = w2p.at[0, 0:GLA_LR].set(gla_w2[l, 0]).at[1, GLA_LR:2 * GLA_LR].set(gla_w2[l, 1])
        gbias = gla_b[l][:, None, :]
        gng = gla_norm_g[l][None, :]
        og_c, sg = _gla(pc, psc, w2p, gbias, gng, gla_consts, None, l, SEQ, n_ctx, 0)
        og_l, _ = _gla(pl_, psl, w2p, gbias, gng, gla_consts, state_gla, l, DEC_SEQ, n_lat, 0)

        alog_row = _lane_row(dn_a_log[l].reshape(-1), S_DA)
        dtb_row = _lane_row(dn_dt_bias[l].reshape(-1), S_DA)
        dng = dn_norm_g[l][None, :]
        od_c, sd = _dn(pc, psc, dn_conv[l], alog_row, dtb_row, dng, None, l, SEQ, n_ctx, 0)
        od_l, _ = _dn(pl_, psl, dn_conv[l], alog_row, dtb_row, dng, state_dn, l, DEC_SEQ, n_lat, 0)

        x1, h2 = _outproj(xs, [og_c, og_l], [oa_c, oa_l], [od_c, od_l], mods_l,
                          norm2_g[l][None, :], w_out_b, 0, n_ctx_rows)
        n_tiles, nct = x1.shape[0] // FFN_TM, n_ctx_rows // FFN_TM
        ffn = functools.partial(_ffn, h2, x1, mods_l, w_ff1_b, w_ff2_b, n_ctx_rows, tm=FFN_TM)
        if l < depth - 1:
            y, w_ff1_b, w_ff2_b, w_out_b, m_next = ffn(
                tile0=0, n_tiles=n_tiles, next_layer=l + 1, cast_next=(w_ff1, w_ff2, w_out),
                mod_next=(cond8, w_mod, b_mod3))
            xs, mods_l = [y], m_next.reshape(8, N_MOD, D_MODEL)
        else:
            xs = [ffn(tile0=0, n_tiles=nct)[0], ffn(tile0=nct, n_tiles=n_tiles - nct)[0]]

        nks.append(nk.reshape(n_ctx, SEQ, ATT_HKV, HEAD_DIM))
        nvs.append(nv.reshape(n_ctx, SEQ, ATT_HKV, HEAD_DIM))
        sgs.append(sg)
        sds.append(sd)

    y_prompt = xs[0].reshape(n_ctx, SEQ, D_MODEL)
    y_sample = xs[1].reshape(n_lat, DEC_SEQ, D_MODEL)
    return (y_prompt, y_sample, jnp.stack(nks, axis=1), jnp.stack(nvs, axis=1),
            jnp.stack(sgs, axis=1), jnp.stack(sds, axis=1))
```

```python
import functools
import math

import numpy as np
import jax
import jax.numpy as jnp
from jax import lax
from jax.experimental import pallas as pl
from jax.experimental.pallas import tpu as pltpu

F32 = jnp.float32
BF16 = jnp.bfloat16

D_MODEL = 2048
SEQ = 256
DEC_SEQ = 1024
PAST_LEN = 256
GRID_W = 64
HEAD_DIM = 128
EPS = 1e-6
GLA_H = 4
GLA_DK = 64
GLA_DV = 128
GLA_LR = 16
GLA_GATE_NORM = 16.0
ATT_HQ = 8
ATT_HKV = 2
ROPE_THETA = 10000.0
DN_H = 4
DN_DK = 128
DN_DV = 128
DN_CONV = 3
CHUNK = 64
FF = 4 * D_MODEL
N_MOD = 6
LANES = 128

P_AQ = 0
P_AKV = 1024
P_GLA = 1536
P_DN = 3072
P_DZ = 4608
P_MAIN = 5120
S_DA = 32
S_DB = 40

VMEM_LIMIT = 56 * 1024 * 1024


def _cparams(sem):
    return pltpu.CompilerParams(dimension_semantics=sem, vmem_limit_bytes=VMEM_LIMIT)


def _bdot(a, b):
    return jnp.dot(a.astype(BF16), b.astype(BF16), preferred_element_type=F32)


def _bdot_nt(a, b):
    return lax.dot_general(a.astype(BF16), b.astype(BF16), (((1,), (1,)), ((), ())),
                           preferred_element_type=F32)


def _bdot_tn(a, b):
    return lax.dot_general(a.astype(BF16), b.astype(BF16), (((0,), (0,)), ((), ())),
                           preferred_element_type=F32)


def _split3(x):
    hi = x.astype(BF16)
    r = x - hi.astype(F32)
    mid = r.astype(BF16)
    lo = (r - mid.astype(F32)).astype(BF16)
    return hi, mid, lo


def _dot_exact_lhs(lhs_bf16, x):
    hi, mid, lo = _split3(x)
    d = functools.partial(jnp.dot, preferred_element_type=F32)
    return d(lhs_bf16, hi) + d(lhs_bf16, mid) + d(lhs_bf16, lo)


def _dot_exact_lhs_wide(lhs_bf16, x):
    n = x.shape[1]
    hi = x.astype(BF16)
    mid = (x - hi.astype(F32)).astype(BF16)
    r = jnp.dot(lhs_bf16, jnp.concatenate([hi, mid], axis=1), preferred_element_type=F32)
    return r[:, 0:n] + r[:, n:2 * n]


def _dot3(a, b):
    ah, am, _ = _split3(a)
    bh, bm, _ = _split3(b)
    d = functools.partial(jnp.dot, preferred_element_type=F32)
    return d(ah, bh) + (d(ah, bm) + d(am, bh))


def _sigmoid(x):
    return 0.5 * jnp.tanh(0.5 * x) + 0.5


def _silu(x):
    return x * _sigmoid(x)


def _softplus(x):
    return jnp.maximum(x, 0.0) + jnp.log1p(jnp.exp(-jnp.abs(x)))


def _rms(x, g):
    return x * lax.rsqrt(jnp.mean(x * x, axis=-1, keepdims=True) + EPS) * g


def _mods_kernel(c_ref, w_ref, b_ref, o_ref):
    o_ref[...] = _bdot(_silu(c_ref[...]), w_ref[...]) + b_ref[...]


def _modulation(cond8, w_mod, b_mod3, depth):
    n_out = N_MOD * D_MODEL
    tn = 1024
    return pl.pallas_call(
        _mods_kernel,
        grid=(depth, n_out // tn),
        in_specs=[
            pl.BlockSpec((8, D_MODEL), lambda l, j: (0, 0)),
            pl.BlockSpec((None, D_MODEL, tn), lambda l, j: (l, 0, j)),
            pl.BlockSpec((None, 1, tn), lambda l, j: (l, 0, j)),
        ],
        out_specs=pl.BlockSpec((None, 8, tn), lambda l, j: (l, 0, j)),
        out_shape=jax.ShapeDtypeStruct((depth, 8, n_out), F32),
        compiler_params=_cparams(("arbitrary", "arbitrary")),
        name="modulation",
    )(cond8, w_mod, b_mod3)


def _group_of_tile(i, tm, n_ctx_rows):
    n_ctx_tiles = n_ctx_rows // tm
    per_lat = DEC_SEQ // tm
    return jnp.where(i < n_ctx_tiles, 0, 1 + (i - n_ctx_tiles) // per_lat)


def _row_specs(arrays, width, tm, nct):
    if len(arrays) == 1:
        return [pl.BlockSpec((tm, width), lambda i, *_: (i, 0))]
    return [pl.BlockSpec((tm, width), lambda i, *_: (jnp.minimum(i, nct - 1), 0)),
            pl.BlockSpec((tm, width), lambda i, *_: (jnp.maximum(i - nct, 0), 0))]


def _row_load(refs, nct, rows=slice(None)):
    if len(refs) == 1:
        return refs[0][rows, :]
    return jnp.where(pl.program_id(0) < nct, refs[0][rows, :], refs[1][rows, :])


INPROJ_TM = 1024


def _inproj_kernel(x_ref, mod_ref, g_ref, w_ref, ws_ref, p_ref, ps_ref, h_sc):
    @pl.when(pl.program_id(1) == 0)
    def _():
        m = mod_ref[...]
        h = _rms(x_ref[...], g_ref[...]) * (1.0 + m[1:2]) + m[0:1]
        hb = h.astype(BF16)
        h_sc[...] = hb
        ps_ref[...] = jnp.dot(hb, ws_ref[...], preferred_element_type=F32)

    h = h_sc[...]
    half = w_ref.shape[1] // 2
    for piece in range(2):
        cols = slice(piece * half, (piece + 1) * half)
        p_ref[:, cols] = jnp.dot(h, w_ref[:, cols], preferred_element_type=F32)


def _inproj(x, mods_l, g1, w_main, w_small, layer, n_ctx_rows, tile0, n_tiles, grp_tile0, tn=1024):
    tm = INPROJ_TM
    grp = functools.partial(_group_of_tile, tm=tm, n_ctx_rows=n_ctx_rows)
    return pl.pallas_call(
        _inproj_kernel,
        grid=(n_tiles, P_MAIN // tn),
        in_specs=[
            pl.BlockSpec((tm, D_MODEL), lambda i, j: (tile0 + i, 0)),
            pl.BlockSpec((None, N_MOD, D_MODEL), lambda i, j: (grp(grp_tile0 + i), 0, 0)),
            pl.BlockSpec((1, D_MODEL), lambda i, j: (0, 0)),
            pl.BlockSpec((None, D_MODEL, tn), lambda i, j: (layer, 0, j)),
            pl.BlockSpec((None, D_MODEL, LANES), lambda i, j: (layer, 0, 0)),
        ],
        out_specs=[
            pl.BlockSpec((tm, tn), lambda i, j: (i, j)),
            pl.BlockSpec((tm, LANES), lambda i, j: (i, 0)),
        ],
        out_shape=[
            jax.ShapeDtypeStruct((n_tiles * tm, P_MAIN), F32),
            jax.ShapeDtypeStruct((n_tiles * tm, LANES), F32),
        ],
        scratch_shapes=[pltpu.VMEM((tm, D_MODEL), BF16)],
        compiler_params=_cparams(("arbitrary", "arbitrary")),
        name="inproj",
    )(x, mods_l, g1, w_main, w_small)


def _outproj_kernel(nct, n_x, *refs):
    x_refs = refs[:n_x]
    (ogc_ref, ogl_ref, oac_ref, oal_ref, odc_ref, odl_ref, mod_ref, g_ref, w_ref,
     x1_ref, h2_ref) = refs[n_x:]
    m = mod_ref[...]
    d = functools.partial(jnp.dot, preferred_element_type=F32)
    half = x1_ref.shape[0] // 2
    for r in range(2):
        rows = slice(r * half, (r + 1) * half)
        mix = (d(_row_load((ogc_ref, ogl_ref), nct, rows), w_ref[0:512, :])
               + d(_row_load((oac_ref, oal_ref), nct, rows), w_ref[512:1536, :])
               + d(_row_load((odc_ref, odl_ref), nct, rows), w_ref[1536:2048, :]))
        x1 = _row_load(x_refs, nct, rows) + m[2:3] * mix
        x1_ref[rows, :] = x1
        h2_ref[rows, :] = (_rms(x1, g_ref[...]) * (1.0 + m[4:5]) + m[3:4]).astype(BF16)


def _outproj(xs, o_gla, o_att, o_dn, mods_l, g2, w_out, layer, n_ctx_rows, tm=512):
    rows = sum(x.shape[0] for x in xs)
    nct = n_ctx_rows // tm
    grp = functools.partial(_group_of_tile, tm=tm, n_ctx_rows=n_ctx_rows)
    return pl.pallas_call(
        functools.partial(_outproj_kernel, nct, len(xs)),
        grid=(rows // tm,),
        in_specs=(_row_specs(xs, D_MODEL, tm, nct) + _row_specs(o_gla, 512, tm, nct)
                  + _row_specs(o_att, 1024, tm, nct) + _row_specs(o_dn, 512, tm, nct) + [
                      pl.BlockSpec((None, N_MOD, D_MODEL), lambda i: (grp(i), 0, 0)),
                      pl.BlockSpec((1, D_MODEL), lambda i: (0, 0)),
                      pl.BlockSpec((None, D_MODEL, D_MODEL), lambda i: (layer, 0, 0),
                                   pipeline_mode=pl.Buffered(1)),
                  ]),
        out_specs=[
            pl.BlockSpec((tm, D_MODEL), lambda i: (i, 0)),
            pl.BlockSpec((tm, D_MODEL), lambda i: (i, 0)),
        ],
        out_shape=[
            jax.ShapeDtypeStruct((rows, D_MODEL), F32),
            jax.ShapeDtypeStruct((rows, D_MODEL), BF16),
        ],
        compiler_params=_cparams(("arbitrary",)),
        name="outproj",
    )(*xs, *o_gla, *o_att, *o_dn, mods_l, g2, w_out)


FFN_TM = 512
FFN_OUT_CHUNK = 512


def _ffn_kernel(n_cast, has_mod, h_ref, x1_ref, mod_ref, w1_ref, w2_ref, *refs):
    n_in = n_cast + (3 if has_mod else 0)
    y_ref = refs[n_in]
    for src, dst in zip(refs[:n_cast], refs[n_in + 1:n_in + 1 + n_cast]):
        dst[...] = src[...].astype(BF16)
    if has_mod:
        c_ref, wm_ref, bm_ref = refs[n_cast:n_in]
        refs[n_in + 1 + n_cast][...] = _bdot(_silu(c_ref[...]), wm_ref[...]) + bm_ref[...]
    j = pl.program_id(1)
    tf = w1_ref.shape[1]

    @pl.when(j == 0)
    def _():
        y_ref[...] = jnp.zeros(y_ref.shape, F32)

    h = h_ref[...]
    for half in range(2):
        cols = slice(half * (tf // 2), (half + 1) * (tf // 2))
        t = jnp.dot(h, w1_ref[:, cols], preferred_element_type=F32)
        t = jnp.square(jnp.maximum(t, 0.0)).astype(BF16)
        for nc in range(D_MODEL // FFN_OUT_CHUNK):
            ncs = slice(nc * FFN_OUT_CHUNK, (nc + 1) * FFN_OUT_CHUNK)
            y_ref[:, ncs] += jnp.dot(t, w2_ref[cols, ncs], preferred_element_type=F32)

    @pl.when(j == pl.num_programs(1) - 1)
    def _():
        y_ref[...] = x1_ref[...] + mod_ref[5:6, :] * y_ref[...]


def _ffn(h2, x1, mods_l, w1, w2, n_ctx_rows, tile0, n_tiles, tm, cast_next=(), mod_next=(),
         next_layer=0, tf=1024):
    grp = functools.partial(_group_of_tile, tm=tm, n_ctx_rows=n_ctx_rows)
    nj = FF // tf
    n_steps = n_tiles * nj
    cast_in, cast_out, cast_shape = [], [], []
    for wn in cast_next:
        rows, cols = wn.shape[1:]
        rb = rows // n_steps
        assert rb * n_steps == rows and rb % 16 == 0
        cast_in.append(pl.BlockSpec((None, rb, cols), lambda i, j: (next_layer, i * nj + j, 0)))
        cast_out.append(pl.BlockSpec((None, rb, cols), lambda i, j: (0, i * nj + j, 0)))
        cast_shape.append(jax.ShapeDtypeStruct((1, rows, cols), BF16))
    if mod_next:
        n_out = N_MOD * D_MODEL
        n_slabs = min(n_steps, n_out // LANES)
        cb = n_out // n_slabs
        assert cb * n_slabs == n_out and cb % LANES == 0

        def slab(i, j):
            return jnp.minimum(i * nj + j, n_slabs - 1)

        cast_in += [pl.BlockSpec((8, D_MODEL), lambda i, j: (0, 0)),
                    pl.BlockSpec((None, D_MODEL, cb), lambda i, j: (next_layer, 0, slab(i, j))),
                    pl.BlockSpec((None, 1, cb), lambda i, j: (next_layer, 0, slab(i, j)))]
        cast_out.append(pl.BlockSpec((8, cb), lambda i, j: (0, slab(i, j))))
        cast_shape.append(jax.ShapeDtypeStruct((8, n_out), F32))
    return pl.pallas_call(
        functools.partial(_ffn_kernel, len(cast_next), bool(mod_next)),
        grid=(n_tiles, nj),
        in_specs=[
            pl.BlockSpec((tm, D_MODEL), lambda i, j: (tile0 + i, 0)),
            pl.BlockSpec((tm, D_MODEL), lambda i, j: (tile0 + i, 0)),
            pl.BlockSpec((None, N_MOD, D_MODEL), lambda i, j: (grp(tile0 + i), 0, 0)),
            pl.BlockSpec((None, D_MODEL, tf), lambda i, j: (0, 0, j)),
            pl.BlockSpec((None, tf, D_MODEL), lambda i, j: (0, j, 0)),
        ] + cast_in,
        out_specs=[pl.BlockSpec((tm, D_MODEL), lambda i, j: (i, 0))] + cast_out,
        out_shape=[jax.ShapeDtypeStruct((n_tiles * tm, D_MODEL), F32)] + cast_shape,
        compiler_params=_cparams(("arbitrary", "arbitrary")),
        name="ffn",
    )(h2, x1, mods_l, w1, w2, *cast_next, *mod_next)


def _rope_tables():
    half = HEAD_DIM // 2
    pos = np.arange(DEC_SEQ)
    row = (pos // GRID_W).astype(np.float32)
    col = (pos % GRID_W).astype(np.float32)
    inv = (ROPE_THETA ** (-np.arange(0, half, 2, dtype=np.float32) / half)).astype(np.float32)
    ar = row[:, None] * inv[None, :]
    ac = col[:, None] * inv[None, :]
    cos = np.concatenate([np.cos(ar), np.cos(ar), np.cos(ac), np.cos(ac)], axis=1)
    sin = np.concatenate([-np.sin(ar), np.sin(ar), -np.sin(ac), np.sin(ac)], axis=1)
    return jnp.asarray(cos, F32), jnp.asarray(sin, F32)


def _rope(x, cos, sin):
    lane = lax.broadcasted_iota(jnp.int32, x.shape, 1)
    first = (lane % 64) < 32
    partner = jnp.where(first, pltpu.roll(x, 96, 1), pltpu.roll(x, 32, 1))
    return x * cos + partner * sin


ATT_G = ATT_HQ // ATT_HKV


def _group_attention(q_heads, kb, vb, store):
    s = [lax.dot_general(q.astype(BF16), kb, (((1,), (1,)), ((), ())), preferred_element_type=F32)
         for q in q_heads]
    p = [jnp.exp(si - jnp.max(si, axis=-1, keepdims=True)) for si in s]
    l = [jnp.sum(pi, axis=-1, keepdims=True) for pi in p]
    for g in range(len(q_heads)):
        o = jnp.dot(p[g].astype(BF16), vb, preferred_element_type=F32) / l[g]
        store(g, o.astype(BF16))


def _att_ctx_kernel(aq_ref, akv_ref, qg_ref, kg_ref, o_ref, nk_ref, nv_ref):
    qg = qg_ref[...] * (HEAD_DIM ** -0.5)
    kg = kg_ref[...]
    for hk in range(ATT_HKV):
        ks = slice(hk * HEAD_DIM, (hk + 1) * HEAD_DIM)
        kn = _rms(akv_ref[:, ks], kg)
        v = akv_ref[:, 256 + hk * HEAD_DIM:256 + (hk + 1) * HEAD_DIM]
        nk_ref[:, ks] = kn
        nv_ref[:, ks] = v

        def head_cols(g, hk=hk):
            return slice((hk * ATT_G + g) * HEAD_DIM, (hk * ATT_G + g + 1) * HEAD_DIM)

        def store(g, val):
            o_ref[:, head_cols(g)] = val

        qs = [_rms(aq_ref[:, head_cols(g)], qg) for g in range(ATT_G)]
        _group_attention(qs, kn.astype(BF16), v.astype(BF16), store)


def _att_lat_kernel(aq_ref, akv_ref, qg_ref, kg_ref, cos_ref, sin_ref, ck_ref, cv_ref, o_ref,
                    k_sc, v_sc):
    qg = qg_ref[...] * (HEAD_DIM ** -0.5)
    kg = kg_ref[...]
    qb = 256
    for hk in range(ATT_HKV):
        ks = slice(hk * HEAD_DIM, (hk + 1) * HEAD_DIM)
        k_sc[hk, 0:PAST_LEN, :] = ck_ref[:, ks].astype(BF16)
        v_sc[hk, 0:PAST_LEN, :] = cv_ref[:, ks].astype(BF16)
        kn = _rope(_rms(akv_ref[:, ks], kg), cos_ref[...], sin_ref[...])
        k_sc[hk, PAST_LEN:, :] = kn.astype(BF16)
        v_sc[hk, PAST_LEN:, :] = akv_ref[:, 256 + hk * HEAD_DIM:256 + (hk + 1) * HEAD_DIM].astype(BF16)

    def body(r, carry):
        r0 = pl.multiple_of(r * qb, qb)
        cos = cos_ref[pl.ds(r0, qb), :]
        sin = sin_ref[pl.ds(r0, qb), :]
        for hk in range(ATT_HKV):
            def head_cols(g, hk=hk):
                return slice((hk * ATT_G + g) * HEAD_DIM, (hk * ATT_G + g + 1) * HEAD_DIM)

            def store(g, val):
                o_ref[pl.ds(r0, qb), head_cols(g)] = val

            qs = [_rope(_rms(aq_ref[pl.ds(r0, qb), head_cols(g)], qg), cos, sin)
                  for g in range(ATT_G)]
            _group_attention(qs, k_sc[hk], v_sc[hk], store)
        return carry

    lax.fori_loop(0, DEC_SEQ // qb, body, 0)


def _att_ctx(p_main, qg, kg, n_seq):
    L = SEQ
    return pl.pallas_call(
        _att_ctx_kernel,
        grid=(n_seq,),
        in_specs=[
            pl.BlockSpec((L, 1024), lambda b: (b, P_AQ // 1024)),
            pl.BlockSpec((L, 512), lambda b: (b, P_AKV // 512)),
            pl.BlockSpec((1, HEAD_DIM), lambda b: (0, 0)),
            pl.BlockSpec((1, HEAD_DIM), lambda b: (0, 0)),
        ],
        out_specs=[
            pl.BlockSpec((L, 1024), lambda b: (b, 0)),
            pl.BlockSpec((None, L, 256), lambda b: (b, 0, 0)),
            pl.BlockSpec((None, L, 256), lambda b: (b, 0, 0)),
        ],
        out_shape=[
            jax.ShapeDtypeStruct((n_seq * L, 1024), BF16),
            jax.ShapeDtypeStruct((n_seq, L, 256), F32),
            jax.ShapeDtypeStruct((n_seq, L, 256), F32),
        ],
        compiler_params=_cparams(("arbitrary",)),
        name="att_ctx",
    )(p_main, p_main, qg, kg)


def _att_lat(p_main, qg, kg, cos, sin, ck, cv, layer, n_seq, row_blk0):
    L = DEC_SEQ
    return pl.pallas_call(
        _att_lat_kernel,
        grid=(n_seq,),
        in_specs=[
            pl.BlockSpec((L, 1024), lambda b: (row_blk0 + b, P_AQ // 1024)),
            pl.BlockSpec((L, 512), lambda b: (row_blk0 + b, P_AKV // 512)),
            pl.BlockSpec((1, HEAD_DIM), lambda b: (0, 0)),
            pl.BlockSpec((1, HEAD_DIM), lambda b: (0, 0)),
            pl.BlockSpec((L, HEAD_DIM), lambda b: (0, 0)),
            pl.BlockSpec((L, HEAD_DIM), lambda b: (0, 0)),
            pl.BlockSpec((None, None, PAST_LEN, 256), lambda b: (b, layer, 0, 0)),
            pl.BlockSpec((None, None, PAST_LEN, 256), lambda b: (b, layer, 0, 0)),
        ],
        out_specs=pl.BlockSpec((L, 1024), lambda b: (b, 0)),
        out_shape=jax.ShapeDtypeStruct((n_seq * L, 1024), BF16),
        scratch_shapes=[
            pltpu.VMEM((ATT_HKV, PAST_LEN + L, HEAD_DIM), BF16),
            pltpu.VMEM((ATT_HKV, PAST_LEN + L, HEAD_DIM), BF16),
        ],
        compiler_params=_cparams(("arbitrary",)),
        name="att_lat",
    )(p_main, p_main, qg, kg, cos, sin, ck, cv)


def _seqs_per_step(L, n_seq):
    return 2 if (L <= SEQ and n_seq % 2 == 0) else 1


_GLA_LEVELS = (32, 16, 8, 4, 2, 1)


def _gla_consts():
    C = CHUNK
    i = np.arange(C)[:, None]
    t = np.arange(C)[None, :]
    mats = [t <= i, t > i]
    masks = [i == t]
    for s in _GLA_LEVELS:
        m = (i // s) * s
        if s > 1:
            mats.append((t > m) & (t <= i))
        mats.append((t > i) & (t <= m + s))
        masks.append((i // (2 * s) == t // (2 * s)) & ((i // s) % 2 == 1) & ((t // s) % 2 == 0))
    fwd = np.concatenate(mats, 0)
    bwd = np.concatenate([mm[::-1, ::-1] for mm in mats], 0)
    mstack = np.stack([fwd, bwd]).astype(np.float32)
    mf = np.stack([np.concatenate([mm, mm], 0) for mm in masks])
    mb = np.stack([np.concatenate([mm[::-1, ::-1], mm[::-1, ::-1]], 0) for mm in masks])
    lmask = np.stack([mf, mb]).astype(np.float32)
    return jnp.asarray(mstack, BF16), jnp.asarray(lmask, F32)


def _hosted_cast(refs, n_in, n_out, has_cast):
    if not has_cast:
        return refs
    refs[n_in + 1 + n_out][...] = refs[n_in][...].astype(BF16)
    return refs[:n_in] + refs[n_in + 1:n_in + 1 + n_out] + refs[n_in + 2 + n_out:]


def _cast_specs(cast, n_steps):
    if cast is None:
        return [], [], [], []
    w, layer = cast
    rows, cols = w.shape[1:]
    rb = rows // n_steps
    assert rb * n_steps == rows and rb % 16 == 0
    return ([pl.BlockSpec((None, rb, cols), lambda b: (layer, b, 0))],
            [pl.BlockSpec((None, rb, cols), lambda b: (0, b, 0))],
            [jax.ShapeDtypeStruct((1, rows, cols), BF16)], [w])


def _gla_kernel(L, nb, has_state, has_cast, *refs):
    refs = _hosted_cast(refs, 7 + has_state, 2, has_cast)
    if has_state:
        (pg_ref, ps_ref, w2_ref, gb_ref, ng_ref, ms_ref, lm_ref, s0_ref,
         o_ref, so_ref, la_sc, of_sc, ob_sc, st_sc) = refs
    else:
        (pg_ref, ps_ref, w2_ref, gb_ref, ng_ref, ms_ref, lm_ref,
         o_ref, so_ref, la_sc, of_sc, ob_sc, st_sc) = refs
    n = L // CHUNK
    C = CHUNK

    ps = ps_ref[...]
    for z in range(2):
        x = _dot3(ps, w2_ref[z]) + gb_ref[z]
        la_sc[z] = (jnp.minimum(x, 0.0) - jnp.log1p(jnp.exp(-jnp.abs(x)))) * (1.0 / GLA_GATE_NORM)

    if has_state:
        for sq in range(nb):
            for z in range(2):
                for p in range(2):
                    pair = jnp.concatenate([s0_ref[sq, z, 2 * p], s0_ref[sq, z, 2 * p + 1]], axis=0)
                    st_sc[sq, z, p] = jnp.transpose(pair)
    else:
        st_sc[...] = jnp.zeros(st_sc.shape, F32)

    lane = lax.broadcasted_iota(jnp.int32, (C, LANES), 1)
    first = lane < 64

    def stack_heads(t):
        return jnp.concatenate([jnp.where(first, t, 0.0), jnp.where(first, 0.0, t)], axis=0)

    chains = [(t, sq, z, p) for t in range(2) for sq in range(nb) for z in range(2) for p in range(2)]
    nch = len(chains)
    nch1 = nch // 2

    def body(it, carry):
        def chunk_rows(t, sq, z):
            c = 2 * it + t
            cc = c if z == 0 else n - 1 - c
            return pl.ds(pl.multiple_of(sq * L + cc * C, C), C)

        rows = [chunk_rows(t, sq, z) for t, sq, z, p in chains]
        last = [C - 1, 0]
        q = [pg_ref[rows[i], p * LANES:(p + 1) * LANES] * (GLA_DK ** -0.5)
             for i, (t, sq, z, p) in enumerate(chains)]
        k = [pg_ref[rows[i], 256 + p * LANES:256 + (p + 1) * LANES]
             for i, (t, sq, z, p) in enumerate(chains)]
        e = [jnp.exp(_dot_exact_lhs_wide(ms_ref[z], la_sc[z, rows[i], p * LANES:(p + 1) * LANES]))
             for i, (t, sq, z, p) in enumerate(chains)]
        a2 = [_bdot_nt(stack_heads(q[i]), k[i]) * lm_ref[z, 0]
              for i, (t, sq, z, p) in enumerate(chains)]
        blk = 2
        for li, s in enumerate(_GLA_LEVELS):
            if s > 1:
                qs = [q[i] * e[i][blk * C:(blk + 1) * C] for i in range(nch)]
                blk += 1
            else:
                qs = q
            ks = [k[i] * e[i][blk * C:(blk + 1) * C] for i in range(nch)]
            blk += 1
            a2 = [a2[i] + _bdot_nt(stack_heads(qs[i]), ks[i]) * lm_ref[z, li + 1]
                  for i, (t, sq, z, p) in enumerate(chains)]
        kl = [k[i] * e[i][C:2 * C] for i in range(nch)]
        v = [[pg_ref[rows[i], 512 + (2 * p + hh) * LANES:512 + (2 * p + hh + 1) * LANES]
              for hh in range(2)] for i, (t, sq, z, p) in enumerate(chains)]
        intra = [[_bdot(a2[i][hh * C:(hh + 1) * C], v[i][hh]) for hh in range(2)]
                 for i in range(nch)]
        zs = [[_bdot_tn(v[i][hh], kl[i]) for hh in range(2)] for i in range(nch)]
        qe = [stack_heads(q[i] * e[i][0:C]) for i in range(nch)]
        st = [st_sc[sq, z, p] for t, sq, z, p in chains[0:nch1]]
        for i, (t, sq, z, p) in enumerate(chains):
            j = i % nch1
            inter = _bdot_nt(qe[i], st[j])
            osc = of_sc if z == 0 else ob_sc
            for hh in range(2):
                h = 2 * p + hh
                osc[rows[i], h * LANES:(h + 1) * LANES] = intra[i][hh] + inter[hh * C:(hh + 1) * C]
            st[j] = (st[j] * e[i][last[z]:last[z] + 1, :]
                     + jnp.where(first[0:1], zs[i][0], zs[i][1]))
        for j, (t, sq, z, p) in enumerate(chains[0:nch1]):
            st_sc[sq, z, p] = st[j]
        return carry

    lax.fori_loop(0, n // 2, body, 0)

    ng = ng_ref[...]

    def fin(c, carry):
        r0 = pl.multiple_of(c * C, C)
        rows = pl.ds(r0, C)
        for h in range(GLA_H):
            cols = slice(h * LANES, (h + 1) * LANES)
            o = of_sc[rows, cols] + ob_sc[rows, cols]
            gate = _silu(pg_ref[rows, 1024 + h * LANES:1024 + (h + 1) * LANES])
            o_ref[rows, cols] = (_rms(o, ng) * gate).astype(BF16)
        return carry

    lax.fori_loop(0, nb * n, fin, 0)
    for sq in range(nb):
        for z in range(2):
            for p in range(2):
                pair = jnp.transpose(st_sc[sq, z, p])
                so_ref[sq, z, 2 * p] = pair[0:GLA_DK]
                so_ref[sq, z, 2 * p + 1] = pair[GLA_DK:2 * GLA_DK]


def _gla(p_main, p_small, w2p, gbias, ng, consts, s0, layer, L, n_seq, row_blk0, cast=None):
    mstack, lmask = consts
    has_state = s0 is not None
    nb = _seqs_per_step(L, n_seq)
    assert row_blk0 % nb == 0
    blk0 = row_blk0 // nb
    state_blk = (nb, 2, GLA_H, GLA_DK, GLA_DV)
    in_specs = [
        pl.BlockSpec((nb * L, 1536), lambda b: (blk0 + b, P_GLA // 1536)),
        pl.BlockSpec((nb * L, LANES), lambda b: (blk0 + b, 0)),
        pl.BlockSpec((2, LANES, 256), lambda b: (0, 0, 0)),
        pl.BlockSpec((2, 1, 256), lambda b: (0, 0, 0)),
        pl.BlockSpec((1, GLA_DV), lambda b: (0, 0)),
        pl.BlockSpec(mstack.shape, lambda b: (0, 0, 0)),
        pl.BlockSpec(lmask.shape, lambda b: (0, 0, 0, 0)),
    ]
    args = [p_main, p_small, w2p, gbias, ng, mstack, lmask]
    if has_state:
        in_specs.append(pl.BlockSpec((nb, None) + state_blk[1:], lambda b: (b, layer, 0, 0, 0, 0)))
        args.append(s0)
    c_in, c_out, c_shape, c_arg = _cast_specs(cast, n_seq // nb)
    return pl.pallas_call(
        functools.partial(_gla_kernel, L, nb, has_state, cast is not None),
        grid=(n_seq // nb,),
        in_specs=in_specs + c_in,
        out_specs=[
            pl.BlockSpec((nb * L, 512), lambda b: (b, 0)),
            pl.BlockSpec(state_blk, lambda b: (b, 0, 0, 0, 0)),
        ] + c_out,
        out_shape=[
            jax.ShapeDtypeStruct((n_seq * L, 512), BF16),
            jax.ShapeDtypeStruct((n_seq, 2, GLA_H, GLA_DK, GLA_DV), F32),
        ] + c_shape,
        scratch_shapes=[
            pltpu.VMEM((2, nb * L, 256), F32),
            pltpu.VMEM((nb * L, 512), F32),
            pltpu.VMEM((nb * L, 512), F32),
            pltpu.VMEM((nb, 2, 2, GLA_DV, LANES), F32),
        ],
        compiler_params=_cparams(("arbitrary",)),
        name="gla_lat" if has_state else "gla_ctx",
    )(*args, *c_arg)


def _dn_kernel(L, nb, has_state, has_cast, *refs):
    refs = _hosted_cast(refs, 7 + has_state, 2, has_cast)
    if has_state:
        (pd_ref, dz_ref, ps_ref, cw_ref, al_ref, dtb_ref, ng_ref, s0_ref,
         o_ref, so_ref, qkv_sc, g_sc, bb_sc, of_sc, ob_sc, s_sc, prep_sc, att_sc, dec_sc) = refs
    else:
        (pd_ref, dz_ref, ps_ref, cw_ref, al_ref, dtb_ref, ng_ref,
         o_ref, so_ref, qkv_sc, g_sc, bb_sc, of_sc, ob_sc, s_sc, prep_sc, att_sc, dec_sc) = refs
    n = L // CHUNK
    C = CHUNK

    R = nb * L
    pos = lax.broadcasted_iota(jnp.int32, (R, LANES), 0) & (L - 1)
    not_first = pos > 0
    not_last = pos < L - 1

    def conv_silu(j):
        c0 = pl.multiple_of(j * LANES, LANES)
        x = pd_ref[:, pl.ds(c0, LANES)]
        w = cw_ref[:, pl.ds(c0, LANES)]
        prev = jnp.where(not_first, pltpu.roll(x, 1, 0), 0.0)
        nxt = jnp.where(not_last, pltpu.roll(x, R - 1, 0), 0.0)
        return c0, _silu(prev * w[0:1] + x * w[1:2] + nxt * w[2:3])

    def qk_tile(j, carry):
        c0, y = conv_silu(j)
        inv = lax.rsqrt(jnp.sum(y * y, axis=-1, keepdims=True) + EPS)
        qkv_sc[:, pl.ds(c0, LANES)] = y * (inv * jnp.where(j < DN_H, DN_DK ** -0.5, 1.0))
        return carry

    def v_tile(j, carry):
        c0, y = conv_silu(j)
        qkv_sc[:, pl.ds(c0, LANES)] = y
        return carry

    lax.fori_loop(0, 2 * DN_H, qk_tile, 0)
    lax.fori_loop(2 * DN_H, 3 * DN_H, v_tile, 0)

    ps = ps_ref[...]
    lane_l = lax.broadcasted_iota(jnp.int32, (R, LANES), 1)
    g_sc[...] = -jnp.exp(al_ref[...]) * _softplus(ps + dtb_ref[...])
    b_all = _sigmoid(ps)
    for idx in range(2 * DN_H):
        bcol = jnp.sum(jnp.where(lane_l == S_DB + idx, b_all, 0.0), axis=-1, keepdims=True)
        bb_sc[idx] = jnp.broadcast_to(bcol, (R, LANES))

    if has_state:
        s_sc[...] = s0_ref[...]
    else:
        s_sc[...] = jnp.zeros(s_sc.shape, F32)

    ri = lax.broadcasted_iota(jnp.int32, (C, LANES), 0)
    lane_c = lax.broadcasted_iota(jnp.int32, (C, LANES), 1)
    ci = lane_c & (C - 1)
    first = lane_c < C
    incl = [ci <= ri, ci >= ri]
    strict = [ci < ri, ci > ri]
    rs = lax.broadcasted_iota(jnp.int32, (C, C), 0)
    cs = lax.broadcasted_iota(jnp.int32, (C, C), 1)
    m_b = [(cs <= rs).astype(BF16), (cs >= rs).astype(BF16)]
    nu = 2 * nb
    chains = [(sq, z, h) for sq in range(nu) for z in range(2) for h in range(DN_H)]
    nch = len(chains)
    pairs = [(sq, z, p) for sq in range(nu) for z in range(2) for p in range(DN_H // 2)]
    npr = len(pairs)

    def stack_heads(t):
        return jnp.concatenate([jnp.where(first, t, 0.0), jnp.where(first, 0.0, t)], axis=0)

    def hi_mid(x):
        hi = x.astype(BF16).astype(F32)
        return hi, x - hi

    def pair_products(lhs_list, x):
        xh, xm = hi_mid(x)
        rhs = jnp.concatenate([stack_heads(xh), stack_heads(xm)], axis=1).astype(BF16)
        parts = []
        for l in lhs_list:
            parts.extend(hi_mid(l))
        r = jnp.dot(jnp.concatenate(parts, axis=0).astype(BF16), rhs, preferred_element_type=F32)
        out = []
        for i in range(len(lhs_list)):
            blk = r[2 * C * i:2 * C * i + C] + r[2 * C * i + C:2 * C * (i + 1)]
            out.append(blk[:, 0:LANES] + blk[:, LANES:2 * LANES])
        return out

    heads = [((2 * sq + z) * DN_H + 2 * p, (2 * sq + z) * DN_H + 2 * p + 1)
             for sq, z, p in pairs]

    def chunk_rows(cs):
        def one(sq, z):
            c = cs[sq // nb]
            start = (sq % nb) * L + (c if z == 0 else n - 1 - c) * C
            return pl.ds(start if isinstance(start, int) else pl.multiple_of(start, C), C)
        return [one(sq, z) for sq in range(nu) for z in range(2)]

    def prepare_products(cs):
        rows = chunk_rows(cs)
        last = [C - 1, 0]

        def ld(base, sq, z, h):
            return qkv_sc[rows[2 * sq + z], base + h * LANES:base + (h + 1) * LANES]

        q = [ld(0, *ch) for ch in chains]
        k = [ld(512, *ch) for ch in chains]
        v = [ld(1024, *ch) for ch in chains]
        bb = [bb_sc[z * DN_H + h, rows[2 * sq + z], :] for sq, z, h in chains]
        gc = [_dot_exact_lhs(m_b[z], g_sc[rows[2 * sq + z], :])
              for sq in range(nu) for z in range(2)]
        gct =[jnp.transpose(jnp.concatenate([g, g], axis=0)) for g in gc]
        gi = [jnp.broadcast_to(jnp.sum(jnp.where(lane_c == S_DA + z * DN_H + h, gc[2 * sq + z], 0.0),
                                       axis=-1, keepdims=True), (C, LANES)) for sq, z, h in chains]
        gjrow = [gct[2 * sq + z][S_DA + z * DN_H + h:S_DA + z * DN_H + h + 1, :]
                 for sq, z, h in chains]
        gam = []
        for (sq, z, p), (i0, i1) in zip(pairs, heads):
            diff = jnp.where(first, gi[i0], gi[i1]) - jnp.where(first[0:1], gjrow[i0], gjrow[i1])
            gam.append(jnp.where(incl[z], jnp.exp(jnp.where(incl[z], diff, 0.0)), 0.0))
        kb = [k[i] * bb[i] for i in range(nch)]
        kq = [_bdot_nt(jnp.concatenate([kb[i0], q[i0], kb[i1], q[i1]], axis=0),
                       jnp.concatenate([k[i0], k[i1]], axis=0)) for i0, i1 in heads]
        gt = [gi[i][last[z]:last[z] + 1, :] for i, (sq, z, h) in enumerate(chains)]
        return q, k, v, bb, kb, gi, gt, gam, kq

    def prepare_finish(parts):
        q, k, v, bb, kb, gi, gt, gam, kq = parts
        a = [jnp.where(strict[z], jnp.where(first, kq[j][0:C], kq[j][2 * C:3 * C]) * gam[j], 0.0)
             for j, (sq, z, p) in enumerate(pairs)]
        att = [jnp.where(first, kq[j][C:2 * C], kq[j][3 * C:4 * C]) * gam[j] for j in range(npr)]
        nt = [-a[j] for j in range(npr)]
        pw = [pair_products([a[j]], a[j])[0] for j in range(npr)]
        for step in range(5):
            if step < 4:
                both = [pair_products([nt[j], pw[j]], pw[j]) for j in range(npr)]
                nt = [nt[j] + pw[j] + both[j][0] for j in range(npr)]
                pw = [both[j][1] for j in range(npr)]
            else:
                nt = [nt[j] + pw[j] + pair_products([nt[j]], pw[j])[0] for j in range(npr)]
        eg = [jnp.exp(gi[i]) for i in range(nch)]
        rhs = [jnp.concatenate([v[i] * bb[i], kb[i] * eg[i]], axis=1) for i in range(nch)]
        corr = [_bdot(stack_heads(nt[j]), jnp.concatenate([rhs[i0], rhs[i1]], axis=0))
                for j, (i0, i1) in enumerate(heads)]
        for j, (i0, i1) in enumerate(heads):
            for half, i in enumerate((i0, i1)):
                uw = rhs[i] + corr[j][half * C:(half + 1) * C]
                prep_sc[i, 0] = uw[:, 0:LANES]
                prep_sc[i, 1] = uw[:, LANES:]
                prep_sc[i, 2] = q[i] * eg[i]
                prep_sc[i, 3] = k[i] * jnp.exp(gt[i] - gi[i])
                dec_sc[i] = jnp.broadcast_to(jnp.exp(gt[i]), (8, LANES))
            att_sc[j] = att[j]

    nch1, npr1 = nch // 2, npr // 2

    def apply_state_products(slot, s):
        return [_bdot(jnp.concatenate([prep_sc[slot * nch1 + i, 1], prep_sc[slot * nch1 + i, 2]],
                                      axis=0), s[i]) for i in range(nch1)]

    def apply_finish(rows, slot, s, ws):
        base = slot * nch1
        v_new = [prep_sc[base + i, 0] - ws[i][0:C] for i in range(nch1)]
        av = []
        for j in range(npr1):
            i0, i1 = heads[slot * npr1 + j]
            av.append(_bdot(stack_heads(att_sc[slot * npr1 + j]),
                            jnp.concatenate([v_new[i0 - base], v_new[i1 - base]], axis=0)))
        upd = [_bdot_tn(prep_sc[base + i, 3], v_new[i]) for i in range(nch1)]
        for j in range(npr1):
            for half, ig in enumerate(heads[slot * npr1 + j]):
                sq, z, h = chains[ig]
                osc = of_sc if z == 0 else ob_sc
                osc[rows[2 * sq + z], h * LANES:(h + 1) * LANES] = (
                    ws[ig - base][C:2 * C] + av[j][half * C:(half + 1) * C])
        return [s[i] * dec_sc[base + i, 0:1, :] + upd[i] for i in range(nch1)]

    def apply_two(cs, parts_fn):
        rows = chunk_rows(cs)
        s = [s_sc[sq, z, h] for sq, z, h in chains[0:nch1]]
        ws = apply_state_products(0, s)
        parts = parts_fn()
        s = apply_finish(rows, 0, s, ws)
        ws = apply_state_products(1, s)
        s = apply_finish(rows, 1, s, ws)
        for i, (sq, z, h) in enumerate(chains[0:nch1]):
            s_sc[sq, z, h] = s[i]
        return parts

    prepare_finish(prepare_products((0, 1)))

    def body(it, carry):
        c = 2 * it
        parts = apply_two((c, c + 1), lambda: prepare_products((c + 2, c + 3)))
        prepare_finish(parts)
        return carry

    lax.fori_loop(0, n // 2 - 1, body, 0)
    apply_two((n - 2, n - 1), lambda: None)

    ng = ng_ref[...]

    def fin(c, carry):
        r0 = pl.multiple_of(c * C, C)
        rows = pl.ds(r0, C)
        for h in range(DN_H):
            cols = slice(h * LANES, (h + 1) * LANES)
            o = of_sc[rows, cols] + ob_sc[rows, cols]
            o_ref[rows, cols] = (_rms(o, ng) * _silu(dz_ref[rows, cols])).astype(BF16)
        return carry

    lax.fori_loop(0, nb * n, fin, 0)
    so_ref[...] = s_sc[...]


def _dn(p_main, p_small, conv_w, alog_row, dtb_row, ng, s0, layer, L, n_seq, row_blk0, cast=None):
    has_state = s0 is not None
    nb = _seqs_per_step(L, n_seq)
    assert row_blk0 % nb == 0
    row_blk0, L_blk = row_blk0 // nb, nb * L
    in_specs = [
        pl.BlockSpec((L_blk, 1536), lambda b: (row_blk0 + b, P_DN // 1536)),
        pl.BlockSpec((L_blk, 512), lambda b: (row_blk0 + b, P_DZ // 512)),
        pl.BlockSpec((L_blk, LANES), lambda b: (row_blk0 + b, 0)),
        pl.BlockSpec((DN_CONV, 3 * DN_H * DN_DK), lambda b: (0, 0)),
        pl.BlockSpec((1, LANES), lambda b: (0, 0)),
        pl.BlockSpec((1, LANES), lambda b: (0, 0)),
        pl.BlockSpec((1, DN_DV), lambda b: (0, 0)),
    ]
    args = [p_main, p_main, p_small, conv_w, alog_row, dtb_row, ng]
    if has_state:
        in_specs.append(pl.BlockSpec((nb, None, 2, DN_H, DN_DK, DN_DV),
                                     lambda b: (b, layer, 0, 0, 0, 0)))
        args.append(s0)
    c_in, c_out, c_shape, c_arg = _cast_specs(cast, n_seq // nb)
    return pl.pallas_call(
        functools.partial(_dn_kernel, L, nb, has_state, cast is not None),
        grid=(n_seq // nb,),
        in_specs=in_specs + c_in,
        out_specs=[
            pl.BlockSpec((L_blk, 512), lambda b: (b, 0)),
            pl.BlockSpec((nb, 2, DN_H, DN_DK, DN_DV), lambda b: (b, 0, 0, 0, 0)),
        ] + c_out,
        out_shape=[
            jax.ShapeDtypeStruct((n_seq * L, 512), BF16),
            jax.ShapeDtypeStruct((n_seq, 2, DN_H, DN_DK, DN_DV), F32),
        ] + c_shape,
        scratch_shapes=[
            pltpu.VMEM((L_blk, 1536), F32),
            pltpu.VMEM((L_blk, LANES), F32),
            pltpu.VMEM((2 * DN_H, L_blk, LANES), F32),
            pltpu.VMEM((L_blk, 512), F32),
            pltpu.VMEM((L_blk, 512), F32),
            pltpu.VMEM((nb, 2, DN_H, DN_DK, DN_DV), F32),
            pltpu.VMEM((2 * nb * 2 * DN_H, 4, CHUNK, LANES), F32),
            pltpu.VMEM((2 * nb * DN_H, CHUNK, LANES), F32),
            pltpu.VMEM((2 * nb * 2 * DN_H, 8, LANES), F32),
        ],
        compiler_params=_cparams(("arbitrary",)),
        name="dn_lat" if has_state else "dn_ctx",
    )(*args, *c_arg)


_W_IN_SPLITS = (256, 256, 512, 512, 32, 1024, 256, 256, 1536, 512, 8, 8)


def _regroup_kernel(wt_ref, main_ref, small_ref):
    off = np.cumsum((0,) + _W_IN_SPLITS)
    gq, gk, gv, gr, glr, aq, ak, av, dqkv, dz, da, db = [
        wt_ref[int(off[i]):int(off[i + 1]), :] for i in range(12)]
    main_t = jnp.concatenate([aq, ak, av, gq, gk, gv, gr, dqkv, dz], axis=0)
    main_ref[...] = jnp.transpose(main_t).astype(BF16)
    pad = jnp.zeros((LANES - 48, wt_ref.shape[1]), F32)
    small_t = jnp.concatenate([glr, da, db, pad], axis=0)
    small_ref[...] = jnp.transpose(small_t).astype(BF16)


def _prep_w_in(w_in, tr=256):
    depth, _, in_cols = w_in.shape
    return pl.pallas_call(
        _regroup_kernel,
        grid=(depth, D_MODEL // tr),
        in_specs=[pl.BlockSpec((None, in_cols, tr), lambda l, r: (l, 0, r))],
        out_specs=[
            pl.BlockSpec((None, tr, P_MAIN), lambda l, r: (l, r, 0)),
            pl.BlockSpec((None, tr, LANES), lambda l, r: (l, r, 0)),
        ],
        out_shape=[
            jax.ShapeDtypeStruct((depth, D_MODEL, P_MAIN), BF16),
            jax.ShapeDtypeStruct((depth, D_MODEL, LANES), BF16),
        ],
        compiler_params=_cparams(("arbitrary", "arbitrary")),
        name="w_in_regroup",
    )(jnp.swapaxes(w_in, 1, 2))


def _lane_row(vals, offset):
    return jnp.zeros((1, LANES), F32).at[0, offset:offset + vals.shape[0]].set(vals.astype(F32))


def kernel(x_prompt, x_sample, cache_k, cache_v, state_gla, state_dn, c, c_ctx, norm1_g, norm2_g,
           w_mod, b_mod, w_in, gla_w2, gla_b, gla_norm_g, q_norm_g, k_norm_g, dn_conv, dn_a_log,
           dn_dt_bias, dn_norm_g, w_out, w_ff1, w_ff2):
    n_ctx, n_lat = x_prompt.shape[0], x_sample.shape[0]
    depth = w_in.shape[0]
    n_ctx_rows = n_ctx * SEQ
    assert n_ctx_rows % DEC_SEQ == 0 and n_lat <= 7

    xs = [x_prompt.reshape(n_ctx_rows, D_MODEL), x_sample.reshape(n_lat * DEC_SEQ, D_MODEL)]
    cond8 = jnp.concatenate([c_ctx[None, :], c, jnp.zeros((7 - n_lat, D_MODEL), F32)], axis=0)
    b_mod3 = b_mod.reshape(depth, 1, N_MOD * D_MODEL)
    mods_l = _modulation(cond8, w_mod, b_mod3, 1).reshape(8, N_MOD, D_MODEL)

    cos, sin = _rope_tables()
    gla_consts = _gla_consts()
    ck = cache_k.reshape(n_lat, depth, PAST_LEN, ATT_HKV * HEAD_DIM)
    cv = cache_v.reshape(n_lat, depth, PAST_LEN, ATT_HKV * HEAD_DIM)

    w_main, w_small = _prep_w_in(w_in)
    w_out_b, w_ff1_b, w_ff2_b = w_out[0:1].astype(BF16), None, None

    nks, nvs, sgs, sds = [], [], [], []
    for l in range(depth):
        tc, tl = n_ctx_rows // INPROJ_TM, n_lat * DEC_SEQ // INPROJ_TM
        inproj = functools.partial(_inproj, mods_l=mods_l, g1=norm1_g[l][None, :], w_main=w_main,
                                   w_small=w_small, layer=l, n_ctx_rows=n_ctx_rows)
        pc, psc = inproj(xs[0], tile0=0, n_tiles=tc, grp_tile0=0)
        if len(xs) == 2:
            pl_, psl = inproj(xs[1], tile0=0, n_tiles=tl, grp_tile0=tc)
        else:
            pl_, psl = inproj(xs[0], tile0=tc, n_tiles=tl, grp_tile0=tc)

        qg, kg = q_norm_g[l][None, :], k_norm_g[l][None, :]
        oa_c, nk, nv = _att_ctx(pc, qg, kg, n_ctx)
        oa_l = _att_lat(pl_, qg, kg, cos, sin, ck, cv, l, n_lat, 0)

        w2p = jnp.zeros((2, LANES, GLA_H * GLA_DK), F32)
        w2p = w2p.at[0, 0:GLA_LR].set(gla_w2[l, 0]).at[1, GLA_LR:2 * GLA_LR].set(gla_w2[l, 1])
        gbias = gla_b[l][:, None, :]
        gng = gla_norm_g[l][None, :]
        og_c, sg, *hosted = _gla(pc, psc, w2p, gbias, gng, gla_consts, None, l, SEQ, n_ctx, 0,
                                 cast=(w_ff1, 0) if l == 0 else None)
        if hosted:
            w_ff1_b = hosted[0]
        og_l, _ = _gla(pl_, psl, w2p, gbias, gng, gla_consts, state_gla, l, DEC_SEQ, n_lat, 0)

        alog_row = _lane_row(dn_a_log[l].reshape(-1), S_DA)
        dtb_row = _lane_row(dn_dt_bias[l].reshape(-1), S_DA)
        dng = dn_norm_g[l][None, :]
        od_c, sd, *hosted = _dn(pc, psc, dn_conv[l], alog_row, dtb_row, dng, None, l, SEQ, n_ctx, 0,
                                cast=(w_ff2, 0) if l == 0 else None)
        if hosted:
            w_ff2_b = hosted[0]
        od_l, _ = _dn(pl_, psl, dn_conv[l], alog_row, dtb_row, dng, state_dn, l, DEC_SEQ, n_lat, 0)

        x1, h2 = _outproj(xs, [og_c, og_l], [oa_c, oa_l], [od_c, od_l], mods_l,
                          norm2_g[l][None, :], w_out_b, 0, n_ctx_rows)
        n_tiles, nct = x1.shape[0] // FFN_TM, n_ctx_rows // FFN_TM
        ffn = functools.partial(_ffn, h2, x1, mods_l, w_ff1_b, w_ff2_b, n_ctx_rows, tm=FFN_TM)
        if l < depth - 1:
            y, w_ff1_b, w_ff2_b, w_out_b, m_next = ffn(
                tile0=0, n_tiles=n_tiles, next_layer=l + 1, cast_next=(w_ff1, w_ff2, w_out),
                mod_next=(cond8, w_mod, b_mod3))
            xs, mods_l = [y], m_next.reshape(8, N_MOD, D_MODEL)
        else:
            xs = [ffn(tile0=0, n_tiles=nct)[0], ffn(tile0=nct, n_tiles=n_tiles - nct)[0]]

        nks.append(nk.reshape(n_ctx, SEQ, ATT_HKV, HEAD_DIM))
        nvs.append(nv.reshape(n_ctx, SEQ, ATT_HKV, HEAD_DIM))
        sgs.append(sg)
        sds.append(sd)

    y_prompt = xs[0].reshape(n_ctx, SEQ, D_MODEL)
    y_sample = xs[1].reshape(n_lat, DEC_SEQ, D_MODEL)
    return (y_prompt, y_sample, jnp.stack(nks, axis=1), jnp.stack(nvs, axis=1),
            jnp.stack(sgs, axis=1), jnp.stack(sds, axis=1))
```

```python
import functools

import numpy as np
import jax
import jax.numpy as jnp
from jax import lax
from jax.experimental import pallas as pl
from jax.experimental.pallas import tpu as pltpu

F32 = jnp.float32
BF16 = jnp.bfloat16

D_MODEL = 2048
SEQ = 256
DEC_SEQ = 1024
PAST_LEN = 256
GRID_W = 64
HEAD_DIM = 128
EPS = 1e-6
GLA_H = 4
GLA_DK = 64
GLA_DV = 128
GLA_LR = 16
GLA_GATE_NORM = 16.0
ATT_HQ = 8
ATT_HKV = 2
ROPE_THETA = 10000.0
DN_H = 4
DN_DK = 128
DN_DV = 128
DN_CONV = 3
CHUNK = 64
FF = 4 * D_MODEL
N_MOD = 6
LANES = 128

P_AQ = 0
P_AKV = 1024
P_GLA = 1536
P_DN = 3072
P_DZ = 4608
P_MAIN = 5120
S_DA = 32
S_DB = 40

VMEM_LIMIT = 56 * 1024 * 1024


def _cparams(sem):
    return pltpu.CompilerParams(dimension_semantics=sem, vmem_limit_bytes=VMEM_LIMIT)


def _bdot(a, b):
    return jnp.dot(a.astype(BF16), b.astype(BF16), preferred_element_type=F32)


def _bdot_nt(a, b):
    return lax.dot_general(a.astype(BF16), b.astype(BF16), (((1,), (1,)), ((), ())),
                           preferred_element_type=F32)


def _bdot_tn(a, b):
    return lax.dot_general(a.astype(BF16), b.astype(BF16), (((0,), (0,)), ((), ())),
                           preferred_element_type=F32)


def _split3(x):
    hi = x.astype(BF16)
    r = x - hi.astype(F32)
    mid = r.astype(BF16)
    lo = (r - mid.astype(F32)).astype(BF16)
    return hi, mid, lo


def _dot_exact_lhs(lhs_bf16, x):
    hi, mid, lo = _split3(x)
    d = functools.partial(jnp.dot, preferred_element_type=F32)
    return d(lhs_bf16, hi) + d(lhs_bf16, mid) + d(lhs_bf16, lo)


def _dot_exact_lhs_wide(lhs_bf16, x):
    n = x.shape[1]
    hi = x.astype(BF16)
    mid = (x - hi.astype(F32)).astype(BF16)
    r = jnp.dot(lhs_bf16, jnp.concatenate([hi, mid], axis=1), preferred_element_type=F32)
    return r[:, 0:n] + r[:, n:2 * n]


def _dot3(a, b):
    ah, am, _ = _split3(a)
    bh, bm, _ = _split3(b)
    d = functools.partial(jnp.dot, preferred_element_type=F32)
    return d(ah, bh) + (d(ah, bm) + d(am, bh))


def _sigmoid(x):
    return 0.5 * jnp.tanh(0.5 * x) + 0.5


def _silu(x):
    return x * _sigmoid(x)


def _softplus(x):
    return jnp.maximum(x, 0.0) + jnp.log1p(jnp.exp(-jnp.abs(x)))


def _rms(x, g):
    return x * lax.rsqrt(jnp.mean(x * x, axis=-1, keepdims=True) + EPS) * g


def _mods_kernel(c_ref, w_ref, b_ref, o_ref):
    o_ref[...] = _bdot(_silu(c_ref[...]), w_ref[...]) + b_ref[...]


def _modulation(cond8, w_mod, b_mod3, depth):
    n_out = N_MOD * D_MODEL
    tn = 1024
    return pl.pallas_call(
        _mods_kernel,
        grid=(depth, n_out // tn),
        in_specs=[
            pl.BlockSpec((8, D_MODEL), lambda l, j: (0, 0)),
            pl.BlockSpec((None, D_MODEL, tn), lambda l, j: (l, 0, j)),
            pl.BlockSpec((None, 1, tn), lambda l, j: (l, 0, j)),
        ],
        out_specs=pl.BlockSpec((None, 8, tn), lambda l, j: (l, 0, j)),
        out_shape=jax.ShapeDtypeStruct((depth, 8, n_out), F32),
        compiler_params=_cparams(("arbitrary", "arbitrary")),
        name="modulation",
    )(cond8, w_mod, b_mod3)


def _group_of_tile(i, tm, n_ctx_rows):
    n_ctx_tiles = n_ctx_rows // tm
    per_lat = DEC_SEQ // tm
    return jnp.where(i < n_ctx_tiles, 0, 1 + (i - n_ctx_tiles) // per_lat)


def _row_specs(arrays, width, tm, nct):
    if len(arrays) == 1:
        return [pl.BlockSpec((tm, width), lambda i, *_: (i, 0))]
    return [pl.BlockSpec((tm, width), lambda i, *_: (jnp.minimum(i, nct - 1), 0)),
            pl.BlockSpec((tm, width), lambda i, *_: (jnp.maximum(i - nct, 0), 0))]


def _row_load(refs, nct, rows=slice(None)):
    if len(refs) == 1:
        return refs[0][rows, :]
    return jnp.where(pl.program_id(0) < nct, refs[0][rows, :], refs[1][rows, :])


INPROJ_TM = 1024


def _inproj_kernel(x_ref, mod_ref, g_ref, w_ref, ws_ref, p_ref, ps_ref, h_sc):
    @pl.when(pl.program_id(1) == 0)
    def _():
        m = mod_ref[...]
        h = _rms(x_ref[...], g_ref[...]) * (1.0 + m[1:2]) + m[0:1]
        hb = h.astype(BF16)
        h_sc[...] = hb
        ps_ref[...] = jnp.dot(hb, ws_ref[...], preferred_element_type=F32)

    h = h_sc[...]
    half = w_ref.shape[1] // 2
    for piece in range(2):
        cols = slice(piece * half, (piece + 1) * half)
        p_ref[:, cols] = jnp.dot(h, w_ref[:, cols], preferred_element_type=F32)


def _inproj(x, mods_l, g1, w_main, w_small, layer, n_ctx_rows, tile0, n_tiles, grp_tile0, tn=1024):
    tm = INPROJ_TM
    grp = functools.partial(_group_of_tile, tm=tm, n_ctx_rows=n_ctx_rows)
    return pl.pallas_call(
        _inproj_kernel,
        grid=(n_tiles, P_MAIN // tn),
        in_specs=[
            pl.BlockSpec((tm, D_MODEL), lambda i, j: (tile0 + i, 0)),
            pl.BlockSpec((None, N_MOD, D_MODEL), lambda i, j: (grp(grp_tile0 + i), 0, 0)),
            pl.BlockSpec((1, D_MODEL), lambda i, j: (0, 0)),
            pl.BlockSpec((None, D_MODEL, tn), lambda i, j: (layer, 0, j)),
            pl.BlockSpec((None, D_MODEL, LANES), lambda i, j: (layer, 0, 0)),
        ],
        out_specs=[
            pl.BlockSpec((tm, tn), lambda i, j: (i, j)),
            pl.BlockSpec((tm, LANES), lambda i, j: (i, 0)),
        ],
        out_shape=[
            jax.ShapeDtypeStruct((n_tiles * tm, P_MAIN), F32),
            jax.ShapeDtypeStruct((n_tiles * tm, LANES), F32),
        ],
        scratch_shapes=[pltpu.VMEM((tm, D_MODEL), BF16)],
        compiler_params=_cparams(("arbitrary", "arbitrary")),
        name="inproj",
    )(x, mods_l, g1, w_main, w_small)


def _outproj_kernel(nct, n_x, *refs):
    x_refs = refs[:n_x]
    (ogc_ref, ogl_ref, oac_ref, oal_ref, odc_ref, odl_ref, mod_ref, g_ref, w_ref,
     x1_ref, h2_ref) = refs[n_x:]
    m = mod_ref[...]
    d = functools.partial(jnp.dot, preferred_element_type=F32)
    half = x1_ref.shape[0] // 2
    for r in range(2):
        rows = slice(r * half, (r + 1) * half)
        mix = (d(_row_load((ogc_ref, ogl_ref), nct, rows), w_ref[0:512, :])
               + d(_row_load((oac_ref, oal_ref), nct, rows), w_ref[512:1536, :])
               + d(_row_load((odc_ref, odl_ref), nct, rows), w_ref[1536:2048, :]))
        x1 = _row_load(x_refs, nct, rows) + m[2:3] * mix
        x1_ref[rows, :] = x1
        h2_ref[rows, :] = (_rms(x1, g_ref[...]) * (1.0 + m[4:5]) + m[3:4]).astype(BF16)


def _outproj(xs, o_gla, o_att, o_dn, mods_l, g2, w_out, layer, n_ctx_rows, tm=512):
    rows = sum(x.shape[0] for x in xs)
    nct = n_ctx_rows // tm
    grp = functools.partial(_group_of_tile, tm=tm, n_ctx_rows=n_ctx_rows)
    return pl.pallas_call(
        functools.partial(_outproj_kernel, nct, len(xs)),
        grid=(rows // tm,),
        in_specs=(_row_specs(xs, D_MODEL, tm, nct) + _row_specs(o_gla, 512, tm, nct)
                  + _row_specs(o_att, 1024, tm, nct) + _row_specs(o_dn, 512, tm, nct) + [
                      pl.BlockSpec((None, N_MOD, D_MODEL), lambda i: (grp(i), 0, 0)),
                      pl.BlockSpec((1, D_MODEL), lambda i: (0, 0)),
                      pl.BlockSpec((None, D_MODEL, D_MODEL), lambda i: (layer, 0, 0),
                                   pipeline_mode=pl.Buffered(1)),
                  ]),
        out_specs=[
            pl.BlockSpec((tm, D_MODEL), lambda i: (i, 0)),
            pl.BlockSpec((tm, D_MODEL), lambda i: (i, 0)),
        ],
        out_shape=[
            jax.ShapeDtypeStruct((rows, D_MODEL), F32),
            jax.ShapeDtypeStruct((rows, D_MODEL), BF16),
        ],
        compiler_params=_cparams(("arbitrary",)),
        name="outproj",
    )(*xs, *o_gla, *o_att, *o_dn, mods_l, g2, w_out)


FFN_TM = 512
FFN_OUT_CHUNK = 512


def _ffn_kernel(n_cast, has_mod, h_ref, x1_ref, mod_ref, w1_ref, w2_ref, *refs):
    n_in = n_cast + (3 if has_mod else 0)
    y_ref = refs[n_in]
    for src, dst in zip(refs[:n_cast], refs[n_in + 1:n_in + 1 + n_cast]):
        dst[...] = src[...].astype(BF16)
    if has_mod:
        c_ref, wm_ref, bm_ref = refs[n_cast:n_in]
        refs[n_in + 1 + n_cast][...] = _bdot(_silu(c_ref[...]), wm_ref[...]) + bm_ref[...]
    j = pl.program_id(1)
    tf = w1_ref.shape[1]

    @pl.when(j == 0)
    def _():
        y_ref[...] = jnp.zeros(y_ref.shape, F32)

    h = h_ref[...]
    for half in range(2):
        cols = slice(half * (tf // 2), (half + 1) * (tf // 2))
        t = jnp.dot(h, w1_ref[:, cols], preferred_element_type=F32)
        t = jnp.square(jnp.maximum(t, 0.0)).astype(BF16)
        for nc in range(D_MODEL // FFN_OUT_CHUNK):
            ncs = slice(nc * FFN_OUT_CHUNK, (nc + 1) * FFN_OUT_CHUNK)
            y_ref[:, ncs] += jnp.dot(t, w2_ref[cols, ncs], preferred_element_type=F32)

    @pl.when(j == pl.num_programs(1) - 1)
    def _():
        y_ref[...] = x1_ref[...] + mod_ref[5:6, :] * y_ref[...]


def _ffn(h2, x1, mods_l, w1, w2, n_ctx_rows, tile0, n_tiles, tm, cast_next=(), mod_next=(),
         next_layer=0, tf=1024):
    grp = functools.partial(_group_of_tile, tm=tm, n_ctx_rows=n_ctx_rows)
    nj = FF // tf
    n_steps = n_tiles * nj
    cast_in, cast_out, cast_shape = [], [], []
    for wn in cast_next:
        rows, cols = wn.shape[1:]
        rb = rows // n_steps
        assert rb * n_steps == rows and rb % 16 == 0
        cast_in.append(pl.BlockSpec((None, rb, cols), lambda i, j: (next_layer, i * nj + j, 0)))
        cast_out.append(pl.BlockSpec((None, rb, cols), lambda i, j: (0, i * nj + j, 0)))
        cast_shape.append(jax.ShapeDtypeStruct((1, rows, cols), BF16))
    if mod_next:
        n_out = N_MOD * D_MODEL
        n_slabs = min(n_steps, n_out // LANES)
        cb = n_out // n_slabs
        assert cb * n_slabs == n_out and cb % LANES == 0

        def slab(i, j):
            return jnp.minimum(i * nj + j, n_slabs - 1)

        cast_in += [pl.BlockSpec((8, D_MODEL), lambda i, j: (0, 0)),
                    pl.BlockSpec((None, D_MODEL, cb), lambda i, j: (next_layer, 0, slab(i, j))),
                    pl.BlockSpec((None, 1, cb), lambda i, j: (next_layer, 0, slab(i, j)))]
        cast_out.append(pl.BlockSpec((8, cb), lambda i, j: (0, slab(i, j))))
        cast_shape.append(jax.ShapeDtypeStruct((8, n_out), F32))
    return pl.pallas_call(
        functools.partial(_ffn_kernel, len(cast_next), bool(mod_next)),
        grid=(n_tiles, nj),
        in_specs=[
            pl.BlockSpec((tm, D_MODEL), lambda i, j: (tile0 + i, 0)),
            pl.BlockSpec((tm, D_MODEL), lambda i, j: (tile0 + i, 0)),
            pl.BlockSpec((None, N_MOD, D_MODEL), lambda i, j: (grp(tile0 + i), 0, 0)),
            pl.BlockSpec((None, D_MODEL, tf), lambda i, j: (0, 0, j)),
            pl.BlockSpec((None, tf, D_MODEL), lambda i, j: (0, j, 0)),
        ] + cast_in,
        out_specs=[pl.BlockSpec((tm, D_MODEL), lambda i, j: (i, 0))] + cast_out,
        out_shape=[jax.ShapeDtypeStruct((n_tiles * tm, D_MODEL), F32)] + cast_shape,
        compiler_params=_cparams(("arbitrary", "arbitrary")),
        name="ffn",
    )(h2, x1, mods_l, w1, w2, *cast_next, *mod_next)


def _rope_tables():
    half = HEAD_DIM // 2
    pos = np.arange(DEC_SEQ)
    row = (pos // GRID_W).astype(np.float32)
    col = (pos % GRID_W).astype(np.float32)
    inv = (ROPE_THETA ** (-np.arange(0, half, 2, dtype=np.float32) / half)).astype(np.float32)
    ar = row[:, None] * inv[None, :]
    ac = col[:, None] * inv[None, :]
    cos = np.concatenate([np.cos(ar), np.cos(ar), np.cos(ac), np.cos(ac)], axis=1)
    sin = np.concatenate([-np.sin(ar), np.sin(ar), -np.sin(ac), np.sin(ac)], axis=1)
    return jnp.asarray(cos, F32), jnp.asarray(sin, F32)


def _rope(x, cos, sin):
    lane = lax.broadcasted_iota(jnp.int32, x.shape, 1)
    first = (lane % 64) < 32
    partner = jnp.where(first, pltpu.roll(x, 96, 1), pltpu.roll(x, 32, 1))
    return x * cos + partner * sin


ATT_G = ATT_HQ // ATT_HKV


def _head_cols(hk, g):
    return slice((hk * ATT_G + g) * HEAD_DIM, (hk * ATT_G + g + 1) * HEAD_DIM)


def _attention_heads(heads, store):
    s = [lax.dot_general(q.astype(BF16), kb, (((1,), (1,)), ((), ())), preferred_element_type=F32)
         for q, kb, vb in heads]
    p = [jnp.exp(si - jnp.max(si, axis=-1, keepdims=True)) for si in s]
    l = [jnp.sum(pi, axis=-1, keepdims=True) for pi in p]
    for i, (q, kb, vb) in enumerate(heads):
        o = jnp.dot(p[i].astype(BF16), vb, preferred_element_type=F32) / l[i]
        store(i, o.astype(BF16))


def _att_ctx_kernel(has_cast, *refs):
    aq_ref, akv_ref, qg_ref, kg_ref, o_ref, nk_ref, nv_ref = _hosted_cast(refs, 4, 3, has_cast)
    qg = qg_ref[...] * (HEAD_DIM ** -0.5)
    kg = kg_ref[...]
    heads = []
    for hk in range(ATT_HKV):
        ks = slice(hk * HEAD_DIM, (hk + 1) * HEAD_DIM)
        kn = _rms(akv_ref[:, ks], kg)
        v = akv_ref[:, 256 + hk * HEAD_DIM:256 + (hk + 1) * HEAD_DIM]
        nk_ref[:, ks] = kn
        nv_ref[:, ks] = v
        kb, vb = kn.astype(BF16), v.astype(BF16)
        heads += [(_rms(aq_ref[:, _head_cols(hk, g)], qg), kb, vb) for g in range(ATT_G)]

    def store(i, val):
        o_ref[:, i * HEAD_DIM:(i + 1) * HEAD_DIM] = val

    _attention_heads(heads, store)


def _att_lat_kernel(aq_ref, akv_ref, qg_ref, kg_ref, cos_ref, sin_ref, ck_ref, cv_ref, o_ref,
                    k_sc, v_sc):
    qg = qg_ref[...] * (HEAD_DIM ** -0.5)
    kg = kg_ref[...]
    qb = 256
    for hk in range(ATT_HKV):
        ks = slice(hk * HEAD_DIM, (hk + 1) * HEAD_DIM)
        k_sc[hk, 0:PAST_LEN, :] = ck_ref[:, ks].astype(BF16)
        v_sc[hk, 0:PAST_LEN, :] = cv_ref[:, ks].astype(BF16)
        kn = _rope(_rms(akv_ref[:, ks], kg), cos_ref[...], sin_ref[...])
        k_sc[hk, PAST_LEN:, :] = kn.astype(BF16)
        v_sc[hk, PAST_LEN:, :] = akv_ref[:, 256 + hk * HEAD_DIM:256 + (hk + 1) * HEAD_DIM].astype(BF16)

    def body(r, carry):
        r0 = pl.multiple_of(r * qb, qb)
        cos = cos_ref[pl.ds(r0, qb), :]
        sin = sin_ref[pl.ds(r0, qb), :]
        for hk in range(ATT_HKV):
            kb, vb = k_sc[hk], v_sc[hk]
            heads = [(_rope(_rms(aq_ref[pl.ds(r0, qb), _head_cols(hk, g)], qg), cos, sin), kb, vb)
                     for g in range(ATT_G)]

            def store(g, val, hk=hk):
                o_ref[pl.ds(r0, qb), _head_cols(hk, g)] = val

            _attention_heads(heads, store)
        return carry

    lax.fori_loop(0, DEC_SEQ // qb, body, 0)


def _att_ctx(p_main, qg, kg, n_seq, cast=None):
    L = SEQ
    c_in, c_out, c_shape, c_arg = _cast_specs(cast, n_seq)
    return pl.pallas_call(
        functools.partial(_att_ctx_kernel, cast is not None),
        grid=(n_seq,),
        in_specs=[
            pl.BlockSpec((L, 1024), lambda b: (b, P_AQ // 1024)),
            pl.BlockSpec((L, 512), lambda b: (b, P_AKV // 512)),
            pl.BlockSpec((1, HEAD_DIM), lambda b: (0, 0)),
            pl.BlockSpec((1, HEAD_DIM), lambda b: (0, 0)),
        ] + c_in,
        out_specs=[
            pl.BlockSpec((L, 1024), lambda b: (b, 0)),
            pl.BlockSpec((None, L, 256), lambda b: (b, 0, 0)),
            pl.BlockSpec((None, L, 256), lambda b: (b, 0, 0)),
        ] + c_out,
        out_shape=[
            jax.ShapeDtypeStruct((n_seq * L, 1024), BF16),
            jax.ShapeDtypeStruct((n_seq, L, 256), F32),
            jax.ShapeDtypeStruct((n_seq, L, 256), F32),
        ] + c_shape,
        compiler_params=_cparams(("arbitrary",)),
        name="att_ctx",
    )(p_main, p_main, qg, kg, *c_arg)


def _att_lat(p_main, qg, kg, cos, sin, ck, cv, layer, n_seq, row_blk0):
    L = DEC_SEQ
    return pl.pallas_call(
        _att_lat_kernel,
        grid=(n_seq,),
        in_specs=[
            pl.BlockSpec((L, 1024), lambda b: (row_blk0 + b, P_AQ // 1024)),
            pl.BlockSpec((L, 512), lambda b: (row_blk0 + b, P_AKV // 512)),
            pl.BlockSpec((1, HEAD_DIM), lambda b: (0, 0)),
            pl.BlockSpec((1, HEAD_DIM), lambda b: (0, 0)),
            pl.BlockSpec((L, HEAD_DIM), lambda b: (0, 0)),
            pl.BlockSpec((L, HEAD_DIM), lambda b: (0, 0)),
            pl.BlockSpec((None, None, PAST_LEN, 256), lambda b: (b, layer, 0, 0)),
            pl.BlockSpec((None, None, PAST_LEN, 256), lambda b: (b, layer, 0, 0)),
        ],
        out_specs=pl.BlockSpec((L, 1024), lambda b: (b, 0)),
        out_shape=jax.ShapeDtypeStruct((n_seq * L, 1024), BF16),
        scratch_shapes=[
            pltpu.VMEM((ATT_HKV, PAST_LEN + L, HEAD_DIM), BF16),
            pltpu.VMEM((ATT_HKV, PAST_LEN + L, HEAD_DIM), BF16),
        ],
        compiler_params=_cparams(("arbitrary",)),
        name="att_lat",
    )(p_main, p_main, qg, kg, cos, sin, ck, cv)


def _seqs_per_step(L, n_seq):
    return 2 if (L <= SEQ and n_seq % 2 == 0) else 1


_GLA_LEVELS = (32, 16, 8, 4, 2, 1)


def _gla_consts():
    C = CHUNK
    i = np.arange(C)[:, None]
    t = np.arange(C)[None, :]
    mats = [t <= i, t > i]
    masks = [i == t]
    for s in _GLA_LEVELS:
        m = (i // s) * s
        if s > 1:
            mats.append((t > m) & (t <= i))
        mats.append((t > i) & (t <= m + s))
        masks.append((i // (2 * s) == t // (2 * s)) & ((i // s) % 2 == 1) & ((t // s) % 2 == 0))
    fwd = np.concatenate(mats, 0)
    bwd = np.concatenate([mm[::-1, ::-1] for mm in mats], 0)
    mstack = np.stack([fwd, bwd]).astype(np.float32)
    mf = np.stack([np.concatenate([mm, mm], 0) for mm in masks])
    mb = np.stack([np.concatenate([mm[::-1, ::-1], mm[::-1, ::-1]], 0) for mm in masks])
    lmask = np.stack([mf, mb]).astype(np.float32)
    return jnp.asarray(mstack, BF16), jnp.asarray(lmask, F32)


def _hosted_cast(refs, n_in, n_out, has_cast):
    if not has_cast:
        return refs
    refs[n_in + 1 + n_out][...] = refs[n_in][...].astype(BF16)
    return refs[:n_in] + refs[n_in + 1:n_in + 1 + n_out] + refs[n_in + 2 + n_out:]


def _cast_specs(cast, n_steps):
    if cast is None:
        return [], [], [], []
    w, layer = cast
    rows, cols = w.shape[1:]
    rb = rows // n_steps
    assert rb * n_steps == rows and rb % 16 == 0
    return ([pl.BlockSpec((None, rb, cols), lambda b: (layer, b, 0))],
            [pl.BlockSpec((None, rb, cols), lambda b: (0, b, 0))],
            [jax.ShapeDtypeStruct((1, rows, cols), BF16)], [w])


def _gla_kernel(L, nb, has_state, has_cast, *refs):
    refs = _hosted_cast(refs, 7 + has_state, 2, has_cast)
    if has_state:
        (pg_ref, ps_ref, w2_ref, gb_ref, ng_ref, ms_ref, lm_ref, s0_ref,
         o_ref, so_ref, la_sc, of_sc, ob_sc, st_sc) = refs
    else:
        (pg_ref, ps_ref, w2_ref, gb_ref, ng_ref, ms_ref, lm_ref,
         o_ref, so_ref, la_sc, of_sc, ob_sc, st_sc) = refs
    n = L // CHUNK
    C = CHUNK

    ps = ps_ref[...]
    for z in range(2):
        x = _dot3(ps, w2_ref[z]) + gb_ref[z]
        la_sc[z] = (jnp.minimum(x, 0.0) - jnp.log1p(jnp.exp(-jnp.abs(x)))) * (1.0 / GLA_GATE_NORM)

    if has_state:
        for sq in range(nb):
            for z in range(2):
                for p in range(2):
                    pair = jnp.concatenate([s0_ref[sq, z, 2 * p], s0_ref[sq, z, 2 * p + 1]], axis=0)
                    st_sc[sq, z, p] = jnp.transpose(pair)
    else:
        st_sc[...] = jnp.zeros(st_sc.shape, F32)

    lane = lax.broadcasted_iota(jnp.int32, (C, LANES), 1)
    first = lane < 64

    def stack_heads(t):
        return jnp.concatenate([jnp.where(first, t, 0.0), jnp.where(first, 0.0, t)], axis=0)

    chains = [(t, sq, z, p) for t in range(2) for sq in range(nb) for z in range(2) for p in range(2)]
    nch = len(chains)
    nch1 = nch // 2

    def body(it, carry):
        def chunk_rows(t, sq, z):
            c = 2 * it + t
            cc = c if z == 0 else n - 1 - c
            return pl.ds(pl.multiple_of(sq * L + cc * C, C), C)

        rows = [chunk_rows(t, sq, z) for t, sq, z, p in chains]
        last = [C - 1, 0]
        q = [pg_ref[rows[i], p * LANES:(p + 1) * LANES] * (GLA_DK ** -0.5)
             for i, (t, sq, z, p) in enumerate(chains)]
        k = [pg_ref[rows[i], 256 + p * LANES:256 + (p + 1) * LANES]
             for i, (t, sq, z, p) in enumerate(chains)]
        e = [jnp.exp(_dot_exact_lhs_wide(ms_ref[z], la_sc[z, rows[i], p * LANES:(p + 1) * LANES]))
             for i, (t, sq, z, p) in enumerate(chains)]
        a2 = [_bdot_nt(stack_heads(q[i]), k[i]) * lm_ref[z, 0]
              for i, (t, sq, z, p) in enumerate(chains)]
        blk = 2
        for li, s in enumerate(_GLA_LEVELS):
            if s > 1:
                qs = [q[i] * e[i][blk * C:(blk + 1) * C] for i in range(nch)]
                blk += 1
            else:
                qs = q
            ks = [k[i] * e[i][blk * C:(blk + 1) * C] for i in range(nch)]
            blk += 1
            a2 = [a2[i] + _bdot_nt(stack_heads(qs[i]), ks[i]) * lm_ref[z, li + 1]
                  for i, (t, sq, z, p) in enumerate(chains)]
        kl = [k[i] * e[i][C:2 * C] for i in range(nch)]
        v = [[pg_ref[rows[i], 512 + (2 * p + hh) * LANES:512 + (2 * p + hh + 1) * LANES]
              for hh in range(2)] for i, (t, sq, z, p) in enumerate(chains)]
        intra = [[_bdot(a2[i][hh * C:(hh + 1) * C], v[i][hh]) for hh in range(2)]
                 for i in range(nch)]
        zs = [[_bdot_tn(v[i][hh], kl[i]) for hh in range(2)] for i in range(nch)]
        qe = [stack_heads(q[i] * e[i][0:C]) for i in range(nch)]
        st = [st_sc[sq, z, p] for t, sq, z, p in chains[0:nch1]]
        for i, (t, sq, z, p) in enumerate(chains):
            j = i % nch1
            inter = _bdot_nt(qe[i], st[j])
            osc = of_sc if z == 0 else ob_sc
            for hh in range(2):
                h = 2 * p + hh
                osc[rows[i], h * LANES:(h + 1) * LANES] = intra[i][hh] + inter[hh * C:(hh + 1) * C]
            st[j] = (st[j] * e[i][last[z]:last[z] + 1, :]
                     + jnp.where(first[0:1], zs[i][0], zs[i][1]))
        for j, (t, sq, z, p) in enumerate(chains[0:nch1]):
            st_sc[sq, z, p] = st[j]
        return carry

    lax.fori_loop(0, n // 2, body, 0)

    ng = ng_ref[...]

    def fin(c, carry):
        r0 = pl.multiple_of(c * C, C)
        rows = pl.ds(r0, C)
        for h in range(GLA_H):
            cols = slice(h * LANES, (h + 1) * LANES)
            o = of_sc[rows, cols] + ob_sc[rows, cols]
            gate = _silu(pg_ref[rows, 1024 + h * LANES:1024 + (h + 1) * LANES])
            o_ref[rows, cols] = (_rms(o, ng) * gate).astype(BF16)
        return carry

    lax.fori_loop(0, nb * n, fin, 0)
    for sq in range(nb):
        for z in range(2):
            for p in range(2):
                pair = jnp.transpose(st_sc[sq, z, p])
                so_ref[sq, z, 2 * p] = pair[0:GLA_DK]
                so_ref[sq, z, 2 * p + 1] = pair[GLA_DK:2 * GLA_DK]


def _gla(p_main, p_small, w2p, gbias, ng, consts, s0, layer, L, n_seq, row_blk0, cast=None):
    mstack, lmask = consts
    has_state = s0 is not None
    nb = _seqs_per_step(L, n_seq)
    assert row_blk0 % nb == 0
    blk0 = row_blk0 // nb
    state_blk = (nb, 2, GLA_H, GLA_DK, GLA_DV)
    in_specs = [
        pl.BlockSpec((nb * L, 1536), lambda b: (blk0 + b, P_GLA // 1536)),
        pl.BlockSpec((nb * L, LANES), lambda b: (blk0 + b, 0)),
        pl.BlockSpec((2, LANES, 256), lambda b: (0, 0, 0)),
        pl.BlockSpec((2, 1, 256), lambda b: (0, 0, 0)),
        pl.BlockSpec((1, GLA_DV), lambda b: (0, 0)),
        pl.BlockSpec(mstack.shape, lambda b: (0, 0, 0)),
        pl.BlockSpec(lmask.shape, lambda b: (0, 0, 0, 0)),
    ]
    args = [p_main, p_small, w2p, gbias, ng, mstack, lmask]
    if has_state:
        in_specs.append(pl.BlockSpec((nb, None) + state_blk[1:], lambda b: (b, layer, 0, 0, 0, 0)))
        args.append(s0)
    c_in, c_out, c_shape, c_arg = _cast_specs(cast, n_seq // nb)
    return pl.pallas_call(
        functools.partial(_gla_kernel, L, nb, has_state, cast is not None),
        grid=(n_seq // nb,),
        in_specs=in_specs + c_in,
        out_specs=[
            pl.BlockSpec((nb * L, 512), lambda b: (b, 0)),
            pl.BlockSpec(state_blk, lambda b: (b, 0, 0, 0, 0)),
        ] + c_out,
        out_shape=[
            jax.ShapeDtypeStruct((n_seq * L, 512), BF16),
            jax.ShapeDtypeStruct((n_seq, 2, GLA_H, GLA_DK, GLA_DV), F32),
        ] + c_shape,
        scratch_shapes=[
            pltpu.VMEM((2, nb * L, 256), F32),
            pltpu.VMEM((nb * L, 512), F32),
            pltpu.VMEM((nb * L, 512), F32),
            pltpu.VMEM((nb, 2, 2, GLA_DV, LANES), F32),
        ],
        compiler_params=_cparams(("arbitrary",)),
        name="gla_lat" if has_state else "gla_ctx",
    )(*args, *c_arg)


def _dn_kernel(L, nb, has_state, has_cast, *refs):
    refs = _hosted_cast(refs, 7 + has_state, 2, has_cast)
    if has_state:
        (pd_ref, dz_ref, ps_ref, cw_ref, al_ref, dtb_ref, ng_ref, s0_ref,
         o_ref, so_ref, qkv_sc, g_sc, bb_sc, of_sc, ob_sc, s_sc, prep_sc, att_sc, dec_sc) = refs
    else:
        (pd_ref, dz_ref, ps_ref, cw_ref, al_ref, dtb_ref, ng_ref,
         o_ref, so_ref, qkv_sc, g_sc, bb_sc, of_sc, ob_sc, s_sc, prep_sc, att_sc, dec_sc) = refs
    n = L // CHUNK
    C = CHUNK

    R = nb * L
    pos = lax.broadcasted_iota(jnp.int32, (R, LANES), 0) & (L - 1)
    not_first = pos > 0
    not_last = pos < L - 1

    def conv_silu(j):
        c0 = pl.multiple_of(j * LANES, LANES)
        x = pd_ref[:, pl.ds(c0, LANES)]
        w = cw_ref[:, pl.ds(c0, LANES)]
        prev = jnp.where(not_first, pltpu.roll(x, 1, 0), 0.0)
        nxt = jnp.where(not_last, pltpu.roll(x, R - 1, 0), 0.0)
        return c0, _silu(prev * w[0:1] + x * w[1:2] + nxt * w[2:3])

    def qk_tile(j, carry):
        c0, y = conv_silu(j)
        inv = lax.rsqrt(jnp.sum(y * y, axis=-1, keepdims=True) + EPS)
        qkv_sc[:, pl.ds(c0, LANES)] = y * (inv * jnp.where(j < DN_H, DN_DK ** -0.5, 1.0))
        return carry

    def v_tile(j, carry):
        c0, y = conv_silu(j)
        qkv_sc[:, pl.ds(c0, LANES)] = y
        return carry

    lax.fori_loop(0, 2 * DN_H, qk_tile, 0)
    lax.fori_loop(2 * DN_H, 3 * DN_H, v_tile, 0)

    ps = ps_ref[...]
    lane_l = lax.broadcasted_iota(jnp.int32, (R, LANES), 1)
    g_sc[...] = -jnp.exp(al_ref[...]) * _softplus(ps + dtb_ref[...])
    b_all = _sigmoid(ps)
    for idx in range(2 * DN_H):
        bcol = jnp.sum(jnp.where(lane_l == S_DB + idx, b_all, 0.0), axis=-1, keepdims=True)
        bb_sc[idx] = jnp.broadcast_to(bcol, (R, LANES))

    if has_state:
        s_sc[...] = s0_ref[...]
    else:
        s_sc[...] = jnp.zeros(s_sc.shape, F32)

    ri = lax.broadcasted_iota(jnp.int32, (C, LANES), 0)
    lane_c = lax.broadcasted_iota(jnp.int32, (C, LANES), 1)
    ci = lane_c & (C - 1)
    first = lane_c < C
    incl = [ci <= ri, ci >= ri]
    strict = [ci < ri, ci > ri]
    rs = lax.broadcasted_iota(jnp.int32, (C, C), 0)
    cs = lax.broadcasted_iota(jnp.int32, (C, C), 1)
    m_b = [(cs <= rs).astype(BF16), (cs >= rs).astype(BF16)]
    nu = 2 * nb
    chains = [(sq, z, h) for sq in range(nu) for z in range(2) for h in range(DN_H)]
    nch = len(chains)
    pairs = [(sq, z, p) for sq in range(nu) for z in range(2) for p in range(DN_H // 2)]
    npr = len(pairs)

    def stack_heads(t):
        return jnp.concatenate([jnp.where(first, t, 0.0), jnp.where(first, 0.0, t)], axis=0)

    def hi_mid(x):
        hi = x.astype(BF16).astype(F32)
        return hi, x - hi

    def pair_products(lhs_list, x):
        xh, xm = hi_mid(x)
        rhs = jnp.concatenate([stack_heads(xh), stack_heads(xm)], axis=1).astype(BF16)
        parts = []
        for l in lhs_list:
            parts.extend(hi_mid(l))
        r = jnp.dot(jnp.concatenate(parts, axis=0).astype(BF16), rhs, preferred_element_type=F32)
        out = []
        for i in range(len(lhs_list)):
            blk = r[2 * C * i:2 * C * i + C] + r[2 * C * i + C:2 * C * (i + 1)]
            out.append(blk[:, 0:LANES] + blk[:, LANES:2 * LANES])
        return out

    heads = [((2 * sq + z) * DN_H + 2 * p, (2 * sq + z) * DN_H + 2 * p + 1)
             for sq, z, p in pairs]

    def chunk_rows(cs):
        def one(sq, z):
            c = cs[sq // nb]
            start = (sq % nb) * L + (c if z == 0 else n - 1 - c) * C
            return pl.ds(start if isinstance(start, int) else pl.multiple_of(start, C), C)
        return [one(sq, z) for sq in range(nu) for z in range(2)]

    def prepare_products(cs):
        rows = chunk_rows(cs)
        last = [C - 1, 0]

        def ld(base, sq, z, h):
            return qkv_sc[rows[2 * sq + z], base + h * LANES:base + (h + 1) * LANES]

        q = [ld(0, *ch) for ch in chains]
        k = [ld(512, *ch) for ch in chains]
        v = [ld(1024, *ch) for ch in chains]
        bb = [bb_sc[z * DN_H + h, rows[2 * sq + z], :] for sq, z, h in chains]
        gc = [_dot_exact_lhs(m_b[z], g_sc[rows[2 * sq + z], :])
              for sq in range(nu) for z in range(2)]
        gct =[jnp.transpose(jnp.concatenate([g, g], axis=0)) for g in gc]
        gi = [jnp.broadcast_to(jnp.sum(jnp.where(lane_c == S_DA + z * DN_H + h, gc[2 * sq + z], 0.0),
                                       axis=-1, keepdims=True), (C, LANES)) for sq, z, h in chains]
        gjrow = [gct[2 * sq + z][S_DA + z * DN_H + h:S_DA + z * DN_H + h + 1, :]
                 for sq, z, h in chains]
        gam = []
        for (sq, z, p), (i0, i1) in zip(pairs, heads):
            diff = jnp.where(first, gi[i0], gi[i1]) - jnp.where(first[0:1], gjrow[i0], gjrow[i1])
            gam.append(jnp.where(incl[z], jnp.exp(jnp.where(incl[z], diff, 0.0)), 0.0))
        kb = [k[i] * bb[i] for i in range(nch)]
        kq = [_bdot_nt(jnp.concatenate([kb[i0], q[i0], kb[i1], q[i1]], axis=0),
                       jnp.concatenate([k[i0], k[i1]], axis=0)) for i0, i1 in heads]
        gt = [gi[i][last[z]:last[z] + 1, :] for i, (sq, z, h) in enumerate(chains)]
        return q, k, v, bb, kb, gi, gt, gam, kq

    def prepare_finish(parts):
        q, k, v, bb, kb, gi, gt, gam, kq = parts
        a = [jnp.where(strict[z], jnp.where(first, kq[j][0:C], kq[j][2 * C:3 * C]) * gam[j], 0.0)
             for j, (sq, z, p) in enumerate(pairs)]
        att = [jnp.where(first, kq[j][C:2 * C], kq[j][3 * C:4 * C]) * gam[j] for j in range(npr)]
        nt = [-a[j] for j in range(npr)]
        pw = [pair_products([a[j]], a[j])[0] for j in range(npr)]
        for step in range(5):
            if step < 4:
                both = [pair_products([nt[j], pw[j]], pw[j]) for j in range(npr)]
                nt = [nt[j] + pw[j] + both[j][0] for j in range(npr)]
                pw = [both[j][1] for j in range(npr)]
            else:
                nt = [nt[j] + pw[j] + pair_products([nt[j]], pw[j])[0] for j in range(npr)]
        eg = [jnp.exp(gi[i]) for i in range(nch)]
        rhs = [jnp.concatenate([v[i] * bb[i], kb[i] * eg[i]], axis=1) for i in range(nch)]
        corr = [_bdot(stack_heads(nt[j]), jnp.concatenate([rhs[i0], rhs[i1]], axis=0))
                for j, (i0, i1) in enumerate(heads)]
        for j, (i0, i1) in enumerate(heads):
            for half, i in enumerate((i0, i1)):
                uw = rhs[i] + corr[j][half * C:(half + 1) * C]
                prep_sc[i, 0] = uw[:, 0:LANES]
                prep_sc[i, 1] = uw[:, LANES:]
                prep_sc[i, 2] = q[i] * eg[i]
                prep_sc[i, 3] = k[i] * jnp.exp(gt[i] - gi[i])
                dec_sc[i] = jnp.broadcast_to(jnp.exp(gt[i]), (8, LANES))
            att_sc[j] = att[j]

    nch1, npr1 = nch // 2, npr // 2

    def apply_state_products(slot, s):
        return [_bdot(jnp.concatenate([prep_sc[slot * nch1 + i, 1], prep_sc[slot * nch1 + i, 2]],
                                      axis=0), s[i]) for i in range(nch1)]

    def apply_finish(rows, slot, s, ws):
        base = slot * nch1
        v_new = [prep_sc[base + i, 0] - ws[i][0:C] for i in range(nch1)]
        av = []
        for j in range(npr1):
            i0, i1 = heads[slot * npr1 + j]
            av.append(_bdot(stack_heads(att_sc[slot * npr1 + j]),
                            jnp.concatenate([v_new[i0 - base], v_new[i1 - base]], axis=0)))
        upd = [_bdot_tn(prep_sc[base + i, 3], v_new[i]) for i in range(nch1)]
        for j in range(npr1):
            for half, ig in enumerate(heads[slot * npr1 + j]):
                sq, z, h = chains[ig]
                osc = of_sc if z == 0 else ob_sc
                osc[rows[2 * sq + z], h * LANES:(h + 1) * LANES] = (
                    ws[ig - base][C:2 * C] + av[j][half * C:(half + 1) * C])
        return [s[i] * dec_sc[base + i, 0:1, :] + upd[i] for i in range(nch1)]

    def apply_two(cs, parts_fn):
        rows = chunk_rows(cs)
        s = [s_sc[sq, z, h] for sq, z, h in chains[0:nch1]]
        ws = apply_state_products(0, s)
        parts = parts_fn()
        s = apply_finish(rows, 0, s, ws)
        ws = apply_state_products(1, s)
        s = apply_finish(rows, 1, s, ws)
        for i, (sq, z, h) in enumerate(chains[0:nch1]):
            s_sc[sq, z, h] = s[i]
        return parts

    prepare_finish(prepare_products((0, 1)))

    def body(it, carry):
        c = 2 * it
        parts = apply_two((c, c + 1), lambda: prepare_products((c + 2, c + 3)))
        prepare_finish(parts)
        return carry

    lax.fori_loop(0, n // 2 - 1, body, 0)
    apply_two((n - 2, n - 1), lambda: None)

    ng = ng_ref[...]

    def fin(c, carry):
        r0 = pl.multiple_of(c * C, C)
        rows = pl.ds(r0, C)
        for h in range(DN_H):
            cols = slice(h * LANES, (h + 1) * LANES)
            o = of_sc[rows, cols] + ob_sc[rows, cols]
            o_ref[rows, cols] = (_rms(o, ng) * _silu(dz_ref[rows, cols])).astype(BF16)
        return carry

    lax.fori_loop(0, nb * n, fin, 0)
    so_ref[...] = s_sc[...]


def _dn(p_main, p_small, conv_w, alog_row, dtb_row, ng, s0, layer, L, n_seq, row_blk0, cast=None):
    has_state = s0 is not None
    nb = _seqs_per_step(L, n_seq)
    assert row_blk0 % nb == 0
    row_blk0, L_blk = row_blk0 // nb, nb * L
    in_specs = [
        pl.BlockSpec((L_blk, 1536), lambda b: (row_blk0 + b, P_DN // 1536)),
        pl.BlockSpec((L_blk, 512), lambda b: (row_blk0 + b, P_DZ // 512)),
        pl.BlockSpec((L_blk, LANES), lambda b: (row_blk0 + b, 0)),
        pl.BlockSpec((DN_CONV, 3 * DN_H * DN_DK), lambda b: (0, 0)),
        pl.BlockSpec((1, LANES), lambda b: (0, 0)),
        pl.BlockSpec((1, LANES), lambda b: (0, 0)),
        pl.BlockSpec((1, DN_DV), lambda b: (0, 0)),
    ]
    args = [p_main, p_main, p_small, conv_w, alog_row, dtb_row, ng]
    if has_state:
        in_specs.append(pl.BlockSpec((nb, None, 2, DN_H, DN_DK, DN_DV),
                                     lambda b: (b, layer, 0, 0, 0, 0)))
        args.append(s0)
    c_in, c_out, c_shape, c_arg = _cast_specs(cast, n_seq // nb)
    return pl.pallas_call(
        functools.partial(_dn_kernel, L, nb, has_state, cast is not None),
        grid=(n_seq // nb,),
        in_specs=in_specs + c_in,
        out_specs=[
            pl.BlockSpec((L_blk, 512), lambda b: (b, 0)),
            pl.BlockSpec((nb, 2, DN_H, DN_DK, DN_DV), lambda b: (b, 0, 0, 0, 0)),
        ] + c_out,
        out_shape=[
            jax.ShapeDtypeStruct((n_seq * L, 512), BF16),
            jax.ShapeDtypeStruct((n_seq, 2, DN_H, DN_DK, DN_DV), F32),
        ] + c_shape,
        scratch_shapes=[
            pltpu.VMEM((L_blk, 1536), F32),
            pltpu.VMEM((L_blk, LANES), F32),
            pltpu.VMEM((2 * DN_H, L_blk, LANES), F32),
            pltpu.VMEM((L_blk, 512), F32),
            pltpu.VMEM((L_blk, 512), F32),
            pltpu.VMEM((nb, 2, DN_H, DN_DK, DN_DV), F32),
            pltpu.VMEM((2 * nb * 2 * DN_H, 4, CHUNK, LANES), F32),
            pltpu.VMEM((2 * nb * DN_H, CHUNK, LANES), F32),
            pltpu.VMEM((2 * nb * 2 * DN_H, 8, LANES), F32),
        ],
        compiler_params=_cparams(("arbitrary",)),
        name="dn_lat" if has_state else "dn_ctx",
    )(*args, *c_arg)


_W_IN_SPLITS = (256, 256, 512, 512, 32, 1024, 256, 256, 1536, 512, 8, 8)


def _regroup_kernel(wt_ref, main_ref, small_ref):
    off = np.cumsum((0,) + _W_IN_SPLITS)
    gq, gk, gv, gr, glr, aq, ak, av, dqkv, dz, da, db = [
        wt_ref[int(off[i]):int(off[i + 1]), :] for i in range(12)]
    main_t = jnp.concatenate([aq, ak, av, gq, gk, gv, gr, dqkv, dz], axis=0)
    main_ref[...] = jnp.transpose(main_t).astype(BF16)
    pad = jnp.zeros((LANES - 48, wt_ref.shape[1]), F32)
    small_t = jnp.concatenate([glr, da, db, pad], axis=0)
    small_ref[...] = jnp.transpose(small_t).astype(BF16)


def _prep_w_in(w_in, tr=256):
    depth, _, in_cols = w_in.shape
    return pl.pallas_call(
        _regroup_kernel,
        grid=(depth, D_MODEL // tr),
        in_specs=[pl.BlockSpec((None, in_cols, tr), lambda l, r: (l, 0, r))],
        out_specs=[
            pl.BlockSpec((None, tr, P_MAIN), lambda l, r: (l, r, 0)),
            pl.BlockSpec((None, tr, LANES), lambda l, r: (l, r, 0)),
        ],
        out_shape=[
            jax.ShapeDtypeStruct((depth, D_MODEL, P_MAIN), BF16),
            jax.ShapeDtypeStruct((depth, D_MODEL, LANES), BF16),
        ],
        compiler_params=_cparams(("arbitrary", "arbitrary")),
        name="w_in_regroup",
    )(jnp.swapaxes(w_in, 1, 2))


def _lane_row(vals, offset):
    return jnp.zeros((1, LANES), F32).at[0, offset:offset + vals.shape[0]].set(vals.astype(F32))


def kernel(x_prompt, x_sample, cache_k, cache_v, state_gla, state_dn, c, c_ctx, norm1_g, norm2_g,
           w_mod, b_mod, w_in, gla_w2, gla_b, gla_norm_g, q_norm_g, k_norm_g, dn_conv, dn_a_log,
           dn_dt_bias, dn_norm_g, w_out, w_ff1, w_ff2):
    n_ctx, n_lat = x_prompt.shape[0], x_sample.shape[0]
    depth = w_in.shape[0]
    n_ctx_rows = n_ctx * SEQ
    assert n_ctx_rows % DEC_SEQ == 0 and n_lat <= 7

    xs = [x_prompt.reshape(n_ctx_rows, D_MODEL), x_sample.reshape(n_lat * DEC_SEQ, D_MODEL)]
    cond8 = jnp.concatenate([c_ctx[None, :], c, jnp.zeros((7 - n_lat, D_MODEL), F32)], axis=0)
    b_mod3 = b_mod.reshape(depth, 1, N_MOD * D_MODEL)
    mods_l = _modulation(cond8, w_mod, b_mod3, 1).reshape(8, N_MOD, D_MODEL)

    cos, sin = _rope_tables()
    gla_consts = _gla_consts()
    ck = cache_k.reshape(n_lat, depth, PAST_LEN, ATT_HKV * HEAD_DIM)
    cv = cache_v.reshape(n_lat, depth, PAST_LEN, ATT_HKV * HEAD_DIM)

    w_main, w_small = _prep_w_in(w_in)
    w_out_b, w_ff1_b, w_ff2_b = None, None, None

    nks, nvs, sgs, sds = [], [], [], []
    for l in range(depth):
        tc, tl = n_ctx_rows // INPROJ_TM, n_lat * DEC_SEQ // INPROJ_TM
        inproj = functools.partial(_inproj, mods_l=mods_l, g1=norm1_g[l][None, :], w_main=w_main,
                                   w_small=w_small, layer=l, n_ctx_rows=n_ctx_rows)
        pc, psc = inproj(xs[0], tile0=0, n_tiles=tc, grp_tile0=0)
        if len(xs) == 2:
            pl_, psl = inproj(xs[1], tile0=0, n_tiles=tl, grp_tile0=tc)
        else:
            pl_, psl = inproj(xs[0], tile0=tc, n_tiles=tl, grp_tile0=tc)

        qg, kg = q_norm_g[l][None, :], k_norm_g[l][None, :]
        oa_c, nk, nv, *hosted = _att_ctx(pc, qg, kg, n_ctx, cast=(w_out, 0) if l == 0 else None)
        if hosted:
            w_out_b = hosted[0]
        oa_l = _att_lat(pl_, qg, kg, cos, sin, ck, cv, l, n_lat, 0)

        w2p = jnp.zeros((2, LANES, GLA_H * GLA_DK), F32)
        w2p = w2p.at[0, 0:GLA_LR].set(gla_w2[l, 0]).at[1, GLA_LR:2 * GLA_LR].set(gla_w2[l, 1])
        gbias = gla_b[l][:, None, :]
        gng = gla_norm_g[l][None, :]
        og_c, sg, *hosted = _gla(pc, psc, w2p, gbias, gng, gla_consts, None, l, SEQ, n_ctx, 0,
                                 cast=(w_ff1, 0) if l == 0 else None)
        if hosted:
            w_ff1_b = hosted[0]
        og_l, _ = _gla(pl_, psl, w2p, gbias, gng, gla_consts, state_gla, l, DEC_SEQ, n_lat, 0)

        alog_row = _lane_row(dn_a_log[l].reshape(-1), S_DA)
        dtb_row = _lane_row(dn_dt_bias[l].reshape(-1), S_DA)
        dng = dn_norm_g[l][None, :]
        od_c, sd, *hosted = _dn(pc, psc, dn_conv[l], alog_row, dtb_row, dng, None, l, SEQ, n_ctx, 0,
                                cast=(w_ff2, 0) if l == 0 else None)
        if hosted:
            w_ff2_b = hosted[0]
        od_l, _ = _dn(pl_, psl, dn_conv[l], alog_row, dtb_row, dng, state_dn, l, DEC_SEQ, n_lat, 0)

        x1, h2 = _outproj(xs, [og_c, og_l], [oa_c, oa_l], [od_c, od_l], mods_l,
                          norm2_g[l][None, :], w_out_b, 0, n_ctx_rows)
        n_tiles, nct = x1.shape[0] // FFN_TM, n_ctx_rows // FFN_TM
        ffn = functools.partial(_ffn, h2, x1, mods_l, w_ff1_b, w_ff2_b, n_ctx_rows, tm=FFN_TM)
        if l < depth - 1:
            y, w_ff1_b, w_ff2_b, w_out_b, m_next = ffn(
                tile0=0, n_tiles=n_tiles, next_layer=l + 1, cast_next=(w_ff1, w_ff2, w_out),
                mod_next=(cond8, w_mod, b_mod3))
            xs, mods_l = [y], m_next.reshape(8, N_MOD, D_MODEL)
        else:
            xs = [ffn(tile0=0, n_tiles=nct)[0], ffn(tile0=nct, n_tiles=n_tiles - nct)[0]]

        nks.append(nk.reshape(n_ctx, SEQ, ATT_HKV, HEAD_DIM))
        nvs.append(nv.reshape(n_ctx, SEQ, ATT_HKV, HEAD_DIM))
        sgs.append(sg)
        sds.append(sd)

    y_prompt = xs[0].reshape(n_ctx, SEQ, D_MODEL)
    y_sample = xs[1].reshape(n_lat, DEC_SEQ, D_MODEL)
    return (y_prompt, y_sample, jnp.stack(nks, axis=1), jnp.stack(nvs, axis=1),
            jnp.stack(sgs, axis=1), jnp.stack(sds, axis=1))
```

```python
import functools
import math

import numpy as np
import jax
import jax.numpy as jnp
from jax import lax
from jax.experimental import pallas as pl
from jax.experimental.pallas import tpu as pltpu

F32 = jnp.float32
BF16 = jnp.bfloat16

D_MODEL = 2048
SEQ = 256
DEC_SEQ = 1024
PAST_LEN = 256
GRID_W = 64
HEAD_DIM = 128
EPS = 1e-6
GLA_H = 4
GLA_DK = 64
GLA_DV = 128
GLA_LR = 16
GLA_GATE_NORM = 16.0
ATT_HQ = 8
ATT_HKV = 2
ROPE_THETA = 10000.0
DN_H = 4
DN_DK = 128
DN_DV = 128
DN_CONV = 3
CHUNK = 64
FF = 4 * D_MODEL
N_MOD = 6
LANES = 128

P_AQ = 0
P_AKV = 1024
P_GLA = 1536
P_DN = 3072
P_DZ = 4608
P_MAIN = 5120
S_DA = 32
S_DB = 40

VMEM_LIMIT = 56 * 1024 * 1024


def _cparams(sem):
    return pltpu.CompilerParams(dimension_semantics=sem, vmem_limit_bytes=VMEM_LIMIT)


def _bdot(a, b):
    return jnp.dot(a.astype(BF16), b.astype(BF16), preferred_element_type=F32)


def _bdot_nt(a, b):
    return lax.dot_general(a.astype(BF16), b.astype(BF16), (((1,), (1,)), ((), ())),
                           preferred_element_type=F32)


def _bdot_tn(a, b):
    return lax.dot_general(a.astype(BF16), b.astype(BF16), (((0,), (0,)), ((), ())),
                           preferred_element_type=F32)


def _split3(x):
    hi = x.astype(BF16)
    r = x - hi.astype(F32)
    mid = r.astype(BF16)
    lo = (r - mid.astype(F32)).astype(BF16)
    return hi, mid, lo


def _dot_exact_lhs(lhs_bf16, x):
    hi, mid, lo = _split3(x)
    d = functools.partial(jnp.dot, preferred_element_type=F32)
    return d(lhs_bf16, hi) + d(lhs_bf16, mid) + d(lhs_bf16, lo)


def _dot_exact_lhs_wide(lhs_bf16, x):
    n = x.shape[1]
    hi = x.astype(BF16)
    mid = (x - hi.astype(F32)).astype(BF16)
    r = jnp.dot(lhs_bf16, jnp.concatenate([hi, mid], axis=1), preferred_element_type=F32)
    return r[:, 0:n] + r[:, n:2 * n]


def _dot3(a, b):
    ah, am, _ = _split3(a)
    bh, bm, _ = _split3(b)
    d = functools.partial(jnp.dot, preferred_element_type=F32)
    return d(ah, bh) + (d(ah, bm) + d(am, bh))


def _sigmoid(x):
    return 0.5 * jnp.tanh(0.5 * x) + 0.5


def _silu(x):
    return x * _sigmoid(x)


def _softplus(x):
    return jnp.maximum(x, 0.0) + jnp.log1p(jnp.exp(-jnp.abs(x)))


def _rms(x, g):
    return x * lax.rsqrt(jnp.mean(x * x, axis=-1, keepdims=True) + EPS) * g


def _mods_kernel(c_ref, w_ref, b_ref, o_ref):
    o_ref[...] = _bdot(_silu(c_ref[...]), w_ref[...]) + b_ref[...]


def _modulation(cond8, w_mod, b_mod3, depth):
    n_out = N_MOD * D_MODEL
    tn = 1024
    return pl.pallas_call(
        _mods_kernel,
        grid=(depth, n_out // tn),
        in_specs=[
            pl.BlockSpec((8, D_MODEL), lambda l, j: (0, 0)),
            pl.BlockSpec((None, D_MODEL, tn), lambda l, j: (l, 0, j)),
            pl.BlockSpec((None, 1, tn), lambda l, j: (l, 0, j)),
        ],
        out_specs=pl.BlockSpec((None, 8, tn), lambda l, j: (l, 0, j)),
        out_shape=jax.ShapeDtypeStruct((depth, 8, n_out), F32),
        compiler_params=_cparams(("arbitrary", "arbitrary")),
        name="modulation",
    )(cond8, w_mod, b_mod3)


def _group_of_tile(i, tm, n_ctx_rows):
    n_ctx_tiles = n_ctx_rows // tm
    per_lat = DEC_SEQ // tm
    return jnp.where(i < n_ctx_tiles, 0, 1 + (i - n_ctx_tiles) // per_lat)


def _row_specs(arrays, width, tm, nct):
    if len(arrays) == 1:
        return [pl.BlockSpec((tm, width), lambda i, *_: (i, 0))]
    return [pl.BlockSpec((tm, width), lambda i, *_: (jnp.minimum(i, nct - 1), 0)),
            pl.BlockSpec((tm, width), lambda i, *_: (jnp.maximum(i - nct, 0), 0))]


def _row_load(refs, nct, rows=slice(None)):
    if len(refs) == 1:
        return refs[0][rows, :]
    return jnp.where(pl.program_id(0) < nct, refs[0][rows, :], refs[1][rows, :])


INPROJ_TM = 1024


def _inproj_kernel(x_ref, mod_ref, g_ref, w_ref, ws_ref, p_ref, ps_ref, h_sc):
    @pl.when(pl.program_id(1) == 0)
    def _():
        m = mod_ref[...]
        h = _rms(x_ref[...], g_ref[...]) * (1.0 + m[1:2]) + m[0:1]
        hb = h.astype(BF16)
        h_sc[...] = hb
        ps_ref[...] = jnp.dot(hb, ws_ref[...], preferred_element_type=F32)

    h = h_sc[...]
    half = w_ref.shape[1] // 2
    for piece in range(2):
        cols = slice(piece * half, (piece + 1) * half)
        p_ref[:, cols] = jnp.dot(h, w_ref[:, cols], preferred_element_type=F32)


def _inproj(x, mods_l, g1, w_main, w_small, layer, n_ctx_rows, tile0, n_tiles, grp_tile0, tn=1024):
    tm = INPROJ_TM
    grp = functools.partial(_group_of_tile, tm=tm, n_ctx_rows=n_ctx_rows)
    return pl.pallas_call(
        _inproj_kernel,
        grid=(n_tiles, P_MAIN // tn),
        in_specs=[
            pl.BlockSpec((tm, D_MODEL), lambda i, j: (tile0 + i, 0)),
            pl.BlockSpec((None, N_MOD, D_MODEL), lambda i, j: (grp(grp_tile0 + i), 0, 0)),
            pl.BlockSpec((1, D_MODEL), lambda i, j: (0, 0)),
            pl.BlockSpec((None, D_MODEL, tn), lambda i, j: (layer, 0, j)),
            pl.BlockSpec((None, D_MODEL, LANES), lambda i, j: (layer, 0, 0)),
        ],
        out_specs=[
            pl.BlockSpec((tm, tn), lambda i, j: (i, j)),
            pl.BlockSpec((tm, LANES), lambda i, j: (i, 0)),
        ],
        out_shape=[
            jax.ShapeDtypeStruct((n_tiles * tm, P_MAIN), F32),
            jax.ShapeDtypeStruct((n_tiles * tm, LANES), F32),
        ],
        scratch_shapes=[pltpu.VMEM((tm, D_MODEL), BF16)],
        compiler_params=_cparams(("arbitrary", "arbitrary")),
        name="inproj",
    )(x, mods_l, g1, w_main, w_small)


def _outproj_kernel(nct, n_x, *refs):
    x_refs = refs[:n_x]
    (ogc_ref, ogl_ref, oac_ref, oal_ref, odc_ref, odl_ref, mod_ref, g_ref, w_ref,
     x1_ref, h2_ref) = refs[n_x:]
    m = mod_ref[...]
    d = functools.partial(jnp.dot, preferred_element_type=F32)
    half = x1_ref.shape[0] // 2
    for r in range(2):
        rows = slice(r * half, (r + 1) * half)
        mix = (d(_row_load((ogc_ref, ogl_ref), nct, rows), w_ref[0:512, :])
               + d(_row_load((oac_ref, oal_ref), nct, rows), w_ref[512:1536, :])
               + d(_row_load((odc_ref, odl_ref), nct, rows), w_ref[1536:2048, :]))
        x1 = _row_load(x_refs, nct, rows) + m[2:3] * mix
        x1_ref[rows, :] = x1
        h2_ref[rows, :] = (_rms(x1, g_ref[...]) * (1.0 + m[4:5]) + m[3:4]).astype(BF16)


def _outproj(xs, o_gla, o_att, o_dn, mods_l, g2, w_out, layer, n_ctx_rows, tm=512):
    rows = sum(x.shape[0] for x in xs)
    nct = n_ctx_rows // tm
    grp = functools.partial(_group_of_tile, tm=tm, n_ctx_rows=n_ctx_rows)
    return pl.pallas_call(
        functools.partial(_outproj_kernel, nct, len(xs)),
        grid=(rows // tm,),
        in_specs=(_row_specs(xs, D_MODEL, tm, nct) + _row_specs(o_gla, 512, tm, nct)
                  + _row_specs(o_att, 1024, tm, nct) + _row_specs(o_dn, 512, tm, nct) + [
                      pl.BlockSpec((None, N_MOD, D_MODEL), lambda i: (grp(i), 0, 0)),
                      pl.BlockSpec((1, D_MODEL), lambda i: (0, 0)),
                      pl.BlockSpec((None, D_MODEL, D_MODEL), lambda i: (layer, 0, 0),
                                   pipeline_mode=pl.Buffered(1)),
                  ]),
        out_specs=[
            pl.BlockSpec((tm, D_MODEL), lambda i: (i, 0)),
            pl.BlockSpec((tm, D_MODEL), lambda i: (i, 0)),
        ],
        out_shape=[
            jax.ShapeDtypeStruct((rows, D_MODEL), F32),
            jax.ShapeDtypeStruct((rows, D_MODEL), BF16),
        ],
        compiler_params=_cparams(("arbitrary",)),
        name="outproj",
    )(*xs, *o_gla, *o_att, *o_dn, mods_l, g2, w_out)


FFN_TM = 512
FFN_OUT_CHUNK = 512


def _ffn_kernel(n_cast, has_mod, h_ref, x1_ref, mod_ref, w1_ref, w2_ref, *refs):
    n_in = n_cast + (3 if has_mod else 0)
    y_ref = refs[n_in]
    for src, dst in zip(refs[:n_cast], refs[n_in + 1:n_in + 1 + n_cast]):
        dst[...] = src[...].astype(BF16)
    if has_mod:
        c_ref, wm_ref, bm_ref = refs[n_cast:n_in]
        refs[n_in + 1 + n_cast][...] = _bdot(_silu(c_ref[...]), wm_ref[...]) + bm_ref[...]
    j = pl.program_id(1)
    tf = w1_ref.shape[1]

    @pl.when(j == 0)
    def _():
        y_ref[...] = jnp.zeros(y_ref.shape, F32)

    h = h_ref[...]
    for half in range(2):
        cols = slice(half * (tf // 2), (half + 1) * (tf // 2))
        t = jnp.dot(h, w1_ref[:, cols], preferred_element_type=F32)
        t = jnp.square(jnp.maximum(t, 0.0)).astype(BF16)
        for nc in range(D_MODEL // FFN_OUT_CHUNK):
            ncs = slice(nc * FFN_OUT_CHUNK, (nc + 1) * FFN_OUT_CHUNK)
            y_ref[:, ncs] += jnp.dot(t, w2_ref[cols, ncs], preferred_element_type=F32)

    @pl.when(j == pl.num_programs(1) - 1)
    def _():
        y_ref[...] = x1_ref[...] + mod_ref[5:6, :] * y_ref[...]


def _ffn(h2, x1, mods_l, w1, w2, n_ctx_rows, tile0, n_tiles, tm, cast_next=(), mod_next=(),
         next_layer=0, tf=1024):
    grp = functools.partial(_group_of_tile, tm=tm, n_ctx_rows=n_ctx_rows)
    nj = FF // tf
    n_steps = n_tiles * nj
    cast_in, cast_out, cast_shape = [], [], []
    for wn in cast_next:
        rows, cols = wn.shape[1:]
        rb = rows // n_steps
        assert rb * n_steps == rows and rb % 16 == 0
        cast_in.append(pl.BlockSpec((None, rb, cols), lambda i, j: (next_layer, i * nj + j, 0)))
        cast_out.append(pl.BlockSpec((None, rb, cols), lambda i, j: (0, i * nj + j, 0)))
        cast_shape.append(jax.ShapeDtypeStruct((1, rows, cols), BF16))
    if mod_next:
        n_out = N_MOD * D_MODEL
        n_slabs = min(n_steps, n_out // LANES)
        cb = n_out // n_slabs
        assert cb * n_slabs == n_out and cb % LANES == 0

        def slab(i, j):
            return jnp.minimum(i * nj + j, n_slabs - 1)

        cast_in += [pl.BlockSpec((8, D_MODEL), lambda i, j: (0, 0)),
                    pl.BlockSpec((None, D_MODEL, cb), lambda i, j: (next_layer, 0, slab(i, j))),
                    pl.BlockSpec((None, 1, cb), lambda i, j: (next_layer, 0, slab(i, j)))]
        cast_out.append(pl.BlockSpec((8, cb), lambda i, j: (0, slab(i, j))))
        cast_shape.append(jax.ShapeDtypeStruct((8, n_out), F32))
    return pl.pallas_call(
        functools.partial(_ffn_kernel, len(cast_next), bool(mod_next)),
        grid=(n_tiles, nj),
        in_specs=[
            pl.BlockSpec((tm, D_MODEL), lambda i, j: (tile0 + i, 0)),
            pl.BlockSpec((tm, D_MODEL), lambda i, j: (tile0 + i, 0)),
            pl.BlockSpec((None, N_MOD, D_MODEL), lambda i, j: (grp(tile0 + i), 0, 0)),
            pl.BlockSpec((None, D_MODEL, tf), lambda i, j: (0, 0, j)),
            pl.BlockSpec((None, tf, D_MODEL), lambda i, j: (0, j, 0)),
        ] + cast_in,
        out_specs=[pl.BlockSpec((tm, D_MODEL), lambda i, j: (i, 0))] + cast_out,
        out_shape=[jax.ShapeDtypeStruct((n_tiles * tm, D_MODEL), F32)] + cast_shape,
        compiler_params=_cparams(("arbitrary", "arbitrary")),
        name="ffn",
    )(h2, x1, mods_l, w1, w2, *cast_next, *mod_next)


def _rope_tables():
    half = HEAD_DIM // 2
    pos = np.arange(DEC_SEQ)
    row = (pos // GRID_W).astype(np.float32)
    col = (pos % GRID_W).astype(np.float32)
    inv = (ROPE_THETA ** (-np.arange(0, half, 2, dtype=np.float32) / half)).astype(np.float32)
    ar = row[:, None] * inv[None, :]
    ac = col[:, None] * inv[None, :]
    cos = np.concatenate([np.cos(ar), np.cos(ar), np.cos(ac), np.cos(ac)], axis=1)
    sin = np.concatenate([-np.sin(ar), np.sin(ar), -np.sin(ac), np.sin(ac)], axis=1)
    return jnp.asarray(cos, F32), jnp.asarray(sin, F32)


def _rope(x, cos, sin):
    lane = lax.broadcasted_iota(jnp.int32, x.shape, 1)
    first = (lane % 64) < 32
    partner = jnp.where(first, pltpu.roll(x, 96, 1), pltpu.roll(x, 32, 1))
    return x * cos + partner * sin


ATT_G = ATT_HQ // ATT_HKV


def _group_attention(q_heads, kb, vb, store):
    s = [lax.dot_general(q.astype(BF16), kb, (((1,), (1,)), ((), ())), preferred_element_type=F32)
         for q in q_heads]
    p = [jnp.exp(si - jnp.max(si, axis=-1, keepdims=True)) for si in s]
    l = [jnp.sum(pi, axis=-1, keepdims=True) for pi in p]
    for g in range(len(q_heads)):
        o = jnp.dot(p[g].astype(BF16), vb, preferred_element_type=F32) / l[g]
        store(g, o.astype(BF16))


def _att_ctx_kernel(aq_ref, akv_ref, qg_ref, kg_ref, o_ref, nk_ref, nv_ref):
    qg = qg_ref[...] * (HEAD_DIM ** -0.5)
    kg = kg_ref[...]
    for hk in range(ATT_HKV):
        ks = slice(hk * HEAD_DIM, (hk + 1) * HEAD_DIM)
        kn = _rms(akv_ref[:, ks], kg)
        v = akv_ref[:, 256 + hk * HEAD_DIM:256 + (hk + 1) * HEAD_DIM]
        nk_ref[:, ks] = kn
        nv_ref[:, ks] = v

        def head_cols(g, hk=hk):
            return slice((hk * ATT_G + g) * HEAD_DIM, (hk * ATT_G + g + 1) * HEAD_DIM)

        def store(g, val):
            o_ref[:, head_cols(g)] = val

        qs = [_rms(aq_ref[:, head_cols(g)], qg) for g in range(ATT_G)]
        _group_attention(qs, kn.astype(BF16), v.astype(BF16), store)


def _att_lat_kernel(aq_ref, akv_ref, qg_ref, kg_ref, cos_ref, sin_ref, ck_ref, cv_ref, o_ref,
                    k_sc, v_sc):
    qg = qg_ref[...] * (HEAD_DIM ** -0.5)
    kg = kg_ref[...]
    qb = 256
    for hk in range(ATT_HKV):
        ks = slice(hk * HEAD_DIM, (hk + 1) * HEAD_DIM)
        k_sc[hk, 0:PAST_LEN, :] = ck_ref[:, ks].astype(BF16)
        v_sc[hk, 0:PAST_LEN, :] = cv_ref[:, ks].astype(BF16)
        kn = _rope(_rms(akv_ref[:, ks], kg), cos_ref[...], sin_ref[...])
        k_sc[hk, PAST_LEN:, :] = kn.astype(BF16)
        v_sc[hk, PAST_LEN:, :] = akv_ref[:, 256 + hk * HEAD_DIM:256 + (hk + 1) * HEAD_DIM].astype(BF16)

    def body(r, carry):
        r0 = pl.multiple_of(r * qb, qb)
        cos = cos_ref[pl.ds(r0, qb), :]
        sin = sin_ref[pl.ds(r0, qb), :]
        for hk in range(ATT_HKV):
            def head_cols(g, hk=hk):
                return slice((hk * ATT_G + g) * HEAD_DIM, (hk * ATT_G + g + 1) * HEAD_DIM)

            def store(g, val):
                o_ref[pl.ds(r0, qb), head_cols(g)] = val

            qs = [_rope(_rms(aq_ref[pl.ds(r0, qb), head_cols(g)], qg), cos, sin)
                  for g in range(ATT_G)]
            _group_attention(qs, k_sc[hk], v_sc[hk], store)
        return carry

    lax.fori_loop(0, DEC_SEQ // qb, body, 0)


def _att_ctx(p_main, qg, kg, n_seq):
    L = SEQ
    return pl.pallas_call(
        _att_ctx_kernel,
        grid=(n_seq,),
        in_specs=[
            pl.BlockSpec((L, 1024), lambda b: (b, P_AQ // 1024)),
            pl.BlockSpec((L, 512), lambda b: (b, P_AKV // 512)),
            pl.BlockSpec((1, HEAD_DIM), lambda b: (0, 0)),
            pl.BlockSpec((1, HEAD_DIM), lambda b: (0, 0)),
        ],
        out_specs=[
            pl.BlockSpec((L, 1024), lambda b: (b, 0)),
            pl.BlockSpec((None, L, 256), lambda b: (b, 0, 0)),
            pl.BlockSpec((None, L, 256), lambda b: (b, 0, 0)),
        ],
        out_shape=[
            jax.ShapeDtypeStruct((n_seq * L, 1024), BF16),
            jax.ShapeDtypeStruct((n_seq, L, 256), F32),
            jax.ShapeDtypeStruct((n_seq, L, 256), F32),
        ],
        compiler_params=_cparams(("arbitrary",)),
        name="att_ctx",
    )(p_main, p_main, qg, kg)


def _att_lat(p_main, qg, kg, cos, sin, ck, cv, layer, n_seq, row_blk0):
    L = DEC_SEQ
    return pl.pallas_call(
        _att_lat_kernel,
        grid=(n_seq,),
        in_specs=[
            pl.BlockSpec((L, 1024), lambda b: (row_blk0 + b, P_AQ // 1024)),
            pl.BlockSpec((L, 512), lambda b: (row_blk0 + b, P_AKV // 512)),
            pl.BlockSpec((1, HEAD_DIM), lambda b: (0, 0)),
            pl.BlockSpec((1, HEAD_DIM), lambda b: (0, 0)),
            pl.BlockSpec((L, HEAD_DIM), lambda b: (0, 0)),
            pl.BlockSpec((L, HEAD_DIM), lambda b: (0, 0)),
            pl.BlockSpec((None, None, PAST_LEN, 256), lambda b: (b, layer, 0, 0)),
            pl.BlockSpec((None, None, PAST_LEN, 256), lambda b: (b, layer, 0, 0)),
        ],
        out_specs=pl.BlockSpec((L, 1024), lambda b: (b, 0)),
        out_shape=jax.ShapeDtypeStruct((n_seq * L, 1024), BF16),
        scratch_shapes=[
            pltpu.VMEM((ATT_HKV, PAST_LEN + L, HEAD_DIM), BF16),
            pltpu.VMEM((ATT_HKV, PAST_LEN + L, HEAD_DIM), BF16),
        ],
        compiler_params=_cparams(("arbitrary",)),
        name="att_lat",
    )(p_main, p_main, qg, kg, cos, sin, ck, cv)


def _seqs_per_step(L, n_seq):
    return 2 if (L <= SEQ and n_seq % 2 == 0) else 1


_GLA_LEVELS = (32, 16, 8, 4, 2, 1)


def _gla_consts():
    C = CHUNK
    i = np.arange(C)[:, None]
    t = np.arange(C)[None, :]
    mats = [t <= i, t > i]
    masks = [i == t]
    for s in _GLA_LEVELS:
        m = (i // s) * s
        if s > 1:
            mats.append((t > m) & (t <= i))
        mats.append((t > i) & (t <= m + s))
        masks.append((i // (2 * s) == t // (2 * s)) & ((i // s) % 2 == 1) & ((t // s) % 2 == 0))
    fwd = np.concatenate(mats, 0)
    bwd = np.concatenate([mm[::-1, ::-1] for mm in mats], 0)
    mstack = np.stack([fwd, bwd]).astype(np.float32)
    mf = np.stack([np.concatenate([mm, mm], 0) for mm in masks])
    mb = np.stack([np.concatenate([mm[::-1, ::-1], mm[::-1, ::-1]], 0) for mm in masks])
    lmask = np.stack([mf, mb]).astype(np.float32)
    return jnp.asarray(mstack, BF16), jnp.asarray(lmask, F32)


def _hosted_cast(refs, n_in, n_out, has_cast):
    if not has_cast:
        return refs
    refs[n_in + 1 + n_out][...] = refs[n_in][...].astype(BF16)
    return refs[:n_in] + refs[n_in + 1:n_in + 1 + n_out] + refs[n_in + 2 + n_out:]


def _cast_specs(cast, n_steps):
    if cast is None:
        return [], [], [], []
    w, layer = cast
    rows, cols = w.shape[1:]
    rb = rows // n_steps
    assert rb * n_steps == rows and rb % 16 == 0
    return ([pl.BlockSpec((None, rb, cols), lambda b: (layer, b, 0))],
            [pl.BlockSpec((None, rb, cols), lambda b: (0, b, 0))],
            [jax.ShapeDtypeStruct((1, rows, cols), BF16)], [w])


def _gla_kernel(L, nb, has_state, has_cast, *refs):
    refs = _hosted_cast(refs, 7 + has_state, 2, has_cast)
    if has_state:
        (pg_ref, ps_ref, w2_ref, gb_ref, ng_ref, ms_ref, lm_ref, s0_ref,
         o_ref, so_ref, la_sc, of_sc, ob_sc, st_sc) = refs
    else:
        (pg_ref, ps_ref, w2_ref, gb_ref, ng_ref, ms_ref, lm_ref,
         o_ref, so_ref, la_sc, of_sc, ob_sc, st_sc) = refs
    n = L // CHUNK
    C = CHUNK

    ps = ps_ref[...]
    for z in range(2):
        x = _dot3(ps, w2_ref[z]) + gb_ref[z]
        la_sc[z] = (jnp.minimum(x, 0.0) - jnp.log1p(jnp.exp(-jnp.abs(x)))) * (1.0 / GLA_GATE_NORM)

    if has_state:
        for sq in range(nb):
            for z in range(2):
                for p in range(2):
                    pair = jnp.concatenate([s0_ref[sq, z, 2 * p], s0_ref[sq, z, 2 * p + 1]], axis=0)
                    st_sc[sq, z, p] = jnp.transpose(pair)
    else:
        st_sc[...] = jnp.zeros(st_sc.shape, F32)

    lane = lax.broadcasted_iota(jnp.int32, (C, LANES), 1)
    first = lane < 64

    def stack_heads(t):
        return jnp.concatenate([jnp.where(first, t, 0.0), jnp.where(first, 0.0, t)], axis=0)

    chains = [(t, sq, z, p) for t in range(2) for sq in range(nb) for z in range(2) for p in range(2)]
    nch = len(chains)
    nch1 = nch // 2

    def body(it, carry):
        def chunk_rows(t, sq, z):
            c = 2 * it + t
            cc = c if z == 0 else n - 1 - c
            return pl.ds(pl.multiple_of(sq * L + cc * C, C), C)

        rows = [chunk_rows(t, sq, z) for t, sq, z, p in chains]
        last = [C - 1, 0]
        q = [pg_ref[rows[i], p * LANES:(p + 1) * LANES] * (GLA_DK ** -0.5)
             for i, (t, sq, z, p) in enumerate(chains)]
        k = [pg_ref[rows[i], 256 + p * LANES:256 + (p + 1) * LANES]
             for i, (t, sq, z, p) in enumerate(chains)]
        e = [jnp.exp(_dot_exact_lhs_wide(ms_ref[z], la_sc[z, rows[i], p * LANES:(p + 1) * LANES]))
             for i, (t, sq, z, p) in enumerate(chains)]
        a2 = [_bdot_nt(stack_heads(q[i]), k[i]) * lm_ref[z, 0]
              for i, (t, sq, z, p) in enumerate(chains)]
        blk = 2
        for li, s in enumerate(_GLA_LEVELS):
            if s > 1:
                qs = [q[i] * e[i][blk * C:(blk + 1) * C] for i in range(nch)]
                blk += 1
            else:
                qs = q
            ks = [k[i] * e[i][blk * C:(blk + 1) * C] for i in range(nch)]
            blk += 1
            a2 = [a2[i] + _bdot_nt(stack_heads(qs[i]), ks[i]) * lm_ref[z, li + 1]
                  for i, (t, sq, z, p) in enumerate(chains)]
        kl = [k[i] * e[i][C:2 * C] for i in range(nch)]
        v = [[pg_ref[rows[i], 512 + (2 * p + hh) * LANES:512 + (2 * p + hh + 1) * LANES]
              for hh in range(2)] for i, (t, sq, z, p) in enumerate(chains)]
        intra = [[_bdot(a2[i][hh * C:(hh + 1) * C], v[i][hh]) for hh in range(2)]
                 for i in range(nch)]
        zs = [[_bdot_tn(v[i][hh], kl[i]) for hh in range(2)] for i in range(nch)]
        qe = [stack_heads(q[i] * e[i][0:C]) for i in range(nch)]
        st = [st_sc[sq, z, p] for t, sq, z, p in chains[0:nch1]]
        for i, (t, sq, z, p) in enumerate(chains):
            j = i % nch1
            inter = _bdot_nt(qe[i], st[j])
            osc = of_sc if z == 0 else ob_sc
            for hh in range(2):
                h = 2 * p + hh
                osc[rows[i], h * LANES:(h + 1) * LANES] = intra[i][hh] + inter[hh * C:(hh + 1) * C]
            st[j] = (st[j] * e[i][last[z]:last[z] + 1, :]
                     + jnp.where(first[0:1], zs[i][0], zs[i][1]))
        for j, (t, sq, z, p) in enumerate(chains[0:nch1]):
            st_sc[sq, z, p] = st[j]
        return carry

    lax.fori_loop(0, n // 2, body, 0)

    ng = ng_ref[...]

    def fin(c, carry):
        r0 = pl.multiple_of(c * C, C)
        rows = pl.ds(r0, C)
        for h in range(GLA_H):
            cols = slice(h * LANES, (h + 1) * LANES)
            o = of_sc[rows, cols] + ob_sc[rows, cols]
            gate = _silu(pg_ref[rows, 1024 + h * LANES:1024 + (h + 1) * LANES])
            o_ref[rows, cols] = (_rms(o, ng) * gate).astype(BF16)
        return carry

    lax.fori_loop(0, nb * n, fin, 0)
    for sq in range(nb):
        for z in range(2):
            for p in range(2):
                pair = jnp.transpose(st_sc[sq, z, p])
                so_ref[sq, z, 2 * p] = pair[0:GLA_DK]
                so_ref[sq, z, 2 * p + 1] = pair[GLA_DK:2 * GLA_DK]


def _gla(p_main, p_small, w2p, gbias, ng, consts, s0, layer, L, n_seq, row_blk0, cast=None):
    mstack, lmask = consts
    has_state = s0 is not None
    nb = _seqs_per_step(L, n_seq)
    assert row_blk0 % nb == 0
    blk0 = row_blk0 // nb
    state_blk = (nb, 2, GLA_H, GLA_DK, GLA_DV)
    in_specs = [
        pl.BlockSpec((nb * L, 1536), lambda b: (blk0 + b, P_GLA // 1536)),
        pl.BlockSpec((nb * L, LANES), lambda b: (blk0 + b, 0)),
        pl.BlockSpec((2, LANES, 256), lambda b: (0, 0, 0)),
        pl.BlockSpec((2, 1, 256), lambda b: (0, 0, 0)),
        pl.BlockSpec((1, GLA_DV), lambda b: (0, 0)),
        pl.BlockSpec(mstack.shape, lambda b: (0, 0, 0)),
        pl.BlockSpec(lmask.shape, lambda b: (0, 0, 0, 0)),
    ]
    args = [p_main, p_small, w2p, gbias, ng, mstack, lmask]
    if has_state:
        in_specs.append(pl.BlockSpec((nb, None) + state_blk[1:], lambda b: (b, layer, 0, 0, 0, 0)))
        args.append(s0)
    c_in, c_out, c_shape, c_arg = _cast_specs(cast, n_seq // nb)
    return pl.pallas_call(
        functools.partial(_gla_kernel, L, nb, has_state, cast is not None),
        grid=(n_seq // nb,),
        in_specs=in_specs + c_in,
        out_specs=[
            pl.BlockSpec((nb * L, 512), lambda b: (b, 0)),
            pl.BlockSpec(state_blk, lambda b: (b, 0, 0, 0, 0)),
        ] + c_out,
        out_shape=[
            jax.ShapeDtypeStruct((n_seq * L, 512), BF16),
            jax.ShapeDtypeStruct((n_seq, 2, GLA_H, GLA_DK, GLA_DV), F32),
        ] + c_shape,
        scratch_shapes=[
            pltpu.VMEM((2, nb * L, 256), F32),
            pltpu.VMEM((nb * L, 512), F32),
            pltpu.VMEM((nb * L, 512), F32),
            pltpu.VMEM((nb, 2, 2, GLA_DV, LANES), F32),
        ],
        compiler_params=_cparams(("arbitrary",)),
        name="gla_lat" if has_state else "gla_ctx",
    )(*args, *c_arg)


def _dn_kernel(L, nb, has_state, has_cast, *refs):
    refs = _hosted_cast(refs, 7 + has_state, 2, has_cast)
    if has_state:
        (pd_ref, dz_ref, ps_ref, cw_ref, al_ref, dtb_ref, ng_ref, s0_ref,
         o_ref, so_ref, qkv_sc, g_sc, bb_sc, of_sc, ob_sc, s_sc, prep_sc, att_sc, dec_sc) = refs
    else:
        (pd_ref, dz_ref, ps_ref, cw_ref, al_ref, dtb_ref, ng_ref,
         o_ref, so_ref, qkv_sc, g_sc, bb_sc, of_sc, ob_sc, s_sc, prep_sc, att_sc, dec_sc) = refs
    n = L // CHUNK
    C = CHUNK

    R = nb * L
    pos = lax.broadcasted_iota(jnp.int32, (R, LANES), 0) & (L - 1)
    not_first = pos > 0
    not_last = pos < L - 1

    def conv_silu(j):
        c0 = pl.multiple_of(j * LANES, LANES)
        x = pd_ref[:, pl.ds(c0, LANES)]
        w = cw_ref[:, pl.ds(c0, LANES)]
        prev = jnp.where(not_first, pltpu.roll(x, 1, 0), 0.0)
        nxt = jnp.where(not_last, pltpu.roll(x, R - 1, 0), 0.0)
        return c0, _silu(prev * w[0:1] + x * w[1:2] + nxt * w[2:3])

    def qk_tile(j, carry):
        c0, y = conv_silu(j)
        inv = lax.rsqrt(jnp.sum(y * y, axis=-1, keepdims=True) + EPS)
        qkv_sc[:, pl.ds(c0, LANES)] = y * (inv * jnp.where(j < DN_H, DN_DK ** -0.5, 1.0))
        return carry

    def v_tile(j, carry):
        c0, y = conv_silu(j)
        qkv_sc[:, pl.ds(c0, LANES)] = y
        return carry

    lax.fori_loop(0, 2 * DN_H, qk_tile, 0)
    lax.fori_loop(2 * DN_H, 3 * DN_H, v_tile, 0)

    ps = ps_ref[...]
    lane_l = lax.broadcasted_iota(jnp.int32, (R, LANES), 1)
    g_sc[...] = -jnp.exp(al_ref[...]) * _softplus(ps + dtb_ref[...])
    b_all = _sigmoid(ps)
    for idx in range(2 * DN_H):
        bcol = jnp.sum(jnp.where(lane_l == S_DB + idx, b_all, 0.0), axis=-1, keepdims=True)
        bb_sc[idx] = jnp.broadcast_to(bcol, (R, LANES))

    if has_state:
        s_sc[...] = s0_ref[...]
    else:
        s_sc[...] = jnp.zeros(s_sc.shape, F32)

    ri = lax.broadcasted_iota(jnp.int32, (C, LANES), 0)
    lane_c = lax.broadcasted_iota(jnp.int32, (C, LANES), 1)
    ci = lane_c & (C - 1)
    first = lane_c < C
    incl = [ci <= ri, ci >= ri]
    strict = [ci < ri, ci > ri]
    rs = lax.broadcasted_iota(jnp.int32, (C, C), 0)
    cs = lax.broadcasted_iota(jnp.int32, (C, C), 1)
    m_b = [(cs <= rs).astype(BF16), (cs >= rs).astype(BF16)]
    nu = 2 * nb
    chains = [(sq, z, h) for sq in range(nu) for z in range(2) for h in range(DN_H)]
    nch = len(chains)
    pairs = [(sq, z, p) for sq in range(nu) for z in range(2) for p in range(DN_H // 2)]
    npr = len(pairs)

    def stack_heads(t):
        return jnp.concatenate([jnp.where(first, t, 0.0), jnp.where(first, 0.0, t)], axis=0)

    def hi_mid(x):
        hi = x.astype(BF16).astype(F32)
        return hi, x - hi

    def pair_products(lhs_list, x):
        xh, xm = hi_mid(x)
        rhs = jnp.concatenate([stack_heads(xh), stack_heads(xm)], axis=1).astype(BF16)
        parts = []
        for l in lhs_list:
            parts.extend(hi_mid(l))
        r = jnp.dot(jnp.concatenate(parts, axis=0).astype(BF16), rhs, preferred_element_type=F32)
        out = []
        for i in range(len(lhs_list)):
            blk = r[2 * C * i:2 * C * i + C] + r[2 * C * i + C:2 * C * (i + 1)]
            out.append(blk[:, 0:LANES] + blk[:, LANES:2 * LANES])
        return out

    heads = [((2 * sq + z) * DN_H + 2 * p, (2 * sq + z) * DN_H + 2 * p + 1)
             for sq, z, p in pairs]

    def chunk_rows(cs):
        def one(sq, z):
            c = cs[sq // nb]
            start = (sq % nb) * L + (c if z == 0 else n - 1 - c) * C
            return pl.ds(start if isinstance(start, int) else pl.multiple_of(start, C), C)
        return [one(sq, z) for sq in range(nu) for z in range(2)]

    def prepare_products(cs):
        rows = chunk_rows(cs)
        last = [C - 1, 0]

        def ld(base, sq, z, h):
            return qkv_sc[rows[2 * sq + z], base + h * LANES:base + (h + 1) * LANES]

        q = [ld(0, *ch) for ch in chains]
        k = [ld(512, *ch) for ch in chains]
        v = [ld(1024, *ch) for ch in chains]
        bb = [bb_sc[z * DN_H + h, rows[2 * sq + z], :] for sq, z, h in chains]
        gc = [_dot_exact_lhs(m_b[z], g_sc[rows[2 * sq + z], :])
              for sq in range(nu) for z in range(2)]
        gct =[jnp.transpose(jnp.concatenate([g, g], axis=0)) for g in gc]
        gi = [jnp.broadcast_to(jnp.sum(jnp.where(lane_c == S_DA + z * DN_H + h, gc[2 * sq + z], 0.0),
                                       axis=-1, keepdims=True), (C, LANES)) for sq, z, h in chains]
        gjrow = [gct[2 * sq + z][S_DA + z * DN_H + h:S_DA + z * DN_H + h + 1, :]
                 for sq, z, h in chains]
        gam = []
        for (sq, z, p), (i0, i1) in zip(pairs, heads):
            diff = jnp.where(first, gi[i0], gi[i1]) - jnp.where(first[0:1], gjrow[i0], gjrow[i1])
            gam.append(jnp.where(incl[z], jnp.exp(jnp.where(incl[z], diff, 0.0)), 0.0))
        kb = [k[i] * bb[i] for i in range(nch)]
        kq = [_bdot_nt(jnp.concatenate([kb[i0], q[i0], kb[i1], q[i1]], axis=0),
                       jnp.concatenate([k[i0], k[i1]], axis=0)) for i0, i1 in heads]
        gt = [gi[i][last[z]:last[z] + 1, :] for i, (sq, z, h) in enumerate(chains)]
        return q, k, v, bb, kb, gi, gt, gam, kq

    def prepare_finish(parts):
        q, k, v, bb, kb, gi, gt, gam, kq = parts
        a = [jnp.where(strict[z], jnp.where(first, kq[j][0:C], kq[j][2 * C:3 * C]) * gam[j], 0.0)
             for j, (sq, z, p) in enumerate(pairs)]
        att = [jnp.where(first, kq[j][C:2 * C], kq[j][3 * C:4 * C]) * gam[j] for j in range(npr)]
        nt = [-a[j] for j in range(npr)]
        pw = [pair_products([a[j]], a[j])[0] for j in range(npr)]
        for step in range(5):
            if step < 4:
                both = [pair_products([nt[j], pw[j]], pw[j]) for j in range(npr)]
                nt = [nt[j] + pw[j] + both[j][0] for j in range(npr)]
                pw = [both[j][1] for j in range(npr)]
            else:
                nt = [nt[j] + pw[j] + pair_products([nt[j]], pw[j])[0] for j in range(npr)]
        eg = [jnp.exp(gi[i]) for i in range(nch)]
        rhs = [jnp.concatenate([v[i] * bb[i], kb[i] * eg[i]], axis=1) for i in range(nch)]
        corr = [_bdot(stack_heads(nt[j]), jnp.concatenate([rhs[i0], rhs[i1]], axis=0))
                for j, (i0, i1) in enumerate(heads)]
        for j, (i0, i1) in enumerate(heads):
            for half, i in enumerate((i0, i1)):
                uw = rhs[i] + corr[j][half * C:(half + 1) * C]
                prep_sc[i, 0] = uw[:, 0:LANES]
                prep_sc[i, 1] = uw[:, LANES:]
                prep_sc[i, 2] = q[i] * eg[i]
                prep_sc[i, 3] = k[i] * jnp.exp(gt[i] - gi[i])
                dec_sc[i] = jnp.broadcast_to(jnp.exp(gt[i]), (8, LANES))
            att_sc[j] = att[j]

    nch1, npr1 = nch // 2, npr // 2

    def apply_state_products(slot, s):
        return [_bdot(jnp.concatenate([prep_sc[slot * nch1 + i, 1], prep_sc[slot * nch1 + i, 2]],
                                      axis=0), s[i]) for i in range(nch1)]

    def apply_finish(rows, slot, s, ws):
        base = slot * nch1
        v_new = [prep_sc[base + i, 0] - ws[i][0:C] for i in range(nch1)]
        av = []
        for j in range(npr1):
            i0, i1 = heads[slot * npr1 + j]
            av.append(_bdot(stack_heads(att_sc[slot * npr1 + j]),
                            jnp.concatenate([v_new[i0 - base], v_new[i1 - base]], axis=0)))
        upd = [_bdot_tn(prep_sc[base + i, 3], v_new[i]) for i in range(nch1)]
        for j in range(npr1):
            for half, ig in enumerate(heads[slot * npr1 + j]):
                sq, z, h = chains[ig]
                osc = of_sc if z == 0 else ob_sc
                osc[rows[2 * sq + z], h * LANES:(h + 1) * LANES] = (
                    ws[ig - base][C:2 * C] + av[j][half * C:(half + 1) * C])
        return [s[i] * dec_sc[base + i, 0:1, :] + upd[i] for i in range(nch1)]

    def apply_two(cs, parts_fn):
        rows = chunk_rows(cs)
        s = [s_sc[sq, z, h] for sq, z, h in chains[0:nch1]]
        ws = apply_state_products(0, s)
        parts = parts_fn()
        s = apply_finish(rows, 0, s, ws)
        ws = apply_state_products(1, s)
        s = apply_finish(rows, 1, s, ws)
        for i, (sq, z, h) in enumerate(chains[0:nch1]):
            s_sc[sq, z, h] = s[i]
        return parts

    prepare_finish(prepare_products((0, 1)))

    def body(it, carry):
        c = 2 * it
        parts = apply_two((c, c + 1), lambda: prepare_products((c + 2, c + 3)))
        prepare_finish(parts)
        return carry

    lax.fori_loop(0, n // 2 - 1, body, 0)
    apply_two((n - 2, n - 1), lambda: None)

    ng = ng_ref[...]

    def fin(c, carry):
        r0 = pl.multiple_of(c * C, C)
        rows = pl.ds(r0, C)
        for h in range(DN_H):
            cols = slice(h * LANES, (h + 1) * LANES)
            o = of_sc[rows, cols] + ob_sc[rows, cols]
            o_ref[rows, cols] = (_rms(o, ng) * _silu(dz_ref[rows, cols])).astype(BF16)
        return carry

    lax.fori_loop(0, nb * n, fin, 0)
    so_ref[...] = s_sc[...]


def _dn(p_main, p_small, conv_w, alog_row, dtb_row, ng, s0, layer, L, n_seq, row_blk0, cast=None):
    has_state = s0 is not None
    nb = _seqs_per_step(L, n_seq)
    assert row_blk0 % nb == 0
    row_blk0, L_blk = row_blk0 // nb, nb * L
    in_specs = [
        pl.BlockSpec((L_blk, 1536), lambda b: (row_blk0 + b, P_DN // 1536)),
        pl.BlockSpec((L_blk, 512), lambda b: (row_blk0 + b, P_DZ // 512)),
        pl.BlockSpec((L_blk, LANES), lambda b: (row_blk0 + b, 0)),
        pl.BlockSpec((DN_CONV, 3 * DN_H * DN_DK), lambda b: (0, 0)),
        pl.BlockSpec((1, LANES), lambda b: (0, 0)),
        pl.BlockSpec((1, LANES), lambda b: (0, 0)),
        pl.BlockSpec((1, DN_DV), lambda b: (0, 0)),
    ]
    args = [p_main, p_main, p_small, conv_w, alog_row, dtb_row, ng]
    if has_state:
        in_specs.append(pl.BlockSpec((nb, None, 2, DN_H, DN_DK, DN_DV),
                                     lambda b: (b, layer, 0, 0, 0, 0)))
        args.append(s0)
    c_in, c_out, c_shape, c_arg = _cast_specs(cast, n_seq // nb)
    return pl.pallas_call(
        functools.partial(_dn_kernel, L, nb, has_state, cast is not None),
        grid=(n_seq // nb,),
        in_specs=in_specs + c_in,
        out_specs=[
            pl.BlockSpec((L_blk, 512), lambda b: (b, 0)),
            pl.BlockSpec((nb, 2, DN_H, DN_DK, DN_DV), lambda b: (b, 0, 0, 0, 0)),
        ] + c_out,
        out_shape=[
            jax.ShapeDtypeStruct((n_seq * L, 512), BF16),
            jax.ShapeDtypeStruct((n_seq, 2, DN_H, DN_DK, DN_DV), F32),
        ] + c_shape,
        scratch_shapes=[
            pltpu.VMEM((L_blk, 1536), F32),
            pltpu.VMEM((L_blk, LANES), F32),
            pltpu.VMEM((2 * DN_H, L_blk, LANES), F32),
            pltpu.VMEM((L_blk, 512), F32),
            pltpu.VMEM((L_blk, 512), F32),
            pltpu.VMEM((nb, 2, DN_H, DN_DK, DN_DV), F32),
            pltpu.VMEM((2 * nb * 2 * DN_H, 4, CHUNK, LANES), F32),
            pltpu.VMEM((2 * nb * DN_H, CHUNK, LANES), F32),
            pltpu.VMEM((2 * nb * 2 * DN_H, 8, LANES), F32),
        ],
        compiler_params=_cparams(("arbitrary",)),
        name="dn_lat" if has_state else "dn_ctx",
    )(*args, *c_arg)


_W_IN_SPLITS = (256, 256, 512, 512, 32, 1024, 256, 256, 1536, 512, 8, 8)


def _regroup_kernel(wt_ref, main_ref, small_ref):
    off = np.cumsum((0,) + _W_IN_SPLITS)
    gq, gk, gv, gr, glr, aq, ak, av, dqkv, dz, da, db = [
        wt_ref[int(off[i]):int(off[i + 1]), :] for i in range(12)]
    main_t = jnp.concatenate([aq, ak, av, gq, gk, gv, gr, dqkv, dz], axis=0)
    main_ref[...] = jnp.transpose(main_t).astype(BF16)
    pad = jnp.zeros((LANES - 48, wt_ref.shape[1]), F32)
    small_t = jnp.concatenate([glr, da, db, pad], axis=0)
    small_ref[...] = jnp.transpose(small_t).astype(BF16)


def _prep_w_in(w_in, tr=256):
    depth, _, in_cols = w_in.shape
    return pl.pallas_call(
        _regroup_kernel,
        grid=(depth, D_MODEL // tr),
        in_specs=[pl.BlockSpec((None, in_cols, tr), lambda l, r: (l, 0, r))],
        out_specs=[
            pl.BlockSpec((None, tr, P_MAIN), lambda l, r: (l, r, 0)),
            pl.BlockSpec((None, tr, LANES), lambda l, r: (l, r, 0)),
        ],
        out_shape=[
            jax.ShapeDtypeStruct((depth, D_MODEL, P_MAIN), BF16),
            jax.ShapeDtypeStruct((depth, D_MODEL, LANES), BF16),
        ],
        compiler_params=_cparams(("arbitrary", "arbitrary")),
        name="w_in_regroup",
    )(jnp.swapaxes(w_in, 1, 2))


def _lane_row(vals, offset):
    return jnp.zeros((1, LANES), F32).at[0, offset:offset + vals.shape[0]].set(vals.astype(F32))


def kernel(x_prompt, x_sample, cache_k, cache_v, state_gla, state_dn, c, c_ctx, norm1_g, norm2_g,
           w_mod, b_mod, w_in, gla_w2, gla_b, gla_norm_g, q_norm_g, k_norm_g, dn_conv, dn_a_log,
           dn_dt_bias, dn_norm_g, w_out, w_ff1, w_ff2):
    n_ctx, n_lat = x_prompt.shape[0], x_sample.shape[0]
    depth = w_in.shape[0]
    n_ctx_rows = n_ctx * SEQ
    assert n_ctx_rows % DEC_SEQ == 0 and n_lat <= 7

    xs = [x_prompt.reshape(n_ctx_rows, D_MODEL), x_sample.reshape(n_lat * DEC_SEQ, D_MODEL)]
    cond8 = jnp.concatenate([c_ctx[None, :], c, jnp.zeros((7 - n_lat, D_MODEL), F32)], axis=0)
    b_mod3 = b_mod.reshape(depth, 1, N_MOD * D_MODEL)
    mods_l = _modulation(cond8, w_mod, b_mod3, 1).reshape(8, N_MOD, D_MODEL)

    cos, sin = _rope_tables()
    gla_consts = _gla_consts()
    ck = cache_k.reshape(n_lat, depth, PAST_LEN, ATT_HKV * HEAD_DIM)
    cv = cache_v.reshape(n_lat, depth, PAST_LEN, ATT_HKV * HEAD_DIM)

    w_main, w_small = _prep_w_in(w_in)
    w_out_b = w_out[0:1].astype(BF16)

    nks, nvs, sgs, sds = [], [], [], []
    for l in range(depth):
        tc, tl = n_ctx_rows // INPROJ_TM, n_lat * DEC_SEQ // INPROJ_TM
        inproj = functools.partial(_inproj, mods_l=mods_l, g1=norm1_g[l][None, :], w_main=w_main,
                                   w_small=w_small, layer=l, n_ctx_rows=n_ctx_rows)
        pc, psc = inproj(xs[0], tile0=0, n_tiles=tc, grp_tile0=0)
        if len(xs) == 2:
            pl_, psl = inproj(xs[1], tile0=0, n_tiles=tl, grp_tile0=tc)
        else:
            pl_, psl = inproj(xs[0], tile0=tc, n_tiles=tl, grp_tile0=tc)

        qg, kg = q_norm_g[l][None, :], k_norm_g[l][None, :]
        oa_c, nk, nv = _att_ctx(pc, qg, kg, n_ctx)
        oa_l = _att_lat(pl_, qg, kg, cos, sin, ck, cv, l, n_lat, 0)

        w2p = jnp.zeros((2, LANES, GLA_H * GLA_DK), F32)
        w2p = w2p.at[0, 0:GLA_LR].set(gla_w2[l, 0]).at[1, GLA_LR:2 * GLA_LR].set(gla_w2[l, 1])
        gbias = gla_b[l][:, None, :]
        gng = gla_norm_g[l][None, :]
        og_c, sg, *hosted = _gla(pc, psc, w2p, gbias, gng, gla_consts, None, l, SEQ, n_ctx, 0,
                                 cast=(w_ff1, l))
        w_ff1_b = hosted[0]
        og_l, _ = _gla(pl_, psl, w2p, gbias, gng, gla_consts, state_gla, l, DEC_SEQ, n_lat, 0)

        alog_row = _lane_row(dn_a_log[l].reshape(-1), S_DA)
        dtb_row = _lane_row(dn_dt_bias[l].reshape(-1), S_DA)
        dng = dn_norm_g[l][None, :]
        od_c, sd, *hosted = _dn(pc, psc, dn_conv[l], alog_row, dtb_row, dng, None, l, SEQ, n_ctx, 0,
                                cast=(w_ff2, l))
        w_ff2_b = hosted[0]
        od_l, _ = _dn(pl_, psl, dn_conv[l], alog_row, dtb_row, dng, state_dn, l, DEC_SEQ, n_lat, 0)

        x1, h2 = _outproj(xs, [og_c, og_l], [oa_c, oa_l], [od_c, od_l], mods_l,
                          norm2_g[l][None, :], w_out_b, 0, n_ctx_rows)
        n_tiles, nct = x1.shape[0] // FFN_TM, n_ctx_rows // FFN_TM
        ffn = functools.partial(_ffn, h2, x1, mods_l, w_ff1_b, w_ff2_b, n_ctx_rows, tm=FFN_TM)
        if l < depth - 1:
            y, w_out_b, m_next = ffn(
                tile0=0, n_tiles=n_tiles, next_layer=l + 1, cast_next=(w_out,),
                mod_next=(cond8, w_mod, b_mod3))
            xs, mods_l = [y], m_next.reshape(8, N_MOD, D_MODEL)
        else:
            xs = [ffn(tile0=0, n_tiles=nct)[0], ffn(tile0=nct, n_tiles=n_tiles - nct)[0]]

        nks.append(nk.reshape(n_ctx, SEQ, ATT_HKV, HEAD_DIM))
        nvs.append(nv.reshape(n_ctx, SEQ, ATT_HKV, HEAD_DIM))
        sgs.append(sg)
        sds.append(sd)

    y_prompt = xs[0].reshape(n_ctx, SEQ, D_MODEL)
    y_sample = xs[1].reshape(n_lat, DEC_SEQ, D_MODEL)
    return (y_prompt, y_sample, jnp.stack(nks, axis=1), jnp.stack(nvs, axis=1),
            jnp.stack(sgs, axis=1), jnp.stack(sds, axis=1))
```

```python
import functools
import math

import numpy as np
import jax
import jax.numpy as jnp
from jax import lax
from jax.experimental import pallas as pl
from jax.experimental.pallas import tpu as pltpu

F32 = jnp.float32
BF16 = jnp.bfloat16

D_MODEL = 2048
SEQ = 256
DEC_SEQ = 1024
PAST_LEN = 256
GRID_W = 64
HEAD_DIM = 128
EPS = 1e-6
GLA_H = 4
GLA_DK = 64
GLA_DV = 128
GLA_LR = 16
GLA_GATE_NORM = 16.0
ATT_HQ = 8
ATT_HKV = 2
ROPE_THETA = 10000.0
DN_H = 4
DN_DK = 128
DN_DV = 128
DN_CONV = 3
CHUNK = 64
FF = 4 * D_MODEL
N_MOD = 6
LANES = 128

P_AQ = 0
P_AKV = 1024
P_GLA = 1536
P_DN = 3072
P_DZ = 4608
P_MAIN = 5120
S_DA = 32
S_DB = 40

VMEM_LIMIT = 56 * 1024 * 1024


def _cparams(sem):
    return pltpu.CompilerParams(dimension_semantics=sem, vmem_limit_bytes=VMEM_LIMIT)


def _bdot(a, b):
    return jnp.dot(a.astype(BF16), b.astype(BF16), preferred_element_type=F32)


def _bdot_nt(a, b):
    return lax.dot_general(a.astype(BF16), b.astype(BF16), (((1,), (1,)), ((), ())),
                           preferred_element_type=F32)


def _bdot_tn(a, b):
    return lax.dot_general(a.astype(BF16), b.astype(BF16), (((0,), (0,)), ((), ())),
                           preferred_element_type=F32)


def _split3(x):
    hi = x.astype(BF16)
    r = x - hi.astype(F32)
    mid = r.astype(BF16)
    lo = (r - mid.astype(F32)).astype(BF16)
    return hi, mid, lo


def _dot_exact_lhs(lhs_bf16, x):
    hi, mid, lo = _split3(x)
    d = functools.partial(jnp.dot, preferred_element_type=F32)
    return d(lhs_bf16, hi) + d(lhs_bf16, mid) + d(lhs_bf16, lo)


def _dot_exact_lhs_wide(lhs_bf16, x):
    n = x.shape[1]
    hi = x.astype(BF16)
    mid = (x - hi.astype(F32)).astype(BF16)
    r = jnp.dot(lhs_bf16, jnp.concatenate([hi, mid], axis=1), preferred_element_type=F32)
    return r[:, 0:n] + r[:, n:2 * n]


def _dot3(a, b):
    ah, am, _ = _split3(a)
    bh, bm, _ = _split3(b)
    d = functools.partial(jnp.dot, preferred_element_type=F32)
    return d(ah, bh) + (d(ah, bm) + d(am, bh))


def _sigmoid(x):
    return 0.5 * jnp.tanh(0.5 * x) + 0.5


def _silu(x):
    return x * _sigmoid(x)


def _softplus(x):
    return jnp.maximum(x, 0.0) + jnp.log(1.0 + jnp.exp(-jnp.abs(x)))


def _rms(x, g):
    return x * lax.rsqrt(jnp.mean(x * x, axis=-1, keepdims=True) + EPS) * g


def _mods_kernel(c_ref, w_ref, b_ref, o_ref):
    o_ref[...] = _bdot(_silu(c_ref[...]), w_ref[...]) + b_ref[...]


def _modulation(cond8, w_mod, b_mod3, depth):
    n_out = N_MOD * D_MODEL
    tn = 1024
    return pl.pallas_call(
        _mods_kernel,
        grid=(depth, n_out // tn),
        in_specs=[
            pl.BlockSpec((8, D_MODEL), lambda l, j: (0, 0)),
            pl.BlockSpec((None, D_MODEL, tn), lambda l, j: (l, 0, j)),
            pl.BlockSpec((None, 1, tn), lambda l, j: (l, 0, j)),
        ],
        out_specs=pl.BlockSpec((None, 8, tn), lambda l, j: (l, 0, j)),
        out_shape=jax.ShapeDtypeStruct((depth, 8, n_out), F32),
        compiler_params=_cparams(("arbitrary", "arbitrary")),
        name="modulation",
    )(cond8, w_mod, b_mod3)


def _group_of_tile(i, tm, n_ctx_rows):
    n_ctx_tiles = n_ctx_rows // tm
    per_lat = DEC_SEQ // tm
    return jnp.where(i < n_ctx_tiles, 0, 1 + (i - n_ctx_tiles) // per_lat)


def _row_specs(arrays, width, tm, nct):
    if len(arrays) == 1:
        return [pl.BlockSpec((tm, width), lambda i, *_: (i, 0))]
    return [pl.BlockSpec((tm, width), lambda i, *_: (jnp.minimum(i, nct - 1), 0)),
            pl.BlockSpec((tm, width), lambda i, *_: (jnp.maximum(i - nct, 0), 0))]


def _row_load(refs, nct, rows=slice(None)):
    if len(refs) == 1:
        return refs[0][rows, :]
    return jnp.where(pl.program_id(0) < nct, refs[0][rows, :], refs[1][rows, :])


INPROJ_TM = 1024


def _inproj_kernel(x_ref, mod_ref, g_ref, w_ref, ws_ref, p_ref, ps_ref, h_sc):
    @pl.when(pl.program_id(1) == 0)
    def _():
        m = mod_ref[...]
        h = _rms(x_ref[...], g_ref[...]) * (1.0 + m[1:2]) + m[0:1]
        hb = h.astype(BF16)
        h_sc[...] = hb
        ps_ref[...] = jnp.dot(hb, ws_ref[...], preferred_element_type=F32)

    h = h_sc[...]
    half = w_ref.shape[1] // 2
    for piece in range(2):
        cols = slice(piece * half, (piece + 1) * half)
        p_ref[:, cols] = jnp.dot(h, w_ref[:, cols], preferred_element_type=F32)


def _inproj(x, mods_l, g1, w_main, w_small, layer, n_ctx_rows, tile0, n_tiles, grp_tile0, tn=1024):
    tm = INPROJ_TM
    grp = functools.partial(_group_of_tile, tm=tm, n_ctx_rows=n_ctx_rows)
    return pl.pallas_call(
        _inproj_kernel,
        grid=(n_tiles, P_MAIN // tn),
        in_specs=[
            pl.BlockSpec((tm, D_MODEL), lambda i, j: (tile0 + i, 0)),
            pl.BlockSpec((None, N_MOD, D_MODEL), lambda i, j: (grp(grp_tile0 + i), 0, 0)),
            pl.BlockSpec((1, D_MODEL), lambda i, j: (0, 0)),
            pl.BlockSpec((None, D_MODEL, tn), lambda i, j: (layer, 0, j)),
            pl.BlockSpec((None, D_MODEL, LANES), lambda i, j: (layer, 0, 0)),
        ],
        out_specs=[
            pl.BlockSpec((tm, tn), lambda i, j: (i, j)),
            pl.BlockSpec((tm, LANES), lambda i, j: (i, 0)),
        ],
        out_shape=[
            jax.ShapeDtypeStruct((n_tiles * tm, P_MAIN), F32),
            jax.ShapeDtypeStruct((n_tiles * tm, LANES), F32),
        ],
        scratch_shapes=[pltpu.VMEM((tm, D_MODEL), BF16)],
        compiler_params=_cparams(("arbitrary", "arbitrary")),
        name="inproj",
    )(x, mods_l, g1, w_main, w_small)


def _outproj_kernel(nct, n_x, *refs):
    x_refs = refs[:n_x]
    (ogc_ref, ogl_ref, oac_ref, oal_ref, odc_ref, odl_ref, mod_ref, g_ref, w_ref,
     x1_ref, h2_ref) = refs[n_x:]
    m = mod_ref[...]
    d = functools.partial(jnp.dot, preferred_element_type=F32)
    half = x1_ref.shape[0] // 2
    for r in range(2):
        rows = slice(r * half, (r + 1) * half)
        mix = (d(_row_load((ogc_ref, ogl_ref), nct, rows), w_ref[0:512, :])
               + d(_row_load((oac_ref, oal_ref), nct, rows), w_ref[512:1536, :])
               + d(_row_load((odc_ref, odl_ref), nct, rows), w_ref[1536:2048, :]))
        x1 = _row_load(x_refs, nct, rows) + m[2:3] * mix
        x1_ref[rows, :] = x1
        h2_ref[rows, :] = (_rms(x1, g_ref[...]) * (1.0 + m[4:5]) + m[3:4]).astype(BF16)


def _outproj(xs, o_gla, o_att, o_dn, mods_l, g2, w_out, layer, n_ctx_rows, tm=512):
    rows = sum(x.shape[0] for x in xs)
    nct = n_ctx_rows // tm
    grp = functools.partial(_group_of_tile, tm=tm, n_ctx_rows=n_ctx_rows)
    return pl.pallas_call(
        functools.partial(_outproj_kernel, nct, len(xs)),
        grid=(rows // tm,),
        in_specs=(_row_specs(xs, D_MODEL, tm, nct) + _row_specs(o_gla, 512, tm, nct)
                  + _row_specs(o_att, 1024, tm, nct) + _row_specs(o_dn, 512, tm, nct) + [
                      pl.BlockSpec((None, N_MOD, D_MODEL), lambda i: (grp(i), 0, 0)),
                      pl.BlockSpec((1, D_MODEL), lambda i: (0, 0)),
                      pl.BlockSpec((None, D_MODEL, D_MODEL), lambda i: (layer, 0, 0),
                                   pipeline_mode=pl.Buffered(1)),
                  ]),
        out_specs=[
            pl.BlockSpec((tm, D_MODEL), lambda i: (i, 0)),
            pl.BlockSpec((tm, D_MODEL), lambda i: (i, 0)),
        ],
        out_shape=[
            jax.ShapeDtypeStruct((rows, D_MODEL), F32),
            jax.ShapeDtypeStruct((rows, D_MODEL), BF16),
        ],
        compiler_params=_cparams(("arbitrary",)),
        name="outproj",
    )(*xs, *o_gla, *o_att, *o_dn, mods_l, g2, w_out)


FFN_TM = 512
FFN_OUT_CHUNK = 512


def _ffn_kernel(n_cast, has_mod, h_ref, x1_ref, mod_ref, w1_ref, w2_ref, *refs):
    n_in = n_cast + (3 if has_mod else 0)
    y_ref = refs[n_in]
    for src, dst in zip(refs[:n_cast], refs[n_in + 1:n_in + 1 + n_cast]):
        dst[...] = src[...].astype(BF16)
    if has_mod:
        c_ref, wm_ref, bm_ref = refs[n_cast:n_in]
        refs[n_in + 1 + n_cast][...] = _bdot(_silu(c_ref[...]), wm_ref[...]) + bm_ref[...]
    j = pl.program_id(1)
    tf = w1_ref.shape[1]

    @pl.when(j == 0)
    def _():
        y_ref[...] = jnp.zeros(y_ref.shape, F32)

    h = h_ref[...]
    for half in range(2):
        cols = slice(half * (tf // 2), (half + 1) * (tf // 2))
        t = jnp.dot(h, w1_ref[:, cols], preferred_element_type=F32)
        t = jnp.square(jnp.maximum(t, 0.0)).astype(BF16)
        for nc in range(D_MODEL // FFN_OUT_CHUNK):
            ncs = slice(nc * FFN_OUT_CHUNK, (nc + 1) * FFN_OUT_CHUNK)
            y_ref[:, ncs] += jnp.dot(t, w2_ref[cols, ncs], preferred_element_type=F32)

    @pl.when(j == pl.num_programs(1) - 1)
    def _():
        y_ref[...] = x1_ref[...] + mod_ref[5:6, :] * y_ref[...]


def _ffn(h2, x1, mods_l, w1, w2, n_ctx_rows, tile0, n_tiles, tm, cast_next=(), mod_next=(),
         next_layer=0, tf=1024):
    grp = functools.partial(_group_of_tile, tm=tm, n_ctx_rows=n_ctx_rows)
    nj = FF // tf
    n_steps = n_tiles * nj
    cast_in, cast_out, cast_shape = [], [], []
    for wn in cast_next:
        rows, cols = wn.shape[1:]
        rb = rows // n_steps
        assert rb * n_steps == rows and rb % 16 == 0
        cast_in.append(pl.BlockSpec((None, rb, cols), lambda i, j: (next_layer, i * nj + j, 0)))
        cast_out.append(pl.BlockSpec((None, rb, cols), lambda i, j: (0, i * nj + j, 0)))
        cast_shape.append(jax.ShapeDtypeStruct((1, rows, cols), BF16))
    if mod_next:
        n_out = N_MOD * D_MODEL
        n_slabs = min(n_steps, n_out // LANES)
        cb = n_out // n_slabs
        assert cb * n_slabs == n_out and cb % LANES == 0

        def slab(i, j):
            return jnp.minimum(i * nj + j, n_slabs - 1)

        cast_in += [pl.BlockSpec((8, D_MODEL), lambda i, j: (0, 0)),
                    pl.BlockSpec((None, D_MODEL, cb), lambda i, j: (next_layer, 0, slab(i, j))),
                    pl.BlockSpec((None, 1, cb), lambda i, j: (next_layer, 0, slab(i, j)))]
        cast_out.append(pl.BlockSpec((8, cb), lambda i, j: (0, slab(i, j))))
        cast_shape.append(jax.ShapeDtypeStruct((8, n_out), F32))
    return pl.pallas_call(
        functools.partial(_ffn_kernel, len(cast_next), bool(mod_next)),
        grid=(n_tiles, nj),
        in_specs=[
            pl.BlockSpec((tm, D_MODEL), lambda i, j: (tile0 + i, 0)),
            pl.BlockSpec((tm, D_MODEL), lambda i, j: (tile0 + i, 0)),
            pl.BlockSpec((None, N_MOD, D_MODEL), lambda i, j: (grp(tile0 + i), 0, 0)),
            pl.BlockSpec((None, D_MODEL, tf), lambda i, j: (0, 0, j)),
            pl.BlockSpec((None, tf, D_MODEL), lambda i, j: (0, j, 0)),
        ] + cast_in,
        out_specs=[pl.BlockSpec((tm, D_MODEL), lambda i, j: (i, 0))] + cast_out,
        out_shape=[jax.ShapeDtypeStruct((n_tiles * tm, D_MODEL), F32)] + cast_shape,
        compiler_params=_cparams(("arbitrary", "arbitrary")),
        name="ffn",
    )(h2, x1, mods_l, w1, w2, *cast_next, *mod_next)


def _rope_tables():
    half = HEAD_DIM // 2
    pos = np.arange(DEC_SEQ)
    row = (pos // GRID_W).astype(np.float32)
    col = (pos % GRID_W).astype(np.float32)
    inv = (ROPE_THETA ** (-np.arange(0, half, 2, dtype=np.float32) / half)).astype(np.float32)
    ar = row[:, None] * inv[None, :]
    ac = col[:, None] * inv[None, :]
    cos = np.concatenate([np.cos(ar), np.cos(ar), np.cos(ac), np.cos(ac)], axis=1)
    sin = np.concatenate([-np.sin(ar), np.sin(ar), -np.sin(ac), np.sin(ac)], axis=1)
    return jnp.asarray(cos, F32), jnp.asarray(sin, F32)


def _rope(x, cos, sin):
    lane = lax.broadcasted_iota(jnp.int32, x.shape, 1)
    first = (lane % 64) < 32
    partner = jnp.where(first, pltpu.roll(x, 96, 1), pltpu.roll(x, 32, 1))
    return x * cos + partner * sin


ATT_G = ATT_HQ // ATT_HKV


def _group_attention(q_heads, kb, vb, store):
    s = [lax.dot_general(q.astype(BF16), kb, (((1,), (1,)), ((), ())), preferred_element_type=F32)
         for q in q_heads]
    p = [jnp.exp(si - jnp.max(si, axis=-1, keepdims=True)) for si in s]
    l = [jnp.sum(pi, axis=-1, keepdims=True) for pi in p]
    for g in range(len(q_heads)):
        o = jnp.dot(p[g].astype(BF16), vb, preferred_element_type=F32) / l[g]
        store(g, o.astype(BF16))


def _att_ctx_kernel(aq_ref, akv_ref, qg_ref, kg_ref, o_ref, nk_ref, nv_ref):
    qg = qg_ref[...] * (HEAD_DIM ** -0.5)
    kg = kg_ref[...]
    for hk in range(ATT_HKV):
        ks = slice(hk * HEAD_DIM, (hk + 1) * HEAD_DIM)
        kn = _rms(akv_ref[:, ks], kg)
        v = akv_ref[:, 256 + hk * HEAD_DIM:256 + (hk + 1) * HEAD_DIM]
        nk_ref[:, ks] = kn
        nv_ref[:, ks] = v

        def head_cols(g, hk=hk):
            return slice((hk * ATT_G + g) * HEAD_DIM, (hk * ATT_G + g + 1) * HEAD_DIM)

        def store(g, val):
            o_ref[:, head_cols(g)] = val

        qs = [_rms(aq_ref[:, head_cols(g)], qg) for g in range(ATT_G)]
        _group_attention(qs, kn.astype(BF16), v.astype(BF16), store)


def _att_lat_kernel(aq_ref, akv_ref, qg_ref, kg_ref, cos_ref, sin_ref, ck_ref, cv_ref, o_ref,
                    k_sc, v_sc):
    qg = qg_ref[...] * (HEAD_DIM ** -0.5)
    kg = kg_ref[...]
    qb = 256
    for hk in range(ATT_HKV):
        ks = slice(hk * HEAD_DIM, (hk + 1) * HEAD_DIM)
        k_sc[hk, 0:PAST_LEN, :] = ck_ref[:, ks].astype(BF16)
        v_sc[hk, 0:PAST_LEN, :] = cv_ref[:, ks].astype(BF16)
        kn = _rope(_rms(akv_ref[:, ks], kg), cos_ref[...], sin_ref[...])
        k_sc[hk, PAST_LEN:, :] = kn.astype(BF16)
        v_sc[hk, PAST_LEN:, :] = akv_ref[:, 256 + hk * HEAD_DIM:256 + (hk + 1) * HEAD_DIM].astype(BF16)

    def body(r, carry):
        r0 = pl.multiple_of(r * qb, qb)
        cos = cos_ref[pl.ds(r0, qb), :]
        sin = sin_ref[pl.ds(r0, qb), :]
        for hk in range(ATT_HKV):
            def head_cols(g, hk=hk):
                return slice((hk * ATT_G + g) * HEAD_DIM, (hk * ATT_G + g + 1) * HEAD_DIM)

            def store(g, val):
                o_ref[pl.ds(r0, qb), head_cols(g)] = val

            qs = [_rope(_rms(aq_ref[pl.ds(r0, qb), head_cols(g)], qg), cos, sin)
                  for g in range(ATT_G)]
            _group_attention(qs, k_sc[hk], v_sc[hk], store)
        return carry

    lax.fori_loop(0, DEC_SEQ // qb, body, 0)


def _att_ctx(p_main, qg, kg, n_seq):
    L = SEQ
    return pl.pallas_call(
        _att_ctx_kernel,
        grid=(n_seq,),
        in_specs=[
            pl.BlockSpec((L, 1024), lambda b: (b, P_AQ // 1024)),
            pl.BlockSpec((L, 512), lambda b: (b, P_AKV // 512)),
            pl.BlockSpec((1, HEAD_DIM), lambda b: (0, 0)),
            pl.BlockSpec((1, HEAD_DIM), lambda b: (0, 0)),
        ],
        out_specs=[
            pl.BlockSpec((L, 1024), lambda b: (b, 0)),
            pl.BlockSpec((None, L, 256), lambda b: (b, 0, 0)),
            pl.BlockSpec((None, L, 256), lambda b: (b, 0, 0)),
        ],
        out_shape=[
            jax.ShapeDtypeStruct((n_seq * L, 1024), BF16),
            jax.ShapeDtypeStruct((n_seq, L, 256), F32),
            jax.ShapeDtypeStruct((n_seq, L, 256), F32),
        ],
        compiler_params=_cparams(("arbitrary",)),
        name="att_ctx",
    )(p_main, p_main, qg, kg)


def _att_lat(p_main, qg, kg, cos, sin, ck, cv, layer, n_seq, row_blk0):
    L = DEC_SEQ
    return pl.pallas_call(
        _att_lat_kernel,
        grid=(n_seq,),
        in_specs=[
            pl.BlockSpec((L, 1024), lambda b: (row_blk0 + b, P_AQ // 1024)),
            pl.BlockSpec((L, 512), lambda b: (row_blk0 + b, P_AKV // 512)),
            pl.BlockSpec((1, HEAD_DIM), lambda b: (0, 0)),
            pl.BlockSpec((1, HEAD_DIM), lambda b: (0, 0)),
            pl.BlockSpec((L, HEAD_DIM), lambda b: (0, 0)),
            pl.BlockSpec((L, HEAD_DIM), lambda b: (0, 0)),
            pl.BlockSpec((None, None, PAST_LEN, 256), lambda b: (b, layer, 0, 0)),
            pl.BlockSpec((None, None, PAST_LEN, 256), lambda b: (b, layer, 0, 0)),
        ],
        out_specs=pl.BlockSpec((L, 1024), lambda b: (b, 0)),
        out_shape=jax.ShapeDtypeStruct((n_seq * L, 1024), BF16),
        scratch_shapes=[
            pltpu.VMEM((ATT_HKV, PAST_LEN + L, HEAD_DIM), BF16),
            pltpu.VMEM((ATT_HKV, PAST_LEN + L, HEAD_DIM), BF16),
        ],
        compiler_params=_cparams(("arbitrary",)),
        name="att_lat",
    )(p_main, p_main, qg, kg, cos, sin, ck, cv)


def _seqs_per_step(L, n_seq):
    return 2 if (L <= SEQ and n_seq % 2 == 0) else 1


_GLA_LEVELS = (32, 16, 8, 4, 2, 1)


def _gla_consts():
    C = CHUNK
    i = np.arange(C)[:, None]
    t = np.arange(C)[None, :]
    mats = [t <= i, t > i]
    masks = [i == t]
    for s in _GLA_LEVELS:
        m = (i // s) * s
        if s > 1:
            mats.append((t > m) & (t <= i))
        mats.append((t > i) & (t <= m + s))
        masks.append((i // (2 * s) == t // (2 * s)) & ((i // s) % 2 == 1) & ((t // s) % 2 == 0))
    fwd = np.concatenate(mats, 0)
    bwd = np.concatenate([mm[::-1, ::-1] for mm in mats], 0)
    mstack = np.stack([fwd, bwd]).astype(np.float32)
    mf = np.stack([np.concatenate([mm, mm], 0) for mm in masks])
    mb = np.stack([np.concatenate([mm[::-1, ::-1], mm[::-1, ::-1]], 0) for mm in masks])
    lmask = np.stack([mf, mb]).astype(np.float32)
    return jnp.asarray(mstack, BF16), jnp.asarray(lmask, F32)


def _hosted_cast(refs, n_in, n_out, has_cast):
    if not has_cast:
        return refs
    refs[n_in + 1 + n_out][...] = refs[n_in][...].astype(BF16)
    return refs[:n_in] + refs[n_in + 1:n_in + 1 + n_out] + refs[n_in + 2 + n_out:]


def _cast_specs(cast, n_steps):
    if cast is None:
        return [], [], [], []
    w, layer = cast
    rows, cols = w.shape[1:]
    rb = rows // n_steps
    assert rb * n_steps == rows and rb % 16 == 0
    return ([pl.BlockSpec((None, rb, cols), lambda b: (layer, b, 0))],
            [pl.BlockSpec((None, rb, cols), lambda b: (0, b, 0))],
            [jax.ShapeDtypeStruct((1, rows, cols), BF16)], [w])


def _gla_kernel(L, nb, has_state, has_cast, *refs):
    refs = _hosted_cast(refs, 7 + has_state, 2, has_cast)
    if has_state:
        (pg_ref, ps_ref, w2_ref, gb_ref, ng_ref, ms_ref, lm_ref, s0_ref,
         o_ref, so_ref, la_sc, of_sc, ob_sc, st_sc) = refs
    else:
        (pg_ref, ps_ref, w2_ref, gb_ref, ng_ref, ms_ref, lm_ref,
         o_ref, so_ref, la_sc, of_sc, ob_sc, st_sc) = refs
    n = L // CHUNK
    C = CHUNK

    ps = ps_ref[...]
    for z in range(2):
        x = _dot3(ps, w2_ref[z]) + gb_ref[z]
        la_sc[z] = (jnp.minimum(x, 0.0) - jnp.log(1.0 + jnp.exp(-jnp.abs(x)))) * (1.0 / GLA_GATE_NORM)

    if has_state:
        for sq in range(nb):
            for z in range(2):
                for p in range(2):
                    pair = jnp.concatenate([s0_ref[sq, z, 2 * p], s0_ref[sq, z, 2 * p + 1]], axis=0)
                    st_sc[sq, z, p] = jnp.transpose(pair)
    else:
        st_sc[...] = jnp.zeros(st_sc.shape, F32)

    lane = lax.broadcasted_iota(jnp.int32, (C, LANES), 1)
    first = lane < 64

    def stack_heads(t):
        return jnp.concatenate([jnp.where(first, t, 0.0), jnp.where(first, 0.0, t)], axis=0)

    chains = [(t, sq, z, p) for t in range(2) for sq in range(nb) for z in range(2) for p in range(2)]
    nch = len(chains)
    nch1 = nch // 2

    def body(it, carry):
        def chunk_rows(t, sq, z):
            c = 2 * it + t
            cc = c if z == 0 else n - 1 - c
            return pl.ds(pl.multiple_of(sq * L + cc * C, C), C)

        rows = [chunk_rows(t, sq, z) for t, sq, z, p in chains]
        last = [C - 1, 0]
        q = [pg_ref[rows[i], p * LANES:(p + 1) * LANES] * (GLA_DK ** -0.5)
             for i, (t, sq, z, p) in enumerate(chains)]
        k = [pg_ref[rows[i], 256 + p * LANES:256 + (p + 1) * LANES]
             for i, (t, sq, z, p) in enumerate(chains)]
        e = [jnp.exp(_dot_exact_lhs_wide(ms_ref[z], la_sc[z, rows[i], p * LANES:(p + 1) * LANES]))
             for i, (t, sq, z, p) in enumerate(chains)]
        a2 = [_bdot_nt(stack_heads(q[i]), k[i]) * lm_ref[z, 0]
              for i, (t, sq, z, p) in enumerate(chains)]
        blk = 2
        for li, s in enumerate(_GLA_LEVELS):
            if s > 1:
                qs = [q[i] * e[i][blk * C:(blk + 1) * C] for i in range(nch)]
                blk += 1
            else:
                qs = q
            ks = [k[i] * e[i][blk * C:(blk + 1) * C] for i in range(nch)]
            blk += 1
            a2 = [a2[i] + _bdot_nt(stack_heads(qs[i]), ks[i]) * lm_ref[z, li + 1]
                  for i, (t, sq, z, p) in enumerate(chains)]
        kl = [k[i] * e[i][C:2 * C] for i in range(nch)]
        v = [[pg_ref[rows[i], 512 + (2 * p + hh) * LANES:512 + (2 * p + hh + 1) * LANES]
              for hh in range(2)] for i, (t, sq, z, p) in enumerate(chains)]
        intra = [[_bdot(a2[i][hh * C:(hh + 1) * C], v[i][hh]) for hh in range(2)]
                 for i in range(nch)]
        zs = [[_bdot_tn(v[i][hh], kl[i]) for hh in range(2)] for i in range(nch)]
        qe = [stack_heads(q[i] * e[i][0:C]) for i in range(nch)]
        st = [st_sc[sq, z, p] for t, sq, z, p in chains[0:nch1]]
        for i, (t, sq, z, p) in enumerate(chains):
            j = i % nch1
            inter = _bdot_nt(qe[i], st[j])
            osc = of_sc if z == 0 else ob_sc
            for hh in range(2):
                h = 2 * p + hh
                osc[rows[i], h * LANES:(h + 1) * LANES] = intra[i][hh] + inter[hh * C:(hh + 1) * C]
            st[j] = (st[j] * e[i][last[z]:last[z] + 1, :]
                     + jnp.where(first[0:1], zs[i][0], zs[i][1]))
        for j, (t, sq, z, p) in enumerate(chains[0:nch1]):
            st_sc[sq, z, p] = st[j]
        return carry

    lax.fori_loop(0, n // 2, body, 0)

    ng = ng_ref[...]

    def fin(c, carry):
        r0 = pl.multiple_of(c * C, C)
        rows = pl.ds(r0, C)
        for h in range(GLA_H):
            cols = slice(h * LANES, (h + 1) * LANES)
            o = of_sc[rows, cols] + ob_sc[rows, cols]
            gate = _silu(pg_ref[rows, 1024 + h * LANES:1024 + (h + 1) * LANES])
            o_ref[rows, cols] = (_rms(o, ng) * gate).astype(BF16)
        return carry

    lax.fori_loop(0, nb * n, fin, 0)
    for sq in range(nb):
        for z in range(2):
            for p in range(2):
                pair = jnp.transpose(st_sc[sq, z, p])
                so_ref[sq, z, 2 * p] = pair[0:GLA_DK]
                so_ref[sq, z, 2 * p + 1] = pair[GLA_DK:2 * GLA_DK]


def _gla(p_main, p_small, w2p, gbias, ng, consts, s0, layer, L, n_seq, row_blk0, cast=None):
    mstack, lmask = consts
    has_state = s0 is not None
    nb = _seqs_per_step(L, n_seq)
    assert row_blk0 % nb == 0
    blk0 = row_blk0 // nb
    state_blk = (nb, 2, GLA_H, GLA_DK, GLA_DV)
    in_specs = [
        pl.BlockSpec((nb * L, 1536), lambda b: (blk0 + b, P_GLA // 1536)),
        pl.BlockSpec((nb * L, LANES), lambda b: (blk0 + b, 0)),
        pl.BlockSpec((2, LANES, 256), lambda b: (0, 0, 0)),
        pl.BlockSpec((2, 1, 256), lambda b: (0, 0, 0)),
        pl.BlockSpec((1, GLA_DV), lambda b: (0, 0)),
        pl.BlockSpec(mstack.shape, lambda b: (0, 0, 0)),
        pl.BlockSpec(lmask.shape, lambda b: (0, 0, 0, 0)),
    ]
    args = [p_main, p_small, w2p, gbias, ng, mstack, lmask]
    if has_state:
        in_specs.append(pl.BlockSpec((nb, None) + state_blk[1:], lambda b: (b, layer, 0, 0, 0, 0)))
        args.append(s0)
    c_in, c_out, c_shape, c_arg = _cast_specs(cast, n_seq // nb)
    return pl.pallas_call(
        functools.partial(_gla_kernel, L, nb, has_state, cast is not None),
        grid=(n_seq // nb,),
        in_specs=in_specs + c_in,
        out_specs=[
            pl.BlockSpec((nb * L, 512), lambda b: (b, 0)),
            pl.BlockSpec(state_blk, lambda b: (b, 0, 0, 0, 0)),
        ] + c_out,
        out_shape=[
            jax.ShapeDtypeStruct((n_seq * L, 512), BF16),
            jax.ShapeDtypeStruct((n_seq, 2, GLA_H, GLA_DK, GLA_DV), F32),
        ] + c_shape,
        scratch_shapes=[
            pltpu.VMEM((2, nb * L, 256), F32),
            pltpu.VMEM((nb * L, 512), F32),
            pltpu.VMEM((nb * L, 512), F32),
            pltpu.VMEM((nb, 2, 2, GLA_DV, LANES), F32),
        ],
        compiler_params=_cparams(("arbitrary",)),
        name="gla_lat" if has_state else "gla_ctx",
    )(*args, *c_arg)


def _dn_kernel(L, nb, has_state, has_cast, *refs):
    refs = _hosted_cast(refs, 7 + has_state, 2, has_cast)
    if has_state:
        (pd_ref, dz_ref, ps_ref, cw_ref, al_ref, dtb_ref, ng_ref, s0_ref,
         o_ref, so_ref, qkv_sc, g_sc, bb_sc, of_sc, ob_sc, s_sc, prep_sc, att_sc, dec_sc) = refs
    else:
        (pd_ref, dz_ref, ps_ref, cw_ref, al_ref, dtb_ref, ng_ref,
         o_ref, so_ref, qkv_sc, g_sc, bb_sc, of_sc, ob_sc, s_sc, prep_sc, att_sc, dec_sc) = refs
    n = L // CHUNK
    C = CHUNK

    R = nb * L
    pos = lax.broadcasted_iota(jnp.int32, (R, LANES), 0) & (L - 1)
    not_first = pos > 0
    not_last = pos < L - 1

    def conv_silu(j):
        c0 = pl.multiple_of(j * LANES, LANES)
        x = pd_ref[:, pl.ds(c0, LANES)]
        w = cw_ref[:, pl.ds(c0, LANES)]
        prev = jnp.where(not_first, pltpu.roll(x, 1, 0), 0.0)
        nxt = jnp.where(not_last, pltpu.roll(x, R - 1, 0), 0.0)
        return c0, _silu(prev * w[0:1] + x * w[1:2] + nxt * w[2:3])

    def qk_tile(j, carry):
        c0, y = conv_silu(j)
        inv = lax.rsqrt(jnp.sum(y * y, axis=-1, keepdims=True) + EPS)
        qkv_sc[:, pl.ds(c0, LANES)] = y * (inv * jnp.where(j < DN_H, DN_DK ** -0.5, 1.0))
        return carry

    def v_tile(j, carry):
        c0, y = conv_silu(j)
        qkv_sc[:, pl.ds(c0, LANES)] = y
        return carry

    lax.fori_loop(0, 2 * DN_H, qk_tile, 0)
    lax.fori_loop(2 * DN_H, 3 * DN_H, v_tile, 0)

    ps = ps_ref[...]
    lane_l = lax.broadcasted_iota(jnp.int32, (R, LANES), 1)
    g_sc[...] = -jnp.exp(al_ref[...]) * _softplus(ps + dtb_ref[...])
    b_all = _sigmoid(ps)
    for idx in range(2 * DN_H):
        bcol = jnp.sum(jnp.where(lane_l == S_DB + idx, b_all, 0.0), axis=-1, keepdims=True)
        bb_sc[idx] = jnp.broadcast_to(bcol, (R, LANES))

    if has_state:
        s_sc[...] = s0_ref[...]
    else:
        s_sc[...] = jnp.zeros(s_sc.shape, F32)

    ri = lax.broadcasted_iota(jnp.int32, (C, LANES), 0)
    lane_c = lax.broadcasted_iota(jnp.int32, (C, LANES), 1)
    ci = lane_c & (C - 1)
    first = lane_c < C
    incl = [ci <= ri, ci >= ri]
    strict = [ci < ri, ci > ri]
    rs = lax.broadcasted_iota(jnp.int32, (C, C), 0)
    cs = lax.broadcasted_iota(jnp.int32, (C, C), 1)
    m_b = [(cs <= rs).astype(BF16), (cs >= rs).astype(BF16)]
    nu = 2 * nb
    chains = [(sq, z, h) for sq in range(nu) for z in range(2) for h in range(DN_H)]
    nch = len(chains)
    pairs = [(sq, z, p) for sq in range(nu) for z in range(2) for p in range(DN_H // 2)]
    npr = len(pairs)

    def stack_heads(t):
        return jnp.concatenate([jnp.where(first, t, 0.0), jnp.where(first, 0.0, t)], axis=0)

    def hi_mid(x):
        hi = x.astype(BF16).astype(F32)
        return hi, x - hi

    def pair_products(lhs_list, x):
        xh, xm = hi_mid(x)
        rhs = jnp.concatenate([stack_heads(xh), stack_heads(xm)], axis=1).astype(BF16)
        parts = []
        for l in lhs_list:
            parts.extend(hi_mid(l))
        r = jnp.dot(jnp.concatenate(parts, axis=0).astype(BF16), rhs, preferred_element_type=F32)
        out = []
        for i in range(len(lhs_list)):
            blk = r[2 * C * i:2 * C * i + C] + r[2 * C * i + C:2 * C * (i + 1)]
            out.append(blk[:, 0:LANES] + blk[:, LANES:2 * LANES])
        return out

    heads = [((2 * sq + z) * DN_H + 2 * p, (2 * sq + z) * DN_H + 2 * p + 1)
             for sq, z, p in pairs]

    def chunk_rows(cs):
        def one(sq, z):
            c = cs[sq // nb]
            start = (sq % nb) * L + (c if z == 0 else n - 1 - c) * C
            return pl.ds(start if isinstance(start, int) else pl.multiple_of(start, C), C)
        return [one(sq, z) for sq in range(nu) for z in range(2)]

    def prepare_products(cs):
        rows = chunk_rows(cs)
        last = [C - 1, 0]

        def ld(base, sq, z, h):
            return qkv_sc[rows[2 * sq + z], base + h * LANES:base + (h + 1) * LANES]

        q = [ld(0, *ch) for ch in chains]
        k = [ld(512, *ch) for ch in chains]
        v = [ld(1024, *ch) for ch in chains]
        bb = [bb_sc[z * DN_H + h, rows[2 * sq + z], :] for sq, z, h in chains]
        gc = [_dot_exact_lhs(m_b[z], g_sc[rows[2 * sq + z], :])
              for sq in range(nu) for z in range(2)]
        gct =[jnp.transpose(jnp.concatenate([g, g], axis=0)) for g in gc]
        gi = [jnp.broadcast_to(jnp.sum(jnp.where(lane_c == S_DA + z * DN_H + h, gc[2 * sq + z], 0.0),
                                       axis=-1, keepdims=True), (C, LANES)) for sq, z, h in chains]
        gjrow = [gct[2 * sq + z][S_DA + z * DN_H + h:S_DA + z * DN_H + h + 1, :]
                 for sq, z, h in chains]
        gam = []
        for (sq, z, p), (i0, i1) in zip(pairs, heads):
            diff = jnp.where(first, gi[i0], gi[i1]) - jnp.where(first[0:1], gjrow[i0], gjrow[i1])
            gam.append(jnp.where(incl[z], jnp.exp(jnp.where(incl[z], diff, 0.0)), 0.0))
        kb = [k[i] * bb[i] for i in range(nch)]
        kq = [_bdot_nt(jnp.concatenate([kb[i0], q[i0], kb[i1], q[i1]], axis=0),
                       jnp.concatenate([k[i0], k[i1]], axis=0)) for i0, i1 in heads]
        gt = [gi[i][last[z]:last[z] + 1, :] for i, (sq, z, h) in enumerate(chains)]
        return q, k, v, bb, kb, gi, gt, gam, kq

    def prepare_finish(parts):
        q, k, v, bb, kb, gi, gt, gam, kq = parts
        a = [jnp.where(strict[z], jnp.where(first, kq[j][0:C], kq[j][2 * C:3 * C]) * gam[j], 0.0)
             for j, (sq, z, p) in enumerate(pairs)]
        att = [jnp.where(first, kq[j][C:2 * C], kq[j][3 * C:4 * C]) * gam[j] for j in range(npr)]
        nt = [-a[j] for j in range(npr)]
        pw = [pair_products([a[j]], a[j])[0] for j in range(npr)]
        for step in range(5):
            if step < 4:
                both = [pair_products([nt[j], pw[j]], pw[j]) for j in range(npr)]
                nt = [nt[j] + pw[j] + both[j][0] for j in range(npr)]
                pw = [both[j][1] for j in range(npr)]
            else:
                nt = [nt[j] + pw[j] + pair_products([nt[j]], pw[j])[0] for j in range(npr)]
        eg = [jnp.exp(gi[i]) for i in range(nch)]
        rhs = [jnp.concatenate([v[i] * bb[i], kb[i] * eg[i]], axis=1) for i in range(nch)]
        corr = [_bdot(stack_heads(nt[j]), jnp.concatenate([rhs[i0], rhs[i1]], axis=0))
                for j, (i0, i1) in enumerate(heads)]
        for j, (i0, i1) in enumerate(heads):
            for half, i in enumerate((i0, i1)):
                uw = rhs[i] + corr[j][half * C:(half + 1) * C]
                prep_sc[i, 0] = uw[:, 0:LANES]
                prep_sc[i, 1] = uw[:, LANES:]
                prep_sc[i, 2] = q[i] * eg[i]
                prep_sc[i, 3] = k[i] * jnp.exp(gt[i] - gi[i])
                dec_sc[i] = jnp.broadcast_to(jnp.exp(gt[i]), (8, LANES))
            att_sc[j] = att[j]

    nch1, npr1 = nch // 2, npr // 2

    def apply_state_products(slot, s):
        return [_bdot(jnp.concatenate([prep_sc[slot * nch1 + i, 1], prep_sc[slot * nch1 + i, 2]],
                                      axis=0), s[i]) for i in range(nch1)]

    def apply_finish(rows, slot, s, ws):
        base = slot * nch1
        v_new = [prep_sc[base + i, 0] - ws[i][0:C] for i in range(nch1)]
        av = []
        for j in range(npr1):
            i0, i1 = heads[slot * npr1 + j]
            av.append(_bdot(stack_heads(att_sc[slot * npr1 + j]),
                            jnp.concatenate([v_new[i0 - base], v_new[i1 - base]], axis=0)))
        upd = [_bdot_tn(prep_sc[base + i, 3], v_new[i]) for i in range(nch1)]
        for j in range(npr1):
            for half, ig in enumerate(heads[slot * npr1 + j]):
                sq, z, h = chains[ig]
                osc = of_sc if z == 0 else ob_sc
                osc[rows[2 * sq + z], h * LANES:(h + 1) * LANES] = (
                    ws[ig - base][C:2 * C] + av[j][half * C:(half + 1) * C])
        return [s[i] * dec_sc[base + i, 0:1, :] + upd[i] for i in range(nch1)]

    def apply_two(cs, parts_fn):
        rows = chunk_rows(cs)
        s = [s_sc[sq, z, h] for sq, z, h in chains[0:nch1]]
        ws = apply_state_products(0, s)
        parts = parts_fn()
        s = apply_finish(rows, 0, s, ws)
        ws = apply_state_products(1, s)
        s = apply_finish(rows, 1, s, ws)
        for i, (sq, z, h) in enumerate(chains[0:nch1]):
            s_sc[sq, z, h] = s[i]
        return parts

    prepare_finish(prepare_products((0, 1)))

    def body(it, carry):
        c = 2 * it
        parts = apply_two((c, c + 1), lambda: prepare_products((c + 2, c + 3)))
        prepare_finish(parts)
        return carry

    lax.fori_loop(0, n // 2 - 1, body, 0)
    apply_two((n - 2, n - 1), lambda: None)

    ng = ng_ref[...]

    def fin(c, carry):
        r0 = pl.multiple_of(c * C, C)
        rows = pl.ds(r0, C)
        for h in range(DN_H):
            cols = slice(h * LANES, (h + 1) * LANES)
            o = of_sc[rows, cols] + ob_sc[rows, cols]
            o_ref[rows, cols] = (_rms(o, ng) * _silu(dz_ref[rows, cols])).astype(BF16)
        return carry

    lax.fori_loop(0, nb * n, fin, 0)
    so_ref[...] = s_sc[...]


def _dn(p_main, p_small, conv_w, alog_row, dtb_row, ng, s0, layer, L, n_seq, row_blk0, cast=None):
    has_state = s0 is not None
    nb = _seqs_per_step(L, n_seq)
    assert row_blk0 % nb == 0
    row_blk0, L_blk = row_blk0 // nb, nb * L
    in_specs = [
        pl.BlockSpec((L_blk, 1536), lambda b: (row_blk0 + b, P_DN // 1536)),
        pl.BlockSpec((L_blk, 512), lambda b: (row_blk0 + b, P_DZ // 512)),
        pl.BlockSpec((L_blk, LANES), lambda b: (row_blk0 + b, 0)),
        pl.BlockSpec((DN_CONV, 3 * DN_H * DN_DK), lambda b: (0, 0)),
        pl.BlockSpec((1, LANES), lambda b: (0, 0)),
        pl.BlockSpec((1, LANES), lambda b: (0, 0)),
        pl.BlockSpec((1, DN_DV), lambda b: (0, 0)),
    ]
    args = [p_main, p_main, p_small, conv_w, alog_row, dtb_row, ng]
    if has_state:
        in_specs.append(pl.BlockSpec((nb, None, 2, DN_H, DN_DK, DN_DV),
                                     lambda b: (b, layer, 0, 0, 0, 0)))
        args.append(s0)
    c_in, c_out, c_shape, c_arg = _cast_specs(cast, n_seq // nb)
    return pl.pallas_call(
        functools.partial(_dn_kernel, L, nb, has_state, cast is not None),
        grid=(n_seq // nb,),
        in_specs=in_specs + c_in,
        out_specs=[
            pl.BlockSpec((L_blk, 512), lambda b: (b, 0)),
            pl.BlockSpec((nb, 2, DN_H, DN_DK, DN_DV), lambda b: (b, 0, 0, 0, 0)),
        ] + c_out,
        out_shape=[
            jax.ShapeDtypeStruct((n_seq * L, 512), BF16),
            jax.ShapeDtypeStruct((n_seq, 2, DN_H, DN_DK, DN_DV), F32),
        ] + c_shape,
        scratch_shapes=[
            pltpu.VMEM((L_blk, 1536), F32),
            pltpu.VMEM((L_blk, LANES), F32),
            pltpu.VMEM((2 * DN_H, L_blk, LANES), F32),
            pltpu.VMEM((L_blk, 512), F32),
            pltpu.VMEM((L_blk, 512), F32),
            pltpu.VMEM((nb, 2, DN_H, DN_DK, DN_DV), F32),
            pltpu.VMEM((2 * nb * 2 * DN_H, 4, CHUNK, LANES), F32),
            pltpu.VMEM((2 * nb * DN_H, CHUNK, LANES), F32),
            pltpu.VMEM((2 * nb * 2 * DN_H, 8, LANES), F32),
        ],
        compiler_params=_cparams(("arbitrary",)),
        name="dn_lat" if has_state else "dn_ctx",
    )(*args, *c_arg)


_W_IN_SPLITS = (256, 256, 512, 512, 32, 1024, 256, 256, 1536, 512, 8, 8)


def _regroup_kernel(wt_ref, main_ref, small_ref):
    off = np.cumsum((0,) + _W_IN_SPLITS)
    gq, gk, gv, gr, glr, aq, ak, av, dqkv, dz, da, db = [
        wt_ref[int(off[i]):int(off[i + 1]), :] for i in range(12)]
    main_t = jnp.concatenate([aq, ak, av, gq, gk, gv, gr, dqkv, dz], axis=0)
    main_ref[...] = jnp.transpose(main_t).astype(BF16)
    pad = jnp.zeros((LANES - 48, wt_ref.shape[1]), F32)
    small_t = jnp.concatenate([glr, da, db, pad], axis=0)
    small_ref[...] = jnp.transpose(small_t).astype(BF16)


def _prep_w_in(w_in, tr=256):
    depth, _, in_cols = w_in.shape
    return pl.pallas_call(
        _regroup_kernel,
        grid=(depth, D_MODEL // tr),
        in_specs=[pl.BlockSpec((None, in_cols, tr), lambda l, r: (l, 0, r))],
        out_specs=[
            pl.BlockSpec((None, tr, P_MAIN), lambda l, r: (l, r, 0)),
            pl.BlockSpec((None, tr, LANES), lambda l, r: (l, r, 0)),
        ],
        out_shape=[
            jax.ShapeDtypeStruct((depth, D_MODEL, P_MAIN), BF16),
            jax.ShapeDtypeStruct((depth, D_MODEL, LANES), BF16),
        ],
        compiler_params=_cparams(("arbitrary", "arbitrary")),
        name="w_in_regroup",
    )(jnp.swapaxes(w_in, 1, 2))


def _lane_row(vals, offset):
    return jnp.zeros((1, LANES), F32).at[0, offset:offset + vals.shape[0]].set(vals.astype(F32))


def kernel(x_prompt, x_sample, cache_k, cache_v, state_gla, state_dn, c, c_ctx, norm1_g, norm2_g,
           w_mod, b_mod, w_in, gla_w2, gla_b, gla_norm_g, q_norm_g, k_norm_g, dn_conv, dn_a_log,
           dn_dt_bias, dn_norm_g, w_out, w_ff1, w_ff2):
    n_ctx, n_lat = x_prompt.shape[0], x_sample.shape[0]
    depth = w_in.shape[0]
    n_ctx_rows = n_ctx * SEQ
    assert n_ctx_rows % DEC_SEQ == 0 and n_lat <= 7

    xs = [x_prompt.reshape(n_ctx_rows, D_MODEL), x_sample.reshape(n_lat * DEC_SEQ, D_MODEL)]
    cond8 = jnp.concatenate([c_ctx[None, :], c, jnp.zeros((7 - n_lat, D_MODEL), F32)], axis=0)
    b_mod3 = b_mod.reshape(depth, 1, N_MOD * D_MODEL)
    mods_l = _modulation(cond8, w_mod, b_mod3, 1).reshape(8, N_MOD, D_MODEL)

    cos, sin = _rope_tables()
    gla_consts = _gla_consts()
    ck = cache_k.reshape(n_lat, depth, PAST_LEN, ATT_HKV * HEAD_DIM)
    cv = cache_v.reshape(n_lat, depth, PAST_LEN, ATT_HKV * HEAD_DIM)

    w_main, w_small = _prep_w_in(w_in)
    w_out_b, w_ff1_b, w_ff2_b = w_out[0:1].astype(BF16), None, None

    nks, nvs, sgs, sds = [], [], [], []
    for l in range(depth):
        tc, tl = n_ctx_rows // INPROJ_TM, n_lat * DEC_SEQ // INPROJ_TM
        inproj = functools.partial(_inproj, mods_l=mods_l, g1=norm1_g[l][None, :], w_main=w_main,
                                   w_small=w_small, layer=l, n_ctx_rows=n_ctx_rows)
        pc, psc = inproj(xs[0], tile0=0, n_tiles=tc, grp_tile0=0)
        if len(xs) == 2:
            pl_, psl = inproj(xs[1], tile0=0, n_tiles=tl, grp_tile0=tc)
        else:
            pl_, psl = inproj(xs[0], tile0=tc, n_tiles=tl, grp_tile0=tc)

        qg, kg = q_norm_g[l][None, :], k_norm_g[l][None, :]
        oa_c, nk, nv = _att_ctx(pc, qg, kg, n_ctx)
        oa_l = _att_lat(pl_, qg, kg, cos, sin, ck, cv, l, n_lat, 0)

        w2p = jnp.zeros((2, LANES, GLA_H * GLA_DK), F32)
        w2p = w2p.at[0, 0:GLA_LR].set(gla_w2[l, 0]).at[1, GLA_LR:2 * GLA_LR].set(gla_w2[l, 1])
        gbias = gla_b[l][:, None, :]
        gng = gla_norm_g[l][None, :]
        og_c, sg, *hosted = _gla(pc, psc, w2p, gbias, gng, gla_consts, None, l, SEQ, n_ctx, 0,
                                 cast=(w_ff1, 0) if l == 0 else None)
        if hosted:
            w_ff1_b = hosted[0]
        og_l, _ = _gla(pl_, psl, w2p, gbias, gng, gla_consts, state_gla, l, DEC_SEQ, n_lat, 0)

        alog_row = _lane_row(dn_a_log[l].reshape(-1), S_DA)
        dtb_row = _lane_row(dn_dt_bias[l].reshape(-1), S_DA)
        dng = dn_norm_g[l][None, :]
        od_c, sd, *hosted = _dn(pc, psc, dn_conv[l], alog_row, dtb_row, dng, None, l, SEQ, n_ctx, 0,
                                cast=(w_ff2, 0) if l == 0 else None)
        if hosted:
            w_ff2_b = hosted[0]
        od_l, _ = _dn(pl_, psl, dn_conv[l], alog_row, dtb_row, dng, state_dn, l, DEC_SEQ, n_lat, 0)

        x1, h2 = _outproj(xs, [og_c, og_l], [oa_c, oa_l], [od_c, od_l], mods_l,
                          norm2_g[l][None, :], w_out_b, 0, n_ctx_rows)
        n_tiles, nct = x1.shape[0] // FFN_TM, n_ctx_rows // FFN_TM
        ffn = functools.partial(_ffn, h2, x1, mods_l, w_ff1_b, w_ff2_b, n_ctx_rows, tm=FFN_TM)
        if l < depth - 1:
            y, w_ff1_b, w_ff2_b, w_out_b, m_next = ffn(
                tile0=0, n_tiles=n_tiles, next_layer=l + 1, cast_next=(w_ff1, w_ff2, w_out),
                mod_next=(cond8, w_mod, b_mod3))
            xs, mods_l = [y], m_next.reshape(8, N_MOD, D_MODEL)
        else:
            xs = [ffn(tile0=0, n_tiles=nct)[0], ffn(tile0=nct, n_tiles=n_tiles - nct)[0]]

        nks.append(nk.reshape(n_ctx, SEQ, ATT_HKV, HEAD_DIM))
        nvs.append(nv.reshape(n_ctx, SEQ, ATT_HKV, HEAD_DIM))
        sgs.append(sg)
        sds.append(sd)

    y_prompt = xs[0].reshape(n_ctx, SEQ, D_MODEL)
    y_sample = xs[1].reshape(n_lat, DEC_SEQ, D_MODEL)
    return (y_prompt, y_sample, jnp.stack(nks, axis=1), jnp.stack(nvs, axis=1),
            jnp.stack(sgs, axis=1), jnp.stack(sds, axis=1))
```
